```python
import jax, jax.numpy as jnp
from jax import lax
import numpy as np

D_MODEL = 1024
BATCH = 32
SEQ = 2048
DEPTH = 4

N_MIXERS = 2
N_FOX = (DEPTH + 1) // 2
N_HGRN = DEPTH // 2
FOX_HEADS = 16
FOX_HEAD_DIM = D_MODEL // FOX_HEADS
FOX_Q_BLOCK = 128
MASK_VALUE = -1e30
HGRN_EXPAND = 128
HGRN_HEADS = D_MODEL // HGRN_EXPAND
HGRN_KEY_DIM = HGRN_EXPAND
HGRN_VAL_DIM = D_MODEL // HGRN_HEADS
HGRN_KEY_WIDTH = HGRN_HEADS * HGRN_KEY_DIM
HGRN_CHUNK = 32
N_EXPERTS = 32
TOP_K = 4
D_FF = D_MODEL
SWIGLU_LIMIT = 7.0
SWIGLU_ALPHA = 1.702
MOE_ROW_BLOCK = 512
N_MOD = 6
EPS = 1e-6

kernel_name = 'fox_hgrn2_moe_adaln_hybrid'


def rms_norm(x, w):
    xf = x.astype(jnp.float32)
    y = xf * lax.rsqrt(jnp.mean(xf * xf, axis=-1, keepdims=True) + EPS)
    return (y * w.astype(jnp.float32)).astype(x.dtype)


def fox_attention(h, w_in, b_f, q_norm, k_norm, w_out):
    B, S, D = h.shape
    H, Dh, QB = FOX_HEADS, FOX_HEAD_DIM, FOX_Q_BLOCK
    f32 = jnp.float32
    proj = h @ w_in
    q = proj[..., 0 * D:1 * D].reshape(B, S, H, Dh)
    k = proj[..., 1 * D:2 * D].reshape(B, S, H, Dh)
    v = proj[..., 2 * D:3 * D].reshape(B, S, H, Dh)
    g = proj[..., 3 * D:4 * D]
    f_logit = proj[..., 4 * D:] + b_f
    q = rms_norm(q, q_norm)
    k = rms_norm(k, k_norm)
    log_f = jax.nn.log_sigmoid(f_logit.astype(f32))
    cum = jnp.cumsum(log_f, axis=1).transpose(0, 2, 1)
    q, k, v = (t.transpose(0, 2, 1, 3) for t in (q, k, v))
    scale = Dh ** -0.5
    outs = []
    for blk in range(S // QB):
        q0, q1 = blk * QB, (blk + 1) * QB
        s = jnp.einsum('bhqd,bhkd->bhqk', q[:, :, q0:q1], k[:, :, :q1]).astype(f32) * scale
        s = s + cum[:, :, q0:q1, None] - cum[:, :, None, :q1]
        causal = jnp.arange(q0, q1)[:, None] >= jnp.arange(q1)[None, :]
        s = jnp.where(causal, s, MASK_VALUE)
        p = jax.nn.softmax(s, axis=-1).astype(v.dtype)
        outs.append(jnp.einsum('bhqk,bhkd->bhqd', p, v[:, :, :q1]))
    o = jnp.concatenate(outs, axis=2).transpose(0, 2, 1, 3).reshape(B, S, D)
    o = o * jax.nn.sigmoid(g)
    return o @ w_out


def hgrn2_mixer(h, w_in, lower_bound, g_norm, w_out):
    B, S, D = h.shape
    H, DK, DV, C = HGRN_HEADS, HGRN_KEY_DIM, HGRN_VAL_DIM, HGRN_CHUNK
    KW = HGRN_KEY_WIDTH
    n = S // C
    f32 = jnp.float32
    proj = h @ w_in
    q = jax.nn.silu(proj[..., :KW].astype(f32))
    f_raw = proj[..., KW:2 * KW].astype(f32)
    v = proj[..., 2 * KW:2 * KW + H * DV].astype(f32)
    g = proj[..., 2 * KW + H * DV:].astype(f32)
    lb = lower_bound.astype(f32)
    f = lb + (1.0 - lb) * jax.nn.sigmoid(f_raw)
    log_f = jnp.log(f)
    k = (1.0 - lb) * jax.nn.sigmoid(-f_raw)

    def chunks(t, d):
        return t.reshape(B, n, C, H, d).transpose(1, 0, 3, 2, 4)

    xs = (chunks(q, DK), chunks(k, DK), chunks(v, DV), chunks(log_f, DK))
    causal = jnp.tril(jnp.ones((C, C), dtype=bool))[:, :, None]

    def step(state, inp):
        qc, kc, vc, lfc = inp
        b = jnp.cumsum(lfc, axis=2)
        diff = b[:, :, :, None, :] - b[:, :, None, :, :]
        decay = jnp.where(causal, jnp.exp(jnp.where(causal, diff, 0.0)), 0.0)
        scores = jnp.einsum('bhtd,bhsd,bhtsd->bhts', qc, kc, decay)
        o = (jnp.einsum('bhts,bhsv->bhtv', scores, vc)
             + jnp.einsum('bhtd,bhdv->bhtv', qc * jnp.exp(b), state))
        b_end = b[:, :, -1:, :]
        state = (jnp.exp(b_end[:, :, 0, :, None]) * state
                 + jnp.einsum('bhsd,bhsv->bhdv', kc * jnp.exp(b_end - b), vc))
        return state, o

    state0 = jnp.zeros((B, H, DK, DV), f32)
    _, o = lax.scan(step, state0, xs)
    o = o.transpose(1, 0, 3, 2, 4).reshape(B, S, H, DV)
    o = rms_norm(o, g_norm) * jax.nn.silu(g).reshape(B, S, H, DV)
    return o.reshape(B, S, H * DV).astype(h.dtype) @ w_out


def clamped_swiglu(gu):
    gate = jnp.minimum(gu[..., 0::2], SWIGLU_LIMIT)
    up = jnp.clip(gu[..., 1::2], -SWIGLU_LIMIT, SWIGLU_LIMIT)
    return (up + 1.0) * gate * jax.nn.sigmoid(SWIGLU_ALPHA * gate)


def moe_ffn(h, w_router, b_router, w_gu, b_gu, w_down, b_down):
    B, S, D = h.shape
    E, K, R = N_EXPERTS, TOP_K, MOE_ROW_BLOCK
    xf = h.reshape(B * S, D)
    T = xf.shape[0]
    logits = (xf @ w_router + b_router).astype(jnp.float32)
    top_logit, top_idx = lax.top_k(logits, K)
    top_w = jax.nn.softmax(top_logit, axis=-1)
    flat_e = top_idx.reshape(-1)
    flat_w = top_w.reshape(-1)
    flat_tok = jnp.arange(T * K, dtype=jnp.int32) // K
    order = jnp.argsort(flat_e)
    e_sorted = flat_e[order]
    counts = jnp.bincount(flat_e, length=E)
    padded = (counts + R - 1) // R * R
    pad_end = jnp.cumsum(padded)
    pad_start = pad_end - padded
    start = jnp.cumsum(counts) - counts
    dest = pad_start[e_sorted] + (jnp.arange(T * K) - start[e_sorted])
    n_blocks = -(-(T * K) // R) + E
    tok_buf = jnp.zeros((n_blocks * R,), jnp.int32).at[dest].set(flat_tok[order])
    w_buf = jnp.zeros((n_blocks * R,), xf.dtype).at[dest].set(flat_w[order].astype(xf.dtype))
    blk_e = jnp.minimum(jnp.searchsorted(pad_end, jnp.arange(n_blocks) * R, side='right'), E - 1)

    def expert_block(args):
        tok, wgt, e = args
        xb = xf[tok]
        y = clamped_swiglu(xb @ w_gu[e] + b_gu[e]) @ w_down[e] + b_down[e]
        return y * wgt[:, None]

    rows = lax.map(expert_block, (tok_buf.reshape(n_blocks, R), w_buf.reshape(n_blocks, R), blk_e))
    out = jnp.zeros_like(xf).at[tok_buf].add(rows.reshape(n_blocks * R, D))
    return out.reshape(B, S, D)


def setup_inputs(seed: int = 0) -> dict:
    key = jax.random.key(seed)
    ks = jax.random.split(key, 22)
    D, H, Dh = D_MODEL, FOX_HEADS, FOX_HEAD_DIM
    KW, DV = HGRN_KEY_WIDTH, HGRN_VAL_DIM
    E, F = N_EXPERTS, D_FF

    def nrm(k, shape, std):
        return jax.random.normal(k, shape, jnp.float32) * std

    return {
        'x': nrm(ks[0], (BATCH, SEQ, D), 1.0),
        'c': nrm(ks[1], (BATCH, D), 1.0),
        'fox_w_in': nrm(ks[2], (N_FOX, D, 4 * D + H), D ** -0.5),
        'fox_b_f': 2.0 + nrm(ks[3], (N_FOX, H), 0.5),
        'fox_q_norm': 1.0 + nrm(ks[4], (N_FOX, Dh), 0.02),
        'fox_k_norm': 1.0 + nrm(ks[5], (N_FOX, Dh), 0.02),
        'fox_w_out': nrm(ks[6], (N_FOX, D, D), D ** -0.5),
        'hgrn_w_in': nrm(ks[7], (N_HGRN, D, 2 * KW + HGRN_HEADS * DV + D), D ** -0.5),
        'hgrn_lb': nrm(ks[8], (N_HGRN, KW), 0.5),
        'hgrn_g_norm': 1.0 + nrm(ks[9], (N_HGRN, DV), 0.02),
        'hgrn_w_out': nrm(ks[10], (N_HGRN, HGRN_HEADS * DV, D), D ** -0.5),
        'ada_w': nrm(ks[11], (DEPTH, D, N_MOD * D), 0.5 * D ** -0.5),
        'ada_b': nrm(ks[12], (DEPTH, N_MOD * D), 0.02),
        'norm1_w': 1.0 + nrm(ks[13], (DEPTH, D), 0.02),
        'norm2_w': 1.0 + nrm(ks[14], (DEPTH, D), 0.02),
        'router_w': nrm(ks[15], (DEPTH, D, E), D ** -0.5),
        'router_b': nrm(ks[16], (DEPTH, E), 0.01),
        'exp_w_gu': nrm(ks[17], (DEPTH, E, D, 2 * F), D ** -0.5),
        'exp_b_gu': nrm(ks[18], (DEPTH, E, 2 * F), 0.02),
        'exp_w_down': nrm(ks[19], (DEPTH, E, F, D), F ** -0.5),
        'exp_b_down': nrm(ks[20], (DEPTH, E, D), 0.02),
    }


def reference(x, c, fox_w_in, fox_b_f, fox_q_norm, fox_k_norm, fox_w_out,
              hgrn_w_in, hgrn_lb, hgrn_g_norm, hgrn_w_out,
              ada_w, ada_b, norm1_w, norm2_w,
              router_w, router_b, exp_w_gu, exp_b_gu, exp_w_down, exp_b_down):
    lb_soft = jax.nn.softmax(hgrn_lb.astype(jnp.float32), axis=0)
    lower = jnp.cumsum(lb_soft, axis=0) - lb_soft[0]
    c_act = jax.nn.silu(c)
    for i in range(DEPTH):
        mod = c_act @ ada_w[i] + ada_b[i]
        sh1, sc1, g1, sh2, sc2, g2 = (m[:, None, :] for m in jnp.split(mod, N_MOD, axis=-1))
        h = rms_norm(x, norm1_w[i]) * (1.0 + sc1) + sh1
        j = i // N_MIXERS
        if i % N_MIXERS == 0:
            y = fox_attention(h, fox_w_in[j], fox_b_f[j], fox_q_norm[j], fox_k_norm[j], fox_w_out[j])
        else:
            y = hgrn2_mixer(h, hgrn_w_in[j], lower[j], hgrn_g_norm[j], hgrn_w_out[j])
        x = x + g1 * y
        h = rms_norm(x, norm2_w[i]) * (1.0 + sc2) + sh2
        x = x + g2 * moe_ffn(h, router_w[i], router_b[i], exp_w_gu[i], exp_b_gu[i],
                             exp_w_down[i], exp_b_down[i])
    return x
```

```python
import functools

import jax
import jax.numpy as jnp
from jax import lax
from jax.experimental import pallas as pl
from jax.experimental.pallas import tpu as pltpu
from jax.experimental.pallas import tpu_sc as plsc

F32 = jnp.float32
BF16 = jnp.bfloat16
U32 = jnp.uint32
I32 = jnp.int32

EPS = 1e-6
MASK_VALUE = -1e30
TOP_K = 4
HGRN_DIM = 128
HGRN_CHUNK = 32
SWIGLU_LIMIT = 7.0
SWIGLU_ALPHA = 1.702
LANES = 128
SUBLANES = 8
SC_CORES = 2
SC_SUBCORES = 16
SC_ROWS = 64
VMEM_LIMIT = 56 * 1024 * 1024
HIGHEST = lax.Precision.HIGHEST


def _cparams(*sem):
    return pltpu.CompilerParams(dimension_semantics=sem, vmem_limit_bytes=VMEM_LIMIT)


def _resident(shape, index_map):
    return pl.BlockSpec(shape, index_map, pipeline_mode=pl.Buffered(1))


def _sigmoid(x):
    return 1.0 / (1.0 + jnp.exp(-x))


def _bf16_bits(x):
    return lax.bitcast_convert_type(x.astype(BF16).astype(F32), U32)


def _pack_halves(x):
    n = x.shape[1] // 2
    lo = _bf16_bits(x[:, :n])
    hi = _bf16_bits(x[:, n:])
    return (hi & jnp.uint32(0xFFFF0000)) | (lo >> 16)


def _unpack_halves(w):
    lo = lax.bitcast_convert_type(w << 16, F32)
    hi = lax.bitcast_convert_type(w & jnp.uint32(0xFFFF0000), F32)
    return lo, hi


def _mod_kernel(c_ref, w_ref, b_ref, o_ref):
    c = c_ref[...]
    ca = c * _sigmoid(c)
    o_ref[0] = jnp.dot(ca, w_ref[0], preferred_element_type=F32, precision=HIGHEST) + b_ref[0]


def _adaln_mod(c, ada_w, ada_b):
    depth, d, n6 = ada_w.shape
    b = c.shape[0]
    nmod = n6 // d
    return pl.pallas_call(
        _mod_kernel,
        grid=(depth, nmod),
        in_specs=[
            pl.BlockSpec((b, d), lambda i, j: (0, 0)),
            pl.BlockSpec((1, d, d), lambda i, j: (i, 0, j)),
            pl.BlockSpec((1, 1, d), lambda i, j: (i, 0, j)),
        ],
        out_specs=pl.BlockSpec((1, b, d), lambda i, j: (i, 0, j)),
        out_shape=jax.ShapeDtypeStruct((depth, b, n6), F32),
        compiler_params=_cparams("parallel", "parallel"),
        name="adaln_mod",
    )(c, ada_w, ada_b.reshape(depth, 1, n6))


def _norm_mod(x, nw, sc, sh):
    ms = jnp.mean(x * x, axis=-1, keepdims=True)
    return x * lax.rsqrt(ms + EPS) * nw * (1.0 + sc) + sh


def _in_kernel(x_ref, nw_ref, sh_ref, sc_ref, w_ref, *rest, n_chunk, with_f):
    if with_f:
        wf_ref, bf_ref, o_ref, of_ref = rest
    else:
        (o_ref,) = rest
    h = _norm_mod(x_ref[...], nw_ref[...], sc_ref[...], sh_ref[...]).astype(BF16)
    n = o_ref.shape[1]
    for c0 in range(0, n, n_chunk):
        o_ref[:, c0:c0 + n_chunk] = jnp.dot(
            h, w_ref[:, c0:c0 + n_chunk], preferred_element_type=F32).astype(o_ref.dtype)
    if with_f:
        of_ref[...] = jnp.dot(h, wf_ref[...], preferred_element_type=F32) + bf_ref[...]


def _in_proj(x, nw, mod, w, out_dtype, seq, tm, wf=None, bf=None):
    t, d = x.shape
    n = w.shape[1]
    per_seq = seq // tm
    with_f = wf is not None
    mod_spec = lambda j: pl.BlockSpec((None, 1, d), lambda i: (i // per_seq, 0, j))
    in_specs = [
        pl.BlockSpec((tm, d), lambda i: (i, 0)),
        _resident((1, d), lambda i: (0, 0)),
        mod_spec(0), mod_spec(1),
        _resident((d, n), lambda i: (0, 0)),
    ]
    out_specs = [pl.BlockSpec((tm, n), lambda i: (i, 0))]
    out_shape = [jax.ShapeDtypeStruct((t, n), out_dtype)]
    args = [x, nw, mod, mod, w]
    if with_f:
        in_specs += [_resident((d, LANES), lambda i: (0, 0)), _resident((1, LANES), lambda i: (0, 0))]
        out_specs.append(pl.BlockSpec((tm, LANES), lambda i: (i, 0)))
        out_shape.append(jax.ShapeDtypeStruct((t, LANES), F32))
        args += [wf, bf]
    return pl.pallas_call(
        functools.partial(_in_kernel, n_chunk=min(n, 1024), with_f=with_f),
        grid=(t // tm,),
        in_specs=in_specs,
        out_specs=out_specs,
        out_shape=out_shape,
        compiler_params=_cparams("parallel"),
        name="in_proj_fox" if with_f else "in_proj_hgrn",
    )(*args)


def _cum_kernel(f_ref, o_ref, *, rows):
    s = f_ref.shape[0]
    r = lax.broadcasted_iota(I32, (rows, rows), 0)
    c = lax.broadcasted_iota(I32, (rows, rows), 1)
    tri = (r >= c).astype(F32)
    carry = jnp.zeros((1, LANES), F32)
    for r0 in range(0, s, rows):
        x = f_ref[r0:r0 + rows, :]
        lf = jnp.minimum(x, 0.0) - jnp.log(1.0 + jnp.exp(-jnp.abs(x)))
        cs = jnp.dot(tri, lf, preferred_element_type=F32, precision=HIGHEST) + carry
        o_ref[r0:r0 + rows, :] = cs
        carry = cs[rows - 1:rows, :]


def _fox_cum(flogit, seq):
    t = flogit.shape[0]
    return pl.pallas_call(
        functools.partial(_cum_kernel, rows=min(seq, 128)),
        grid=(t // seq,),
        in_specs=[pl.BlockSpec((seq, LANES), lambda b: (b, 0))],
        out_specs=pl.BlockSpec((seq, LANES), lambda b: (b, 0)),
        out_shape=jax.ShapeDtypeStruct((t, LANES), F32),
        compiler_params=_cparams("parallel"),
        name="fox_cum",
    )(flogit)


def _attn_kernel(q_ref, k_ref, v_ref, g_ref, cum_ref, cumt_ref, qw_ref, kw_ref, o_ref,
                 q0_s, q1_s, kn_s, *, dh, blk):
    s_len = q_ref.shape[0]
    hp = pl.program_id(1)
    lane = lax.broadcasted_iota(I32, (1, LANES), 1)
    first = lane < dh
    scale = dh ** -0.5

    def headnorm(z, w):
        z2 = z * z
        s0 = jnp.sum(jnp.where(first, z2, 0.0), axis=-1, keepdims=True)
        s1 = jnp.sum(jnp.where(first, 0.0, z2), axis=-1, keepdims=True)
        ms = jnp.where(first, s0, s1) * (1.0 / dh)
        return z * lax.rsqrt(ms + EPS) * w

    qn = headnorm(q_ref[...].astype(F32), qw_ref[...]) * scale
    q0_s[...] = jnp.where(first, qn, 0.0).astype(BF16)
    q1_s[...] = jnp.where(first, 0.0, qn).astype(BF16)
    kn_s[...] = headnorm(k_ref[...].astype(F32), kw_ref[...]).astype(BF16)

    rr = lax.broadcasted_iota(I32, (blk, blk), 0)
    cc = lax.broadcasted_iota(I32, (blk, blk), 1)
    causal = rr >= cc

    for qi in range(s_len // blk):
        q_lo = qi * blk
        cum_tile = cum_ref[q_lo:q_lo + blk, :]
        outs = []
        for h, q_s in enumerate((q0_s, q1_s)):
            hh = 2 * hp + h
            cq = jnp.sum(jnp.where(lane == hh, cum_tile, 0.0), axis=-1, keepdims=True)
            qh = q_s[q_lo:q_lo + blk, :]

            def step(off, carry, masked, qh=qh, cq=cq, hh=hh):
                m, l, acc = carry
                kb = kn_s[pl.ds(off, blk), :]
                vb = v_ref[pl.ds(off, blk), :]
                ck = cumt_ref[0, pl.ds(hh, 1), pl.ds(off, blk)]
                s = lax.dot_general(qh, kb, (((1,), (1,)), ((), ())),
                                    preferred_element_type=F32) + (cq - ck)
                if masked:
                    s = jnp.where(causal, s, MASK_VALUE)
                m_new = jnp.maximum(m, jnp.max(s, axis=-1, keepdims=True))
                p = jnp.exp(s - m_new)
                alpha = jnp.exp(m - m_new)
                l = alpha * l + jnp.sum(p, axis=-1, keepdims=True)
                acc = alpha * acc + jnp.dot(p.astype(BF16), vb, preferred_element_type=F32)
                return m_new, l, acc

            carry = (jnp.full((blk, 1), MASK_VALUE, F32), jnp.zeros((blk, 1), F32),
                     jnp.zeros((blk, LANES), F32))
            if qi > 0:
                carry = lax.fori_loop(
                    0, qi, lambda j, c, step=step: step(pl.multiple_of(j * blk, blk), c, False), carry)
            m, l, acc = step(q_lo, carry, True)
            outs.append(acc / l)
        o = jnp.where(first, outs[0], outs[1])
        g = g_ref[q_lo:q_lo + blk, :].astype(F32)
        o_ref[q_lo:q_lo + blk, :] = (o * _sigmoid(g)).astype(o_ref.dtype)


def _fox_attention(qkvg, cum, cumt, qw, kw, seq, d, dh):
    t = qkvg.shape[0]
    h = d // dh
    nb = d // LANES
    blk = min(seq, 256)
    col = lambda off: pl.BlockSpec((seq, LANES), lambda b, p: (b, off + p))
    return pl.pallas_call(
        functools.partial(_attn_kernel, dh=dh, blk=blk),
        grid=(t // seq, nb),
        in_specs=[
            col(0), col(nb), col(2 * nb), col(3 * nb),
            pl.BlockSpec((seq, LANES), lambda b, p: (b, 0)),
            pl.BlockSpec((1, h, seq), lambda b, p: (b, 0, 0)),
            pl.BlockSpec((1, LANES), lambda b, p: (0, 0)),
            pl.BlockSpec((1, LANES), lambda b, p: (0, 0)),
        ],
        out_specs=pl.BlockSpec((seq, LANES), lambda b, p: (b, p)),
        out_shape=jax.ShapeDtypeStruct((t, d), BF16),
        scratch_shapes=[pltpu.VMEM((seq, LANES), BF16)] * 3,
        compiler_params=_cparams("parallel", "parallel"),
        name="fox_attention",
    )(qkvg, qkvg, qkvg, qkvg, cum, cumt, qw, kw)


def _hgrn_kernel(q_ref, f_ref, v_ref, g_ref, lb_ref, gw_ref, o_ref, *, hpb, chunk):
    s_len = q_ref.shape[0]
    n_chunks = s_len // chunk
    groups = chunk // SUBLANES
    r = lax.broadcasted_iota(I32, (chunk, chunk), 0)
    c = lax.broadcasted_iota(I32, (chunk, chunk), 1)
    tri = (r >= c).astype(F32)
    sub = lax.broadcasted_iota(I32, (SUBLANES, LANES), 0)
    gw = gw_ref[...]

    def one_head(hd, rows, state_t):
        cols = slice(hd * LANES, (hd + 1) * LANES)
        lb = lb_ref[:, cols]
        q_raw = q_ref[rows, cols]
        fr = f_ref[rows, cols]
        v = v_ref[rows, cols]
        g = g_ref[rows, cols]
        q = q_raw * _sigmoid(q_raw)
        f = lb + (1.0 - lb) * _sigmoid(fr)
        lf = jnp.log(f)
        k = (1.0 - lb) * _sigmoid(-fr)
        b = jnp.dot(tri, lf, preferred_element_type=F32, precision=HIGHEST)
        b_end = b[chunk - 1:chunk, :]
        qe = (q * jnp.exp(b)).astype(BF16)
        o = lax.dot_general(qe, state_t.astype(BF16), (((1,), (1,)), ((), ())),
                            preferred_element_type=F32)
        o_g = [o[i * SUBLANES:(i + 1) * SUBLANES, :] for i in range(groups)]
        for s in range(chunk):
            g0 = s // SUBLANES
            bs = b[s:s + 1, :]
            ks = k[s:s + 1, :]
            vs = v[s:s + 1, :]
            for gi in range(g0, groups):
                sl = slice(gi * SUBLANES, (gi + 1) * SUBLANES)
                diff = b[sl, :] - bs
                if gi == g0:
                    live = sub >= (s - g0 * SUBLANES)
                    p = jnp.where(live, q[sl, :] * jnp.exp(jnp.where(live, diff, 0.0)) * ks, 0.0)
                else:
                    p = q[sl, :] * jnp.exp(diff) * ks
                a = jnp.sum(p, axis=-1, keepdims=True)
                o_g[gi] = o_g[gi] + a * vs
        o = jnp.concatenate(o_g, axis=0)
        ke = (k * jnp.exp(b_end - b)).astype(BF16)
        upd = jnp.dot(jnp.transpose(v).astype(BF16), ke, preferred_element_type=F32)
        state_t = jnp.exp(b_end) * state_t + upd
        ms = jnp.mean(o * o, axis=-1, keepdims=True)
        y = o * lax.rsqrt(ms + EPS) * gw * (g * _sigmoid(g))
        o_ref[rows, cols] = y.astype(o_ref.dtype)
        return state_t

    def body(ci, states):
        rows = pl.ds(pl.multiple_of(ci * chunk, chunk), chunk)
        return tuple(one_head(hd, rows, states[hd]) for hd in range(hpb))

    init = tuple(jnp.zeros((LANES, LANES), F32) for _ in range(hpb))
    lax.fori_loop(0, n_chunks, body, init)


def _hgrn_mixer(proj, lower, gw, seq, d, hpb):
    t = proj.shape[0]
    w = hpb * LANES
    nb = d // w
    col = lambda off: pl.BlockSpec((seq, w), lambda b, p: (b, off + p))
    return pl.pallas_call(
        functools.partial(_hgrn_kernel, hpb=hpb, chunk=HGRN_CHUNK),
        grid=(t // seq, nb),
        in_specs=[
            col(0), col(nb), col(2 * nb), col(3 * nb),
            pl.BlockSpec((1, w), lambda b, p: (0, p)),
            pl.BlockSpec((1, LANES), lambda b, p: (0, 0)),
        ],
        out_specs=pl.BlockSpec((seq, w), lambda b, p: (b, p)),
        out_shape=jax.ShapeDtypeStruct((t, d), BF16),
        compiler_params=_cparams("parallel", "parallel"),
        name="hgrn_mixer",
    )(proj, proj, proj, proj, lower, gw)


def _post_kernel(o_ref, x_ref, wo_ref, g1_ref, nw_ref, sh_ref, sc_ref, wr_ref, br_ref,
                 x1_ref, hp_ref, ri_ref, rw_ref, cnt_ref, run_s):
    i = pl.program_id(0)
    tm = x_ref.shape[0]

    @pl.when(i == 0)
    def _():
        run_s[...] = jnp.zeros_like(run_s)

    y = jnp.dot(o_ref[...], wo_ref[...], preferred_element_type=F32)
    x1 = x_ref[...] + g1_ref[...] * y
    x1_ref[...] = x1
    h2 = _norm_mod(x1, nw_ref[...], sc_ref[...], sh_ref[...])
    hp_ref[...] = _pack_halves(h2)
    logits = jnp.dot(h2.astype(BF16), wr_ref[...], preferred_element_type=F32) + br_ref[...]

    lane = lax.broadcasted_iota(I32, (tm, LANES), 1).astype(F32)
    work = logits
    idx, val = [], []
    for _ in range(TOP_K):
        mx = jnp.max(work, axis=-1, keepdims=True)
        ix = jnp.min(jnp.where(work == mx, lane, float(LANES)), axis=-1, keepdims=True)
        idx.append(ix)
        val.append(mx)
        work = jnp.where(lane == ix, -jnp.inf, work)
    ex = [jnp.exp(v - val[0]) for v in val]
    den = ex[0] + ex[1] + ex[2] + ex[3]
    wts = [e / den for e in ex]

    hot = jnp.zeros((tm, LANES), F32)
    for ix in idx:
        hot = hot + jnp.where(lane == ix, 1.0, 0.0)
    r = lax.broadcasted_iota(I32, (tm, tm), 0)
    c = lax.broadcasted_iota(I32, (tm, tm), 1)
    strict = jnp.where(r > c, 1.0, 0.0).astype(BF16)
    before = jnp.dot(strict, hot.astype(BF16), preferred_element_type=F32) + run_s[...]
    ranks = [jnp.sum(jnp.where(lane == ix, before, 0.0), axis=-1, keepdims=True) for ix in idx]
    run_s[...] = run_s[...] + jnp.sum(hot, axis=0, keepdims=True)

    ri = jnp.zeros((tm, LANES), F32)
    rw = jnp.zeros((tm, LANES), F32)
    for k in range(TOP_K):
        ri = jnp.where(lane == float(k), idx[k], ri)
        ri = jnp.where(lane == float(TOP_K + k), ranks[k], ri)
        rw = jnp.where(lane == float(k), wts[k], rw)
    ri_ref[...] = ri.astype(I32)
    rw_ref[...] = rw
    cnt_ref[...] = run_s[...].astype(I32)


def _post_mixer(o, x, wo, mod, nw2, wr, br, seq, tm):
    t, d = x.shape
    per_seq = seq // tm
    mod_spec = lambda j: pl.BlockSpec((None, 1, d), lambda i: (i // per_seq, 0, j))
    row = lambda w: pl.BlockSpec((tm, w), lambda i: (i, 0))
    return pl.pallas_call(
        _post_kernel,
        grid=(t // tm,),
        in_specs=[
            row(d), row(d),
            _resident((d, d), lambda i: (0, 0)),
            mod_spec(2),
            _resident((1, d), lambda i: (0, 0)),
            mod_spec(3), mod_spec(4),
            _resident((d, LANES), lambda i: (0, 0)),
            _resident((1, LANES), lambda i: (0, 0)),
        ],
        out_specs=[row(d), row(d // 2), row(LANES), row(LANES),
                   pl.BlockSpec((1, LANES), lambda i: (0, 0))],
        out_shape=[
            jax.ShapeDtypeStruct((t, d), F32),
            jax.ShapeDtypeStruct((t, d // 2), U32),
            jax.ShapeDtypeStruct((t, LANES), I32),
            jax.ShapeDtypeStruct((t, LANES), F32),
            jax.ShapeDtypeStruct((1, LANES), I32),
        ],
        scratch_shapes=[pltpu.VMEM((1, LANES), F32)],
        compiler_params=_cparams("arbitrary"),
        name="post_mixer",
    )(o, x, wo, mod, nw2, mod, mod, wr, br)


def _sc_mesh():
    return plsc.VectorSubcoreMesh(core_axis_name="c", subcore_axis_name="s")


def _sc_scatter_rows(x, posf, n_rows):
    t, w = x.shape
    workers = SC_CORES * SC_SUBCORES
    per_w = t // workers

    @functools.partial(
        pl.kernel, mesh=_sc_mesh(),
        out_type=jax.ShapeDtypeStruct((n_rows, w), x.dtype),
        scratch_types=[pltpu.VMEM((SC_ROWS,), I32)] * TOP_K
        + [pltpu.VMEM((SC_ROWS, w), x.dtype), pltpu.SemaphoreType.DMA])
    def scatter_kernel(x_hbm, pos_hbm, out_hbm, i0, i1, i2, i3, rows_v, sem):
        idxs = (i0, i1, i2, i3)
        wid = lax.axis_index("s") * SC_CORES + lax.axis_index("c")
        base = wid * per_w

        @pl.loop(0, per_w // SC_ROWS)
        def _(i):
            off = base + i * SC_ROWS
            for k in range(TOP_K):
                pltpu.sync_copy(pos_hbm.at[pl.ds(k * t + off, SC_ROWS)], idxs[k])
            pltpu.sync_copy(x_hbm.at[pl.ds(off, SC_ROWS)], rows_v)
            copies = [pltpu.async_copy(rows_v, out_hbm.at[idxs[k]], sem) for k in range(TOP_K)]
            for cp in copies:
                cp.wait()

    return scatter_kernel(x, posf)


def _sc_gather_rows(src, posf):
    n = posf.shape[0]
    w = src.shape[1]
    workers = SC_CORES * SC_SUBCORES
    per_w = n // workers

    @functools.partial(
        pl.kernel, mesh=_sc_mesh(),
        out_type=jax.ShapeDtypeStruct((n, w), src.dtype),
        scratch_types=[pltpu.VMEM((SC_ROWS,), I32), pltpu.VMEM((SC_ROWS, w), src.dtype),
                       pltpu.SemaphoreType.DMA])
    def gather_kernel(src_hbm, pos_hbm, out_hbm, idx_v, rows_v, sem):
        wid = lax.axis_index("s") * SC_CORES + lax.axis_index("c")
        base = wid * per_w

        @pl.loop(0, per_w // SC_ROWS)
        def _(i):
            off = base + i * SC_ROWS
            pltpu.sync_copy(pos_hbm.at[pl.ds(off, SC_ROWS)], idx_v)
            pltpu.async_copy(src_hbm.at[idx_v], rows_v, sem).wait()
            pltpu.sync_copy(rows_v, out_hbm.at[pl.ds(off, SC_ROWS)])

    return gather_kernel(src, posf)


def _expert_kernel(be_ref, nu_ref, x_ref, wg_ref, wu_ref, bg_ref, bu_ref, wd_ref, bd_ref, y_ref):
    i = pl.program_id(0)

    @pl.when(i < nu_ref[0])
    def _():
        half = x_ref.shape[1]
        lo, hi = _unpack_halves(x_ref[...])
        lo = lo.astype(BF16)
        hi = hi.astype(BF16)

        def proj(w_ref, b_ref):
            return (jnp.dot(lo, w_ref[0, :half, :], preferred_element_type=F32)
                    + jnp.dot(hi, w_ref[0, half:, :], preferred_element_type=F32) + b_ref[0])

        gate = jnp.minimum(proj(wg_ref, bg_ref), SWIGLU_LIMIT)
        up = jnp.clip(proj(wu_ref, bu_ref), -SWIGLU_LIMIT, SWIGLU_LIMIT)
        h = (up + 1.0) * gate * _sigmoid(SWIGLU_ALPHA * gate)
        y = jnp.dot(h.astype(BF16), wd_ref[0], preferred_element_type=F32) + bd_ref[0]
        y_ref[...] = _pack_halves(y)

    @pl.when(i >= nu_ref[0])
    def _():
        y_ref[...] = jnp.zeros_like(y_ref)


def _expert_ffn(xs, blk_e, n_used, wg, wu, bg, bu, wd, bd, rb):
    rows, half = xs.shape
    e, d, f = wg.shape
    n_blocks = rows // rb
    wspec = lambda shape: pl.BlockSpec(shape, lambda i, be, nu: (be[i], 0, 0))
    return pl.pallas_call(
        _expert_kernel,
        grid_spec=pltpu.PrefetchScalarGridSpec(
            num_scalar_prefetch=2,
            grid=(n_blocks,),
            in_specs=[
                pl.BlockSpec((rb, half), lambda i, be, nu: (i, 0)),
                wspec((1, d, f)), wspec((1, d, f)),
                wspec((1, 1, f)), wspec((1, 1, f)),
                wspec((1, f, d)), wspec((1, 1, d)),
            ],
            out_specs=pl.BlockSpec((rb, d // 2), lambda i, be, nu: (i, 0)),
        ),
        out_shape=jax.ShapeDtypeStruct((rows, d // 2), U32),
        compiler_params=_cparams("arbitrary"),
        name="expert_ffn",
    )(blk_e, n_used, xs, wg, wu, bg, bu, wd, bd)


def _combine_kernel(x_ref, y0_ref, y1_ref, y2_ref, y3_ref, rw_ref, g2_ref, o_ref):
    half = y0_ref.shape[1]
    rw = rw_ref[...]
    lane = lax.broadcasted_iota(I32, rw.shape, 1)
    acc_lo = jnp.zeros(y0_ref.shape, F32)
    acc_hi = jnp.zeros(y0_ref.shape, F32)
    for k, y_ref in enumerate((y0_ref, y1_ref, y2_ref, y3_ref)):
        wk = jnp.sum(jnp.where(lane == k, rw, 0.0), axis=-1, keepdims=True)
        lo, hi = _unpack_halves(y_ref[...])
        acc_lo = acc_lo + wk * lo
        acc_hi = acc_hi + wk * hi
    g2 = g2_ref[...]
    o_ref[:, :half] = x_ref[:, :half] + g2[:, :half] * acc_lo
    o_ref[:, half:] = x_ref[:, half:] + g2[:, half:] * acc_hi


def _combine(x1, yt, rw, mod, seq, tm):
    t, d = x1.shape
    per_seq = seq // tm
    nt = t // tm
    yspec = lambda k: pl.BlockSpec((tm, d // 2), lambda i: (k * nt + i, 0))
    return pl.pallas_call(
        _combine_kernel,
        grid=(nt,),
        in_specs=[
            pl.BlockSpec((tm, d), lambda i: (i, 0)),
            yspec(0), yspec(1), yspec(2), yspec(3),
            pl.BlockSpec((tm, LANES), lambda i: (i, 0)),
            pl.BlockSpec((None, 1, d), lambda i: (i // per_seq, 0, 5)),
        ],
        out_specs=pl.BlockSpec((tm, d), lambda i: (i, 0)),
        out_shape=jax.ShapeDtypeStruct((t, d), F32),
        compiler_params=_cparams("parallel"),
        name="moe_combine",
    )(x1, yt, yt, yt, yt, rw, mod)


def _row_tile(seq):
    return min(seq, 512)


def _moe_block_rows(t):
    return min(512, t * TOP_K // 8)


def kernel(x, c, fox_w_in, fox_b_f, fox_q_norm, fox_k_norm, fox_w_out, hgrn_w_in, hgrn_lb, hgrn_g_norm, hgrn_w_out, ada_w, ada_b, norm1_w, norm2_w, router_w, router_b, exp_w_gu, exp_b_gu, exp_w_down, exp_b_down):
    bsz, seq, d = x.shape
    t = bsz * seq
    depth = ada_w.shape[0]
    n_exp = router_w.shape[-1]
    f_dim = exp_w_down.shape[2]
    h_fox = fox_w_in.shape[-1] - 4 * d
    dh = d // h_fox
    tm = _row_tile(seq)
    rb = _moe_block_rows(t)
    n_blocks = (t * TOP_K) // rb + n_exp
    hgrn_hpb = 2 if d % (2 * LANES) == 0 else 1

    mod_all = _adaln_mod(c, ada_w, ada_b).reshape(depth, bsz, 1, -1)
    lb_soft = jax.nn.softmax(hgrn_lb.astype(F32), axis=0)
    lower = jnp.cumsum(lb_soft, axis=0) - lb_soft[0]

    xf = x.reshape(t, d)
    for i in range(depth):
        mod = mod_all[i]
        j = i // 2
        nw1 = norm1_w[i].reshape(1, d)
        if i % 2 == 0:
            w_in = fox_w_in[j]
            w_main = w_in[:, :4 * d].astype(BF16)
            w_f = jnp.pad(w_in[:, 4 * d:], ((0, 0), (0, LANES - h_fox))).astype(BF16)
            b_f = jnp.pad(fox_b_f[j], (0, LANES - h_fox)).reshape(1, LANES)
            qkvg, flogit = _in_proj(xf, nw1, mod, w_main, BF16, seq, tm, w_f, b_f)
            cum = _fox_cum(flogit, seq)
            cumt = jnp.transpose(cum.reshape(bsz, seq, LANES)[:, :, :h_fox], (0, 2, 1))
            qw = jnp.tile(fox_q_norm[j], LANES // dh).reshape(1, LANES)
            kw = jnp.tile(fox_k_norm[j], LANES // dh).reshape(1, LANES)
            o = _fox_attention(qkvg, cum, cumt, qw, kw, seq, d, dh)
            w_out = fox_w_out[j].astype(BF16)
        else:
            (proj,) = _in_proj(xf, nw1, mod, hgrn_w_in[j].astype(BF16), F32, seq, tm)
            o = _hgrn_mixer(proj, lower[j].reshape(1, d), hgrn_g_norm[j].reshape(1, HGRN_DIM),
                            seq, d, hgrn_hpb)
            w_out = hgrn_w_out[j].astype(BF16)

        wr = jnp.pad(router_w[i], ((0, 0), (0, LANES - n_exp))).astype(BF16)
        br = jnp.pad(router_b[i], (0, LANES - n_exp), constant_values=MASK_VALUE).reshape(1, LANES)
        x1, hp, ri, rw, cnt = _post_mixer(o, xf, w_out, mod, norm2_w[i].reshape(1, d), wr, br, seq, tm)

        counts = cnt[0, :n_exp]
        padded = (counts + rb - 1) // rb * rb
        pad_end = jnp.cumsum(padded)
        pad_start = pad_end - padded
        top_idx = ri[:, :TOP_K]
        pos = pad_start[top_idx] + ri[:, TOP_K:2 * TOP_K]
        posf = jnp.transpose(pos).reshape(-1).astype(I32)
        blk_e = jnp.minimum(
            jnp.searchsorted(pad_end, jnp.arange(n_blocks, dtype=I32) * rb, side="right"),
            n_exp - 1).astype(I32)
        n_used = (pad_end[-1] // rb).astype(I32).reshape(1)

        xs = _sc_scatter_rows(hp, posf, n_blocks * rb)

        w_gu = exp_w_gu[i].reshape(n_exp, d, f_dim, 2)
        wg = w_gu[..., 0].astype(BF16)
        wu = w_gu[..., 1].astype(BF16)
        b_gu = exp_b_gu[i].reshape(n_exp, 1, f_dim, 2)
        ys = _expert_ffn(xs, blk_e, n_used, wg, wu, b_gu[..., 0], b_gu[..., 1],
                         exp_w_down[i].astype(BF16), exp_b_down[i].reshape(n_exp, 1, d), rb)

        yt = _sc_gather_rows(ys, posf)
        xf = _combine(x1, yt, rw, mod, seq, tm)
    return xf.reshape(bsz, seq, d)
```

```python
import functools

import jax
import jax.numpy as jnp
from jax import lax
from jax.experimental import pallas as pl
from jax.experimental.pallas import tpu as pltpu
from jax.experimental.pallas import tpu_sc as plsc

F32 = jnp.float32
BF16 = jnp.bfloat16
U32 = jnp.uint32
I32 = jnp.int32

EPS = 1e-6
MASK_VALUE = -1e30
TOP_K = 4
HGRN_DIM = 128
HGRN_CHUNK = 32
SWIGLU_LIMIT = 7.0
SWIGLU_ALPHA = 1.702
LANES = 128
SUBLANES = 8
SC_CORES = 2
SC_SUBCORES = 16
SC_ROWS = 64
VMEM_LIMIT = 56 * 1024 * 1024
HIGHEST = lax.Precision.HIGHEST


def _cparams(*sem):
    return pltpu.CompilerParams(dimension_semantics=sem, vmem_limit_bytes=VMEM_LIMIT)


def _resident(shape, index_map):
    return pl.BlockSpec(shape, index_map, pipeline_mode=pl.Buffered(1))


def _sigmoid(x):
    return 1.0 / (1.0 + jnp.exp(-x))


def _bf16_bits(x):
    return lax.bitcast_convert_type(x.astype(BF16).astype(F32), U32)


def _pack_halves(x):
    n = x.shape[1] // 2
    lo = _bf16_bits(x[:, :n])
    hi = _bf16_bits(x[:, n:])
    return (hi & jnp.uint32(0xFFFF0000)) | (lo >> 16)


def _unpack_halves(w):
    lo = lax.bitcast_convert_type(w << 16, F32)
    hi = lax.bitcast_convert_type(w & jnp.uint32(0xFFFF0000), F32)
    return lo, hi


def _mod_kernel(c_ref, w_ref, b_ref, o_ref):
    c = c_ref[...]
    ca = c * _sigmoid(c)
    o_ref[0] = jnp.dot(ca, w_ref[0], preferred_element_type=F32, precision=HIGHEST) + b_ref[0]


def _adaln_mod(c, ada_w, ada_b):
    depth, d, n6 = ada_w.shape
    b = c.shape[0]
    nmod = n6 // d
    return pl.pallas_call(
        _mod_kernel,
        grid=(depth, nmod),
        in_specs=[
            pl.BlockSpec((b, d), lambda i, j: (0, 0)),
            pl.BlockSpec((1, d, d), lambda i, j: (i, 0, j)),
            pl.BlockSpec((1, 1, d), lambda i, j: (i, 0, j)),
        ],
        out_specs=pl.BlockSpec((1, b, d), lambda i, j: (i, 0, j)),
        out_shape=jax.ShapeDtypeStruct((depth, b, n6), F32),
        compiler_params=_cparams("parallel", "parallel"),
        name="adaln_mod",
    )(c, ada_w, ada_b.reshape(depth, 1, n6))


def _norm_mod(x, nw, sc, sh):
    ms = jnp.mean(x * x, axis=-1, keepdims=True)
    return x * lax.rsqrt(ms + EPS) * nw * (1.0 + sc) + sh


def _in_kernel(x_ref, nw_ref, sh_ref, sc_ref, w_ref, *rest, n_chunk, with_f):
    if with_f:
        wf_ref, bf_ref, o_ref, of_ref = rest
    else:
        (o_ref,) = rest
    h = _norm_mod(x_ref[...], nw_ref[...], sc_ref[...], sh_ref[...]).astype(BF16)
    n = o_ref.shape[1]
    for c0 in range(0, n, n_chunk):
        o_ref[:, c0:c0 + n_chunk] = jnp.dot(
            h, w_ref[:, c0:c0 + n_chunk], preferred_element_type=F32).astype(o_ref.dtype)
    if with_f:
        of_ref[...] = jnp.dot(h, wf_ref[...], preferred_element_type=F32) + bf_ref[...]


def _in_proj(x, nw, mod, w, out_dtype, seq, tm, wf=None, bf=None):
    t, d = x.shape
    n = w.shape[1]
    per_seq = seq // tm
    with_f = wf is not None
    mod_spec = lambda j: pl.BlockSpec((None, 1, d), lambda i: (i // per_seq, 0, j))
    in_specs = [
        pl.BlockSpec((tm, d), lambda i: (i, 0)),
        _resident((1, d), lambda i: (0, 0)),
        mod_spec(0), mod_spec(1),
        _resident((d, n), lambda i: (0, 0)),
    ]
    out_specs = [pl.BlockSpec((tm, n), lambda i: (i, 0))]
    out_shape = [jax.ShapeDtypeStruct((t, n), out_dtype)]
    args = [x, nw, mod, mod, w]
    if with_f:
        in_specs += [_resident((d, LANES), lambda i: (0, 0)), _resident((1, LANES), lambda i: (0, 0))]
        out_specs.append(pl.BlockSpec((tm, LANES), lambda i: (i, 0)))
        out_shape.append(jax.ShapeDtypeStruct((t, LANES), F32))
        args += [wf, bf]
    return pl.pallas_call(
        functools.partial(_in_kernel, n_chunk=min(n, 1024), with_f=with_f),
        grid=(t // tm,),
        in_specs=in_specs,
        out_specs=out_specs,
        out_shape=out_shape,
        compiler_params=_cparams("parallel"),
        name="in_proj_fox" if with_f else "in_proj_hgrn",
    )(*args)


def _cum_kernel(f_ref, o_ref, *, rows):
    s = f_ref.shape[0]
    r = lax.broadcasted_iota(I32, (rows, rows), 0)
    c = lax.broadcasted_iota(I32, (rows, rows), 1)
    tri = (r >= c).astype(F32)
    carry = jnp.zeros((1, LANES), F32)
    for r0 in range(0, s, rows):
        x = f_ref[r0:r0 + rows, :]
        lf = jnp.minimum(x, 0.0) - jnp.log(1.0 + jnp.exp(-jnp.abs(x)))
        cs = jnp.dot(tri, lf, preferred_element_type=F32, precision=HIGHEST) + carry
        o_ref[r0:r0 + rows, :] = cs
        carry = cs[rows - 1:rows, :]


def _fox_cum(flogit, seq):
    t = flogit.shape[0]
    return pl.pallas_call(
        functools.partial(_cum_kernel, rows=min(seq, 128)),
        grid=(t // seq,),
        in_specs=[pl.BlockSpec((seq, LANES), lambda b: (b, 0))],
        out_specs=pl.BlockSpec((seq, LANES), lambda b: (b, 0)),
        out_shape=jax.ShapeDtypeStruct((t, LANES), F32),
        compiler_params=_cparams("parallel"),
        name="fox_cum",
    )(flogit)


LOG2E = 1.4426950408889634


def _split3(c):
    c1 = c.astype(BF16).astype(F32)
    r = c - c1
    c2 = r.astype(BF16).astype(F32)
    return c1, c2, r - c2


def _attn_kernel(q_ref, k_ref, v_ref, g_ref, cum_ref, qw_ref, kw_ref, o_ref, qs_s, ka_s, *, dh, tq):
    s_len = q_ref.shape[0]
    nq = s_len // tq
    hp = pl.program_id(1)
    lane = lax.broadcasted_iota(I32, (1, LANES), 1)
    first = lane < dh

    def headnorm(z, w):
        z2 = z * z
        s0 = jnp.sum(jnp.where(first, z2, 0.0), axis=-1, keepdims=True)
        s1 = jnp.sum(jnp.where(first, 0.0, z2), axis=-1, keepdims=True)
        ms = jnp.where(first, s0, s1) * (1.0 / dh)
        return z * lax.rsqrt(ms + EPS) * w

    qn = headnorm(q_ref[...].astype(F32), qw_ref[...]) * (dh ** -0.5 * LOG2E)
    kn = headnorm(k_ref[...].astype(F32), kw_ref[...])
    cum = cum_ref[...] * LOG2E
    c_h = [_split3(jnp.sum(jnp.where(lane == 2 * hp + h, cum, 0.0), axis=-1, keepdims=True))
           for h in range(2)]

    def bias_lanes(entries):
        out = jnp.zeros((s_len, LANES), F32)
        for i, val in entries:
            out = jnp.where(lane == i, val, out)
        return out

    k_hi = bias_lanes([(0, 1.0), (1, 1.0), (2, 1.0)]
                      + [(3 + i, -c_h[0][i]) for i in range(3)]
                      + [(6 + i, -c_h[1][i]) for i in range(3)])
    q_hi = [bias_lanes([(i, c_h[0][i]) for i in range(3)] + [(3, 1.0), (4, 1.0), (5, 1.0)]),
            bias_lanes([(i, c_h[1][i]) for i in range(3)] + [(6, 1.0), (7, 1.0), (8, 1.0)])]
    ka_s[:, :LANES] = kn.astype(BF16)
    ka_s[:, LANES:] = k_hi.astype(BF16)
    q_lo = [jnp.where(first, qn, 0.0), jnp.where(first, 0.0, qn)]
    for qi in range(nq):
        for h in range(2):
            qs_s[qi, h * tq:(h + 1) * tq, :LANES] = q_lo[h][qi * tq:(qi + 1) * tq, :].astype(BF16)
            qs_s[qi, h * tq:(h + 1) * tq, LANES:] = q_hi[h][qi * tq:(qi + 1) * tq, :].astype(BF16)

    def scores(q, off):
        return lax.dot_general(q, ka_s[pl.ds(off, tq), :], (((1,), (1,)), ((), ())),
                               preferred_element_type=F32)

    def update(s, off, carry, masked):
        m, l, acc = carry
        if masked:
            rr = lax.broadcasted_iota(I32, (2 * tq, tq), 0)
            rr = jnp.where(rr >= tq, rr - tq, rr)
            cc = lax.broadcasted_iota(I32, (2 * tq, tq), 1)
            s = jnp.where(rr >= cc, s, MASK_VALUE)
        m_new = jnp.maximum(m, jnp.max(s, axis=-1, keepdims=True))
        p = jnp.exp2(s - m_new)
        alpha = jnp.exp2(m - m_new)
        l = alpha * l + jnp.sum(p, axis=-1, keepdims=True)
        acc = alpha * acc + jnp.dot(p.astype(BF16), v_ref[pl.ds(off, tq), :],
                                    preferred_element_type=F32)
        return m_new, l, acc

    for qi in range(nq):
        q = qs_s[qi]
        s = scores(q, 0)
        carry = (jnp.full((2 * tq, 1), MASK_VALUE, F32), jnp.zeros((2 * tq, 1), F32),
                 jnp.zeros((2 * tq, LANES), F32))
        if qi > 0:
            def body(j, c, q=q):
                off = pl.multiple_of(j * tq, tq)
                s_next = scores(q, off + tq)
                return (s_next,) + update(c[0], off, c[1:], False)

            s, *carry = lax.fori_loop(0, qi, body, (s,) + carry)
        m, l, acc = update(s, qi * tq, carry, True)
        o = jnp.where(first, acc[:tq] / l[:tq], acc[tq:] / l[tq:])
        g = g_ref[qi * tq:(qi + 1) * tq, :].astype(F32)
        o_ref[qi * tq:(qi + 1) * tq, :] = (o * _sigmoid(g)).astype(o_ref.dtype)


def _fox_attention(qkvg, cum, qw, kw, seq, d, dh):
    t = qkvg.shape[0]
    nb = d // LANES
    tq = min(seq, 512)
    col = lambda off: pl.BlockSpec((seq, LANES), lambda b, p: (b, off + p))
    return pl.pallas_call(
        functools.partial(_attn_kernel, dh=dh, tq=tq),
        grid=(t // seq, nb),
        in_specs=[
            col(0), col(nb), col(2 * nb), col(3 * nb),
            pl.BlockSpec((seq, LANES), lambda b, p: (b, 0)),
            pl.BlockSpec((1, LANES), lambda b, p: (0, 0)),
            pl.BlockSpec((1, LANES), lambda b, p: (0, 0)),
        ],
        out_specs=pl.BlockSpec((seq, LANES), lambda b, p: (b, p)),
        out_shape=jax.ShapeDtypeStruct((t, d), BF16),
        scratch_shapes=[pltpu.VMEM((seq // tq, 2 * tq, 2 * LANES), BF16),
                        pltpu.VMEM((seq, 2 * LANES), BF16)],
        compiler_params=_cparams("parallel", "parallel"),
        name="fox_attention",
    )(qkvg, qkvg, qkvg, qkvg, cum, qw, kw)


def _hgrn_kernel(q_ref, f_ref, v_ref, g_ref, lb_ref, gw_ref, o_ref, *, hpb, chunk):
    s_len = q_ref.shape[0]
    pair = 2
    groups = chunk // SUBLANES
    off_w = (groups - 1) * chunk
    r = lax.broadcasted_iota(I32, (chunk, chunk), 0)
    c = lax.broadcasted_iota(I32, (chunk, chunk), 1)
    tri = jnp.where(r >= c, 1.0, 0.0).astype(BF16)
    ro = lax.broadcasted_iota(I32, (chunk, off_w), 0) // SUBLANES
    co = lax.broadcasted_iota(I32, (chunk, off_w), 1) // chunk
    keep = ro == co + 1
    sub = lax.broadcasted_iota(I32, (SUBLANES, LANES), 0)
    gw = gw_ref[...]
    nt = (((1,), (1,)), ((), ()))
    grp = lambda a, i: a[i * SUBLANES:(i + 1) * SUBLANES, :]
    zeros8 = jnp.zeros((SUBLANES, LANES), F32)

    def body(step, states):
        insts = [(ch, hd) for ch in range(pair) for hd in range(hpb)]
        rows = {ch: pl.ds(pl.multiple_of((step * pair + ch) * chunk, chunk), chunk)
                for ch in range(pair)}
        cols = {hd: slice(hd * LANES, (hd + 1) * LANES) for hd in range(hpb)}
        st = {}
        for ch, hd in insts:
            lb = lb_ref[:, cols[hd]]
            q_raw = q_ref[rows[ch], cols[hd]]
            sig = _sigmoid(f_ref[rows[ch], cols[hd]])
            d = dict(v=v_ref[rows[ch], cols[hd]], q=q_raw * _sigmoid(q_raw),
                     k=(1.0 - lb) * (1.0 - sig))
            lf = jnp.log(lb + (1.0 - lb) * sig)
            d["b"] = sum(jnp.dot(tri, part.astype(BF16), preferred_element_type=F32)
                         for part in _split3(lf))
            st[ch, hd] = d
        def from_state(ch, hd, state_t):
            d = st[ch, hd]
            qe = (d["q"] * jnp.exp(d["b"])).astype(BF16)
            d["o"] = lax.dot_general(qe, state_t.astype(BF16), nt, preferred_element_type=F32)
        for hd in range(hpb):
            from_state(0, hd, states[hd])
        for ch, hd in insts:
            d = st[ch, hd]
            b, q, k = d["b"], d["q"], d["k"]
            ends = [b[(i + 1) * SUBLANES - 1:(i + 1) * SUBLANES, :] for i in range(groups)]
            kt = [grp(k, i) * jnp.exp(ends[i] - grp(b, i)) for i in range(groups)]
            qh = jnp.concatenate(
                [zeros8] + [grp(q, i) * jnp.exp(grp(b, i) - ends[i - 1]) for i in range(1, groups)],
                axis=0).astype(BF16)
            kh = jnp.concatenate(
                [kt[j] * jnp.exp(ends[i - 1] - ends[j]) if j < i else zeros8
                 for i in range(1, groups) for j in range(groups)], axis=0).astype(BF16)
            d["a"] = lax.dot_general(qh, kh, nt, preferred_element_type=F32)
            d["ends"], d["kt"] = ends, kt
        def state_update(ch, hd, state_t):
            d = st[ch, hd]
            b_end = d["ends"][-1]
            ke = jnp.concatenate([d["kt"][j] * jnp.exp(b_end - d["ends"][j]) for j in range(groups)],
                                 axis=0).astype(BF16)
            upd = jnp.dot(jnp.transpose(d["v"]).astype(BF16), ke, preferred_element_type=F32)
            return jnp.exp(b_end) * state_t + upd
        states = [state_update(0, hd, states[hd]) for hd in range(hpb)]
        for ch, hd in insts:
            d = st[ch, hd]
            vb = d["v"].astype(BF16)
            v3 = jnp.concatenate([vb] * (groups - 1), axis=0)
            d["off"] = jnp.dot(jnp.where(keep, d["a"], 0.0).astype(BF16), v3,
                               preferred_element_type=F32)
        for hd in range(hpb):
            from_state(1, hd, states[hd])
        states = [state_update(1, hd, states[hd]) for hd in range(hpb)]
        for ch, hd in insts:
            d = st[ch, hd]
            o = d["o"] + d["off"]
            o_g = []
            for gi in range(groups):
                qg, kg, vg, bg = grp(d["q"], gi), grp(d["k"], gi), grp(d["v"], gi), grp(d["b"], gi)
                og = grp(o, gi)
                for s in range(SUBLANES):
                    live = sub >= s
                    e = jnp.exp(jnp.where(live, bg - bg[s:s + 1, :], 0.0))
                    p = jnp.where(live, qg * e * kg[s:s + 1, :], 0.0)
                    og = og + jnp.sum(p, axis=-1, keepdims=True) * vg[s:s + 1, :]
                o_g.append(og)
            o = jnp.concatenate(o_g, axis=0)
            g = g_ref[rows[ch], cols[hd]]
            ms = jnp.mean(o * o, axis=-1, keepdims=True)
            y = o * lax.rsqrt(ms + EPS) * gw * (g * _sigmoid(g))
            o_ref[rows[ch], cols[hd]] = y.astype(o_ref.dtype)
        return tuple(states)

    init = tuple(jnp.zeros((LANES, LANES), F32) for _ in range(hpb))
    lax.fori_loop(0, s_len // (chunk * pair), body, init)


def _hgrn_mixer(proj, lower, gw, seq, d, hpb):
    t = proj.shape[0]
    w = hpb * LANES
    nb = d // w
    col = lambda off: pl.BlockSpec((seq, w), lambda b, p: (b, off + p))
    return pl.pallas_call(
        functools.partial(_hgrn_kernel, hpb=hpb, chunk=HGRN_CHUNK),
        grid=(t // seq, nb),
        in_specs=[
            col(0), col(nb), col(2 * nb), col(3 * nb),
            pl.BlockSpec((1, w), lambda b, p: (0, p)),
            pl.BlockSpec((1, LANES), lambda b, p: (0, 0)),
        ],
        out_specs=pl.BlockSpec((seq, w), lambda b, p: (b, p)),
        out_shape=jax.ShapeDtypeStruct((t, d), BF16),
        compiler_params=_cparams("parallel", "parallel"),
        name="hgrn_mixer",
    )(proj, proj, proj, proj, lower, gw)


def _post_kernel(o_ref, x_ref, wo_ref, g1_ref, nw_ref, sh_ref, sc_ref, wr_ref, br_ref,
                 x1_ref, hp_ref, ri_ref, rw_ref, cnt_ref, run_s):
    i = pl.program_id(0)
    tm = x_ref.shape[0]

    @pl.when(i == 0)
    def _():
        run_s[...] = jnp.zeros_like(run_s)

    y = jnp.dot(o_ref[...], wo_ref[...], preferred_element_type=F32)
    x1 = x_ref[...] + g1_ref[...] * y
    x1_ref[...] = x1
    h2 = _norm_mod(x1, nw_ref[...], sc_ref[...], sh_ref[...])
    hp_ref[...] = _pack_halves(h2)
    logits = jnp.dot(h2.astype(BF16), wr_ref[...], preferred_element_type=F32) + br_ref[...]

    lane = lax.broadcasted_iota(I32, (tm, LANES), 1).astype(F32)
    work = logits
    idx, val = [], []
    for _ in range(TOP_K):
        mx = jnp.max(work, axis=-1, keepdims=True)
        ix = jnp.min(jnp.where(work == mx, lane, float(LANES)), axis=-1, keepdims=True)
        idx.append(ix)
        val.append(mx)
        work = jnp.where(lane == ix, -jnp.inf, work)
    ex = [jnp.exp(v - val[0]) for v in val]
    den = ex[0] + ex[1] + ex[2] + ex[3]
    wts = [e / den for e in ex]

    hot = jnp.zeros((tm, LANES), F32)
    for ix in idx:
        hot = hot + jnp.where(lane == ix, 1.0, 0.0)
    r = lax.broadcasted_iota(I32, (tm, tm), 0)
    c = lax.broadcasted_iota(I32, (tm, tm), 1)
    strict = jnp.where(r > c, 1.0, 0.0).astype(BF16)
    before = jnp.dot(strict, hot.astype(BF16), preferred_element_type=F32) + run_s[...]
    ranks = [jnp.sum(jnp.where(lane == ix, before, 0.0), axis=-1, keepdims=True) for ix in idx]
    run_s[...] = run_s[...] + jnp.sum(hot, axis=0, keepdims=True)

    ri = jnp.zeros((tm, LANES), F32)
    rw = jnp.zeros((tm, LANES), F32)
    for k in range(TOP_K):
        ri = jnp.where(lane == float(k), idx[k], ri)
        ri = jnp.where(lane == float(TOP_K + k), ranks[k], ri)
        rw = jnp.where(lane == float(k), wts[k], rw)
    ri_ref[...] = ri.astype(I32)
    rw_ref[...] = rw
    cnt_ref[...] = run_s[...].astype(I32)


def _post_mixer(o, x, wo, mod, nw2, wr, br, seq, tm):
    t, d = x.shape
    per_seq = seq // tm
    mod_spec = lambda j: pl.BlockSpec((None, 1, d), lambda i: (i // per_seq, 0, j))
    row = lambda w: pl.BlockSpec((tm, w), lambda i: (i, 0))
    return pl.pallas_call(
        _post_kernel,
        grid=(t // tm,),
        in_specs=[
            row(d), row(d),
            _resident((d, d), lambda i: (0, 0)),
            mod_spec(2),
            _resident((1, d), lambda i: (0, 0)),
            mod_spec(3), mod_spec(4),
            _resident((d, LANES), lambda i: (0, 0)),
            _resident((1, LANES), lambda i: (0, 0)),
        ],
        out_specs=[row(d), row(d // 2), row(LANES), row(LANES),
                   pl.BlockSpec((1, LANES), lambda i: (0, 0))],
        out_shape=[
            jax.ShapeDtypeStruct((t, d), F32),
            jax.ShapeDtypeStruct((t, d // 2), U32),
            jax.ShapeDtypeStruct((t, LANES), I32),
            jax.ShapeDtypeStruct((t, LANES), F32),
            jax.ShapeDtypeStruct((1, LANES), I32),
        ],
        scratch_shapes=[pltpu.VMEM((1, LANES), F32)],
        compiler_params=_cparams("arbitrary"),
        name="post_mixer",
    )(o, x, wo, mod, nw2, mod, mod, wr, br)


def _sc_mesh():
    return plsc.VectorSubcoreMesh(core_axis_name="c", subcore_axis_name="s")


def _sc_scatter_rows(x, posf, n_rows):
    t, w = x.shape
    workers = SC_CORES * SC_SUBCORES
    per_w = t // workers

    @functools.partial(
        pl.kernel, mesh=_sc_mesh(),
        out_type=jax.ShapeDtypeStruct((n_rows, w), x.dtype),
        scratch_types=[pltpu.VMEM((SC_ROWS,), I32)] * TOP_K
        + [pltpu.VMEM((SC_ROWS, w), x.dtype), pltpu.SemaphoreType.DMA])
    def scatter_kernel(x_hbm, pos_hbm, out_hbm, i0, i1, i2, i3, rows_v, sem):
        idxs = (i0, i1, i2, i3)
        wid = lax.axis_index("s") * SC_CORES + lax.axis_index("c")
        base = wid * per_w

        @pl.loop(0, per_w // SC_ROWS)
        def _(i):
            off = base + i * SC_ROWS
            for k in range(TOP_K):
                pltpu.sync_copy(pos_hbm.at[pl.ds(k * t + off, SC_ROWS)], idxs[k])
            pltpu.sync_copy(x_hbm.at[pl.ds(off, SC_ROWS)], rows_v)
            copies = [pltpu.async_copy(rows_v, out_hbm.at[idxs[k]], sem) for k in range(TOP_K)]
            for cp in copies:
                cp.wait()

    return scatter_kernel(x, posf)


def _sc_gather_rows(src, posf):
    n = posf.shape[0]
    w = src.shape[1]
    workers = SC_CORES * SC_SUBCORES
    per_w = n // workers

    @functools.partial(
        pl.kernel, mesh=_sc_mesh(),
        out_type=jax.ShapeDtypeStruct((n, w), src.dtype),
        scratch_types=[pltpu.VMEM((SC_ROWS,), I32), pltpu.VMEM((SC_ROWS, w), src.dtype),
                       pltpu.SemaphoreType.DMA])
    def gather_kernel(src_hbm, pos_hbm, out_hbm, idx_v, rows_v, sem):
        wid = lax.axis_index("s") * SC_CORES + lax.axis_index("c")
        base = wid * per_w

        @pl.loop(0, per_w // SC_ROWS)
        def _(i):
            off = base + i * SC_ROWS
            pltpu.sync_copy(pos_hbm.at[pl.ds(off, SC_ROWS)], idx_v)
            pltpu.async_copy(src_hbm.at[idx_v], rows_v, sem).wait()
            pltpu.sync_copy(rows_v, out_hbm.at[pl.ds(off, SC_ROWS)])

    return gather_kernel(src, posf)


def _expert_kernel(be_ref, nu_ref, x_ref, wg_ref, wu_ref, bg_ref, bu_ref, wd_ref, bd_ref, y_ref):
    i = pl.program_id(0)

    @pl.when(i < nu_ref[0])
    def _():
        half = x_ref.shape[1]
        lo, hi = _unpack_halves(x_ref[...])
        lo = lo.astype(BF16)
        hi = hi.astype(BF16)

        def proj(w_ref, b_ref):
            return (jnp.dot(lo, w_ref[0, :half, :], preferred_element_type=F32)
                    + jnp.dot(hi, w_ref[0, half:, :], preferred_element_type=F32) + b_ref[0])

        gate = jnp.minimum(proj(wg_ref, bg_ref), SWIGLU_LIMIT)
        up = jnp.clip(proj(wu_ref, bu_ref), -SWIGLU_LIMIT, SWIGLU_LIMIT)
        h = (up + 1.0) * gate * _sigmoid(SWIGLU_ALPHA * gate)
        y = jnp.dot(h.astype(BF16), wd_ref[0], preferred_element_type=F32) + bd_ref[0]
        y_ref[...] = _pack_halves(y)

    @pl.when(i >= nu_ref[0])
    def _():
        y_ref[...] = jnp.zeros_like(y_ref)


def _expert_ffn(xs, blk_e, n_used, wg, wu, bg, bu, wd, bd, rb):
    rows, half = xs.shape
    e, d, f = wg.shape
    n_blocks = rows // rb
    wspec = lambda shape: pl.BlockSpec(shape, lambda i, be, nu: (be[i], 0, 0))
    return pl.pallas_call(
        _expert_kernel,
        grid_spec=pltpu.PrefetchScalarGridSpec(
            num_scalar_prefetch=2,
            grid=(n_blocks,),
            in_specs=[
                pl.BlockSpec((rb, half), lambda i, be, nu: (i, 0)),
                wspec((1, d, f)), wspec((1, d, f)),
                wspec((1, 1, f)), wspec((1, 1, f)),
                wspec((1, f, d)), wspec((1, 1, d)),
            ],
            out_specs=pl.BlockSpec((rb, d // 2), lambda i, be, nu: (i, 0)),
        ),
        out_shape=jax.ShapeDtypeStruct((rows, d // 2), U32),
        compiler_params=_cparams("arbitrary"),
        name="expert_ffn",
    )(blk_e, n_used, xs, wg, wu, bg, bu, wd, bd)


def _combine_kernel(x_ref, y0_ref, y1_ref, y2_ref, y3_ref, rw_ref, g2_ref, o_ref):
    half = y0_ref.shape[1]
    rw = rw_ref[...]
    lane = lax.broadcasted_iota(I32, rw.shape, 1)
    acc_lo = jnp.zeros(y0_ref.shape, F32)
    acc_hi = jnp.zeros(y0_ref.shape, F32)
    for k, y_ref in enumerate((y0_ref, y1_ref, y2_ref, y3_ref)):
        wk = jnp.sum(jnp.where(lane == k, rw, 0.0), axis=-1, keepdims=True)
        lo, hi = _unpack_halves(y_ref[...])
        acc_lo = acc_lo + wk * lo
        acc_hi = acc_hi + wk * hi
    g2 = g2_ref[...]
    o_ref[:, :half] = x_ref[:, :half] + g2[:, :half] * acc_lo
    o_ref[:, half:] = x_ref[:, half:] + g2[:, half:] * acc_hi


def _combine(x1, yt, rw, mod, seq, tm):
    t, d = x1.shape
    per_seq = seq // tm
    nt = t // tm
    yspec = lambda k: pl.BlockSpec((tm, d // 2), lambda i: (k * nt + i, 0))
    return pl.pallas_call(
        _combine_kernel,
        grid=(nt,),
        in_specs=[
            pl.BlockSpec((tm, d), lambda i: (i, 0)),
            yspec(0), yspec(1), yspec(2), yspec(3),
            pl.BlockSpec((tm, LANES), lambda i: (i, 0)),
            pl.BlockSpec((None, 1, d), lambda i: (i // per_seq, 0, 5)),
        ],
        out_specs=pl.BlockSpec((tm, d), lambda i: (i, 0)),
        out_shape=jax.ShapeDtypeStruct((t, d), F32),
        compiler_params=_cparams("parallel"),
        name="moe_combine",
    )(x1, yt, yt, yt, yt, rw, mod)


def _row_tile(seq):
    return min(seq, 512)


def _moe_block_rows(t):
    return min(512, t * TOP_K // 8)


def kernel(x, c, fox_w_in, fox_b_f, fox_q_norm, fox_k_norm, fox_w_out, hgrn_w_in, hgrn_lb, hgrn_g_norm, hgrn_w_out, ada_w, ada_b, norm1_w, norm2_w, router_w, router_b, exp_w_gu, exp_b_gu, exp_w_down, exp_b_down):
    bsz, seq, d = x.shape
    t = bsz * seq
    depth = ada_w.shape[0]
    n_exp = router_w.shape[-1]
    f_dim = exp_w_down.shape[2]
    h_fox = fox_w_in.shape[-1] - 4 * d
    dh = d // h_fox
    tm = _row_tile(seq)
    rb = _moe_block_rows(t)
    n_blocks = (t * TOP_K) // rb + n_exp
    hgrn_hpb = 4 if d % (4 * LANES) == 0 else 2

    mod_all = _adaln_mod(c, ada_w, ada_b).reshape(depth, bsz, 1, -1)
    lb_soft = jax.nn.softmax(hgrn_lb.astype(F32), axis=0)
    lower = jnp.cumsum(lb_soft, axis=0) - lb_soft[0]

    xf = x.reshape(t, d)
    for i in range(depth):
        mod = mod_all[i]
        j = i // 2
        nw1 = norm1_w[i].reshape(1, d)
        if i % 2 == 0:
            w_in = fox_w_in[j]
            w_main = w_in[:, :4 * d].astype(BF16)
            w_f = jnp.pad(w_in[:, 4 * d:], ((0, 0), (0, LANES - h_fox))).astype(BF16)
            b_f = jnp.pad(fox_b_f[j], (0, LANES - h_fox)).reshape(1, LANES)
            qkvg, flogit = _in_proj(xf, nw1, mod, w_main, BF16, seq, tm, w_f, b_f)
            cum = _fox_cum(flogit, seq)
            qw = jnp.tile(fox_q_norm[j], LANES // dh).reshape(1, LANES)
            kw = jnp.tile(fox_k_norm[j], LANES // dh).reshape(1, LANES)
            o = _fox_attention(qkvg, cum, qw, kw, seq, d, dh)
            w_out = fox_w_out[j].astype(BF16)
        else:
            (proj,) = _in_proj(xf, nw1, mod, hgrn_w_in[j].astype(BF16), F32, seq, tm)
            o = _hgrn_mixer(proj, lower[j].reshape(1, d), hgrn_g_norm[j].reshape(1, HGRN_DIM),
                            seq, d, hgrn_hpb)
            w_out = hgrn_w_out[j].astype(BF16)

        wr = jnp.pad(router_w[i], ((0, 0), (0, LANES - n_exp))).astype(BF16)
        br = jnp.pad(router_b[i], (0, LANES - n_exp), constant_values=MASK_VALUE).reshape(1, LANES)
        x1, hp, ri, rw, cnt = _post_mixer(o, xf, w_out, mod, norm2_w[i].reshape(1, d), wr, br, seq, tm)

        counts = cnt[0, :n_exp]
        padded = (counts + rb - 1) // rb * rb
        pad_end = jnp.cumsum(padded)
        pad_start = pad_end - padded
        top_idx = ri[:, :TOP_K]
        pos = pad_start[top_idx] + ri[:, TOP_K:2 * TOP_K]
        posf = jnp.transpose(pos).reshape(-1).astype(I32)
        blk_lo = jnp.arange(n_blocks, dtype=I32) * rb
        blk_e = jnp.minimum(jnp.sum(pad_end[None, :] <= blk_lo[:, None], axis=1), n_exp - 1).astype(I32)
        n_used = (pad_end[-1] // rb).astype(I32).reshape(1)

        xs = _sc_scatter_rows(hp, posf, n_blocks * rb)

        w_gu = exp_w_gu[i].reshape(n_exp, d, f_dim, 2)
        wg = w_gu[..., 0].astype(BF16)
        wu = w_gu[..., 1].astype(BF16)
        b_gu = exp_b_gu[i].reshape(n_exp, 1, f_dim, 2)
        ys = _expert_ffn(xs, blk_e, n_used, wg, wu, b_gu[..., 0], b_gu[..., 1],
                         exp_w_down[i].astype(BF16), exp_b_down[i].reshape(n_exp, 1, d), rb)

        yt = _sc_gather_rows(ys, posf)
        xf = _combine(x1, yt, rw, mod, seq, tm)
    return xf.reshape(bsz, seq, d)
```

```python
import functools

import jax
import jax.numpy as jnp
import numpy as np
from jax import lax
from jax.experimental import pallas as pl
from jax.experimental.pallas import tpu as pltpu
from jax.experimental.pallas import tpu_sc as plsc

F32 = jnp.float32
BF16 = jnp.bfloat16
U32 = jnp.uint32
I32 = jnp.int32

EPS = 1e-6
MASK_VALUE = -1e30
TOP_K = 4
HGRN_DIM = 128
HGRN_CHUNK = 32
SWIGLU_LIMIT = 7.0
SWIGLU_ALPHA = 1.702
LANES = 128
SUBLANES = 8
SC_CORES = 2
SC_SUBCORES = 16
SC_ROWS = 64
VMEM_LIMIT = 56 * 1024 * 1024
HIGHEST = lax.Precision.HIGHEST


def _cparams(*sem):
    return pltpu.CompilerParams(dimension_semantics=sem, vmem_limit_bytes=VMEM_LIMIT)


def _resident(shape, index_map):
    return pl.BlockSpec(shape, index_map, pipeline_mode=pl.Buffered(1))


def _sigmoid(x):
    return 1.0 / (1.0 + jnp.exp(-x))


def _bf16_bits(x):
    return lax.bitcast_convert_type(x.astype(BF16).astype(F32), U32)


def _pack_halves(x):
    n = x.shape[1] // 2
    lo = _bf16_bits(x[:, :n])
    hi = _bf16_bits(x[:, n:])
    return (hi & jnp.uint32(0xFFFF0000)) | (lo >> 16)


def _unpack_halves(w):
    lo = lax.bitcast_convert_type(w << 16, F32)
    hi = lax.bitcast_convert_type(w & jnp.uint32(0xFFFF0000), F32)
    return lo, hi


def _mod_kernel(c_ref, w_ref, b_ref, o_ref):
    c = c_ref[...]
    ca = c * _sigmoid(c)
    o_ref[0] = jnp.dot(ca, w_ref[0], preferred_element_type=F32, precision=HIGHEST) + b_ref[0]


def _adaln_mod(c, ada_w, ada_b):
    depth, d, n6 = ada_w.shape
    b = c.shape[0]
    nmod = n6 // d
    return pl.pallas_call(
        _mod_kernel,
        grid=(depth, nmod),
        in_specs=[
            pl.BlockSpec((b, d), lambda i, j: (0, 0)),
            pl.BlockSpec((1, d, d), lambda i, j: (i, 0, j)),
            pl.BlockSpec((1, 1, d), lambda i, j: (i, 0, j)),
        ],
        out_specs=pl.BlockSpec((1, b, d), lambda i, j: (i, 0, j)),
        out_shape=jax.ShapeDtypeStruct((depth, b, n6), F32),
        compiler_params=_cparams("parallel", "parallel"),
        name="adaln_mod",
    )(c, ada_w, ada_b.reshape(depth, 1, n6))


def _norm_mod(x, nw, sc, sh):
    ms = jnp.mean(x * x, axis=-1, keepdims=True)
    return x * lax.rsqrt(ms + EPS) * nw * (1.0 + sc) + sh


def _in_kernel(x_ref, nw_ref, sh_ref, sc_ref, w_ref, *rest, n_chunk, with_f):
    if with_f:
        wf_ref, bf_ref, o_ref, of_ref = rest
    else:
        (o_ref,) = rest
    h = _norm_mod(x_ref[...], nw_ref[...], sc_ref[...], sh_ref[...]).astype(BF16)
    n = o_ref.shape[1]
    for c0 in range(0, n, n_chunk):
        o_ref[:, c0:c0 + n_chunk] = jnp.dot(
            h, w_ref[:, c0:c0 + n_chunk], preferred_element_type=F32).astype(o_ref.dtype)
    if with_f:
        of_ref[...] = jnp.dot(h, wf_ref[...], preferred_element_type=F32) + bf_ref[...]


def _in_proj(x, nw, mod, w, out_dtype, seq, tm, wf=None, bf=None):
    t, d = x.shape
    n = w.shape[1]
    per_seq = seq // tm
    with_f = wf is not None
    mod_spec = lambda j: pl.BlockSpec((None, 1, d), lambda i: (i // per_seq, 0, j))
    in_specs = [
        pl.BlockSpec((tm, d), lambda i: (i, 0)),
        _resident((1, d), lambda i: (0, 0)),
        mod_spec(0), mod_spec(1),
        _resident((d, n), lambda i: (0, 0)),
    ]
    out_specs = [pl.BlockSpec((tm, n), lambda i: (i, 0))]
    out_shape = [jax.ShapeDtypeStruct((t, n), out_dtype)]
    args = [x, nw, mod, mod, w]
    if with_f:
        in_specs += [_resident((d, LANES), lambda i: (0, 0)), _resident((1, LANES), lambda i: (0, 0))]
        out_specs.append(pl.BlockSpec((tm, LANES), lambda i: (i, 0)))
        out_shape.append(jax.ShapeDtypeStruct((t, LANES), F32))
        args += [wf, bf]
    return pl.pallas_call(
        functools.partial(_in_kernel, n_chunk=min(n, 1024), with_f=with_f),
        grid=(t // tm,),
        in_specs=in_specs,
        out_specs=out_specs,
        out_shape=out_shape,
        compiler_params=_cparams("parallel"),
        name="in_proj_fox" if with_f else "in_proj_hgrn",
    )(*args)


LOG2E = 1.4426950408889634
EXP2_SPAN = 120.0
SCORE_MARGIN = 1.01
BIAS_LANES = 16


def _split3(c):
    c1 = c.astype(BF16).astype(F32)
    r = c - c1
    c2 = r.astype(BF16).astype(F32)
    return c1, c2, r - c2


def _bias_layout(n_pairs):
    route =np.zeros((3, 3 * LANES, LANES), np.float32)
    const = np.zeros((3, 1, LANES), np.float32)
    for p in range(n_pairs):
        base = p * BIAS_LANES
        for i in range(3):
            const[0, 0, base + i] = 1.0
            for h in range(2):
                route[0, i * LANES + 2 * p + h, base + 3 + 3 * h + i] = -1.0
                route[1 + h, i * LANES + 2 * p + h, base + i] = 1.0
                const[1 + h, 0, base + 3 + 3 * h + i] = 1.0
    return jnp.asarray(route, BF16), jnp.asarray(const, F32)


def _cum_kernel(f_ref, ref_ref, route_ref, const_ref, o_ref, *, rows):
    s = f_ref.shape[0]
    r = lax.broadcasted_iota(I32, (rows, rows), 0)
    c = lax.broadcasted_iota(I32, (rows, rows), 1)
    tri = (r >= c).astype(F32)
    carry = jnp.zeros((1, LANES), F32)
    for r0 in range(0, s, rows):
        x = f_ref[r0:r0 + rows, :]
        lf = jnp.minimum(x, 0.0) - jnp.log(1.0 + jnp.exp(-jnp.abs(x)))
        cs = jnp.dot(tri, lf, preferred_element_type=F32, precision=HIGHEST) + carry
        carry = cs[rows - 1:rows, :]
        c2 = cs * LOG2E
        k_terms = jnp.concatenate(_split3(c2), axis=1).astype(BF16)
        q_terms = jnp.concatenate(_split3(c2 - ref_ref[...]), axis=1).astype(BF16)
        for j, terms in enumerate((k_terms, q_terms, q_terms)):
            routed = jnp.dot(terms, route_ref[j], preferred_element_type=F32) + const_ref[j]
            o_ref[r0:r0 + rows, j * LANES:(j + 1) * LANES] = routed.astype(o_ref.dtype)


def _fox_bias(flogit, ref, seq, n_pairs):
    t = flogit.shape[0]
    route, const = _bias_layout(n_pairs)
    return pl.pallas_call(
        functools.partial(_cum_kernel, rows=min(seq, 128)),
        grid=(t // seq,),
        in_specs=[pl.BlockSpec((seq, LANES), lambda b: (b, 0)),
                  pl.BlockSpec((1, LANES), lambda b: (0, 0)),
                  pl.BlockSpec((3, 3 * LANES, LANES), lambda b: (0, 0, 0)),
                  pl.BlockSpec((3, 1, LANES), lambda b: (0, 0, 0))],
        out_specs=pl.BlockSpec((seq, 3 * LANES), lambda b: (b, 0)),
        out_shape=jax.ShapeDtypeStruct((t, 3 * LANES), BF16),
        compiler_params=_cparams("parallel"),
        name="fox_bias",
    )(flogit, ref, route, const)


def _attn_kernel(q_ref, k_ref, v_ref, g_ref, bias_ref, qw_ref, kw_ref, o_ref, qs_s, ka_s, *,
                 dh, tq, bounded):
    s_len = q_ref.shape[0]
    nq = s_len // tq
    hp = pl.program_id(1)
    lane = lax.broadcasted_iota(I32, (1, LANES), 1)
    first = lane < dh

    def headnorm(z, w):
        z2 = z * z
        s0 = jnp.sum(jnp.where(first, z2, 0.0), axis=-1, keepdims=True)
        s1 = jnp.sum(jnp.where(first, 0.0, z2), axis=-1, keepdims=True)
        ms = jnp.where(first, s0, s1) * (1.0 / dh)
        return z * lax.rsqrt(ms + EPS) * w

    qn = headnorm(q_ref[...].astype(F32), qw_ref[...]) * (dh ** -0.5 * LOG2E)
    kn = headnorm(k_ref[...].astype(F32), kw_ref[...])
    mine = (lane // BIAS_LANES) == hp
    ka_s[:, :LANES] = kn.astype(BF16)
    ka_s[:, LANES:] = bias_ref[:, :LANES]
    q_lo = [jnp.where(first, qn, 0.0), jnp.where(first, 0.0, qn)]
    for qi in range(nq):
        rows = slice(qi * tq, (qi + 1) * tq)
        for h in range(2):
            q_hi = bias_ref[rows, (1 + h) * LANES:(2 + h) * LANES].astype(F32)
            qs_s[qi, h * tq:(h + 1) * tq, :LANES] = q_lo[h][rows, :].astype(BF16)
            qs_s[qi, h * tq:(h + 1) * tq, LANES:] = jnp.where(mine, q_hi, 0.0).astype(BF16)

    nt = (((1,), (1,)), ((), ()))

    def causal(s):
        rr = lax.broadcasted_iota(I32, (2 * tq, tq), 0)
        rr = jnp.where(rr >= tq, rr - tq, rr)
        cc = lax.broadcasted_iota(I32, (2 * tq, tq), 1)
        return jnp.where(rr >= cc, s, MASK_VALUE)

    def finish(qi, acc, l):
        o2 = acc / l
        o = jnp.where(first, o2[:tq], o2[tq:])
        g = g_ref[qi * tq:(qi + 1) * tq, :].astype(F32)
        o_ref[qi * tq:(qi + 1) * tq, :] = (o * _sigmoid(g)).astype(o_ref.dtype)

    if bounded:
        for qi in range(nq):
            acc = part = None
            for t in range(qi + 1):
                s = lax.dot_general(qs_s[qi], ka_s[t * tq:(t + 1) * tq, :], nt,
                                    preferred_element_type=F32)
                e = jnp.exp2(causal(s) if t == qi else s)
                cols = e[:, :LANES]
                for c0 in range(LANES, tq, LANES):
                    cols = cols + e[:, c0:c0 + LANES]
                pv = jnp.dot(e.astype(BF16), v_ref[t * tq:(t + 1) * tq, :],
                             preferred_element_type=F32)
                acc = pv if t == 0 else acc + pv
                part = cols if t == 0 else part + cols
            finish(qi, acc, jnp.sum(part, axis=-1, keepdims=True))
        return

    def scores(q, off):
        return lax.dot_general(q, ka_s[pl.ds(off, tq), :], nt, preferred_element_type=F32)

    def update(s, off, carry, masked):
        m, l, acc = carry
        if masked:
            s = causal(s)
        m_new = jnp.maximum(m, jnp.max(s, axis=-1, keepdims=True))
        p = jnp.exp2(s - m_new)
        alpha = jnp.exp2(m - m_new)
        l = alpha * l + jnp.sum(p, axis=-1, keepdims=True)
        acc = alpha * acc + jnp.dot(p.astype(BF16), v_ref[pl.ds(off, tq), :],
                                    preferred_element_type=F32)
        return m_new, l, acc

    for qi in range(nq):
        q = qs_s[qi]
        s = scores(q, 0)
        carry = (jnp.full((2 * tq, 1), MASK_VALUE, F32), jnp.zeros((2 * tq, 1), F32),
                 jnp.zeros((2 * tq, LANES), F32))
        if qi > 0:
            def body(j, c, q=q):
                off = pl.multiple_of(j * tq, tq)
                s_next = scores(q, off + tq)
                return (s_next,) + update(c[0], off, c[1:], False)

            s, *carry = lax.fori_loop(0, qi, body, (s,) + carry)
        m, l, acc = update(s, qi * tq, carry, True)
        finish(qi, acc, l)


def _fox_attention(qkvg, flogit, qw, kw, seq, d, dh):
    t = qkvg.shape[0]
    nb = d // LANES
    assert 2 * dh == LANES and nb * BIAS_LANES <= LANES
    tq = min(seq, 512)
    col = lambda off: pl.BlockSpec((seq, LANES), lambda b, p: (b, off + p))

    def call(bounded, bias):
        return pl.pallas_call(
            functools.partial(_attn_kernel, dh=dh, tq=tq, bounded=bounded),
            grid=(t // seq, nb),
            in_specs=[
                col(0), col(nb), col(2 * nb), col(3 * nb),
                pl.BlockSpec((seq, 3 * LANES), lambda b, p: (b, 0)),
                pl.BlockSpec((1, LANES), lambda b, p: (0, 0)),
                pl.BlockSpec((1, LANES), lambda b, p: (0, 0)),
            ],
            out_specs=pl.BlockSpec((seq, LANES), lambda b, p: (b, p)),
            out_shape=jax.ShapeDtypeStruct((t, d), BF16),
            scratch_shapes=[pltpu.VMEM((seq // tq, 2 * tq, 2 * LANES), BF16),
                            pltpu.VMEM((seq, 2 * LANES), BF16)],
            compiler_params=_cparams("parallel", "parallel"),
            name="fox_attention" if bounded else "fox_attention_running_max",
        )(qkvg, qkvg, qkvg, qkvg, bias, qw, kw)

    bound = (dh ** 0.5 * LOG2E * SCORE_MARGIN) * jnp.max(jnp.abs(qw)) * jnp.max(jnp.abs(kw)) + 0.1
    fits = bound <= EXP2_SPAN - 4.0
    ref = jnp.where(fits, bound - jnp.maximum(0.0, 2.0 * bound - EXP2_SPAN), 0.0)
    bias = _fox_bias(flogit, jnp.full((1, LANES), ref, F32), seq, nb)
    return lax.cond(fits, lambda: call(True, bias), lambda: call(False, bias))


def _hgrn_kernel(q_ref, f_ref, v_ref, g_ref, lb_ref, gw_ref, o_ref, *, hpb, chunk):
    s_len = q_ref.shape[0]
    pair = 2
    groups = chunk // SUBLANES
    off_w = (groups - 1) * chunk
    r = lax.broadcasted_iota(I32, (chunk, chunk), 0)
    c = lax.broadcasted_iota(I32, (chunk, chunk), 1)
    tri = jnp.where(r >= c, 1.0, 0.0).astype(BF16)
    ro = lax.broadcasted_iota(I32, (chunk, off_w), 0) // SUBLANES
    co = lax.broadcasted_iota(I32, (chunk, off_w), 1) // chunk
    keep = ro == co + 1
    sub = lax.broadcasted_iota(I32, (SUBLANES, LANES), 0)
    gw = gw_ref[...]
    nt = (((1,), (1,)), ((), ()))
    grp = lambda a, i: a[i * SUBLANES:(i + 1) * SUBLANES, :]
    zeros8 = jnp.zeros((SUBLANES, LANES), F32)

    def body(step, states):
        insts = [(ch, hd) for ch in range(pair) for hd in range(hpb)]
        rows = {ch: pl.ds(pl.multiple_of((step * pair + ch) * chunk, chunk), chunk)
                for ch in range(pair)}
        cols = {hd: slice(hd * LANES, (hd + 1) * LANES) for hd in range(hpb)}
        st = {}
        for ch, hd in insts:
            lb = lb_ref[:, cols[hd]]
            q_raw = q_ref[rows[ch], cols[hd]]
            sig = _sigmoid(f_ref[rows[ch], cols[hd]])
            d = dict(v=v_ref[rows[ch], cols[hd]], q=q_raw * _sigmoid(q_raw),
                     k=(1.0 - lb) * (1.0 - sig))
            lf = jnp.log(lb + (1.0 - lb) * sig)
            d["b"] = sum(jnp.dot(tri, part.astype(BF16), preferred_element_type=F32)
                         for part in _split3(lf))
            st[ch, hd] = d
        def from_state(ch, hd, state_t):
            d = st[ch, hd]
            qe = (d["q"] * jnp.exp(d["b"])).astype(BF16)
            d["o"] = lax.dot_general(qe, state_t.astype(BF16), nt, preferred_element_type=F32)
        for hd in range(hpb):
            from_state(0, hd, states[hd])
        for ch, hd in insts:
            d = st[ch, hd]
            b, q, k = d["b"], d["q"], d["k"]
            ends = [b[(i + 1) * SUBLANES - 1:(i + 1) * SUBLANES, :] for i in range(groups)]
            kt = [grp(k, i) * jnp.exp(ends[i] - grp(b, i)) for i in range(groups)]
            qh = jnp.concatenate(
                [zeros8] + [grp(q, i) * jnp.exp(grp(b, i) - ends[i - 1]) for i in range(1, groups)],
                axis=0).astype(BF16)
            kh = jnp.concatenate(
                [kt[j] * jnp.exp(ends[i - 1] - ends[j]) if j < i else zeros8
                 for i in range(1, groups) for j in range(groups)], axis=0).astype(BF16)
            d["a"] = lax.dot_general(qh, kh, nt, preferred_element_type=F32)
            d["ends"], d["kt"] = ends, kt
        def state_update(ch, hd, state_t):
            d = st[ch, hd]
            b_end = d["ends"][-1]
            ke = jnp.concatenate([d["kt"][j] * jnp.exp(b_end - d["ends"][j]) for j in range(groups)],
                                 axis=0).astype(BF16)
            upd = jnp.dot(jnp.transpose(d["v"]).astype(BF16), ke, preferred_element_type=F32)
            return jnp.exp(b_end) * state_t + upd
        states = [state_update(0, hd, states[hd]) for hd in range(hpb)]
        for ch, hd in insts:
            d = st[ch, hd]
            vb = d["v"].astype(BF16)
            v3 = jnp.concatenate([vb] * (groups - 1), axis=0)
            d["off"] = jnp.dot(jnp.where(keep, d["a"], 0.0).astype(BF16), v3,
                               preferred_element_type=F32)
        for hd in range(hpb):
            from_state(1, hd, states[hd])
        states = [state_update(1, hd, states[hd]) for hd in range(hpb)]
        for ch, hd in insts:
            d = st[ch, hd]
            o = d["o"] + d["off"]
            o_g = []
            for gi in range(groups):
                qg, kg, vg, bg = grp(d["q"], gi), grp(d["k"], gi), grp(d["v"], gi), grp(d["b"], gi)
                og = grp(o, gi)
                for s in range(SUBLANES):
                    live = sub >= s
                    e = jnp.exp(jnp.where(live, bg - bg[s:s + 1, :], 0.0))
                    p = jnp.where(live, qg * e * kg[s:s + 1, :], 0.0)
                    og = og + jnp.sum(p, axis=-1, keepdims=True) * vg[s:s + 1, :]
                o_g.append(og)
            o = jnp.concatenate(o_g, axis=0)
            g = g_ref[rows[ch], cols[hd]]
            ms = jnp.mean(o * o, axis=-1, keepdims=True)
            y = o * lax.rsqrt(ms + EPS) * gw * (g * _sigmoid(g))
            o_ref[rows[ch], cols[hd]] = y.astype(o_ref.dtype)
        return tuple(states)

    init = tuple(jnp.zeros((LANES, LANES), F32) for _ in range(hpb))
    lax.fori_loop(0, s_len // (chunk * pair), body, init)


def _hgrn_mixer(proj, lower, gw, seq, d, hpb):
    t = proj.shape[0]
    w = hpb * LANES
    nb = d // w
    col = lambda off: pl.BlockSpec((seq, w), lambda b, p: (b, off + p))
    return pl.pallas_call(
        functools.partial(_hgrn_kernel, hpb=hpb, chunk=HGRN_CHUNK),
        grid=(t // seq, nb),
        in_specs=[
            col(0), col(nb), col(2 * nb), col(3 * nb),
            pl.BlockSpec((1, w), lambda b, p: (0, p)),
            pl.BlockSpec((1, LANES), lambda b, p: (0, 0)),
        ],
        out_specs=pl.BlockSpec((seq, w), lambda b, p: (b, p)),
        out_shape=jax.ShapeDtypeStruct((t, d), BF16),
        compiler_params=_cparams("parallel", "parallel"),
        name="hgrn_mixer",
    )(proj, proj, proj, proj, lower, gw)


def _post_kernel(o_ref, x_ref, wo_ref, g1_ref, nw_ref, sh_ref, sc_ref, wr_ref, br_ref,
                 x1_ref, hp_ref, ri_ref, rw_ref, cnt_ref, run_s):
    i = pl.program_id(0)
    tm = x_ref.shape[0]

    @pl.when(i == 0)
    def _():
        run_s[...] = jnp.zeros_like(run_s)

    y = jnp.dot(o_ref[...], wo_ref[...], preferred_element_type=F32)
    x1 = x_ref[...] + g1_ref[...] * y
    x1_ref[...] = x1
    h2 = _norm_mod(x1, nw_ref[...], sc_ref[...], sh_ref[...])
    hp_ref[...] = _pack_halves(h2)
    logits = jnp.dot(h2.astype(BF16), wr_ref[...], preferred_element_type=F32) + br_ref[...]

    lane = lax.broadcasted_iota(I32, (tm, LANES), 1).astype(F32)
    work = logits
    idx, val = [], []
    for _ in range(TOP_K):
        mx = jnp.max(work, axis=-1, keepdims=True)
        ix = jnp.min(jnp.where(work == mx, lane, float(LANES)), axis=-1, keepdims=True)
        idx.append(ix)
        val.append(mx)
        work = jnp.where(lane == ix, -jnp.inf, work)
    ex = [jnp.exp(v - val[0]) for v in val]
    den = ex[0] + ex[1] + ex[2] + ex[3]
    wts = [e / den for e in ex]

    hot = jnp.zeros((tm, LANES), F32)
    for ix in idx:
        hot = hot + jnp.where(lane == ix, 1.0, 0.0)
    r = lax.broadcasted_iota(I32, (tm, tm), 0)
    c = lax.broadcasted_iota(I32, (tm, tm), 1)
    strict = jnp.where(r > c, 1.0, 0.0).astype(BF16)
    before = jnp.dot(strict, hot.astype(BF16), preferred_element_type=F32) + run_s[...]
    ranks = [jnp.sum(jnp.where(lane == ix, before, 0.0), axis=-1, keepdims=True) for ix in idx]
    run_s[...] = run_s[...] + jnp.sum(hot, axis=0, keepdims=True)

    ri = jnp.zeros((tm, LANES), F32)
    rw = jnp.zeros((tm, LANES), F32)
    for k in range(TOP_K):
        ri = jnp.where(lane == float(k), idx[k], ri)
        ri = jnp.where(lane == float(TOP_K + k), ranks[k], ri)
        rw = jnp.where(lane == float(k), wts[k], rw)
    ri_ref[...] = ri.astype(I32)
    rw_ref[...] = rw
    cnt_ref[...] = run_s[...].astype(I32)


def _post_mixer(o, x, wo, mod, nw2, wr, br, seq, tm):
    t, d = x.shape
    per_seq = seq // tm
    mod_spec = lambda j: pl.BlockSpec((None, 1, d), lambda i: (i // per_seq, 0, j))
    row = lambda w: pl.BlockSpec((tm, w), lambda i: (i, 0))
    return pl.pallas_call(
        _post_kernel,
        grid=(t // tm,),
        in_specs=[
            row(d), row(d),
            _resident((d, d), lambda i: (0, 0)),
            mod_spec(2),
            _resident((1, d), lambda i: (0, 0)),
            mod_spec(3), mod_spec(4),
            _resident((d, LANES), lambda i: (0, 0)),
            _resident((1, LANES), lambda i: (0, 0)),
        ],
        out_specs=[row(d), row(d // 2), row(LANES), row(LANES),
                   pl.BlockSpec((1, LANES), lambda i: (0, 0))],
        out_shape=[
            jax.ShapeDtypeStruct((t, d), F32),
            jax.ShapeDtypeStruct((t, d // 2), U32),
            jax.ShapeDtypeStruct((t, LANES), I32),
            jax.ShapeDtypeStruct((t, LANES), F32),
            jax.ShapeDtypeStruct((1, LANES), I32),
        ],
        scratch_shapes=[pltpu.VMEM((1, LANES), F32)],
        compiler_params=_cparams("arbitrary"),
        name="post_mixer",
    )(o, x, wo, mod, nw2, mod, mod, wr, br)


def _sc_mesh():
    return plsc.VectorSubcoreMesh(core_axis_name="c", subcore_axis_name="s")


def _sc_scatter_rows(x, posf, n_rows):
    t, w = x.shape
    workers = SC_CORES * SC_SUBCORES
    per_w = t // workers

    @functools.partial(
        pl.kernel, mesh=_sc_mesh(),
        out_type=jax.ShapeDtypeStruct((n_rows, w), x.dtype),
        scratch_types=[pltpu.VMEM((SC_ROWS,), I32)] * TOP_K
        + [pltpu.VMEM((SC_ROWS, w), x.dtype), pltpu.SemaphoreType.DMA])
    def scatter_kernel(x_hbm, pos_hbm, out_hbm, i0, i1, i2, i3, rows_v, sem):
        idxs = (i0, i1, i2, i3)
        wid = lax.axis_index("s") * SC_CORES + lax.axis_index("c")
        base = wid * per_w

        @pl.loop(0, per_w // SC_ROWS)
        def _(i):
            off = base + i * SC_ROWS
            for k in range(TOP_K):
                pltpu.sync_copy(pos_hbm.at[pl.ds(k * t + off, SC_ROWS)], idxs[k])
            pltpu.sync_copy(x_hbm.at[pl.ds(off, SC_ROWS)], rows_v)
            copies = [pltpu.async_copy(rows_v, out_hbm.at[idxs[k]], sem) for k in range(TOP_K)]
            for cp in copies:
                cp.wait()

    return scatter_kernel(x, posf)


def _sc_gather_rows(src, posf):
    n = posf.shape[0]
    w = src.shape[1]
    workers = SC_CORES * SC_SUBCORES
    per_w = n // workers

    @functools.partial(
        pl.kernel, mesh=_sc_mesh(),
        out_type=jax.ShapeDtypeStruct((n, w), src.dtype),
        scratch_types=[pltpu.VMEM((SC_ROWS,), I32), pltpu.VMEM((SC_ROWS, w), src.dtype),
                       pltpu.SemaphoreType.DMA])
    def gather_kernel(src_hbm, pos_hbm, out_hbm, idx_v, rows_v, sem):
        wid = lax.axis_index("s") * SC_CORES + lax.axis_index("c")
        base = wid * per_w

        @pl.loop(0, per_w // SC_ROWS)
        def _(i):
            off = base + i * SC_ROWS
            pltpu.sync_copy(pos_hbm.at[pl.ds(off, SC_ROWS)], idx_v)
            pltpu.async_copy(src_hbm.at[idx_v], rows_v, sem).wait()
            pltpu.sync_copy(rows_v, out_hbm.at[pl.ds(off, SC_ROWS)])

    return gather_kernel(src, posf)


def _expert_kernel(be_ref, nu_ref, x_ref, wg_ref, wu_ref, bg_ref, bu_ref, wd_ref, bd_ref, y_ref):
    i = pl.program_id(0)

    @pl.when(i < nu_ref[0])
    def _():
        half = x_ref.shape[1]
        lo, hi = _unpack_halves(x_ref[...])
        lo = lo.astype(BF16)
        hi = hi.astype(BF16)

        def proj(w_ref, b_ref):
            return (jnp.dot(lo, w_ref[0, :half, :], preferred_element_type=F32)
                    + jnp.dot(hi, w_ref[0, half:, :], preferred_element_type=F32) + b_ref[0])

        gate = jnp.minimum(proj(wg_ref, bg_ref), SWIGLU_LIMIT)
        up = jnp.clip(proj(wu_ref, bu_ref), -SWIGLU_LIMIT, SWIGLU_LIMIT)
        h = (up + 1.0) * gate * _sigmoid(SWIGLU_ALPHA * gate)
        y = jnp.dot(h.astype(BF16), wd_ref[0], preferred_element_type=F32) + bd_ref[0]
        y_ref[...] = _pack_halves(y)

    @pl.when(i >= nu_ref[0])
    def _():
        y_ref[...] = jnp.zeros_like(y_ref)


def _expert_ffn(xs, blk_e, n_used, wg, wu, bg, bu, wd, bd, rb):
    rows, half = xs.shape
    e, d, f = wg.shape
    n_blocks = rows // rb
    wspec = lambda shape: pl.BlockSpec(shape, lambda i, be, nu: (be[i], 0, 0))
    return pl.pallas_call(
        _expert_kernel,
        grid_spec=pltpu.PrefetchScalarGridSpec(
            num_scalar_prefetch=2,
            grid=(n_blocks,),
            in_specs=[
                pl.BlockSpec((rb, half), lambda i, be, nu: (i, 0)),
                wspec((1, d, f)), wspec((1, d, f)),
                wspec((1, 1, f)), wspec((1, 1, f)),
                wspec((1, f, d)), wspec((1, 1, d)),
            ],
            out_specs=pl.BlockSpec((rb, d // 2), lambda i, be, nu: (i, 0)),
        ),
        out_shape=jax.ShapeDtypeStruct((rows, d // 2), U32),
        compiler_params=_cparams("arbitrary"),
        name="expert_ffn",
    )(blk_e, n_used, xs, wg, wu, bg, bu, wd, bd)


def _combine_kernel(x_ref, y0_ref, y1_ref, y2_ref, y3_ref, rw_ref, g2_ref, o_ref):
    half = y0_ref.shape[1]
    rw = rw_ref[...]
    lane = lax.broadcasted_iota(I32, rw.shape, 1)
    acc_lo = jnp.zeros(y0_ref.shape, F32)
    acc_hi = jnp.zeros(y0_ref.shape, F32)
    for k, y_ref in enumerate((y0_ref, y1_ref, y2_ref, y3_ref)):
        wk = jnp.sum(jnp.where(lane == k, rw, 0.0), axis=-1, keepdims=True)
        lo, hi = _unpack_halves(y_ref[...])
        acc_lo = acc_lo + wk * lo
        acc_hi = acc_hi + wk * hi
    g2 = g2_ref[...]
    o_ref[:, :half] = x_ref[:, :half] + g2[:, :half] * acc_lo
    o_ref[:, half:] = x_ref[:, half:] + g2[:, half:] * acc_hi


def _combine(x1, yt, rw, mod, seq, tm):
    t, d = x1.shape
    per_seq = seq // tm
    nt = t // tm
    yspec = lambda k: pl.BlockSpec((tm, d // 2), lambda i: (k * nt + i, 0))
    return pl.pallas_call(
        _combine_kernel,
        grid=(nt,),
        in_specs=[
            pl.BlockSpec((tm, d), lambda i: (i, 0)),
            yspec(0), yspec(1), yspec(2), yspec(3),
            pl.BlockSpec((tm, LANES), lambda i: (i, 0)),
            pl.BlockSpec((None, 1, d), lambda i: (i // per_seq, 0, 5)),
        ],
        out_specs=pl.BlockSpec((tm, d), lambda i: (i, 0)),
        out_shape=jax.ShapeDtypeStruct((t, d), F32),
        compiler_params=_cparams("parallel"),
        name="moe_combine",
    )(x1, yt, yt, yt, yt, rw, mod)


def _row_tile(seq):
    return min(seq, 512)


def _moe_block_rows(t):
    return min(512, t * TOP_K // 8)


def kernel(x, c, fox_w_in, fox_b_f, fox_q_norm, fox_k_norm, fox_w_out, hgrn_w_in, hgrn_lb, hgrn_g_norm, hgrn_w_out, ada_w, ada_b, norm1_w, norm2_w, router_w, router_b, exp_w_gu, exp_b_gu, exp_w_down, exp_b_down):
    bsz, seq, d = x.shape
    t = bsz * seq
    depth = ada_w.shape[0]
    n_exp = router_w.shape[-1]
    f_dim = exp_w_down.shape[2]
    h_fox = fox_w_in.shape[-1] - 4 * d
    dh = d // h_fox
    tm = _row_tile(seq)
    rb = _moe_block_rows(t)
    n_blocks = (t * TOP_K) // rb + n_exp
    hgrn_hpb = 4 if d % (4 * LANES) == 0 else 2

    mod_all = _adaln_mod(c, ada_w, ada_b).reshape(depth, bsz, 1, -1)
    lb_soft = jax.nn.softmax(hgrn_lb.astype(F32), axis=0)
    lower = jnp.cumsum(lb_soft, axis=0) - lb_soft[0]

    xf = x.reshape(t, d)
    for i in range(depth):
        mod = mod_all[i]
        j = i // 2
        nw1 = norm1_w[i].reshape(1, d)
        if i % 2 == 0:
            w_in = fox_w_in[j]
            w_main = w_in[:, :4 * d].astype(BF16)
            w_f = jnp.pad(w_in[:, 4 * d:], ((0, 0), (0, LANES - h_fox))).astype(BF16)
            b_f = jnp.pad(fox_b_f[j], (0, LANES - h_fox)).reshape(1, LANES)
            qkvg, flogit = _in_proj(xf, nw1, mod, w_main, BF16, seq, tm, w_f, b_f)
            qw = jnp.tile(fox_q_norm[j], LANES // dh).reshape(1, LANES)
            kw = jnp.tile(fox_k_norm[j], LANES // dh).reshape(1, LANES)
            o = _fox_attention(qkvg, flogit, qw, kw, seq, d, dh)
            w_out = fox_w_out[j].astype(BF16)
        else:
            (proj,) = _in_proj(xf, nw1, mod, hgrn_w_in[j].astype(BF16), F32, seq, tm)
            o = _hgrn_mixer(proj, lower[j].reshape(1, d), hgrn_g_norm[j].reshape(1, HGRN_DIM),
                            seq, d, hgrn_hpb)
            w_out = hgrn_w_out[j].astype(BF16)

        wr = jnp.pad(router_w[i], ((0, 0), (0, LANES - n_exp))).astype(BF16)
        br = jnp.pad(router_b[i], (0, LANES - n_exp), constant_values=MASK_VALUE).reshape(1, LANES)
        x1, hp, ri, rw, cnt = _post_mixer(o, xf, w_out, mod, norm2_w[i].reshape(1, d), wr, br, seq, tm)

        counts = cnt[0, :n_exp]
        padded = (counts + rb - 1) // rb * rb
        pad_end = jnp.cumsum(padded)
        pad_start = pad_end - padded
        top_idx = ri[:, :TOP_K]
        pos = pad_start[top_idx] + ri[:, TOP_K:2 * TOP_K]
        posf = jnp.transpose(pos).reshape(-1).astype(I32)
        blk_lo = jnp.arange(n_blocks, dtype=I32) * rb
        blk_e = jnp.minimum(jnp.sum(pad_end[None, :] <= blk_lo[:, None], axis=1), n_exp - 1).astype(I32)
        n_used = (pad_end[-1] // rb).astype(I32).reshape(1)

        xs = _sc_scatter_rows(hp, posf, n_blocks * rb)

        w_gu = exp_w_gu[i].reshape(n_exp, d, f_dim, 2)
        wg = w_gu[..., 0].astype(BF16)
        wu = w_gu[..., 1].astype(BF16)
        b_gu = exp_b_gu[i].reshape(n_exp, 1, f_dim, 2)
        ys = _expert_ffn(xs, blk_e, n_used, wg, wu, b_gu[..., 0], b_gu[..., 1],
                         exp_w_down[i].astype(BF16), exp_b_down[i].reshape(n_exp, 1, d), rb)

        yt = _sc_gather_rows(ys, posf)
        xf = _combine(x1, yt, rw, mod, seq, tm)
    return xf.reshape(bsz, seq, d)
```

```python
import functools

import jax
import jax.numpy as jnp
import numpy as np
from jax import lax
from jax.experimental import pallas as pl
from jax.experimental.pallas import tpu as pltpu
from jax.experimental.pallas import tpu_sc as plsc

F32 = jnp.float32
BF16 = jnp.bfloat16
U32 = jnp.uint32
I32 = jnp.int32

EPS = 1e-6
MASK_VALUE = -1e30
TOP_K = 4
HGRN_DIM = 128
HGRN_CHUNK = 32
SWIGLU_LIMIT = 7.0
SWIGLU_ALPHA = 1.702
LANES = 128
SUBLANES = 8
SC_CORES = 2
SC_SUBCORES = 16
SC_ROWS = 64
VMEM_LIMIT = 56 * 1024 * 1024
HIGHEST = lax.Precision.HIGHEST


def _cparams(*sem):
    return pltpu.CompilerParams(dimension_semantics=sem, vmem_limit_bytes=VMEM_LIMIT)


def _resident(shape, index_map):
    return pl.BlockSpec(shape, index_map, pipeline_mode=pl.Buffered(1))


def _sigmoid(x):
    return 1.0 / (1.0 + jnp.exp(-x))


def _bf16_bits(x):
    return lax.bitcast_convert_type(x.astype(BF16).astype(F32), U32)


def _pack_halves(x):
    n = x.shape[1] // 2
    lo = _bf16_bits(x[:, :n])
    hi = _bf16_bits(x[:, n:])
    return (hi & jnp.uint32(0xFFFF0000)) | (lo >> 16)


def _unpack_halves(w):
    lo = lax.bitcast_convert_type(w << 16, F32)
    hi = lax.bitcast_convert_type(w & jnp.uint32(0xFFFF0000), F32)
    return lo, hi


def _mod_kernel(c_ref, w_ref, b_ref, o_ref):
    c = c_ref[...]
    ca = c * _sigmoid(c)
    o_ref[0] = jnp.dot(ca, w_ref[0], preferred_element_type=F32, precision=HIGHEST) + b_ref[0]


def _adaln_mod(c, ada_w, ada_b):
    depth, d, n6 = ada_w.shape
    b = c.shape[0]
    nmod = n6 // d
    return pl.pallas_call(
        _mod_kernel,
        grid=(depth, nmod),
        in_specs=[
            pl.BlockSpec((b, d), lambda i, j: (0, 0)),
            pl.BlockSpec((1, d, d), lambda i, j: (i, 0, j)),
            pl.BlockSpec((1, 1, d), lambda i, j: (i, 0, j)),
        ],
        out_specs=pl.BlockSpec((1, b, d), lambda i, j: (i, 0, j)),
        out_shape=jax.ShapeDtypeStruct((depth, b, n6), F32),
        compiler_params=_cparams("parallel", "parallel"),
        name="adaln_mod",
    )(c, ada_w, ada_b.reshape(depth, 1, n6))


def _norm_mod(x, nw, sc, sh):
    ms = jnp.mean(x * x, axis=-1, keepdims=True)
    return x * lax.rsqrt(ms + EPS) * nw * (1.0 + sc) + sh


def _in_kernel(x_ref, nw_ref, sh_ref, sc_ref, w_ref, *rest, n_chunk, with_f):
    if with_f:
        wf_ref, bf_ref, o_ref, of_ref = rest
    else:
        (o_ref,) = rest
    h = _norm_mod(x_ref[...], nw_ref[...], sc_ref[...], sh_ref[...]).astype(BF16)
    n = o_ref.shape[1]
    for c0 in range(0, n, n_chunk):
        o_ref[:, c0:c0 + n_chunk] = jnp.dot(
            h, w_ref[:, c0:c0 + n_chunk], preferred_element_type=F32).astype(o_ref.dtype)
    if with_f:
        of_ref[...] = jnp.dot(h, wf_ref[...], preferred_element_type=F32) + bf_ref[...]


def _in_proj(x, nw, mod, w, out_dtype, seq, tm, wf=None, bf=None):
    t, d = x.shape
    n = w.shape[1]
    per_seq = seq // tm
    with_f = wf is not None
    mod_spec = lambda j: pl.BlockSpec((None, 1, d), lambda i: (i // per_seq, 0, j))
    in_specs = [
        pl.BlockSpec((tm, d), lambda i: (i, 0)),
        _resident((1, d), lambda i: (0, 0)),
        mod_spec(0), mod_spec(1),
        _resident((d, n), lambda i: (0, 0)),
    ]
    out_specs = [pl.BlockSpec((tm, n), lambda i: (i, 0))]
    out_shape = [jax.ShapeDtypeStruct((t, n), out_dtype)]
    args = [x, nw, mod, mod, w]
    if with_f:
        in_specs += [_resident((d, LANES), lambda i: (0, 0)), _resident((1, LANES), lambda i: (0, 0))]
        out_specs.append(pl.BlockSpec((tm, LANES), lambda i: (i, 0)))
        out_shape.append(jax.ShapeDtypeStruct((t, LANES), F32))
        args += [wf, bf]
    return pl.pallas_call(
        functools.partial(_in_kernel, n_chunk=min(n, 1024), with_f=with_f),
        grid=(t // tm,),
        in_specs=in_specs,
        out_specs=out_specs,
        out_shape=out_shape,
        compiler_params=_cparams("parallel"),
        name="in_proj_fox" if with_f else "in_proj_hgrn",
    )(*args)


LOG2E = 1.4426950408889634
EXP2_SPAN = 120.0
SCORE_MARGIN = 1.01
BIAS_LANES = 16


def _split3(c):
    c1 = c.astype(BF16).astype(F32)
    r = c - c1
    c2 = r.astype(BF16).astype(F32)
    return c1, c2, r - c2


def _bias_layout(n_pairs):
    route =np.zeros((3, 3 * LANES, LANES), np.float32)
    const = np.zeros((3, 1, LANES), np.float32)
    for p in range(n_pairs):
        base = p * BIAS_LANES
        for i in range(3):
            const[0, 0, base + i] = 1.0
            for h in range(2):
                route[0, i * LANES + 2 * p + h, base + 3 + 3 * h + i] = -1.0
                route[1 + h, i * LANES + 2 * p + h, base + i] = 1.0
                const[1 + h, 0, base + 3 + 3 * h + i] = 1.0
    return jnp.asarray(route, BF16), jnp.asarray(const, F32)


def _cum_kernel(f_ref, ref_ref, route_ref, const_ref, o_ref, *, rows):
    s = f_ref.shape[0]
    r = lax.broadcasted_iota(I32, (rows, rows), 0)
    c = lax.broadcasted_iota(I32, (rows, rows), 1)
    tri = (r >= c).astype(F32)
    carry = jnp.zeros((1, LANES), F32)
    for r0 in range(0, s, rows):
        x = f_ref[r0:r0 + rows, :]
        lf = jnp.minimum(x, 0.0) - jnp.log(1.0 + jnp.exp(-jnp.abs(x)))
        cs = jnp.dot(tri, lf, preferred_element_type=F32, precision=HIGHEST) + carry
        carry = cs[rows - 1:rows, :]
        c2 = cs * LOG2E
        k_terms = jnp.concatenate(_split3(c2), axis=1).astype(BF16)
        q_terms = jnp.concatenate(_split3(c2 - ref_ref[...]), axis=1).astype(BF16)
        for j, terms in enumerate((k_terms, q_terms, q_terms)):
            routed = jnp.dot(terms, route_ref[j], preferred_element_type=F32) + const_ref[j]
            o_ref[r0:r0 + rows, j * LANES:(j + 1) * LANES] = routed.astype(o_ref.dtype)


def _fox_bias(flogit, ref, seq, n_pairs):
    t = flogit.shape[0]
    route, const = _bias_layout(n_pairs)
    return pl.pallas_call(
        functools.partial(_cum_kernel, rows=min(seq, 128)),
        grid=(t // seq,),
        in_specs=[pl.BlockSpec((seq, LANES), lambda b: (b, 0)),
                  pl.BlockSpec((1, LANES), lambda b: (0, 0)),
                  pl.BlockSpec((3, 3 * LANES, LANES), lambda b: (0, 0, 0)),
                  pl.BlockSpec((3, 1, LANES), lambda b: (0, 0, 0))],
        out_specs=pl.BlockSpec((seq, 3 * LANES), lambda b: (b, 0)),
        out_shape=jax.ShapeDtypeStruct((t, 3 * LANES), BF16),
        compiler_params=_cparams("parallel"),
        name="fox_bias",
    )(flogit, ref, route, const)


def _attn_kernel(q_ref, k_ref, v_ref, g_ref, bias_ref, qw_ref, kw_ref, o_ref, qs_s, ka_s, *,
                 dh, tq, bounded):
    s_len = q_ref.shape[0]
    nq = s_len // tq
    hp = pl.program_id(1)
    lane = lax.broadcasted_iota(I32, (1, LANES), 1)
    first = lane < dh

    def headnorm(z, w):
        z2 = z * z
        s0 = jnp.sum(jnp.where(first, z2, 0.0), axis=-1, keepdims=True)
        s1 = jnp.sum(jnp.where(first, 0.0, z2), axis=-1, keepdims=True)
        ms = jnp.where(first, s0, s1) * (1.0 / dh)
        return z * lax.rsqrt(ms + EPS) * w

    qn = headnorm(q_ref[...].astype(F32), qw_ref[...]) * (dh ** -0.5 * LOG2E)
    kn = headnorm(k_ref[...].astype(F32), kw_ref[...])
    mine = (lane // BIAS_LANES) == hp
    ka_s[:, :LANES] = kn.astype(BF16)
    ka_s[:, LANES:] = bias_ref[:, :LANES]
    q_lo = [jnp.where(first, qn, 0.0), jnp.where(first, 0.0, qn)]
    for qi in range(nq):
        rows = slice(qi * tq, (qi + 1) * tq)
        for h in range(2):
            q_hi = bias_ref[rows, (1 + h) * LANES:(2 + h) * LANES].astype(F32)
            qs_s[qi, h * tq:(h + 1) * tq, :LANES] = q_lo[h][rows, :].astype(BF16)
            qs_s[qi, h * tq:(h + 1) * tq, LANES:] = jnp.where(mine, q_hi, 0.0).astype(BF16)

    nt = (((1,), (1,)), ((), ()))

    def causal(s):
        rr = lax.broadcasted_iota(I32, (2 * tq, tq), 0)
        rr = jnp.where(rr >= tq, rr - tq, rr)
        cc = lax.broadcasted_iota(I32, (2 * tq, tq), 1)
        return jnp.where(rr >= cc, s, MASK_VALUE)

    def finish(qi, acc, l):
        o2 = acc / l
        o = jnp.where(first, o2[:tq], o2[tq:])
        g = g_ref[qi * tq:(qi + 1) * tq, :].astype(F32)
        o_ref[qi * tq:(qi + 1) * tq, :] = (o * _sigmoid(g)).astype(o_ref.dtype)

    if bounded:
        for qi in range(nq):
            acc = part = None
            for t in range(qi + 1):
                s = lax.dot_general(qs_s[qi], ka_s[t * tq:(t + 1) * tq, :], nt,
                                    preferred_element_type=F32)
                e = jnp.exp2(causal(s) if t == qi else s)
                cols = e[:, :LANES]
                for c0 in range(LANES, tq, LANES):
                    cols = cols + e[:, c0:c0 + LANES]
                pv = jnp.dot(e.astype(BF16), v_ref[t * tq:(t + 1) * tq, :],
                             preferred_element_type=F32)
                acc = pv if t == 0 else acc + pv
                part = cols if t == 0 else part + cols
            finish(qi, acc, jnp.sum(part, axis=-1, keepdims=True))
        return

    def scores(q, off):
        return lax.dot_general(q, ka_s[pl.ds(off, tq), :], nt, preferred_element_type=F32)

    def update(s, off, carry, masked):
        m, l, acc = carry
        if masked:
            s = causal(s)
        m_new = jnp.maximum(m, jnp.max(s, axis=-1, keepdims=True))
        p = jnp.exp2(s - m_new)
        alpha = jnp.exp2(m - m_new)
        l = alpha * l + jnp.sum(p, axis=-1, keepdims=True)
        acc = alpha * acc + jnp.dot(p.astype(BF16), v_ref[pl.ds(off, tq), :],
                                    preferred_element_type=F32)
        return m_new, l, acc

    for qi in range(nq):
        q = qs_s[qi]
        s = scores(q, 0)
        carry = (jnp.full((2 * tq, 1), MASK_VALUE, F32), jnp.zeros((2 * tq, 1), F32),
                 jnp.zeros((2 * tq, LANES), F32))
        if qi > 0:
            def body(j, c, q=q):
                off = pl.multiple_of(j * tq, tq)
                s_next = scores(q, off + tq)
                return (s_next,) + update(c[0], off, c[1:], False)

            s, *carry = lax.fori_loop(0, qi, body, (s,) + carry)
        m, l, acc = update(s, qi * tq, carry, True)
        finish(qi, acc, l)


def _fox_attention(qkvg, flogit, qw, kw, seq, d, dh):
    t = qkvg.shape[0]
    nb = d // LANES
    assert 2 * dh == LANES and nb * BIAS_LANES <= LANES
    tq = min(seq, 512)
    col = lambda off: pl.BlockSpec((seq, LANES), lambda b, p: (b, off + p))

    def call(bounded, bias):
        return pl.pallas_call(
            functools.partial(_attn_kernel, dh=dh, tq=tq, bounded=bounded),
            grid=(t // seq, nb),
            in_specs=[
                col(0), col(nb), col(2 * nb), col(3 * nb),
                pl.BlockSpec((seq, 3 * LANES), lambda b, p: (b, 0)),
                pl.BlockSpec((1, LANES), lambda b, p: (0, 0)),
                pl.BlockSpec((1, LANES), lambda b, p: (0, 0)),
            ],
            out_specs=pl.BlockSpec((seq, LANES), lambda b, p: (b, p)),
            out_shape=jax.ShapeDtypeStruct((t, d), BF16),
            scratch_shapes=[pltpu.VMEM((seq // tq, 2 * tq, 2 * LANES), BF16),
                            pltpu.VMEM((seq, 2 * LANES), BF16)],
            compiler_params=_cparams("parallel", "parallel"),
            name="fox_attention" if bounded else "fox_attention_running_max",
        )(qkvg, qkvg, qkvg, qkvg, bias, qw, kw)

    bound = (dh ** 0.5 * LOG2E * SCORE_MARGIN) * jnp.max(jnp.abs(qw)) * jnp.max(jnp.abs(kw)) + 0.1
    fits = bound <= EXP2_SPAN - 4.0
    ref = jnp.where(fits, bound - jnp.maximum(0.0, 2.0 * bound - EXP2_SPAN), 0.0)
    bias = _fox_bias(flogit, jnp.full((1, LANES), ref, F32), seq, nb)
    return lax.cond(fits, lambda: call(True, bias), lambda: call(False, bias))


def _hgrn_kernel(q_ref, f_ref, v_ref, g_ref, lb_ref, gw_ref, o_ref, *, hpb, chunk):
    s_len = q_ref.shape[0]
    pair = 2
    groups = chunk // SUBLANES
    off_w = (groups - 1) * chunk
    r = lax.broadcasted_iota(I32, (chunk, chunk), 0)
    c = lax.broadcasted_iota(I32, (chunk, chunk), 1)
    tri = jnp.where(r >= c, 1.0, 0.0).astype(BF16)
    ro = lax.broadcasted_iota(I32, (chunk, off_w), 0) // SUBLANES
    co = lax.broadcasted_iota(I32, (chunk, off_w), 1) // chunk
    keep = ro == co + 1
    sub = lax.broadcasted_iota(I32, (SUBLANES, LANES), 0)
    gw = gw_ref[...]
    nt = (((1,), (1,)), ((), ()))
    grp = lambda a, i: a[i * SUBLANES:(i + 1) * SUBLANES, :]
    zeros8 = jnp.zeros((SUBLANES, LANES), F32)

    def body(step, states):
        insts = [(ch, hd) for ch in range(pair) for hd in range(hpb)]
        rows = {ch: pl.ds(pl.multiple_of((step * pair + ch) * chunk, chunk), chunk)
                for ch in range(pair)}
        cols = {hd: slice(hd * LANES, (hd + 1) * LANES) for hd in range(hpb)}
        st = {}
        for ch, hd in insts:
            lb = lb_ref[:, cols[hd]]
            q_raw = q_ref[rows[ch], cols[hd]]
            sig = _sigmoid(f_ref[rows[ch], cols[hd]])
            d = dict(v=v_ref[rows[ch], cols[hd]], q=q_raw * _sigmoid(q_raw),
                     k=(1.0 - lb) * (1.0 - sig))
            lf = jnp.log(lb + (1.0 - lb) * sig)
            d["b"] = sum(jnp.dot(tri, part.astype(BF16), preferred_element_type=F32)
                         for part in _split3(lf))
            st[ch, hd] = d
        def from_state(ch, hd, state_t):
            d = st[ch, hd]
            qe = (d["q"] * jnp.exp(d["b"])).astype(BF16)
            d["o"] = lax.dot_general(qe, state_t.astype(BF16), nt, preferred_element_type=F32)
        for hd in range(hpb):
            from_state(0, hd, states[hd])
        for ch, hd in insts:
            d = st[ch, hd]
            b, q, k = d["b"], d["q"], d["k"]
            ends = [b[(i + 1) * SUBLANES - 1:(i + 1) * SUBLANES, :] for i in range(groups)]
            kt = [grp(k, i) * jnp.exp(ends[i] - grp(b, i)) for i in range(groups)]
            qh = jnp.concatenate(
                [zeros8] + [grp(q, i) * jnp.exp(grp(b, i) - ends[i - 1]) for i in range(1, groups)],
                axis=0).astype(BF16)
            kh = jnp.concatenate(
                [kt[j] * jnp.exp(ends[i - 1] - ends[j]) if j < i else zeros8
                 for i in range(1, groups) for j in range(groups)], axis=0).astype(BF16)
            d["a"] = lax.dot_general(qh, kh, nt, preferred_element_type=F32)
            d["ends"], d["kt"] = ends, kt
        def state_update(ch, hd, state_t):
            d = st[ch, hd]
            b_end = d["ends"][-1]
            ke = jnp.concatenate([d["kt"][j] * jnp.exp(b_end - d["ends"][j]) for j in range(groups)],
                                 axis=0).astype(BF16)
            upd = jnp.dot(jnp.transpose(d["v"]).astype(BF16), ke, preferred_element_type=F32)
            return jnp.exp(b_end) * state_t + upd
        states = [state_update(0, hd, states[hd]) for hd in range(hpb)]
        for ch, hd in insts:
            d = st[ch, hd]
            vb = d["v"].astype(BF16)
            v3 = jnp.concatenate([vb] * (groups - 1), axis=0)
            d["off"] = jnp.dot(jnp.where(keep, d["a"], 0.0).astype(BF16), v3,
                               preferred_element_type=F32)
        for hd in range(hpb):
            from_state(1, hd, states[hd])
        states = [state_update(1, hd, states[hd]) for hd in range(hpb)]
        for ch, hd in insts:
            d = st[ch, hd]
            o = d["o"] + d["off"]
            o_g = []
            for gi in range(groups):
                qg, kg, vg, bg = grp(d["q"], gi), grp(d["k"], gi), grp(d["v"], gi), grp(d["b"], gi)
                og = grp(o, gi)
                for s in range(SUBLANES):
                    live = sub >= s
                    e = jnp.exp(jnp.where(live, bg - bg[s:s + 1, :], 0.0))
                    p = jnp.where(live, qg * e * kg[s:s + 1, :], 0.0)
                    og = og + jnp.sum(p, axis=-1, keepdims=True) * vg[s:s + 1, :]
                o_g.append(og)
            o = jnp.concatenate(o_g, axis=0)
            g = g_ref[rows[ch], cols[hd]]
            ms = jnp.mean(o * o, axis=-1, keepdims=True)
            y = o * lax.rsqrt(ms + EPS) * gw * (g * _sigmoid(g))
            o_ref[rows[ch], cols[hd]] = y.astype(o_ref.dtype)
        return tuple(states)

    init = tuple(jnp.zeros((LANES, LANES), F32) for _ in range(hpb))
    lax.fori_loop(0, s_len // (chunk * pair), body, init)


def _hgrn_mixer(proj, lower, gw, seq, d, hpb):
    t = proj.shape[0]
    w = hpb * LANES
    nb = d // w
    col = lambda off: pl.BlockSpec((seq, w), lambda b, p: (b, off + p))
    return pl.pallas_call(
        functools.partial(_hgrn_kernel, hpb=hpb, chunk=HGRN_CHUNK),
        grid=(t // seq, nb),
        in_specs=[
            col(0), col(nb), col(2 * nb), col(3 * nb),
            pl.BlockSpec((1, w), lambda b, p: (0, p)),
            pl.BlockSpec((1, LANES), lambda b, p: (0, 0)),
        ],
        out_specs=pl.BlockSpec((seq, w), lambda b, p: (b, p)),
        out_shape=jax.ShapeDtypeStruct((t, d), BF16),
        compiler_params=_cparams("parallel", "parallel"),
        name="hgrn_mixer",
    )(proj, proj, proj, proj, lower, gw)


def _post_kernel(o_ref, x_ref, wo_ref, g1_ref, nw_ref, sh_ref, sc_ref, wr_ref, br_ref,
                 x1_ref, hp_ref, ri_ref, rw_ref, cnt_ref, run_s):
    i = pl.program_id(0)
    tm = x_ref.shape[0]

    @pl.when(i == 0)
    def _():
        run_s[...] = jnp.zeros_like(run_s)

    y = jnp.dot(o_ref[...], wo_ref[...], preferred_element_type=F32)
    x1 = x_ref[...] + g1_ref[...] * y
    x1_ref[...] = x1
    h2 = _norm_mod(x1, nw_ref[...], sc_ref[...], sh_ref[...])
    hp_ref[...] = _pack_halves(h2)
    logits = jnp.dot(h2.astype(BF16), wr_ref[...], preferred_element_type=F32) + br_ref[...]

    lane = lax.broadcasted_iota(I32, (tm, LANES), 1).astype(F32)
    work = logits
    idx, val = [], []
    for _ in range(TOP_K):
        mx = jnp.max(work, axis=-1, keepdims=True)
        ix = jnp.min(jnp.where(work == mx, lane, float(LANES)), axis=-1, keepdims=True)
        idx.append(ix)
        val.append(mx)
        work = jnp.where(lane == ix, -jnp.inf, work)
    ex = [jnp.exp(v - val[0]) for v in val]
    den = ex[0] + ex[1] + ex[2] + ex[3]
    wts = [e / den for e in ex]

    hot = jnp.zeros((tm, LANES), F32)
    for ix in idx:
        hot = hot + jnp.where(lane == ix, 1.0, 0.0)
    r = lax.broadcasted_iota(I32, (tm, tm), 0)
    c = lax.broadcasted_iota(I32, (tm, tm), 1)
    strict = jnp.where(r > c, 1.0, 0.0).astype(BF16)
    before = jnp.dot(strict, hot.astype(BF16), preferred_element_type=F32) + run_s[...]
    ranks = [jnp.sum(jnp.where(lane == ix, before, 0.0), axis=-1, keepdims=True) for ix in idx]
    run_s[...] = run_s[...] + jnp.sum(hot, axis=0, keepdims=True)

    ri = jnp.zeros((tm, LANES), F32)
    rw = jnp.zeros((tm, LANES), F32)
    for k in range(TOP_K):
        ri = jnp.where(lane == float(k), idx[k], ri)
        ri = jnp.where(lane == float(TOP_K + k), ranks[k], ri)
        rw = jnp.where(lane == float(k), wts[k], rw)
    ri_ref[...] = jnp.transpose(ri)[:2 * TOP_K, :].astype(I32)
    rw_ref[...] = rw
    cnt_ref[...] = run_s[...].astype(I32)


def _post_mixer(o, x, wo, mod, nw2, wr, br, seq, tm):
    t, d = x.shape
    per_seq = seq // tm
    mod_spec = lambda j: pl.BlockSpec((None, 1, d), lambda i: (i // per_seq, 0, j))
    row = lambda w: pl.BlockSpec((tm, w), lambda i: (i, 0))
    return pl.pallas_call(
        _post_kernel,
        grid=(t // tm,),
        in_specs=[
            row(d), row(d),
            _resident((d, d), lambda i: (0, 0)),
            mod_spec(2),
            _resident((1, d), lambda i: (0, 0)),
            mod_spec(3), mod_spec(4),
            _resident((d, LANES), lambda i: (0, 0)),
            _resident((1, LANES), lambda i: (0, 0)),
        ],
        out_specs=[row(d), row(d // 2), pl.BlockSpec((2 * TOP_K, tm), lambda i: (0, i)), row(LANES),
                   pl.BlockSpec((1, LANES), lambda i: (0, 0))],
        out_shape=[
            jax.ShapeDtypeStruct((t, d), F32),
            jax.ShapeDtypeStruct((t, d // 2), U32),
            jax.ShapeDtypeStruct((2 * TOP_K, t), I32),
            jax.ShapeDtypeStruct((t, LANES), F32),
            jax.ShapeDtypeStruct((1, LANES), I32),
        ],
        scratch_shapes=[pltpu.VMEM((1, LANES), F32)],
        compiler_params=_cparams("arbitrary"),
        name="post_mixer",
    )(o, x, wo, mod, nw2, mod, mod, wr, br)


def _sc_mesh():
    return plsc.VectorSubcoreMesh(core_axis_name="c", subcore_axis_name="s")


def _sc_scatter_rows(x, posf, n_rows):
    t, w = x.shape
    workers = SC_CORES * SC_SUBCORES
    per_w = t // workers

    @functools.partial(
        pl.kernel, mesh=_sc_mesh(),
        out_type=jax.ShapeDtypeStruct((n_rows, w), x.dtype),
        scratch_types=[pltpu.VMEM((SC_ROWS,), I32)] * TOP_K
        + [pltpu.VMEM((SC_ROWS, w), x.dtype), pltpu.SemaphoreType.DMA])
    def scatter_kernel(x_hbm, pos_hbm, out_hbm, i0, i1, i2, i3, rows_v, sem):
        idxs = (i0, i1, i2, i3)
        wid = lax.axis_index("s") * SC_CORES + lax.axis_index("c")
        base = wid * per_w

        @pl.loop(0, per_w // SC_ROWS)
        def _(i):
            off = base + i * SC_ROWS
            for k in range(TOP_K):
                pltpu.sync_copy(pos_hbm.at[pl.ds(k * t + off, SC_ROWS)], idxs[k])
            pltpu.sync_copy(x_hbm.at[pl.ds(off, SC_ROWS)], rows_v)
            copies = [pltpu.async_copy(rows_v, out_hbm.at[idxs[k]], sem) for k in range(TOP_K)]
            for cp in copies:
                cp.wait()

    return scatter_kernel(x, posf)


def _sc_gather_rows(src, posf):
    n = posf.shape[0]
    w = src.shape[1]
    workers = SC_CORES * SC_SUBCORES
    per_w = n // workers

    @functools.partial(
        pl.kernel, mesh=_sc_mesh(),
        out_type=jax.ShapeDtypeStruct((n, w), src.dtype),
        scratch_types=[pltpu.VMEM((SC_ROWS,), I32), pltpu.VMEM((SC_ROWS, w), src.dtype),
                       pltpu.SemaphoreType.DMA])
    def gather_kernel(src_hbm, pos_hbm, out_hbm, idx_v, rows_v, sem):
        wid = lax.axis_index("s") * SC_CORES + lax.axis_index("c")
        base = wid * per_w

        @pl.loop(0, per_w // SC_ROWS)
        def _(i):
            off = base + i * SC_ROWS
            pltpu.sync_copy(pos_hbm.at[pl.ds(off, SC_ROWS)], idx_v)
            pltpu.async_copy(src_hbm.at[idx_v], rows_v, sem).wait()
            pltpu.sync_copy(rows_v, out_hbm.at[pl.ds(off, SC_ROWS)])

    return gather_kernel(src, posf)


GU_BLOCK = 2 * LANES
FF_CHUNK = 2 * LANES


def _deinterleave_matrix():
    sel = np.zeros((GU_BLOCK, GU_BLOCK), np.float32)
    j = np.arange(LANES)
    sel[2 * j, j] = 1.0
    sel[2 * j + 1, LANES + j] = 1.0
    return jnp.asarray(sel, BF16)


def _expert_kernel(be_ref, nu_ref, x_ref, wgu_ref, bg_ref, bu_ref, wd_ref, bd_ref, sel_ref, y_ref,
                   wgu_s, wd_s):
    i = pl.program_id(0)
    d, f2 = wgu_s.shape
    f = f2 // 2
    new_expert = jnp.logical_or(i == 0, be_ref[i] != be_ref[jnp.maximum(i - 1, 0)])

    @pl.when(jnp.logical_and(new_expert, i < nu_ref[0]))
    def _():
        for c0 in range(0, f2, GU_BLOCK):
            blk = wgu_ref[0, :, c0:c0 + GU_BLOCK].astype(BF16)
            wgu_s[:, c0:c0 + GU_BLOCK] = jnp.dot(
                blk, sel_ref[...], preferred_element_type=F32).astype(BF16)
        wd_s[...] = wd_ref[0].astype(BF16)

    @pl.when(i < nu_ref[0])
    def _():
        half = x_ref.shape[1]
        lo, hi = _unpack_halves(x_ref[...])
        lo = lo.astype(BF16)
        hi = hi.astype(BF16)

        def hidden(c0):
            cols = slice(2 * c0, 2 * (c0 + FF_CHUNK))
            gu = (jnp.dot(lo, wgu_s[:half, cols], preferred_element_type=F32)
                  + jnp.dot(hi, wgu_s[half:, cols], preferred_element_type=F32))
            parts = []
            for b0 in range(0, FF_CHUNK, LANES):
                gate = gu[:, 2 * b0:2 * b0 + LANES] + bg_ref[0, :, c0 + b0:c0 + b0 + LANES]
                up = gu[:, 2 * b0 + LANES:2 * b0 + 2 * LANES] + bu_ref[0, :, c0 + b0:c0 + b0 + LANES]
                gate = jnp.minimum(gate, SWIGLU_LIMIT)
                up = jnp.clip(up, -SWIGLU_LIMIT, SWIGLU_LIMIT)
                parts.append((up + 1.0) * gate * _sigmoid(SWIGLU_ALPHA * gate))
            return jnp.concatenate(parts, axis=1).astype(BF16)

        chunks = list(range(0, f, FF_CHUNK))
        y = None
        h_next = hidden(chunks[0])
        for n, c0 in enumerate(chunks):
            h = h_next
            if n + 1 < len(chunks):
                h_next = hidden(chunks[n + 1])
            part = jnp.dot(h, wd_s[c0:c0 + FF_CHUNK, :], preferred_element_type=F32)
            y = part if y is None else y + part
        y_ref[...] = _pack_halves(y + bd_ref[0])

    @pl.when(i >= nu_ref[0])
    def _():
        y_ref[...] = jnp.zeros_like(y_ref)


def _expert_ffn(xs, blk_e, n_used, w_gu, bg, bu, w_down, bd, rb, layer):
    rows, half = xs.shape
    _, e, d, f2 = w_gu.shape
    f = f2 // 2
    n_blocks = rows // rb
    wspec = lambda shape: pl.BlockSpec(shape, lambda i, be, nu: (be[i], 0, 0))
    lspec = lambda shape: pl.BlockSpec((None,) + shape, lambda i, be, nu: (layer, be[i], 0, 0))
    return pl.pallas_call(
        _expert_kernel,
        grid_spec=pltpu.PrefetchScalarGridSpec(
            num_scalar_prefetch=2,
            grid=(n_blocks,),
            in_specs=[
                pl.BlockSpec((rb, half), lambda i, be, nu: (i, 0)),
                lspec((1, d, f2)),
                wspec((1, 1, f)), wspec((1, 1, f)),
                lspec((1, f, d)), wspec((1, 1, d)),
                _resident((GU_BLOCK, GU_BLOCK), lambda i, be, nu: (0, 0)),
            ],
            out_specs=pl.BlockSpec((rb, d // 2), lambda i, be, nu: (i, 0)),
            scratch_shapes=[pltpu.VMEM((d, f2), BF16), pltpu.VMEM((f, d), BF16)],
        ),
        out_shape=jax.ShapeDtypeStruct((rows, d // 2), U32),
        compiler_params=_cparams("arbitrary"),
        name="expert_ffn",
    )(blk_e, n_used, xs, w_gu, bg, bu, w_down, bd, _deinterleave_matrix())


def _combine_kernel(x_ref, y0_ref, y1_ref, y2_ref, y3_ref, rw_ref, g2_ref, o_ref):
    half = y0_ref.shape[1]
    rw = rw_ref[...]
    lane = lax.broadcasted_iota(I32, rw.shape, 1)
    acc_lo = jnp.zeros(y0_ref.shape, F32)
    acc_hi = jnp.zeros(y0_ref.shape, F32)
    for k, y_ref in enumerate((y0_ref, y1_ref, y2_ref, y3_ref)):
        wk = jnp.sum(jnp.where(lane == k, rw, 0.0), axis=-1, keepdims=True)
        lo, hi = _unpack_halves(y_ref[...])
        acc_lo = acc_lo + wk * lo
        acc_hi = acc_hi + wk * hi
    g2 = g2_ref[...]
    o_ref[:, :half] = x_ref[:, :half] + g2[:, :half] * acc_lo
    o_ref[:, half:] = x_ref[:, half:] + g2[:, half:] * acc_hi


def _combine(x1, yt, rw, mod, seq, tm):
    t, d = x1.shape
    per_seq = seq // tm
    nt = t // tm
    yspec = lambda k: pl.BlockSpec((tm, d // 2), lambda i: (k * nt + i, 0))
    return pl.pallas_call(
        _combine_kernel,
        grid=(nt,),
        in_specs=[
            pl.BlockSpec((tm, d), lambda i: (i, 0)),
            yspec(0), yspec(1), yspec(2), yspec(3),
            pl.BlockSpec((tm, LANES), lambda i: (i, 0)),
            pl.BlockSpec((None, 1, d), lambda i: (i // per_seq, 0, 5)),
        ],
        out_specs=pl.BlockSpec((tm, d), lambda i: (i, 0)),
        out_shape=jax.ShapeDtypeStruct((t, d), F32),
        compiler_params=_cparams("parallel"),
        name="moe_combine",
    )(x1, yt, yt, yt, yt, rw, mod)


def _row_tile(seq):
    return min(seq, 512)


def _moe_block_rows(t):
    return min(512, t * TOP_K // 8)


def kernel(x, c, fox_w_in, fox_b_f, fox_q_norm, fox_k_norm, fox_w_out, hgrn_w_in, hgrn_lb, hgrn_g_norm, hgrn_w_out, ada_w, ada_b, norm1_w, norm2_w, router_w, router_b, exp_w_gu, exp_b_gu, exp_w_down, exp_b_down):
    bsz, seq, d = x.shape
    t = bsz * seq
    depth = ada_w.shape[0]
    n_exp = router_w.shape[-1]
    f_dim = exp_w_down.shape[2]
    h_fox = fox_w_in.shape[-1] - 4 * d
    dh = d // h_fox
    tm = _row_tile(seq)
    n_split = 2 if bsz % 2 == 0 else 1
    bh, th = bsz // n_split, t // n_split
    rb = _moe_block_rows(th)
    n_blocks = (th * TOP_K) // rb + n_exp
    hgrn_hpb = 4 if d % (4 * LANES) == 0 else 2

    mod_all = _adaln_mod(c, ada_w, ada_b).reshape(depth, bsz, 1, -1)
    lb_soft = jax.nn.softmax(hgrn_lb.astype(F32), axis=0)
    lower = jnp.cumsum(lb_soft, axis=0) - lb_soft[0]
    experts = jnp.arange(n_exp, dtype=I32)

    xf = x.reshape(t, d)
    parts = [xf[s * th:(s + 1) * th] for s in range(n_split)]
    for i in range(depth):
        j = i // 2
        nw1 = norm1_w[i].reshape(1, d)
        if i % 2 == 0:
            w_in = fox_w_in[j]
            w_main = w_in[:, :4 * d].astype(BF16)
            w_f = jnp.pad(w_in[:, 4 * d:], ((0, 0), (0, LANES - h_fox))).astype(BF16)
            b_f = jnp.pad(fox_b_f[j], (0, LANES - h_fox)).reshape(1, LANES)
            qw = jnp.tile(fox_q_norm[j], LANES // dh).reshape(1, LANES)
            kw = jnp.tile(fox_k_norm[j], LANES // dh).reshape(1, LANES)
            w_out = fox_w_out[j].astype(BF16)
        else:
            w_main = hgrn_w_in[j].astype(BF16)
            w_out = hgrn_w_out[j].astype(BF16)
        wr = jnp.pad(router_w[i], ((0, 0), (0, LANES - n_exp))).astype(BF16)
        br = jnp.pad(router_b[i], (0, LANES - n_exp), constant_values=MASK_VALUE).reshape(1, LANES)
        b_gu = exp_b_gu[i].reshape(n_exp, 1, f_dim, 2)

        for s in range(n_split):
            xp = parts[s]
            mod = mod_all[i, s * bh:(s + 1) * bh]
            if i % 2 == 0:
                qkvg, flogit = _in_proj(xp, nw1, mod, w_main, BF16, seq, tm, w_f, b_f)
                o = _fox_attention(qkvg, flogit, qw, kw, seq, d, dh)
            else:
                (proj,) = _in_proj(xp, nw1, mod, w_main, F32, seq, tm)
                o = _hgrn_mixer(proj, lower[j].reshape(1, d), hgrn_g_norm[j].reshape(1, HGRN_DIM),
                                seq, d, hgrn_hpb)
            x1, hp, ri, rw, cnt = _post_mixer(o, xp, w_out, mod, norm2_w[i].reshape(1, d), wr, br,
                                              seq, tm)

            counts = cnt[0, :n_exp]
            padded = (counts + rb - 1) // rb * rb
            pad_end = jnp.cumsum(padded)
            pad_start = pad_end - padded
            top_idx, rank = ri[:TOP_K], ri[TOP_K:]
            start = jnp.sum(jnp.where(top_idx[None] == experts[:, None, None],
                                      pad_start[:, None, None], 0), axis=0)
            posf = (start + rank).reshape(-1).astype(I32)
            blk_lo = jnp.arange(n_blocks, dtype=I32) * rb
            blk_e = jnp.minimum(jnp.sum(pad_end[None, :] <= blk_lo[:, None], axis=1),
                                n_exp - 1).astype(I32)
            n_used = (pad_end[-1] // rb).astype(I32).reshape(1)

            xs = _sc_scatter_rows(hp, posf, n_blocks * rb)
            ys = _expert_ffn(xs, blk_e, n_used, exp_w_gu, b_gu[..., 0], b_gu[..., 1],
                             exp_w_down, exp_b_down[i].reshape(n_exp, 1, d), rb, i)
            yt = _sc_gather_rows(ys, posf)
            parts[s] = _combine(x1, yt, rw, mod, seq, tm)
    return jnp.concatenate(parts, axis=0).reshape(bsz, seq, d)
```

```python
import functools

import jax
import jax.numpy as jnp
import numpy as np
from jax import lax
from jax.experimental import pallas as pl
from jax.experimental.pallas import tpu as pltpu
from jax.experimental.pallas import tpu_sc as plsc

F32 = jnp.float32
BF16 = jnp.bfloat16
U32 = jnp.uint32
I32 = jnp.int32

EPS = 1e-6
MASK_VALUE = -1e30
TOP_K = 4
HGRN_DIM = 128
HGRN_CHUNK = 32
SWIGLU_LIMIT = 7.0
SWIGLU_ALPHA = 1.702
LANES = 128
SUBLANES = 8
SC_CORES = 2
SC_SUBCORES = 16
SC_ROWS = 64
VMEM_LIMIT = 56 * 1024 * 1024
HIGHEST = lax.Precision.HIGHEST


def _cparams(*sem):
    return pltpu.CompilerParams(dimension_semantics=sem, vmem_limit_bytes=VMEM_LIMIT)


def _resident(shape, index_map):
    return pl.BlockSpec(shape, index_map, pipeline_mode=pl.Buffered(1))


def _sigmoid(x):
    return 1.0 / (1.0 + jnp.exp(-x))


def _bf16_bits(x):
    return lax.bitcast_convert_type(x.astype(BF16).astype(F32), U32)


def _pack_halves(x):
    n = x.shape[1] // 2
    lo = _bf16_bits(x[:, :n])
    hi = _bf16_bits(x[:, n:])
    return (hi & jnp.uint32(0xFFFF0000)) | (lo >> 16)


def _unpack_halves(w):
    lo = lax.bitcast_convert_type(w << 16, F32)
    hi = lax.bitcast_convert_type(w & jnp.uint32(0xFFFF0000), F32)
    return lo, hi


def _mod_kernel(c_ref, w_ref, b_ref, o_ref):
    c = c_ref[...]
    ca = c * _sigmoid(c)
    o_ref[0] = jnp.dot(ca, w_ref[0], preferred_element_type=F32, precision=HIGHEST) + b_ref[0]


def _adaln_mod(c, ada_w, ada_b):
    depth, d, n6 = ada_w.shape
    b = c.shape[0]
    nmod = n6 // d
    return pl.pallas_call(
        _mod_kernel,
        grid=(depth, nmod),
        in_specs=[
            pl.BlockSpec((b, d), lambda i, j: (0, 0)),
            pl.BlockSpec((1, d, d), lambda i, j: (i, 0, j)),
            pl.BlockSpec((1, 1, d), lambda i, j: (i, 0, j)),
        ],
        out_specs=pl.BlockSpec((1, b, d), lambda i, j: (i, 0, j)),
        out_shape=jax.ShapeDtypeStruct((depth, b, n6), F32),
        compiler_params=_cparams("parallel", "parallel"),
        name="adaln_mod",
    )(c, ada_w, ada_b.reshape(depth, 1, n6))


def _norm_mod(x, nw, sc, sh):
    ms = jnp.mean(x * x, axis=-1, keepdims=True)
    return x * lax.rsqrt(ms + EPS) * nw * (1.0 + sc) + sh


def _in_kernel(x_ref, nw_ref, sh_ref, sc_ref, w_ref, *rest, n_chunk, with_f):
    if with_f:
        wf_ref, bf_ref, o_ref, of_ref = rest
    else:
        (o_ref,) = rest
    h = _norm_mod(x_ref[...], nw_ref[...], sc_ref[...], sh_ref[...]).astype(BF16)
    n = o_ref.shape[1]
    for c0 in range(0, n, n_chunk):
        o_ref[:, c0:c0 + n_chunk] = jnp.dot(
            h, w_ref[:, c0:c0 + n_chunk], preferred_element_type=F32).astype(o_ref.dtype)
    if with_f:
        of_ref[...] = jnp.dot(h, wf_ref[...], preferred_element_type=F32) + bf_ref[...]


def _in_proj(x, nw, mod, w, out_dtype, seq, tm, wf=None, bf=None):
    t, d = x.shape
    n = w.shape[1]
    per_seq = seq // tm
    with_f = wf is not None
    mod_spec = lambda j: pl.BlockSpec((None, 1, d), lambda i: (i // per_seq, 0, j))
    in_specs = [
        pl.BlockSpec((tm, d), lambda i: (i, 0)),
        _resident((1, d), lambda i: (0, 0)),
        mod_spec(0), mod_spec(1),
        _resident((d, n), lambda i: (0, 0)),
    ]
    out_specs = [pl.BlockSpec((tm, n), lambda i: (i, 0))]
    out_shape = [jax.ShapeDtypeStruct((t, n), out_dtype)]
    args = [x, nw, mod, mod, w]
    if with_f:
        in_specs += [_resident((d, LANES), lambda i: (0, 0)), _resident((1, LANES), lambda i: (0, 0))]
        out_specs.append(pl.BlockSpec((tm, LANES), lambda i: (i, 0)))
        out_shape.append(jax.ShapeDtypeStruct((t, LANES), F32))
        args += [wf, bf]
    return pl.pallas_call(
        functools.partial(_in_kernel, n_chunk=min(n, 1024), with_f=with_f),
        grid=(t // tm,),
        in_specs=in_specs,
        out_specs=out_specs,
        out_shape=out_shape,
        compiler_params=_cparams("parallel"),
        name="in_proj_fox" if with_f else "in_proj_hgrn",
    )(*args)


LOG2E = 1.4426950408889634
EXP2_SPAN = 120.0
SCORE_MARGIN = 1.01
BIAS_LANES = 16


def _split3(c):
    c1 = c.astype(BF16).astype(F32)
    r = c - c1
    c2 = r.astype(BF16).astype(F32)
    return c1, c2, r - c2


def _bias_layout(n_pairs):
    route =np.zeros((3, 3 * LANES, LANES), np.float32)
    const = np.zeros((3, 1, LANES), np.float32)
    for p in range(n_pairs):
        base = p * BIAS_LANES
        for i in range(3):
            const[0, 0, base + i] = 1.0
            for h in range(2):
                route[0, i * LANES + 2 * p + h, base + 3 + 3 * h + i] = -1.0
                route[1 + h, i * LANES + 2 * p + h, base + i] = 1.0
                const[1 + h, 0, base + 3 + 3 * h + i] = 1.0
    return jnp.asarray(route, BF16), jnp.asarray(const, F32)


def _cum_kernel(f_ref, ref_ref, route_ref, const_ref, o_ref, *, rows):
    s = f_ref.shape[0]
    r = lax.broadcasted_iota(I32, (rows, rows), 0)
    c = lax.broadcasted_iota(I32, (rows, rows), 1)
    tri = (r >= c).astype(F32)
    carry = jnp.zeros((1, LANES), F32)
    for r0 in range(0, s, rows):
        x = f_ref[r0:r0 + rows, :]
        lf = jnp.minimum(x, 0.0) - jnp.log(1.0 + jnp.exp(-jnp.abs(x)))
        cs = jnp.dot(tri, lf, preferred_element_type=F32, precision=HIGHEST) + carry
        carry = cs[rows - 1:rows, :]
        c2 = cs * LOG2E
        k_terms = jnp.concatenate(_split3(c2), axis=1).astype(BF16)
        q_terms = jnp.concatenate(_split3(c2 - ref_ref[...]), axis=1).astype(BF16)
        for j, terms in enumerate((k_terms, q_terms, q_terms)):
            routed = jnp.dot(terms, route_ref[j], preferred_element_type=F32) + const_ref[j]
            o_ref[r0:r0 + rows, j * LANES:(j + 1) * LANES] = routed.astype(o_ref.dtype)


def _fox_bias(flogit, ref, seq, n_pairs):
    t = flogit.shape[0]
    route, const = _bias_layout(n_pairs)
    return pl.pallas_call(
        functools.partial(_cum_kernel, rows=min(seq, 128)),
        grid=(t // seq,),
        in_specs=[pl.BlockSpec((seq, LANES), lambda b: (b, 0)),
                  pl.BlockSpec((1, LANES), lambda b: (0, 0)),
                  pl.BlockSpec((3, 3 * LANES, LANES), lambda b: (0, 0, 0)),
                  pl.BlockSpec((3, 1, LANES), lambda b: (0, 0, 0))],
        out_specs=pl.BlockSpec((seq, 3 * LANES), lambda b: (b, 0)),
        out_shape=jax.ShapeDtypeStruct((t, 3 * LANES), BF16),
        compiler_params=_cparams("parallel"),
        name="fox_bias",
    )(flogit, ref, route, const)


def _attn_kernel(q_ref, k_ref, v_ref, g_ref, bias_ref, qw_ref, kw_ref, o_ref, qs_s, ka_s, *,
                 dh, tq, bounded):
    s_len = q_ref.shape[0]
    nq = s_len // tq
    hp = pl.program_id(1)
    lane = lax.broadcasted_iota(I32, (1, LANES), 1)
    first = lane < dh

    def headnorm(z, w):
        z2 = z * z
        s0 = jnp.sum(jnp.where(first, z2, 0.0), axis=-1, keepdims=True)
        s1 = jnp.sum(jnp.where(first, 0.0, z2), axis=-1, keepdims=True)
        ms = jnp.where(first, s0, s1) * (1.0 / dh)
        return z * lax.rsqrt(ms + EPS) * w

    qn = headnorm(q_ref[...].astype(F32), qw_ref[...]) * (dh ** -0.5 * LOG2E)
    kn = headnorm(k_ref[...].astype(F32), kw_ref[...])
    mine = (lane // BIAS_LANES) == hp
    ka_s[:, :LANES] = kn.astype(BF16)
    ka_s[:, LANES:] = bias_ref[:, :LANES]
    q_lo = [jnp.where(first, qn, 0.0), jnp.where(first, 0.0, qn)]
    for qi in range(nq):
        rows = slice(qi * tq, (qi + 1) * tq)
        for h in range(2):
            q_hi = bias_ref[rows, (1 + h) * LANES:(2 + h) * LANES].astype(F32)
            qs_s[qi, h * tq:(h + 1) * tq, :LANES] = q_lo[h][rows, :].astype(BF16)
            qs_s[qi, h * tq:(h + 1) * tq, LANES:] = jnp.where(mine, q_hi, 0.0).astype(BF16)

    nt = (((1,), (1,)), ((), ()))

    def causal(s):
        rr = lax.broadcasted_iota(I32, (2 * tq, tq), 0)
        rr = jnp.where(rr >= tq, rr - tq, rr)
        cc = lax.broadcasted_iota(I32, (2 * tq, tq), 1)
        return jnp.where(rr >= cc, s, MASK_VALUE)

    def finish(qi, acc, l):
        o2 = acc / l
        o = jnp.where(first, o2[:tq], o2[tq:])
        g = g_ref[qi * tq:(qi + 1) * tq, :].astype(F32)
        o_ref[qi * tq:(qi + 1) * tq, :] = (o * _sigmoid(g)).astype(o_ref.dtype)

    if bounded:
        for qi in range(nq):
            acc = part = None
            for t in range(qi + 1):
                s = lax.dot_general(qs_s[qi], ka_s[t * tq:(t + 1) * tq, :], nt,
                                    preferred_element_type=F32)
                e = jnp.exp2(causal(s) if t == qi else s)
                cols = e[:, :LANES]
                for c0 in range(LANES, tq, LANES):
                    cols = cols + e[:, c0:c0 + LANES]
                pv = jnp.dot(e.astype(BF16), v_ref[t * tq:(t + 1) * tq, :],
                             preferred_element_type=F32)
                acc = pv if t == 0 else acc + pv
                part = cols if t == 0 else part + cols
            finish(qi, acc, jnp.sum(part, axis=-1, keepdims=True))
        return

    def scores(q, off):
        return lax.dot_general(q, ka_s[pl.ds(off, tq), :], nt, preferred_element_type=F32)

    def update(s, off, carry, masked):
        m, l, acc = carry
        if masked:
            s = causal(s)
        m_new = jnp.maximum(m, jnp.max(s, axis=-1, keepdims=True))
        p = jnp.exp2(s - m_new)
        alpha = jnp.exp2(m - m_new)
        l = alpha * l + jnp.sum(p, axis=-1, keepdims=True)
        acc = alpha * acc + jnp.dot(p.astype(BF16), v_ref[pl.ds(off, tq), :],
                                    preferred_element_type=F32)
        return m_new, l, acc

    for qi in range(nq):
        q = qs_s[qi]
        s = scores(q, 0)
        carry = (jnp.full((2 * tq, 1), MASK_VALUE, F32), jnp.zeros((2 * tq, 1), F32),
                 jnp.zeros((2 * tq, LANES), F32))
        if qi > 0:
            def body(j, c, q=q):
                off = pl.multiple_of(j * tq, tq)
                s_next = scores(q, off + tq)
                return (s_next,) + update(c[0], off, c[1:], False)

            s, *carry = lax.fori_loop(0, qi, body, (s,) + carry)
        m, l, acc = update(s, qi * tq, carry, True)
        finish(qi, acc, l)


def _fox_attention(qkvg, flogit, qw, kw, seq, d, dh):
    t = qkvg.shape[0]
    nb = d // LANES
    assert 2 * dh == LANES and nb * BIAS_LANES <= LANES
    tq = min(seq, 512)
    col = lambda off: pl.BlockSpec((seq, LANES), lambda b, p: (b, off + p))

    def call(bounded, bias):
        return pl.pallas_call(
            functools.partial(_attn_kernel, dh=dh, tq=tq, bounded=bounded),
            grid=(t // seq, nb),
            in_specs=[
                col(0), col(nb), col(2 * nb), col(3 * nb),
                pl.BlockSpec((seq, 3 * LANES), lambda b, p: (b, 0)),
                pl.BlockSpec((1, LANES), lambda b, p: (0, 0)),
                pl.BlockSpec((1, LANES), lambda b, p: (0, 0)),
            ],
            out_specs=pl.BlockSpec((seq, LANES), lambda b, p: (b, p)),
            out_shape=jax.ShapeDtypeStruct((t, d), BF16),
            scratch_shapes=[pltpu.VMEM((seq // tq, 2 * tq, 2 * LANES), BF16),
                            pltpu.VMEM((seq, 2 * LANES), BF16)],
            compiler_params=_cparams("parallel", "parallel"),
            name="fox_attention" if bounded else "fox_attention_running_max",
        )(qkvg, qkvg, qkvg, qkvg, bias, qw, kw)

    bound = (dh ** 0.5 * LOG2E * SCORE_MARGIN) * jnp.max(jnp.abs(qw)) * jnp.max(jnp.abs(kw)) + 0.1
    fits = bound <= EXP2_SPAN - 4.0
    ref = jnp.where(fits, bound - jnp.maximum(0.0, 2.0 * bound - EXP2_SPAN), 0.0)
    bias = _fox_bias(flogit, jnp.full((1, LANES), ref, F32), seq, nb)
    return lax.cond(fits, lambda: call(True, bias), lambda: call(False, bias))


def _hgrn_kernel(q_ref, f_ref, v_ref, g_ref, lb_ref, gw_ref, o_ref, *, hpb, chunk):
    s_len = q_ref.shape[0]
    pair = 2
    groups = chunk // SUBLANES
    off_w = (groups - 1) * chunk
    r = lax.broadcasted_iota(I32, (chunk, chunk), 0)
    c = lax.broadcasted_iota(I32, (chunk, chunk), 1)
    tri = jnp.where(r >= c, 1.0, 0.0).astype(BF16)
    ro = lax.broadcasted_iota(I32, (chunk, off_w), 0) // SUBLANES
    co = lax.broadcasted_iota(I32, (chunk, off_w), 1) // chunk
    keep = ro == co + 1
    sub = lax.broadcasted_iota(I32, (SUBLANES, LANES), 0)
    gw = gw_ref[...]
    nt = (((1,), (1,)), ((), ()))
    grp = lambda a, i: a[i * SUBLANES:(i + 1) * SUBLANES, :]
    zeros8 = jnp.zeros((SUBLANES, LANES), F32)

    def body(step, states):
        insts = [(ch, hd) for ch in range(pair) for hd in range(hpb)]
        rows = {ch: pl.ds(pl.multiple_of((step * pair + ch) * chunk, chunk), chunk)
                for ch in range(pair)}
        cols = {hd: slice(hd * LANES, (hd + 1) * LANES) for hd in range(hpb)}
        st = {}
        for ch, hd in insts:
            lb = lb_ref[:, cols[hd]]
            q_raw = q_ref[rows[ch], cols[hd]]
            sig = _sigmoid(f_ref[rows[ch], cols[hd]])
            d = dict(v=v_ref[rows[ch], cols[hd]], q=q_raw * _sigmoid(q_raw),
                     k=(1.0 - lb) * (1.0 - sig))
            lf = jnp.log(lb + (1.0 - lb) * sig)
            d["b"] = sum(jnp.dot(tri, part.astype(BF16), preferred_element_type=F32)
                         for part in _split3(lf))
            st[ch, hd] = d
        def from_state(ch, hd, state_t):
            d = st[ch, hd]
            qe = (d["q"] * jnp.exp(d["b"])).astype(BF16)
            d["o"] = lax.dot_general(qe, state_t.astype(BF16), nt, preferred_element_type=F32)
        for hd in range(hpb):
            from_state(0, hd, states[hd])
        for ch, hd in insts:
            d = st[ch, hd]
            b, q, k = d["b"], d["q"], d["k"]
            ends = [b[(i + 1) * SUBLANES - 1:(i + 1) * SUBLANES, :] for i in range(groups)]
            kt = [grp(k, i) * jnp.exp(ends[i] - grp(b, i)) for i in range(groups)]
            qh = jnp.concatenate(
                [zeros8] + [grp(q, i) * jnp.exp(grp(b, i) - ends[i - 1]) for i in range(1, groups)],
                axis=0).astype(BF16)
            kh = jnp.concatenate(
                [kt[j] * jnp.exp(ends[i - 1] - ends[j]) if j < i else zeros8
                 for i in range(1, groups) for j in range(groups)], axis=0).astype(BF16)
            d["a"] = lax.dot_general(qh, kh, nt, preferred_element_type=F32)
            d["ends"], d["kt"] = ends, kt
        def state_update(ch, hd, state_t):
            d = st[ch, hd]
            b_end = d["ends"][-1]
            ke = jnp.concatenate([d["kt"][j] * jnp.exp(b_end - d["ends"][j]) for j in range(groups)],
                                 axis=0).astype(BF16)
            upd = jnp.dot(jnp.transpose(d["v"]).astype(BF16), ke, preferred_element_type=F32)
            return jnp.exp(b_end) * state_t + upd
        states = [state_update(0, hd, states[hd]) for hd in range(hpb)]
        for ch, hd in insts:
            d = st[ch, hd]
            vb = d["v"].astype(BF16)
            v3 = jnp.concatenate([vb] * (groups - 1), axis=0)
            d["off"] = jnp.dot(jnp.where(keep, d["a"], 0.0).astype(BF16), v3,
                               preferred_element_type=F32)
        for hd in range(hpb):
            from_state(1, hd, states[hd])
        states = [state_update(1, hd, states[hd]) for hd in range(hpb)]
        for ch, hd in insts:
            d = st[ch, hd]
            o = d["o"] + d["off"]
            o_g = []
            for gi in range(groups):
                qg, kg, vg, bg = grp(d["q"], gi), grp(d["k"], gi), grp(d["v"], gi), grp(d["b"], gi)
                og = grp(o, gi)
                for s in range(SUBLANES):
                    live = sub >= s
                    e = jnp.exp(jnp.where(live, bg - bg[s:s + 1, :], 0.0))
                    p = jnp.where(live, qg * e * kg[s:s + 1, :], 0.0)
                    og = og + jnp.sum(p, axis=-1, keepdims=True) * vg[s:s + 1, :]
                o_g.append(og)
            o = jnp.concatenate(o_g, axis=0)
            g = g_ref[rows[ch], cols[hd]]
            ms = jnp.mean(o * o, axis=-1, keepdims=True)
            y = o * lax.rsqrt(ms + EPS) * gw * (g * _sigmoid(g))
            o_ref[rows[ch], cols[hd]] = y.astype(o_ref.dtype)
        return tuple(states)

    init = tuple(jnp.zeros((LANES, LANES), F32) for _ in range(hpb))
    lax.fori_loop(0, s_len // (chunk * pair), body, init)


def _hgrn_mixer(proj, lower, gw, seq, d, hpb):
    t = proj.shape[0]
    w = hpb * LANES
    nb = d // w
    col = lambda off: pl.BlockSpec((seq, w), lambda b, p: (b, off + p))
    return pl.pallas_call(
        functools.partial(_hgrn_kernel, hpb=hpb, chunk=HGRN_CHUNK),
        grid=(t // seq, nb),
        in_specs=[
            col(0), col(nb), col(2 * nb), col(3 * nb),
            pl.BlockSpec((1, w), lambda b, p: (0, p)),
            pl.BlockSpec((1, LANES), lambda b, p: (0, 0)),
        ],
        out_specs=pl.BlockSpec((seq, w), lambda b, p: (b, p)),
        out_shape=jax.ShapeDtypeStruct((t, d), BF16),
        compiler_params=_cparams("parallel", "parallel"),
        name="hgrn_mixer",
    )(proj, proj, proj, proj, lower, gw)


def _post_kernel(o_ref, x_ref, wo_ref, g1_ref, nw_ref, sh_ref, sc_ref, wr_ref, br_ref,
                 x1_ref, hp_ref, ri_ref, rw_ref, cnt_ref, run_s):
    i = pl.program_id(0)
    tm = x_ref.shape[0]

    @pl.when(i == 0)
    def _():
        run_s[...] = jnp.zeros_like(run_s)

    y = jnp.dot(o_ref[...], wo_ref[...], preferred_element_type=F32)
    x1 = x_ref[...] + g1_ref[...] * y
    x1_ref[...] = x1
    h2 = _norm_mod(x1, nw_ref[...], sc_ref[...], sh_ref[...])
    hp_ref[...] = _pack_halves(h2)
    logits = jnp.dot(h2.astype(BF16), wr_ref[...], preferred_element_type=F32) + br_ref[...]

    lane = lax.broadcasted_iota(I32, (tm, LANES), 1).astype(F32)
    work = logits
    idx, val = [], []
    for _ in range(TOP_K):
        mx = jnp.max(work, axis=-1, keepdims=True)
        ix = jnp.min(jnp.where(work == mx, lane, float(LANES)), axis=-1, keepdims=True)
        idx.append(ix)
        val.append(mx)
        work = jnp.where(lane == ix, -jnp.inf, work)
    ex = [jnp.exp(v - val[0]) for v in val]
    den = ex[0] + ex[1] + ex[2] + ex[3]
    wts = [e / den for e in ex]

    hot = jnp.zeros((tm, LANES), F32)
    for ix in idx:
        hot = hot + jnp.where(lane == ix, 1.0, 0.0)
    r = lax.broadcasted_iota(I32, (tm, tm), 0)
    c = lax.broadcasted_iota(I32, (tm, tm), 1)
    strict = jnp.where(r > c, 1.0, 0.0).astype(BF16)
    before = jnp.dot(strict, hot.astype(BF16), preferred_element_type=F32) + run_s[...]
    ranks = [jnp.sum(jnp.where(lane == ix, before, 0.0), axis=-1, keepdims=True) for ix in idx]
    run_s[...] = run_s[...] + jnp.sum(hot, axis=0, keepdims=True)

    ri = jnp.zeros((tm, LANES), F32)
    rw = jnp.zeros((tm, LANES), F32)
    for k in range(TOP_K):
        ri = jnp.where(lane == float(k), idx[k], ri)
        ri = jnp.where(lane == float(TOP_K + k), ranks[k], ri)
        rw = jnp.where(lane == float(k), wts[k], rw)
    ri_ref[...] = jnp.transpose(ri)[:2 * TOP_K, :].astype(I32)
    rw_ref[...] = rw
    cnt_ref[...] = run_s[...].astype(I32)


def _post_mixer(o, x, wo, mod, nw2, wr, br, seq, tm):
    t, d = x.shape
    per_seq = seq // tm
    mod_spec = lambda j: pl.BlockSpec((None, 1, d), lambda i: (i // per_seq, 0, j))
    row = lambda w: pl.BlockSpec((tm, w), lambda i: (i, 0))
    return pl.pallas_call(
        _post_kernel,
        grid=(t // tm,),
        in_specs=[
            row(d), row(d),
            _resident((d, d), lambda i: (0, 0)),
            mod_spec(2),
            _resident((1, d), lambda i: (0, 0)),
            mod_spec(3), mod_spec(4),
            _resident((d, LANES), lambda i: (0, 0)),
            _resident((1, LANES), lambda i: (0, 0)),
        ],
        out_specs=[row(d), row(d // 2), pl.BlockSpec((2 * TOP_K, tm), lambda i: (0, i)), row(LANES),
                   pl.BlockSpec((1, LANES), lambda i: (0, 0))],
        out_shape=[
            jax.ShapeDtypeStruct((t, d), F32),
            jax.ShapeDtypeStruct((t, d // 2), U32),
            jax.ShapeDtypeStruct((2 * TOP_K, t), I32),
            jax.ShapeDtypeStruct((t, LANES), F32),
            jax.ShapeDtypeStruct((1, LANES), I32),
        ],
        scratch_shapes=[pltpu.VMEM((1, LANES), F32)],
        compiler_params=_cparams("arbitrary"),
        name="post_mixer",
    )(o, x, wo, mod, nw2, mod, mod, wr, br)


def _sc_mesh():
    return plsc.VectorSubcoreMesh(core_axis_name="c", subcore_axis_name="s")


def _sc_scatter_rows(x, posf, n_rows):
    t, w = x.shape
    workers = SC_CORES * SC_SUBCORES
    per_w = t // workers

    @functools.partial(
        pl.kernel, mesh=_sc_mesh(),
        out_type=jax.ShapeDtypeStruct((n_rows, w), x.dtype),
        scratch_types=[pltpu.VMEM((SC_ROWS,), I32)] * TOP_K
        + [pltpu.VMEM((SC_ROWS, w), x.dtype), pltpu.SemaphoreType.DMA])
    def scatter_kernel(x_hbm, pos_hbm, out_hbm, i0, i1, i2, i3, rows_v, sem):
        idxs = (i0, i1, i2, i3)
        wid = lax.axis_index("s") * SC_CORES + lax.axis_index("c")
        base = wid * per_w

        @pl.loop(0, per_w // SC_ROWS)
        def _(i):
            off = base + i * SC_ROWS
            for k in range(TOP_K):
                pltpu.sync_copy(pos_hbm.at[pl.ds(k * t + off, SC_ROWS)], idxs[k])
            pltpu.sync_copy(x_hbm.at[pl.ds(off, SC_ROWS)], rows_v)
            copies = [pltpu.async_copy(rows_v, out_hbm.at[idxs[k]], sem) for k in range(TOP_K)]
            for cp in copies:
                cp.wait()

    return scatter_kernel(x, posf)


def _sc_gather_rows(src, posf):
    n = posf.shape[0]
    w = src.shape[1]
    workers = SC_CORES * SC_SUBCORES
    per_w = n // workers

    @functools.partial(
        pl.kernel, mesh=_sc_mesh(),
        out_type=jax.ShapeDtypeStruct((n, w), src.dtype),
        scratch_types=[pltpu.VMEM((SC_ROWS,), I32), pltpu.VMEM((SC_ROWS, w), src.dtype),
                       pltpu.SemaphoreType.DMA])
    def gather_kernel(src_hbm, pos_hbm, out_hbm, idx_v, rows_v, sem):
        wid = lax.axis_index("s") * SC_CORES + lax.axis_index("c")
        base = wid * per_w

        @pl.loop(0, per_w // SC_ROWS)
        def _(i):
            off = base + i * SC_ROWS
            pltpu.sync_copy(pos_hbm.at[pl.ds(off, SC_ROWS)], idx_v)
            pltpu.async_copy(src_hbm.at[idx_v], rows_v, sem).wait()
            pltpu.sync_copy(rows_v, out_hbm.at[pl.ds(off, SC_ROWS)])

    return gather_kernel(src, posf)


GU_BLOCK = 2 * LANES
FF_CHUNK = 2 * LANES


def _deinterleave_matrix():
    sel = np.zeros((GU_BLOCK, GU_BLOCK), np.float32)
    j = np.arange(LANES)
    sel[2 * j, j] = 1.0
    sel[2 * j + 1, LANES + j] = 1.0
    return jnp.asarray(sel, BF16)


def _expert_kernel(be_ref, nu_ref, nxt_ref, x_ref, wgu_hbm, bg_ref, bu_ref, wd_hbm, bd_ref, sel_ref,
                   y_ref, wgu_f, wd_f, wgu_s, wd_s, sems, *, layer):
    i = pl.program_id(0)
    d, f2 = wgu_s.shape
    f = f2 // 2
    e_cur = be_ref[i]
    new_expert = jnp.logical_or(i == 0, e_cur != be_ref[jnp.maximum(i - 1, 0)])

    def weight_copies(e):
        return (pltpu.make_async_copy(wgu_hbm.at[layer, e], wgu_f, sems.at[0]),
                pltpu.make_async_copy(wd_hbm.at[layer, e], wd_f, sems.at[1]))

    @pl.when(jnp.logical_and(new_expert, i < nu_ref[0]))
    def _():
        @pl.when(i == 0)
        def _():
            for cp in weight_copies(e_cur):
                cp.start()

        for cp in weight_copies(e_cur):
            cp.wait()
        for c0 in range(0, f2, GU_BLOCK):
            blk = wgu_f[:, c0:c0 + GU_BLOCK].astype(BF16)
            wgu_s[:, c0:c0 + GU_BLOCK] = jnp.dot(
                blk, sel_ref[...], preferred_element_type=F32).astype(BF16)
        wd_s[...] = wd_f[...].astype(BF16)

        @pl.when(nxt_ref[e_cur] >= 0)
        def _():
            for cp in weight_copies(nxt_ref[e_cur]):
                cp.start()

    @pl.when(i < nu_ref[0])
    def _():
        half = x_ref.shape[1]
        lo, hi = _unpack_halves(x_ref[...])
        lo = lo.astype(BF16)
        hi = hi.astype(BF16)

        def hidden(c0):
            cols = slice(2 * c0, 2 * (c0 + FF_CHUNK))
            gu = (jnp.dot(lo, wgu_s[:half, cols], preferred_element_type=F32)
                  + jnp.dot(hi, wgu_s[half:, cols], preferred_element_type=F32))
            parts = []
            for b0 in range(0, FF_CHUNK, LANES):
                gate = gu[:, 2 * b0:2 * b0 + LANES] + bg_ref[0, :, c0 + b0:c0 + b0 + LANES]
                up = gu[:, 2 * b0 + LANES:2 * b0 + 2 * LANES] + bu_ref[0, :, c0 + b0:c0 + b0 + LANES]
                gate = jnp.minimum(gate, SWIGLU_LIMIT)
                up = jnp.clip(up, -SWIGLU_LIMIT, SWIGLU_LIMIT)
                parts.append((up + 1.0) * gate * _sigmoid(SWIGLU_ALPHA * gate))
            return jnp.concatenate(parts, axis=1).astype(BF16)

        chunks = list(range(0, f, FF_CHUNK))
        y = None
        h_next = hidden(chunks[0])
        for n, c0 in enumerate(chunks):
            h = h_next
            if n + 1 < len(chunks):
                h_next = hidden(chunks[n + 1])
            part = jnp.dot(h, wd_s[c0:c0 + FF_CHUNK, :], preferred_element_type=F32)
            y = part if y is None else y + part
        y_ref[...] = _pack_halves(y + bd_ref[0])

    @pl.when(i >= nu_ref[0])
    def _():
        y_ref[...] = jnp.zeros_like(y_ref)


def _expert_ffn(xs, blk_e, n_used, nxt_e, w_gu, bg, bu, w_down, bd, rb, layer):
    rows, half = xs.shape
    _, e, d, f2 = w_gu.shape
    f = f2 // 2
    n_blocks = rows // rb
    wspec = lambda shape: pl.BlockSpec(shape, lambda i, be, nu, nx: (be[i], 0, 0))
    return pl.pallas_call(
        functools.partial(_expert_kernel, layer=layer),
        grid_spec=pltpu.PrefetchScalarGridSpec(
            num_scalar_prefetch=3,
            grid=(n_blocks,),
            in_specs=[
                pl.BlockSpec((rb, half), lambda i, be, nu, nx: (i, 0)),
                pl.BlockSpec(memory_space=pl.ANY),
                wspec((1, 1, f)), wspec((1, 1, f)),
                pl.BlockSpec(memory_space=pl.ANY),
                wspec((1, 1, d)),
                _resident((GU_BLOCK, GU_BLOCK), lambda i, be, nu, nx: (0, 0)),
            ],
            out_specs=pl.BlockSpec((rb, d // 2), lambda i, be, nu, nx: (i, 0)),
            scratch_shapes=[pltpu.VMEM((d, f2), F32), pltpu.VMEM((f, d), F32),
                            pltpu.VMEM((d, f2), BF16), pltpu.VMEM((f, d), BF16),
                            pltpu.SemaphoreType.DMA((2,))],
        ),
        out_shape=jax.ShapeDtypeStruct((rows, d // 2), U32),
        compiler_params=_cparams("arbitrary"),
        name="expert_ffn",
    )(blk_e, n_used, nxt_e, xs, w_gu, bg, bu, w_down, bd, _deinterleave_matrix())


def _combine_kernel(x_ref, y0_ref, y1_ref, y2_ref, y3_ref, rw_ref, g2_ref, o_ref):
    half = y0_ref.shape[1]
    rw = rw_ref[...]
    lane = lax.broadcasted_iota(I32, rw.shape, 1)
    acc_lo = jnp.zeros(y0_ref.shape, F32)
    acc_hi = jnp.zeros(y0_ref.shape, F32)
    for k, y_ref in enumerate((y0_ref, y1_ref, y2_ref, y3_ref)):
        wk = jnp.sum(jnp.where(lane == k, rw, 0.0), axis=-1, keepdims=True)
        lo, hi = _unpack_halves(y_ref[...])
        acc_lo = acc_lo + wk * lo
        acc_hi = acc_hi + wk * hi
    g2 = g2_ref[...]
    o_ref[:, :half] = x_ref[:, :half] + g2[:, :half] * acc_lo
    o_ref[:, half:] = x_ref[:, half:] + g2[:, half:] * acc_hi


def _combine(x1, yt, rw, mod, seq, tm):
    t, d = x1.shape
    per_seq = seq // tm
    nt = t // tm
    yspec = lambda k: pl.BlockSpec((tm, d // 2), lambda i: (k * nt + i, 0))
    return pl.pallas_call(
        _combine_kernel,
        grid=(nt,),
        in_specs=[
            pl.BlockSpec((tm, d), lambda i: (i, 0)),
            yspec(0), yspec(1), yspec(2), yspec(3),
            pl.BlockSpec((tm, LANES), lambda i: (i, 0)),
            pl.BlockSpec((None, 1, d), lambda i: (i // per_seq, 0, 5)),
        ],
        out_specs=pl.BlockSpec((tm, d), lambda i: (i, 0)),
        out_shape=jax.ShapeDtypeStruct((t, d), F32),
        compiler_params=_cparams("parallel"),
        name="moe_combine",
    )(x1, yt, yt, yt, yt, rw, mod)


def _row_tile(seq):
    return min(seq, 512)


def _moe_block_rows(t):
    return min(512, t * TOP_K // 8)


def kernel(x, c, fox_w_in, fox_b_f, fox_q_norm, fox_k_norm, fox_w_out, hgrn_w_in, hgrn_lb, hgrn_g_norm, hgrn_w_out, ada_w, ada_b, norm1_w, norm2_w, router_w, router_b, exp_w_gu, exp_b_gu, exp_w_down, exp_b_down):
    bsz, seq, d = x.shape
    t = bsz * seq
    depth = ada_w.shape[0]
    n_exp = router_w.shape[-1]
    f_dim = exp_w_down.shape[2]
    h_fox = fox_w_in.shape[-1] - 4 * d
    dh = d // h_fox
    tm = _row_tile(seq)
    n_split = 2 if bsz % 2 == 0 else 1
    bh, th = bsz // n_split, t // n_split
    rb = _moe_block_rows(th)
    n_blocks = (th * TOP_K) // rb + n_exp
    hgrn_hpb = 4 if d % (4 * LANES) == 0 else 2

    mod_all = _adaln_mod(c, ada_w, ada_b).reshape(depth, bsz, 1, -1)
    lb_soft = jax.nn.softmax(hgrn_lb.astype(F32), axis=0)
    lower = jnp.cumsum(lb_soft, axis=0) - lb_soft[0]
    experts = jnp.arange(n_exp, dtype=I32)

    xf = x.reshape(t, d)
    parts = [xf[s * th:(s + 1) * th] for s in range(n_split)]
    pending = None
    for i in range(depth):
        j = i // 2
        nw1 = norm1_w[i].reshape(1, d)
        if i % 2 == 0:
            w_in = fox_w_in[j]
            w_main = w_in[:, :4 * d].astype(BF16)
            w_f = jnp.pad(w_in[:, 4 * d:], ((0, 0), (0, LANES - h_fox))).astype(BF16)
            b_f = jnp.pad(fox_b_f[j], (0, LANES - h_fox)).reshape(1, LANES)
            qw = jnp.tile(fox_q_norm[j], LANES // dh).reshape(1, LANES)
            kw = jnp.tile(fox_k_norm[j], LANES // dh).reshape(1, LANES)
            w_out = fox_w_out[j].astype(BF16)
        else:
            w_main = hgrn_w_in[j].astype(BF16)
            w_out = hgrn_w_out[j].astype(BF16)
        wr = jnp.pad(router_w[i], ((0, 0), (0, LANES - n_exp))).astype(BF16)
        br = jnp.pad(router_b[i], (0, LANES - n_exp), constant_values=MASK_VALUE).reshape(1, LANES)
        b_gu = exp_b_gu[i].reshape(n_exp, 1, f_dim, 2)

        def front(s, xp):
            mod = mod_all[i, s * bh:(s + 1) * bh]
            if i % 2 == 0:
                qkvg, flogit = _in_proj(xp, nw1, mod, w_main, BF16, seq, tm, w_f, b_f)
                o = _fox_attention(qkvg, flogit, qw, kw, seq, d, dh)
            else:
                (proj,) = _in_proj(xp, nw1, mod, w_main, F32, seq, tm)
                o = _hgrn_mixer(proj, lower[j].reshape(1, d), hgrn_g_norm[j].reshape(1, HGRN_DIM),
                                seq, d, hgrn_hpb)
            x1, hp, ri, rw, cnt = _post_mixer(o, xp, w_out, mod, norm2_w[i].reshape(1, d), wr, br,
                                              seq, tm)
            counts = cnt[0, :n_exp]
            padded = (counts + rb - 1) // rb * rb
            pad_end = jnp.cumsum(padded)
            pad_start = pad_end - padded
            top_idx, rank = ri[:TOP_K], ri[TOP_K:]
            start = jnp.sum(jnp.where(top_idx[None] == experts[:, None, None],
                                      pad_start[:, None, None], 0), axis=0)
            posf = (start + rank).reshape(-1).astype(I32)
            blk_lo = jnp.arange(n_blocks, dtype=I32) * rb
            blk_e = jnp.minimum(jnp.sum(pad_end[None, :] <= blk_lo[:, None], axis=1),
                                n_exp - 1).astype(I32)
            n_used = (pad_end[-1] // rb).astype(I32).reshape(1)
            later = jnp.where((experts[None, :] > experts[:, None]) & (padded[None, :] > 0),
                              experts[None, :], n_exp)
            nxt_e = jnp.min(later, axis=1)
            nxt_e = jnp.where(nxt_e >= n_exp, -1, nxt_e).astype(I32)
            return dict(x1=x1, hp=hp, rw=rw, posf=posf, blk_e=blk_e, n_used=n_used, nxt_e=nxt_e,
                        mod=mod)

        def experts_of(st, xs):
            return _expert_ffn(xs, st["blk_e"], st["n_used"], st["nxt_e"], exp_w_gu,
                               b_gu[..., 0], b_gu[..., 1], exp_w_down,
                               exp_b_down[i].reshape(n_exp, 1, d), rb, i)

        def combine_of(st, yt):
            return _combine(st["x1"], yt, st["rw"], st["mod"], seq, tm)

        if n_split == 1:
            st = front(0, parts[0])
            xs = _sc_scatter_rows(st["hp"], st["posf"], n_blocks * rb)
            yt = _sc_gather_rows(experts_of(st, xs), st["posf"])
            parts[0] = combine_of(st, yt)
            continue

        tie = lax.optimization_barrier
        sa = front(0, parts[0])
        if pending is not None:
            yt_b, hp_a = tie((pending[1], sa["hp"]))
            parts[1] = combine_of(pending[0], yt_b)
        xs_a = _sc_scatter_rows(sa["hp"], sa["posf"], n_blocks * rb)
        xb, _ = tie((parts[1], sa["hp"]))
        sb = front(1, xb)
        xs_a, _ = tie((xs_a, sb["hp"]))
        xs_b = _sc_scatter_rows(sb["hp"], sb["posf"], n_blocks * rb)
        ys_a = experts_of(sa, xs_a)
        xs_b, _ = tie((xs_b, ys_a))
        yt_a = _sc_gather_rows(ys_a, sa["posf"])
        ys_b = experts_of(sb, xs_b)
        yt_a, _ = tie((yt_a, ys_b))
        yt_b = _sc_gather_rows(ys_b, sb["posf"])
        parts[0] = combine_of(sa, yt_a)
        pending = (sb, yt_b)
    if pending is not None:
        parts[1] = combine_of(*pending)
    return jnp.concatenate(parts, axis=0).reshape(bsz, seq, d)
```

```python
import functools

import jax
import jax.numpy as jnp
import numpy as np
from jax import lax
from jax.experimental import pallas as pl
from jax.experimental.pallas import tpu as pltpu
from jax.experimental.pallas import tpu_sc as plsc

F32 = jnp.float32
BF16 = jnp.bfloat16
U32 = jnp.uint32
I32 = jnp.int32

EPS = 1e-6
MASK_VALUE = -1e30
TOP_K = 4
HGRN_DIM = 128
HGRN_CHUNK = 32
SWIGLU_LIMIT = 7.0
SWIGLU_ALPHA = 1.702
LANES = 128
SUBLANES = 8
SC_CORES = 2
SC_SUBCORES = 16
SC_ROWS = 64
VMEM_LIMIT = 56 * 1024 * 1024
HIGHEST = lax.Precision.HIGHEST


def _cparams(*sem):
    return pltpu.CompilerParams(dimension_semantics=sem, vmem_limit_bytes=VMEM_LIMIT)


def _resident(shape, index_map):
    return pl.BlockSpec(shape, index_map, pipeline_mode=pl.Buffered(1))


def _sigmoid(x):
    return 1.0 / (1.0 + jnp.exp(-x))


def _bf16_bits(x):
    return lax.bitcast_convert_type(x.astype(BF16).astype(F32), U32)


def _pack_halves(x):
    n = x.shape[1] // 2
    lo = _bf16_bits(x[:, :n])
    hi = _bf16_bits(x[:, n:])
    return (hi & jnp.uint32(0xFFFF0000)) | (lo >> 16)


def _unpack_halves(w):
    lo = lax.bitcast_convert_type(w << 16, F32)
    hi = lax.bitcast_convert_type(w & jnp.uint32(0xFFFF0000), F32)
    return lo, hi


def _mod_kernel(c_ref, w_ref, b_ref, o_ref):
    c = c_ref[...]
    ca = c * _sigmoid(c)
    o_ref[0] = jnp.dot(ca, w_ref[0], preferred_element_type=F32, precision=HIGHEST) + b_ref[0]


def _adaln_mod(c, ada_w, ada_b):
    depth, d, n6 = ada_w.shape
    b = c.shape[0]
    nmod = n6 // d
    return pl.pallas_call(
        _mod_kernel,
        grid=(depth, nmod),
        in_specs=[
            pl.BlockSpec((b, d), lambda i, j: (0, 0)),
            pl.BlockSpec((1, d, d), lambda i, j: (i, 0, j)),
            pl.BlockSpec((1, 1, d), lambda i, j: (i, 0, j)),
        ],
        out_specs=pl.BlockSpec((1, b, d), lambda i, j: (i, 0, j)),
        out_shape=jax.ShapeDtypeStruct((depth, b, n6), F32),
        compiler_params=_cparams("parallel", "parallel"),
        name="adaln_mod",
    )(c, ada_w, ada_b.reshape(depth, 1, n6))


def _norm_mod(x, nw, sc, sh):
    ms = jnp.mean(x * x, axis=-1, keepdims=True)
    return x * lax.rsqrt(ms + EPS) * nw * (1.0 + sc) + sh


def _in_kernel(x_ref, nw_ref, sh_ref, sc_ref, w_ref, *rest, n_chunk, with_f):
    if with_f:
        wf_ref, bf_ref, o_ref, of_ref = rest
    else:
        (o_ref,) = rest
    h = _norm_mod(x_ref[...], nw_ref[...], sc_ref[...], sh_ref[...]).astype(BF16)
    n = o_ref.shape[1]
    for c0 in range(0, n, n_chunk):
        o_ref[:, c0:c0 + n_chunk] = jnp.dot(
            h, w_ref[:, c0:c0 + n_chunk], preferred_element_type=F32).astype(o_ref.dtype)
    if with_f:
        of_ref[...] = jnp.dot(h, wf_ref[...], preferred_element_type=F32) + bf_ref[...]


def _in_proj(x, nw, mod, w, out_dtype, seq, tm, wf=None, bf=None, rows=None, row_off=0):
    d = x.shape[1]
    t = x.shape[0] if rows is None else rows
    n = w.shape[1]
    per_seq = seq // tm
    tile_off = row_off // tm
    with_f = wf is not None
    mod_spec = lambda j: pl.BlockSpec((None, 1, d), lambda i: (i // per_seq, 0, j))
    in_specs = [
        pl.BlockSpec((tm, d), lambda i: (i + tile_off, 0)),
        _resident((1, d), lambda i: (0, 0)),
        mod_spec(0), mod_spec(1),
        _resident((d, n), lambda i: (0, 0)),
    ]
    out_specs = [pl.BlockSpec((tm, n), lambda i: (i, 0))]
    out_shape = [jax.ShapeDtypeStruct((t, n), out_dtype)]
    args = [x, nw, mod, mod, w]
    if with_f:
        in_specs += [_resident((d, LANES), lambda i: (0, 0)), _resident((1, LANES), lambda i: (0, 0))]
        out_specs.append(pl.BlockSpec((tm, LANES), lambda i: (i, 0)))
        out_shape.append(jax.ShapeDtypeStruct((t, LANES), F32))
        args += [wf, bf]
    return pl.pallas_call(
        functools.partial(_in_kernel, n_chunk=min(n, 1024), with_f=with_f),
        grid=(t // tm,),
        in_specs=in_specs,
        out_specs=out_specs,
        out_shape=out_shape,
        compiler_params=_cparams("parallel"),
        name="in_proj_fox" if with_f else "in_proj_hgrn",
    )(*args)


LOG2E = 1.4426950408889634
EXP2_SPAN = 120.0
SCORE_MARGIN = 1.01
BIAS_LANES = 16


def _split3(c):
    c1 = c.astype(BF16).astype(F32)
    r = c - c1
    c2 = r.astype(BF16).astype(F32)
    return c1, c2, r - c2


def _bias_layout(n_pairs):
    route =np.zeros((3, 3 * LANES, LANES), np.float32)
    const = np.zeros((3, 1, LANES), np.float32)
    for p in range(n_pairs):
        base = p * BIAS_LANES
        for i in range(3):
            const[0, 0, base + i] = 1.0
            for h in range(2):
                route[0, i * LANES + 2 * p + h, base + 3 + 3 * h + i] = -1.0
                route[1 + h, i * LANES + 2 * p + h, base + i] = 1.0
                const[1 + h, 0, base + 3 + 3 * h + i] = 1.0
    return jnp.asarray(route, BF16), jnp.asarray(const, F32)


def _cum_kernel(f_ref, ref_ref, route_ref, const_ref, o_ref, *, rows):
    s = f_ref.shape[0]
    r = lax.broadcasted_iota(I32, (rows, rows), 0)
    c = lax.broadcasted_iota(I32, (rows, rows), 1)
    tri = (r >= c).astype(F32)
    carry = jnp.zeros((1, LANES), F32)
    for r0 in range(0, s, rows):
        x = f_ref[r0:r0 + rows, :]
        lf = jnp.minimum(x, 0.0) - jnp.log(1.0 + jnp.exp(-jnp.abs(x)))
        cs = jnp.dot(tri, lf, preferred_element_type=F32, precision=HIGHEST) + carry
        carry = cs[rows - 1:rows, :]
        c2 = cs * LOG2E
        k_terms = jnp.concatenate(_split3(c2), axis=1).astype(BF16)
        q_terms = jnp.concatenate(_split3(c2 - ref_ref[...]), axis=1).astype(BF16)
        for j, terms in enumerate((k_terms, q_terms, q_terms)):
            routed = jnp.dot(terms, route_ref[j], preferred_element_type=F32) + const_ref[j]
            o_ref[r0:r0 + rows, j * LANES:(j + 1) * LANES] = routed.astype(o_ref.dtype)


def _fox_bias(flogit, ref, seq, n_pairs):
    t = flogit.shape[0]
    route, const = _bias_layout(n_pairs)
    return pl.pallas_call(
        functools.partial(_cum_kernel, rows=min(seq, 128)),
        grid=(t // seq,),
        in_specs=[pl.BlockSpec((seq, LANES), lambda b: (b, 0)),
                  pl.BlockSpec((1, LANES), lambda b: (0, 0)),
                  pl.BlockSpec((3, 3 * LANES, LANES), lambda b: (0, 0, 0)),
                  pl.BlockSpec((3, 1, LANES), lambda b: (0, 0, 0))],
        out_specs=pl.BlockSpec((seq, 3 * LANES), lambda b: (b, 0)),
        out_shape=jax.ShapeDtypeStruct((t, 3 * LANES), BF16),
        compiler_params=_cparams("parallel"),
        name="fox_bias",
    )(flogit, ref, route, const)


def _attn_kernel(q_ref, k_ref, v_ref, g_ref, bias_ref, qw_ref, kw_ref, o_ref, qs_s, ka_s, *,
                 dh, tq, bounded):
    s_len = q_ref.shape[0]
    nq = s_len // tq
    hp = pl.program_id(1)
    lane = lax.broadcasted_iota(I32, (1, LANES), 1)
    first = lane < dh

    def headnorm(z, w):
        z2 = z * z
        s0 = jnp.sum(jnp.where(first, z2, 0.0), axis=-1, keepdims=True)
        s1 = jnp.sum(jnp.where(first, 0.0, z2), axis=-1, keepdims=True)
        ms = jnp.where(first, s0, s1) * (1.0 / dh)
        return z * lax.rsqrt(ms + EPS) * w

    qn = headnorm(q_ref[...].astype(F32), qw_ref[...]) * (dh ** -0.5 * LOG2E)
    kn = headnorm(k_ref[...].astype(F32), kw_ref[...])
    mine = (lane // BIAS_LANES) == hp
    ka_s[:, :LANES] = kn.astype(BF16)
    ka_s[:, LANES:] = bias_ref[:, :LANES]
    q_lo = [jnp.where(first, qn, 0.0), jnp.where(first, 0.0, qn)]
    for qi in range(nq):
        rows = slice(qi * tq, (qi + 1) * tq)
        for h in range(2):
            q_hi = bias_ref[rows, (1 + h) * LANES:(2 + h) * LANES].astype(F32)
            qs_s[qi, h * tq:(h + 1) * tq, :LANES] = q_lo[h][rows, :].astype(BF16)
            qs_s[qi, h * tq:(h + 1) * tq, LANES:] = jnp.where(mine, q_hi, 0.0).astype(BF16)

    nt = (((1,), (1,)), ((), ()))

    def causal(s):
        rr = lax.broadcasted_iota(I32, (2 * tq, tq), 0)
        rr = jnp.where(rr >= tq, rr - tq, rr)
        cc = lax.broadcasted_iota(I32, (2 * tq, tq), 1)
        return jnp.where(rr >= cc, s, MASK_VALUE)

    def finish(qi, acc, l):
        o2 = acc / l
        o = jnp.where(first, o2[:tq], o2[tq:])
        g = g_ref[qi * tq:(qi + 1) * tq, :].astype(F32)
        o_ref[qi * tq:(qi + 1) * tq, :] = (o * _sigmoid(g)).astype(o_ref.dtype)

    if bounded:
        for qi in range(nq):
            acc = part = None
            for t in range(qi + 1):
                s = lax.dot_general(qs_s[qi], ka_s[t * tq:(t + 1) * tq, :], nt,
                                    preferred_element_type=F32)
                e = jnp.exp2(causal(s) if t == qi else s)
                cols = e[:, :LANES]
                for c0 in range(LANES, tq, LANES):
                    cols = cols + e[:, c0:c0 + LANES]
                pv = jnp.dot(e.astype(BF16), v_ref[t * tq:(t + 1) * tq, :],
                             preferred_element_type=F32)
                acc = pv if t == 0 else acc + pv
                part = cols if t == 0 else part + cols
            finish(qi, acc, jnp.sum(part, axis=-1, keepdims=True))
        return

    def scores(q, off):
        return lax.dot_general(q, ka_s[pl.ds(off, tq), :], nt, preferred_element_type=F32)

    def update(s, off, carry, masked):
        m, l, acc = carry
        if masked:
            s = causal(s)
        m_new = jnp.maximum(m, jnp.max(s, axis=-1, keepdims=True))
        p = jnp.exp2(s - m_new)
        alpha = jnp.exp2(m - m_new)
        l = alpha * l + jnp.sum(p, axis=-1, keepdims=True)
        acc = alpha * acc + jnp.dot(p.astype(BF16), v_ref[pl.ds(off, tq), :],
                                    preferred_element_type=F32)
        return m_new, l, acc

    for qi in range(nq):
        q = qs_s[qi]
        s = scores(q, 0)
        carry = (jnp.full((2 * tq, 1), MASK_VALUE, F32), jnp.zeros((2 * tq, 1), F32),
                 jnp.zeros((2 * tq, LANES), F32))
        if qi > 0:
            def body(j, c, q=q):
                off = pl.multiple_of(j * tq, tq)
                s_next = scores(q, off + tq)
                return (s_next,) + update(c[0], off, c[1:], False)

            s, *carry = lax.fori_loop(0, qi, body, (s,) + carry)
        m, l, acc = update(s, qi * tq, carry, True)
        finish(qi, acc, l)


def _fox_attention(qkvg, flogit, qw, kw, seq, d, dh):
    t = qkvg.shape[0]
    nb = d // LANES
    assert 2 * dh == LANES and nb * BIAS_LANES <= LANES
    tq = min(seq, 512)
    col = lambda off: pl.BlockSpec((seq, LANES), lambda b, p: (b, off + p))

    def call(bounded, bias):
        return pl.pallas_call(
            functools.partial(_attn_kernel, dh=dh, tq=tq, bounded=bounded),
            grid=(t // seq, nb),
            in_specs=[
                col(0), col(nb), col(2 * nb), col(3 * nb),
                pl.BlockSpec((seq, 3 * LANES), lambda b, p: (b, 0)),
                pl.BlockSpec((1, LANES), lambda b, p: (0, 0)),
                pl.BlockSpec((1, LANES), lambda b, p: (0, 0)),
            ],
            out_specs=pl.BlockSpec((seq, LANES), lambda b, p: (b, p)),
            out_shape=jax.ShapeDtypeStruct((t, d), BF16),
            scratch_shapes=[pltpu.VMEM((seq // tq, 2 * tq, 2 * LANES), BF16),
                            pltpu.VMEM((seq, 2 * LANES), BF16)],
            compiler_params=_cparams("parallel", "parallel"),
            name="fox_attention" if bounded else "fox_attention_running_max",
        )(qkvg, qkvg, qkvg, qkvg, bias, qw, kw)

    bound = (dh ** 0.5 * LOG2E * SCORE_MARGIN) * jnp.max(jnp.abs(qw)) * jnp.max(jnp.abs(kw)) + 0.1
    fits = bound <= EXP2_SPAN - 4.0
    ref = jnp.where(fits, bound - jnp.maximum(0.0, 2.0 * bound - EXP2_SPAN), 0.0)
    bias = _fox_bias(flogit, jnp.full((1, LANES), ref, F32), seq, nb)
    return lax.cond(fits, lambda: call(True, bias), lambda: call(False, bias))


HGRN_PREP_ROWS = 256
HGRN_LOG2_SPAN = 100.0


def _hgrn_kernel(q_ref, f_ref, v_ref, g_ref, lb_ref, gw_ref, o_ref, qe_s, ke_s, kd_s, dec_s, *,
                 hpb, chunk):
    s_len = q_ref.shape[0]
    n_chunks = s_len // chunk
    nt = (((1,), (1,)), ((), ()))
    gw = gw_ref[...]

    pr = min(HGRN_PREP_ROWS, s_len)
    rr = lax.broadcasted_iota(I32, (pr, pr), 0)
    cc = lax.broadcasted_iota(I32, (pr, pr), 1)
    same = (rr // chunk) == (cc // chunk)
    cum_m = jnp.where(same & (rr >= cc), 1.0, 0.0).astype(BF16)
    tot_m = jnp.where(same, 1.0, 0.0).astype(BF16)
    lb_all = lb_ref[...]
    lowest = jnp.zeros((1, lb_all.shape[1]), F32)
    for r0 in range(0, s_len, pr):
        sig = _sigmoid(f_ref[r0:r0 + pr, :])
        terms = [t.astype(BF16) for t in _split3(jnp.log2(lb_all + (1.0 - lb_all) * sig))]
        b = sum(jnp.dot(cum_m, t, preferred_element_type=F32) for t in terms)
        tot = sum(jnp.dot(tot_m, t, preferred_element_type=F32) for t in terms)
        k = (1.0 - lb_all) * (1.0 - sig)
        q_raw = q_ref[r0:r0 + pr, :]
        qe_s[r0:r0 + pr, :] = (q_raw * _sigmoid(q_raw) * jnp.exp2(b)).astype(BF16)
        ke_s[r0:r0 + pr, :] = (k * jnp.exp2(-b)).astype(BF16)
        kd_s[r0:r0 + pr, :] = (k * jnp.exp2(tot - b)).astype(BF16)
        for c0 in range(0, pr, chunk):
            ci = (r0 + c0) // chunk
            dec_s[ci] = jnp.exp2(tot[c0:c0 + SUBLANES, :])
        lowest = jnp.minimum(lowest, jnp.min(tot, axis=0, keepdims=True))
    mild = jnp.min(lowest) >= -HGRN_LOG2_SPAN

    @pl.when(mild)
    def _():
        r32 = lax.broadcasted_iota(I32, (chunk, chunk), 0)
        c32 = lax.broadcasted_iota(I32, (chunk, chunk), 1)
        causal = r32 >= c32
        per_step = 4 if n_chunks % 4 == 0 else 2

        def fast_body(step, states):
            insts = [(ch, hd) for ch in range(per_step) for hd in range(hpb)]
            rows = {ch: pl.ds(pl.multiple_of((step * per_step + ch) * chunk, chunk), chunk)
                    for ch in range(per_step)}
            cols = {hd: slice(hd * LANES, (hd + 1) * LANES) for hd in range(hpb)}
            states = list(states)
            qe = {i: qe_s[rows[i[0]], cols[i[1]]] for i in insts}
            v = {i: v_ref[rows[i[0]], cols[i[1]]] for i in insts}
            a = {i: lax.dot_general(qe[i], ke_s[rows[i[0]], cols[i[1]]], nt,
                                    preferred_element_type=F32) for i in insts}
            upd = {i: jnp.dot(jnp.transpose(v[i]).astype(BF16), kd_s[rows[i[0]], cols[i[1]]],
                              preferred_element_type=F32) for i in insts}
            o = {}
            for ch in range(per_step):
                for hd in range(hpb):
                    i = (ch, hd)
                    o[i] = (jnp.dot(jnp.where(causal, a[i], 0.0).astype(BF16), v[i].astype(BF16),
                                    preferred_element_type=F32)
                            + lax.dot_general(qe[i], states[hd].astype(BF16), nt,
                                              preferred_element_type=F32))
                    dec = dec_s[step * per_step + ch, 0:1, cols[hd]]
                    states[hd] = dec * states[hd] + upd[i]
            for i in insts:
                g = g_ref[rows[i[0]], cols[i[1]]]
                ms = jnp.mean(o[i] * o[i], axis=-1, keepdims=True)
                y = o[i] * lax.rsqrt(ms + EPS) * gw * (g * _sigmoid(g))
                o_ref[rows[i[0]], cols[i[1]]] = y.astype(o_ref.dtype)
            return tuple(states)

        init = tuple(jnp.zeros((LANES, LANES), F32) for _ in range(hpb))
        lax.fori_loop(0, n_chunks // per_step, fast_body, init)

    pair = 2
    groups = chunk // SUBLANES
    off_w = (groups - 1) * chunk
    r = lax.broadcasted_iota(I32, (chunk, chunk), 0)
    c = lax.broadcasted_iota(I32, (chunk, chunk), 1)
    tri = jnp.where(r >= c, 1.0, 0.0).astype(BF16)
    ro = lax.broadcasted_iota(I32, (chunk, off_w), 0) // SUBLANES
    co = lax.broadcasted_iota(I32, (chunk, off_w), 1) // chunk
    keep = ro == co + 1
    sub = lax.broadcasted_iota(I32, (SUBLANES, LANES), 0)
    gw = gw_ref[...]
    nt = (((1,), (1,)), ((), ()))
    grp = lambda a, i: a[i * SUBLANES:(i + 1) * SUBLANES, :]
    zeros8 = jnp.zeros((SUBLANES, LANES), F32)

    def body(step, states):
        insts = [(ch, hd) for ch in range(pair) for hd in range(hpb)]
        rows = {ch: pl.ds(pl.multiple_of((step * pair + ch) * chunk, chunk), chunk)
                for ch in range(pair)}
        cols = {hd: slice(hd * LANES, (hd + 1) * LANES) for hd in range(hpb)}
        st = {}
        for ch, hd in insts:
            lb = lb_ref[:, cols[hd]]
            q_raw = q_ref[rows[ch], cols[hd]]
            sig = _sigmoid(f_ref[rows[ch], cols[hd]])
            d = dict(v=v_ref[rows[ch], cols[hd]], q=q_raw * _sigmoid(q_raw),
                     k=(1.0 - lb) * (1.0 - sig))
            lf = jnp.log(lb + (1.0 - lb) * sig)
            d["b"] = sum(jnp.dot(tri, part.astype(BF16), preferred_element_type=F32)
                         for part in _split3(lf))
            st[ch, hd] = d
        def from_state(ch, hd, state_t):
            d = st[ch, hd]
            qe = (d["q"] * jnp.exp(d["b"])).astype(BF16)
            d["o"] = lax.dot_general(qe, state_t.astype(BF16), nt, preferred_element_type=F32)
        for hd in range(hpb):
            from_state(0, hd, states[hd])
        for ch, hd in insts:
            d = st[ch, hd]
            b, q, k = d["b"], d["q"], d["k"]
            ends = [b[(i + 1) * SUBLANES - 1:(i + 1) * SUBLANES, :] for i in range(groups)]
            kt = [grp(k, i) * jnp.exp(ends[i] - grp(b, i)) for i in range(groups)]
            qh = jnp.concatenate(
                [zeros8] + [grp(q, i) * jnp.exp(grp(b, i) - ends[i - 1]) for i in range(1, groups)],
                axis=0).astype(BF16)
            kh = jnp.concatenate(
                [kt[j] * jnp.exp(ends[i - 1] - ends[j]) if j < i else zeros8
                 for i in range(1, groups) for j in range(groups)], axis=0).astype(BF16)
            d["a"] = lax.dot_general(qh, kh, nt, preferred_element_type=F32)
            d["ends"], d["kt"] = ends, kt
        def state_update(ch, hd, state_t):
            d = st[ch, hd]
            b_end = d["ends"][-1]
            ke = jnp.concatenate([d["kt"][j] * jnp.exp(b_end - d["ends"][j]) for j in range(groups)],
                                 axis=0).astype(BF16)
            upd = jnp.dot(jnp.transpose(d["v"]).astype(BF16), ke, preferred_element_type=F32)
            return jnp.exp(b_end) * state_t + upd
        states = [state_update(0, hd, states[hd]) for hd in range(hpb)]
        for ch, hd in insts:
            d = st[ch, hd]
            vb = d["v"].astype(BF16)
            v3 = jnp.concatenate([vb] * (groups - 1), axis=0)
            d["off"] = jnp.dot(jnp.where(keep, d["a"], 0.0).astype(BF16), v3,
                               preferred_element_type=F32)
        for hd in range(hpb):
            from_state(1, hd, states[hd])
        states = [state_update(1, hd, states[hd]) for hd in range(hpb)]
        for ch, hd in insts:
            d = st[ch, hd]
            o = d["o"] + d["off"]
            o_g = []
            for gi in range(groups):
                qg, kg, vg, bg = grp(d["q"], gi), grp(d["k"], gi), grp(d["v"], gi), grp(d["b"], gi)
                og = grp(o, gi)
                for s in range(SUBLANES):
                    live = sub >= s
                    e = jnp.exp(jnp.where(live, bg - bg[s:s + 1, :], 0.0))
                    p = jnp.where(live, qg * e * kg[s:s + 1, :], 0.0)
                    og = og + jnp.sum(p, axis=-1, keepdims=True) * vg[s:s + 1, :]
                o_g.append(og)
            o = jnp.concatenate(o_g, axis=0)
            g = g_ref[rows[ch], cols[hd]]
            ms = jnp.mean(o * o, axis=-1, keepdims=True)
            y = o * lax.rsqrt(ms + EPS) * gw * (g * _sigmoid(g))
            o_ref[rows[ch], cols[hd]] = y.astype(o_ref.dtype)
        return tuple(states)

    @pl.when(jnp.logical_not(mild))
    def _():
        init = tuple(jnp.zeros((LANES, LANES), F32) for _ in range(hpb))
        lax.fori_loop(0, s_len // (chunk * pair), body, init)


def _hgrn_mixer(proj, lower, gw, seq, d, hpb):
    t = proj.shape[0]
    w = hpb * LANES
    nb = d // w
    col = lambda off: pl.BlockSpec((seq, w), lambda b, p: (b, off + p))
    return pl.pallas_call(
        functools.partial(_hgrn_kernel, hpb=hpb, chunk=HGRN_CHUNK),
        grid=(t // seq, nb),
        in_specs=[
            col(0), col(nb), col(2 * nb), col(3 * nb),
            pl.BlockSpec((1, w), lambda b, p: (0, p)),
            pl.BlockSpec((1, LANES), lambda b, p: (0, 0)),
        ],
        out_specs=pl.BlockSpec((seq, w), lambda b, p: (b, p)),
        out_shape=jax.ShapeDtypeStruct((t, d), BF16),
        scratch_shapes=[pltpu.VMEM((seq, w), BF16)] * 3
        + [pltpu.VMEM((seq // HGRN_CHUNK, SUBLANES, w), F32)],
        compiler_params=_cparams("parallel", "parallel"),
        name="hgrn_mixer",
    )(proj, proj, proj, proj, lower, gw)


def _post_kernel(o_ref, x_ref, wo_ref, g1_ref, nw_ref, sh_ref, sc_ref, wr_ref, br_ref,
                 x1_ref, hp_ref, ri_ref, rw_ref, cnt_ref, run_s):
    i = pl.program_id(0)
    tm = x_ref.shape[0]

    @pl.when(i == 0)
    def _():
        run_s[...] = jnp.zeros_like(run_s)

    y = jnp.dot(o_ref[...], wo_ref[...], preferred_element_type=F32)
    x1 = x_ref[...] + g1_ref[...] * y
    x1_ref[...] = x1
    h2 = _norm_mod(x1, nw_ref[...], sc_ref[...], sh_ref[...])
    hp_ref[...] = _pack_halves(h2)
    logits = jnp.dot(h2.astype(BF16), wr_ref[...], preferred_element_type=F32) + br_ref[...]

    lane = lax.broadcasted_iota(I32, (tm, LANES), 1).astype(F32)
    work = logits
    idx, val = [], []
    for _ in range(TOP_K):
        mx = jnp.max(work, axis=-1, keepdims=True)
        ix = jnp.min(jnp.where(work == mx, lane, float(LANES)), axis=-1, keepdims=True)
        idx.append(ix)
        val.append(mx)
        work = jnp.where(lane == ix, -jnp.inf, work)
    ex = [jnp.exp(v - val[0]) for v in val]
    den = ex[0] + ex[1] + ex[2] + ex[3]
    wts = [e / den for e in ex]

    hot = jnp.zeros((tm, LANES), F32)
    for ix in idx:
        hot = hot + jnp.where(lane == ix, 1.0, 0.0)
    r = lax.broadcasted_iota(I32, (tm, tm), 0)
    c = lax.broadcasted_iota(I32, (tm, tm), 1)
    strict = jnp.where(r > c, 1.0, 0.0).astype(BF16)
    before = jnp.dot(strict, hot.astype(BF16), preferred_element_type=F32) + run_s[...]
    ranks = [jnp.sum(jnp.where(lane == ix, before, 0.0), axis=-1, keepdims=True) for ix in idx]
    run_s[...] = run_s[...] + jnp.sum(hot, axis=0, keepdims=True)

    ri = jnp.zeros((tm, LANES), F32)
    rw = jnp.zeros((tm, LANES), F32)
    for k in range(TOP_K):
        ri = jnp.where(lane == float(k), idx[k], ri)
        ri = jnp.where(lane == float(TOP_K + k), ranks[k], ri)
        rw = jnp.where(lane == float(k), wts[k], rw)
    ri_ref[...] = jnp.transpose(ri)[:2 * TOP_K, :].astype(I32)
    rw_ref[...] = rw
    cnt_ref[...] = run_s[...].astype(I32)


def _post_mixer(o, x, wo, mod, nw2, wr, br, seq, tm, row_off=0):
    t, d = o.shape
    per_seq = seq // tm
    tile_off = row_off // tm
    mod_spec = lambda j: pl.BlockSpec((None, 1, d), lambda i: (i // per_seq, 0, j))
    row = lambda w: pl.BlockSpec((tm, w), lambda i: (i, 0))
    return pl.pallas_call(
        _post_kernel,
        grid=(t // tm,),
        in_specs=[
            row(d), pl.BlockSpec((tm, d), lambda i: (i + tile_off, 0)),
            _resident((d, d), lambda i: (0, 0)),
            mod_spec(2),
            _resident((1, d), lambda i: (0, 0)),
            mod_spec(3), mod_spec(4),
            _resident((d, LANES), lambda i: (0, 0)),
            _resident((1, LANES), lambda i: (0, 0)),
        ],
        out_specs=[row(d), row(d // 2), pl.BlockSpec((2 * TOP_K, tm), lambda i: (0, i)), row(LANES),
                   pl.BlockSpec((1, LANES), lambda i: (0, 0))],
        out_shape=[
            jax.ShapeDtypeStruct((t, d), F32),
            jax.ShapeDtypeStruct((t, d // 2), U32),
            jax.ShapeDtypeStruct((2 * TOP_K, t), I32),
            jax.ShapeDtypeStruct((t, LANES), F32),
            jax.ShapeDtypeStruct((1, LANES), I32),
        ],
        scratch_shapes=[pltpu.VMEM((1, LANES), F32)],
        compiler_params=_cparams("arbitrary"),
        name="post_mixer",
    )(o, x, wo, mod, nw2, mod, mod, wr, br)


def _sc_mesh():
    return plsc.VectorSubcoreMesh(core_axis_name="c", subcore_axis_name="s")


def _sc_scatter_rows(x, posf, n_rows):
    t, w = x.shape
    workers = SC_CORES * SC_SUBCORES
    per_w = t // workers

    @functools.partial(
        pl.kernel, mesh=_sc_mesh(),
        out_type=jax.ShapeDtypeStruct((n_rows, w), x.dtype),
        scratch_types=[pltpu.VMEM((SC_ROWS,), I32)] * TOP_K
        + [pltpu.VMEM((SC_ROWS, w), x.dtype), pltpu.SemaphoreType.DMA])
    def scatter_kernel(x_hbm, pos_hbm, out_hbm, i0, i1, i2, i3, rows_v, sem):
        idxs = (i0, i1, i2, i3)
        wid = lax.axis_index("s") * SC_CORES + lax.axis_index("c")
        base = wid * per_w

        @pl.loop(0, per_w // SC_ROWS)
        def _(i):
            off = base + i * SC_ROWS
            for k in range(TOP_K):
                pltpu.sync_copy(pos_hbm.at[pl.ds(k * t + off, SC_ROWS)], idxs[k])
            pltpu.sync_copy(x_hbm.at[pl.ds(off, SC_ROWS)], rows_v)
            copies = [pltpu.async_copy(rows_v, out_hbm.at[idxs[k]], sem) for k in range(TOP_K)]
            for cp in copies:
                cp.wait()

    return scatter_kernel(x, posf)


def _sc_gather_rows(src, posf):
    n = posf.shape[0]
    w = src.shape[1]
    workers = SC_CORES * SC_SUBCORES
    per_w = n // workers

    @functools.partial(
        pl.kernel, mesh=_sc_mesh(),
        out_type=jax.ShapeDtypeStruct((n, w), src.dtype),
        scratch_types=[pltpu.VMEM((SC_ROWS,), I32), pltpu.VMEM((SC_ROWS, w), src.dtype),
                       pltpu.SemaphoreType.DMA])
    def gather_kernel(src_hbm, pos_hbm, out_hbm, idx_v, rows_v, sem):
        wid = lax.axis_index("s") * SC_CORES + lax.axis_index("c")
        base = wid * per_w

        @pl.loop(0, per_w // SC_ROWS)
        def _(i):
            off = base + i * SC_ROWS
            pltpu.sync_copy(pos_hbm.at[pl.ds(off, SC_ROWS)], idx_v)
            pltpu.async_copy(src_hbm.at[idx_v], rows_v, sem).wait()
            pltpu.sync_copy(rows_v, out_hbm.at[pl.ds(off, SC_ROWS)])

    return gather_kernel(src, posf)


GU_BLOCK = 2 * LANES
FF_CHUNK = 2 * LANES


def _deinterleave_matrix():
    sel = np.zeros((GU_BLOCK, GU_BLOCK), np.float32)
    j = np.arange(LANES)
    sel[2 * j, j] = 1.0
    sel[2 * j + 1, LANES + j] = 1.0
    return jnp.asarray(sel, BF16)


def _expert_kernel(be_ref, nu_ref, nxt_ref, x_ref, wgu_hbm, bg_ref, bu_ref, wd_hbm, bd_ref, sel_ref,
                   y_ref, wgu_f, wd_f, wgu_s, wd_s, sems, *, layer):
    i = pl.program_id(0)
    d, f2 = wgu_s.shape
    f = f2 // 2
    e_cur = be_ref[i]
    new_expert = jnp.logical_or(i == 0, e_cur != be_ref[jnp.maximum(i - 1, 0)])

    def weight_copies(e):
        return (pltpu.make_async_copy(wgu_hbm.at[layer, e], wgu_f, sems.at[0]),
                pltpu.make_async_copy(wd_hbm.at[layer, e], wd_f, sems.at[1]))

    @pl.when(jnp.logical_and(new_expert, i < nu_ref[0]))
    def _():
        @pl.when(i == 0)
        def _():
            for cp in weight_copies(e_cur):
                cp.start()

        for cp in weight_copies(e_cur):
            cp.wait()
        for c0 in range(0, f2, GU_BLOCK):
            blk = wgu_f[:, c0:c0 + GU_BLOCK].astype(BF16)
            wgu_s[:, c0:c0 + GU_BLOCK] = jnp.dot(
                blk, sel_ref[...], preferred_element_type=F32).astype(BF16)
        wd_s[...] = wd_f[...].astype(BF16)

        @pl.when(nxt_ref[e_cur] >= 0)
        def _():
            for cp in weight_copies(nxt_ref[e_cur]):
                cp.start()

    @pl.when(i < nu_ref[0])
    def _():
        half = x_ref.shape[1]
        lo, hi = _unpack_halves(x_ref[...])
        lo = lo.astype(BF16)
        hi = hi.astype(BF16)

        def hidden(c0):
            cols = slice(2 * c0, 2 * (c0 + FF_CHUNK))
            gu = (jnp.dot(lo, wgu_s[:half, cols], preferred_element_type=F32)
                  + jnp.dot(hi, wgu_s[half:, cols], preferred_element_type=F32))
            parts = []
            for b0 in range(0, FF_CHUNK, LANES):
                gate = gu[:, 2 * b0:2 * b0 + LANES] + bg_ref[0, :, c0 + b0:c0 + b0 + LANES]
                up = gu[:, 2 * b0 + LANES:2 * b0 + 2 * LANES] + bu_ref[0, :, c0 + b0:c0 + b0 + LANES]
                gate = jnp.minimum(gate, SWIGLU_LIMIT)
                up = jnp.clip(up, -SWIGLU_LIMIT, SWIGLU_LIMIT)
                parts.append((up + 1.0) * gate * _sigmoid(SWIGLU_ALPHA * gate))
            return jnp.concatenate(parts, axis=1).astype(BF16)

        chunks = list(range(0, f, FF_CHUNK))
        y = None
        h_next = hidden(chunks[0])
        for n, c0 in enumerate(chunks):
            h = h_next
            if n + 1 < len(chunks):
                h_next = hidden(chunks[n + 1])
            part = jnp.dot(h, wd_s[c0:c0 + FF_CHUNK, :], preferred_element_type=F32)
            y = part if y is None else y + part
        y_ref[...] = _pack_halves(y + bd_ref[0])

    @pl.when(i >= nu_ref[0])
    def _():
        y_ref[...] = jnp.zeros_like(y_ref)


def _expert_ffn(xs, blk_e, n_used, nxt_e, w_gu, bg, bu, w_down, bd, rb, layer):
    rows, half = xs.shape
    _, e, d, f2 = w_gu.shape
    f = f2 // 2
    n_blocks = rows // rb
    wspec = lambda shape: pl.BlockSpec(shape, lambda i, be, nu, nx: (be[i], 0, 0))
    return pl.pallas_call(
        functools.partial(_expert_kernel, layer=layer),
        grid_spec=pltpu.PrefetchScalarGridSpec(
            num_scalar_prefetch=3,
            grid=(n_blocks,),
            in_specs=[
                pl.BlockSpec((rb, half), lambda i, be, nu, nx: (i, 0)),
                pl.BlockSpec(memory_space=pl.ANY),
                wspec((1, 1, f)), wspec((1, 1, f)),
                pl.BlockSpec(memory_space=pl.ANY),
                wspec((1, 1, d)),
                _resident((GU_BLOCK, GU_BLOCK), lambda i, be, nu, nx: (0, 0)),
            ],
            out_specs=pl.BlockSpec((rb, d // 2), lambda i, be, nu, nx: (i, 0)),
            scratch_shapes=[pltpu.VMEM((d, f2), F32), pltpu.VMEM((f, d), F32),
                            pltpu.VMEM((d, f2), BF16), pltpu.VMEM((f, d), BF16),
                            pltpu.SemaphoreType.DMA((2,))],
        ),
        out_shape=jax.ShapeDtypeStruct((rows, d // 2), U32),
        compiler_params=_cparams("arbitrary"),
        name="expert_ffn",
    )(blk_e, n_used, nxt_e, xs, w_gu, bg, bu, w_down, bd, _deinterleave_matrix())


def _combine_kernel(x_ref, y0_ref, y1_ref, y2_ref, y3_ref, rw_ref, g2_ref, *dst_and_out):
    o_ref = dst_and_out[-1]
    half = y0_ref.shape[1]
    rw = rw_ref[...]
    lane = lax.broadcasted_iota(I32, rw.shape, 1)
    acc_lo = jnp.zeros(y0_ref.shape, F32)
    acc_hi = jnp.zeros(y0_ref.shape, F32)
    for k, y_ref in enumerate((y0_ref, y1_ref, y2_ref, y3_ref)):
        wk = jnp.sum(jnp.where(lane == k, rw, 0.0), axis=-1, keepdims=True)
        lo, hi = _unpack_halves(y_ref[...])
        acc_lo = acc_lo + wk * lo
        acc_hi = acc_hi + wk * hi
    g2 = g2_ref[...]
    o_ref[:, :half] = x_ref[:, :half] + g2[:, :half] * acc_lo
    o_ref[:, half:] = x_ref[:, half:] + g2[:, half:] * acc_hi


def _combine(x1, yt, rw, mod, seq, tm, out_rows=None, row_off=0, dst=None):
    t, d = x1.shape
    per_seq = seq // tm
    nt = t // tm
    tile_off = row_off // tm
    yspec = lambda k: pl.BlockSpec((tm, d // 2), lambda i: (k * nt + i, 0))
    in_specs = [
        pl.BlockSpec((tm, d), lambda i: (i, 0)),
        yspec(0), yspec(1), yspec(2), yspec(3),
        pl.BlockSpec((tm, LANES), lambda i: (i, 0)),
        pl.BlockSpec((None, 1, d), lambda i: (i // per_seq, 0, 5)),
    ]
    args = [x1, yt, yt, yt, yt, rw, mod]
    aliases = {}
    if dst is not None:
        in_specs.append(pl.BlockSpec(memory_space=pl.ANY))
        args.append(dst)
        aliases = {len(args) - 1: 0}
    return pl.pallas_call(
        _combine_kernel,
        grid=(nt,),
        in_specs=in_specs,
        out_specs=pl.BlockSpec((tm, d), lambda i: (i + tile_off, 0)),
        out_shape=jax.ShapeDtypeStruct((t if out_rows is None else out_rows, d), F32),
        input_output_aliases=aliases,
        compiler_params=_cparams("parallel"),
        name="moe_combine",
    )(*args)


def _row_tile(seq):
    return min(seq, 512)


def _moe_block_rows(t):
    return min(512, t * TOP_K // 8)


def kernel(x, c, fox_w_in, fox_b_f, fox_q_norm, fox_k_norm, fox_w_out, hgrn_w_in, hgrn_lb, hgrn_g_norm, hgrn_w_out, ada_w, ada_b, norm1_w, norm2_w, router_w, router_b, exp_w_gu, exp_b_gu, exp_w_down, exp_b_down):
    bsz, seq, d = x.shape
    t = bsz * seq
    depth = ada_w.shape[0]
    n_exp = router_w.shape[-1]
    f_dim = exp_w_down.shape[2]
    h_fox = fox_w_in.shape[-1] - 4 * d
    dh = d // h_fox
    tm = _row_tile(seq)
    n_split = 2 if bsz % 2 == 0 else 1
    bh, th = bsz // n_split, t // n_split
    rb = _moe_block_rows(th)
    n_blocks = (th * TOP_K) // rb + n_exp
    hgrn_hpb = 4 if d % (4 * LANES) == 0 else 2

    mod_all = _adaln_mod(c, ada_w, ada_b).reshape(depth, bsz, 1, -1)
    lb_soft = jax.nn.softmax(hgrn_lb.astype(F32), axis=0)
    lower = jnp.cumsum(lb_soft, axis=0) - lb_soft[0]
    experts = jnp.arange(n_exp, dtype=I32)

    xf = x.reshape(t, d)
    parts = [xf] * n_split
    pending = None
    for i in range(depth):
        j = i // 2
        nw1 = norm1_w[i].reshape(1, d)
        if i % 2 == 0:
            w_in = fox_w_in[j]
            w_main = w_in[:, :4 * d].astype(BF16)
            w_f = jnp.pad(w_in[:, 4 * d:], ((0, 0), (0, LANES - h_fox))).astype(BF16)
            b_f = jnp.pad(fox_b_f[j], (0, LANES - h_fox)).reshape(1, LANES)
            qw = jnp.tile(fox_q_norm[j], LANES // dh).reshape(1, LANES)
            kw = jnp.tile(fox_k_norm[j], LANES // dh).reshape(1, LANES)
            w_out = fox_w_out[j].astype(BF16)
        else:
            w_main = hgrn_w_in[j].astype(BF16)
            w_out = hgrn_w_out[j].astype(BF16)
        wr = jnp.pad(router_w[i], ((0, 0), (0, LANES - n_exp))).astype(BF16)
        br = jnp.pad(router_b[i], (0, LANES - n_exp), constant_values=MASK_VALUE).reshape(1, LANES)
        b_gu = exp_b_gu[i].reshape(n_exp, 1, f_dim, 2)

        def front(s, xp):
            mod = mod_all[i, s * bh:(s + 1) * bh]
            off = s * th if i == 0 else 0
            if i % 2 == 0:
                qkvg, flogit = _in_proj(xp, nw1, mod, w_main, BF16, seq, tm, w_f, b_f,
                                        rows=th, row_off=off)
                o = _fox_attention(qkvg, flogit, qw, kw, seq, d, dh)
            else:
                (proj,) = _in_proj(xp, nw1, mod, w_main, F32, seq, tm, rows=th, row_off=off)
                o = _hgrn_mixer(proj, lower[j].reshape(1, d), hgrn_g_norm[j].reshape(1, HGRN_DIM),
                                seq, d, hgrn_hpb)
            x1, hp, ri, rw, cnt = _post_mixer(o, xp, w_out, mod, norm2_w[i].reshape(1, d), wr, br,
                                              seq, tm, row_off=off)
            counts = cnt[0, :n_exp]
            padded = (counts + rb - 1) // rb * rb
            pad_end = jnp.cumsum(padded)
            pad_start = pad_end - padded
            top_idx, rank = ri[:TOP_K], ri[TOP_K:]
            start = jnp.sum(jnp.where(top_idx[None] == experts[:, None, None],
                                      pad_start[:, None, None], 0), axis=0)
            posf = (start + rank).reshape(-1).astype(I32)
            blk_lo = jnp.arange(n_blocks, dtype=I32) * rb
            blk_e = jnp.minimum(jnp.sum(pad_end[None, :] <= blk_lo[:, None], axis=1),
                                n_exp - 1).astype(I32)
            n_used = (pad_end[-1] // rb).astype(I32).reshape(1)
            later = jnp.where((experts[None, :] > experts[:, None]) & (padded[None, :] > 0),
                              experts[None, :], n_exp)
            nxt_e = jnp.min(later, axis=1)
            nxt_e = jnp.where(nxt_e >= n_exp, -1, nxt_e).astype(I32)
            return dict(x1=x1, hp=hp, rw=rw, posf=posf, blk_e=blk_e, n_used=n_used, nxt_e=nxt_e,
                        mod=mod)

        def experts_of(st, xs):
            return _expert_ffn(xs, st["blk_e"], st["n_used"], st["nxt_e"], exp_w_gu,
                               b_gu[..., 0], b_gu[..., 1], exp_w_down,
                               exp_b_down[i].reshape(n_exp, 1, d), rb, i)

        def combine_of(st, yt, final=False, s=0, dst=None):
            if not final or n_split == 1:
                return _combine(st["x1"], yt, st["rw"], st["mod"], seq, tm)
            return _combine(st["x1"], yt, st["rw"], st["mod"], seq, tm, out_rows=t, row_off=s * th,
                            dst=dst)

        if n_split == 1:
            st = front(0, parts[0])
            xs = _sc_scatter_rows(st["hp"], st["posf"], n_blocks * rb)
            yt = _sc_gather_rows(experts_of(st, xs), st["posf"])
            parts[0] = combine_of(st, yt)
            continue

        tie = lax.optimization_barrier
        sa = front(0, parts[0])
        if pending is not None:
            yt_b, hp_a = tie((pending[1], sa["hp"]))
            parts[1] = combine_of(pending[0], yt_b)
        xs_a = _sc_scatter_rows(sa["hp"], sa["posf"], n_blocks * rb)
        xb, _ = tie((parts[1], sa["hp"]))
        sb = front(1, xb)
        xs_a, _ = tie((xs_a, sb["hp"]))
        xs_b = _sc_scatter_rows(sb["hp"], sb["posf"], n_blocks * rb)
        ys_a = experts_of(sa, xs_a)
        xs_b, _ = tie((xs_b, ys_a))
        yt_a = _sc_gather_rows(ys_a, sa["posf"])
        ys_b = experts_of(sb, xs_b)
        yt_a, _ = tie((yt_a, ys_b))
        yt_b = _sc_gather_rows(ys_b, sb["posf"])
        parts[0] = combine_of(sa, yt_a, final=(i == depth - 1))
        pending = (sb, yt_b)
    if pending is not None:
        return combine_of(*pending, final=True, s=1, dst=parts[0]).reshape(bsz, seq, d)
    return parts[0].reshape(bsz, seq, d)
```

```python
import functools

import jax
import jax.numpy as jnp
import numpy as np
from jax import lax
from jax.experimental import pallas as pl
from jax.experimental.pallas import tpu as pltpu
from jax.experimental.pallas import tpu_sc as plsc

F32 = jnp.float32
BF16 = jnp.bfloat16
U32 = jnp.uint32
I32 = jnp.int32

EPS = 1e-6
MASK_VALUE = -1e30
TOP_K = 4
HGRN_DIM = 128
HGRN_CHUNK = 32
SWIGLU_LIMIT = 7.0
SWIGLU_ALPHA = 1.702
LANES = 128
SUBLANES = 8
SC_CORES = 2
SC_SUBCORES = 16
SC_ROWS = 64
VMEM_LIMIT = 56 * 1024 * 1024
HIGHEST = lax.Precision.HIGHEST


def _cparams(*sem):
    return pltpu.CompilerParams(dimension_semantics=sem, vmem_limit_bytes=VMEM_LIMIT)


def _resident(shape, index_map):
    return pl.BlockSpec(shape, index_map, pipeline_mode=pl.Buffered(1))


def _sigmoid(x):
    return 1.0 / (1.0 + jnp.exp(-x))


def _bf16_bits(x):
    return lax.bitcast_convert_type(x.astype(BF16).astype(F32), U32)


def _pack_halves(x):
    n = x.shape[1] // 2
    lo = _bf16_bits(x[:, :n])
    hi = _bf16_bits(x[:, n:])
    return (hi & jnp.uint32(0xFFFF0000)) | (lo >> 16)


def _unpack_halves(w):
    lo = lax.bitcast_convert_type(w << 16, F32)
    hi = lax.bitcast_convert_type(w & jnp.uint32(0xFFFF0000), F32)
    return lo, hi


def _mod_kernel(c_ref, w_ref, b_ref, o_ref):
    c = c_ref[...]
    ca = c * _sigmoid(c)
    o_ref[0] = jnp.dot(ca, w_ref[0], preferred_element_type=F32, precision=HIGHEST) + b_ref[0]


def _adaln_mod(c, ada_w, ada_b):
    depth, d, n6 = ada_w.shape
    b = c.shape[0]
    nmod = n6 // d
    return pl.pallas_call(
        _mod_kernel,
        grid=(depth, nmod),
        in_specs=[
            pl.BlockSpec((b, d), lambda i, j: (0, 0)),
            pl.BlockSpec((1, d, d), lambda i, j: (i, 0, j)),
            pl.BlockSpec((1, 1, d), lambda i, j: (i, 0, j)),
        ],
        out_specs=pl.BlockSpec((1, b, d), lambda i, j: (i, 0, j)),
        out_shape=jax.ShapeDtypeStruct((depth, b, n6), F32),
        compiler_params=_cparams("parallel", "parallel"),
        name="adaln_mod",
    )(c, ada_w, ada_b.reshape(depth, 1, n6))


def _norm_mod(x, nw, sc, sh):
    ms = jnp.mean(x * x, axis=-1, keepdims=True)
    return x * lax.rsqrt(ms + EPS) * nw * (1.0 + sc) + sh


def _moe_mix(x_ref, y_refs, rw_ref, g2_ref):
    rw = rw_ref[...]
    lane = lax.broadcasted_iota(I32, rw.shape, 1)
    half = y_refs[0].shape[1]
    acc_lo = jnp.zeros(y_refs[0].shape, F32)
    acc_hi = jnp.zeros(y_refs[0].shape, F32)
    for k, y_ref in enumerate(y_refs):
        wk = jnp.sum(jnp.where(lane == k, rw, 0.0), axis=-1, keepdims=True)
        lo, hi = _unpack_halves(y_ref[...])
        acc_lo = acc_lo + wk * lo
        acc_hi = acc_hi + wk * hi
    g2 = g2_ref[...]
    return (x_ref[:, :half] + g2[:, :half] * acc_lo, x_ref[:, half:] + g2[:, half:] * acc_hi)


def _in_kernel(*refs, n_chunk, with_f, with_moe):
    n_in = 1 + (TOP_K + 2 if with_moe else 0) + 4 + (2 if with_f else 0)
    ins, outs = refs[:n_in], refs[n_in:]
    x_ref = ins[0]
    pos = 1
    if with_moe:
        lo, hi = _moe_mix(x_ref, ins[1:1 + TOP_K], ins[1 + TOP_K], ins[2 + TOP_K])
        x = jnp.concatenate([lo, hi], axis=1)
        outs[-1][...] = x
        pos += TOP_K + 2
    else:
        x = x_ref[...]
    nw_ref, sh_ref, sc_ref, w_ref = ins[pos:pos + 4]
    o_ref = outs[0]
    h = _norm_mod(x, nw_ref[...], sc_ref[...], sh_ref[...]).astype(BF16)
    if with_f:
        wf_ref, bf_ref = ins[pos + 4:pos + 6]
        of_ref = outs[1]
    n = o_ref.shape[1]
    for c0 in range(0, n, n_chunk):
        o_ref[:, c0:c0 + n_chunk] = jnp.dot(
            h, w_ref[:, c0:c0 + n_chunk], preferred_element_type=F32).astype(o_ref.dtype)
    if with_f:
        of_ref[...] = jnp.dot(h, wf_ref[...], preferred_element_type=F32) + bf_ref[...]


def _in_proj(x, nw, mod, w, out_dtype, seq, tm, wf=None, bf=None, rows=None, row_off=0, moe=None):
    d = x.shape[1]
    t = x.shape[0] if rows is None else rows
    n = w.shape[1]
    per_seq = seq // tm
    tile_off = row_off // tm
    nt = t // tm
    with_f = wf is not None
    mod_spec = lambda j: pl.BlockSpec((None, 1, d), lambda i: (i // per_seq, 0, j))
    in_specs = [pl.BlockSpec((tm, d), lambda i: (i + tile_off, 0))]
    args = [x]
    if moe is not None:
        yt, rw, mod_prev = moe
        in_specs += [pl.BlockSpec((tm, d // 2), lambda i, k=k: (k * nt + i, 0)) for k in range(TOP_K)]
        in_specs += [pl.BlockSpec((tm, LANES), lambda i: (i, 0)), mod_spec(5)]
        args += [yt] * TOP_K + [rw, mod_prev]
    in_specs += [_resident((1, d), lambda i: (0, 0)), mod_spec(0), mod_spec(1),
                 _resident((d, n), lambda i: (0, 0))]
    args += [nw, mod, mod, w]
    out_specs = [pl.BlockSpec((tm, n), lambda i: (i, 0))]
    out_shape = [jax.ShapeDtypeStruct((t, n), out_dtype)]
    if with_f:
        in_specs += [_resident((d, LANES), lambda i: (0, 0)), _resident((1, LANES), lambda i: (0, 0))]
        out_specs.append(pl.BlockSpec((tm, LANES), lambda i: (i, 0)))
        out_shape.append(jax.ShapeDtypeStruct((t, LANES), F32))
        args += [wf, bf]
    if moe is not None:
        out_specs.append(pl.BlockSpec((tm, d), lambda i: (i, 0)))
        out_shape.append(jax.ShapeDtypeStruct((t, d), F32))
    return pl.pallas_call(
        functools.partial(_in_kernel, n_chunk=min(n, 1024), with_f=with_f, with_moe=moe is not None),
        grid=(t // tm,),
        in_specs=in_specs,
        out_specs=out_specs,
        out_shape=out_shape,
        compiler_params=_cparams("parallel"),
        name="in_proj_fox" if with_f else "in_proj_hgrn",
    )(*args)


LOG2E = 1.4426950408889634
EXP2_SPAN = 120.0
SCORE_MARGIN = 1.01
BIAS_LANES = 16


def _split3(c):
    c1 = c.astype(BF16).astype(F32)
    r = c - c1
    c2 = r.astype(BF16).astype(F32)
    return c1, c2, r - c2


def _bias_layout(n_pairs):
    route =np.zeros((3, 3 * LANES, LANES), np.float32)
    const = np.zeros((3, 1, LANES), np.float32)
    for p in range(n_pairs):
        base = p * BIAS_LANES
        for i in range(3):
            const[0, 0, base + i] = 1.0
            for h in range(2):
                route[0, i * LANES + 2 * p + h, base + 3 + 3 * h + i] = -1.0
                route[1 + h, i * LANES + 2 * p + h, base + i] = 1.0
                const[1 + h, 0, base + 3 + 3 * h + i] = 1.0
    return jnp.asarray(route, BF16), jnp.asarray(const, F32)


def _cum_kernel(f_ref, ref_ref, route_ref, const_ref, o_ref, *, rows):
    s = f_ref.shape[0]
    r = lax.broadcasted_iota(I32, (rows, rows), 0)
    c = lax.broadcasted_iota(I32, (rows, rows), 1)
    tri = (r >= c).astype(F32)
    carry = jnp.zeros((1, LANES), F32)
    for r0 in range(0, s, rows):
        x = f_ref[r0:r0 + rows, :]
        lf = jnp.minimum(x, 0.0) - jnp.log(1.0 + jnp.exp(-jnp.abs(x)))
        cs = jnp.dot(tri, lf, preferred_element_type=F32, precision=HIGHEST) + carry
        carry = cs[rows - 1:rows, :]
        c2 = cs * LOG2E
        k_terms = jnp.concatenate(_split3(c2), axis=1).astype(BF16)
        q_terms = jnp.concatenate(_split3(c2 - ref_ref[...]), axis=1).astype(BF16)
        for j, terms in enumerate((k_terms, q_terms, q_terms)):
            routed = jnp.dot(terms, route_ref[j], preferred_element_type=F32) + const_ref[j]
            o_ref[r0:r0 + rows, j * LANES:(j + 1) * LANES] = routed.astype(o_ref.dtype)


def _fox_bias(flogit, ref, seq, n_pairs):
    t = flogit.shape[0]
    route, const = _bias_layout(n_pairs)
    return pl.pallas_call(
        functools.partial(_cum_kernel, rows=min(seq, 128)),
        grid=(t // seq,),
        in_specs=[pl.BlockSpec((seq, LANES), lambda b: (b, 0)),
                  pl.BlockSpec((1, LANES), lambda b: (0, 0)),
                  pl.BlockSpec((3, 3 * LANES, LANES), lambda b: (0, 0, 0)),
                  pl.BlockSpec((3, 1, LANES), lambda b: (0, 0, 0))],
        out_specs=pl.BlockSpec((seq, 3 * LANES), lambda b: (b, 0)),
        out_shape=jax.ShapeDtypeStruct((t, 3 * LANES), BF16),
        compiler_params=_cparams("parallel"),
        name="fox_bias",
    )(flogit, ref, route, const)


def _attn_kernel(q_ref, k_ref, v_ref, g_ref, bias_ref, qw_ref, kw_ref, o_ref, qs_s, ka_s, *,
                 dh, tq, bounded):
    s_len = q_ref.shape[0]
    nq = s_len // tq
    hp = pl.program_id(1)
    lane = lax.broadcasted_iota(I32, (1, LANES), 1)
    first = lane < dh

    def headnorm(z, w):
        z2 = z * z
        s0 = jnp.sum(jnp.where(first, z2, 0.0), axis=-1, keepdims=True)
        s1 = jnp.sum(jnp.where(first, 0.0, z2), axis=-1, keepdims=True)
        ms = jnp.where(first, s0, s1) * (1.0 / dh)
        return z * lax.rsqrt(ms + EPS) * w

    qn = headnorm(q_ref[...].astype(F32), qw_ref[...]) * (dh ** -0.5 * LOG2E)
    kn = headnorm(k_ref[...].astype(F32), kw_ref[...])
    mine = (lane // BIAS_LANES) == hp
    ka_s[:, :LANES] = kn.astype(BF16)
    ka_s[:, LANES:] = bias_ref[:, :LANES]
    q_lo = [jnp.where(first, qn, 0.0), jnp.where(first, 0.0, qn)]
    for qi in range(nq):
        rows = slice(qi * tq, (qi + 1) * tq)
        for h in range(2):
            q_hi = bias_ref[rows, (1 + h) * LANES:(2 + h) * LANES].astype(F32)
            qs_s[qi, h * tq:(h + 1) * tq, :LANES] = q_lo[h][rows, :].astype(BF16)
            qs_s[qi, h * tq:(h + 1) * tq, LANES:] = jnp.where(mine, q_hi, 0.0).astype(BF16)

    nt = (((1,), (1,)), ((), ()))

    def causal(s):
        rr = lax.broadcasted_iota(I32, (2 * tq, tq), 0)
        rr = jnp.where(rr >= tq, rr - tq, rr)
        cc = lax.broadcasted_iota(I32, (2 * tq, tq), 1)
        return jnp.where(rr >= cc, s, MASK_VALUE)

    def finish(qi, acc, l):
        o2 = acc / l
        o = jnp.where(first, o2[:tq], o2[tq:])
        g = g_ref[qi * tq:(qi + 1) * tq, :].astype(F32)
        o_ref[qi * tq:(qi + 1) * tq, :] = (o * _sigmoid(g)).astype(o_ref.dtype)

    if bounded:
        for qi in range(nq):
            acc = part = None
            for t in range(qi + 1):
                s = lax.dot_general(qs_s[qi], ka_s[t * tq:(t + 1) * tq, :], nt,
                                    preferred_element_type=F32)
                e = jnp.exp2(causal(s) if t == qi else s)
                cols = e[:, :LANES]
                for c0 in range(LANES, tq, LANES):
                    cols = cols + e[:, c0:c0 + LANES]
                pv = jnp.dot(e.astype(BF16), v_ref[t * tq:(t + 1) * tq, :],
                             preferred_element_type=F32)
                acc = pv if t == 0 else acc + pv
                part = cols if t == 0 else part + cols
            finish(qi, acc, jnp.sum(part, axis=-1, keepdims=True))
        return

    def scores(q, off):
        return lax.dot_general(q, ka_s[pl.ds(off, tq), :], nt, preferred_element_type=F32)

    def update(s, off, carry, masked):
        m, l, acc = carry
        if masked:
            s = causal(s)
        m_new = jnp.maximum(m, jnp.max(s, axis=-1, keepdims=True))
        p = jnp.exp2(s - m_new)
        alpha = jnp.exp2(m - m_new)
        l = alpha * l + jnp.sum(p, axis=-1, keepdims=True)
        acc = alpha * acc + jnp.dot(p.astype(BF16), v_ref[pl.ds(off, tq), :],
                                    preferred_element_type=F32)
        return m_new, l, acc

    for qi in range(nq):
        q = qs_s[qi]
        s = scores(q, 0)
        carry = (jnp.full((2 * tq, 1), MASK_VALUE, F32), jnp.zeros((2 * tq, 1), F32),
                 jnp.zeros((2 * tq, LANES), F32))
        if qi > 0:
            def body(j, c, q=q):
                off = pl.multiple_of(j * tq, tq)
                s_next = scores(q, off + tq)
                return (s_next,) + update(c[0], off, c[1:], False)

            s, *carry = lax.fori_loop(0, qi, body, (s,) + carry)
        m, l, acc = update(s, qi * tq, carry, True)
        finish(qi, acc, l)


def _fox_attention(qkvg, flogit, qw, kw, seq, d, dh):
    t = qkvg.shape[0]
    nb = d // LANES
    assert 2 * dh == LANES and nb * BIAS_LANES <= LANES
    tq = min(seq, 512)
    col = lambda off: pl.BlockSpec((seq, LANES), lambda b, p: (b, off + p))

    def call(bounded, bias):
        return pl.pallas_call(
            functools.partial(_attn_kernel, dh=dh, tq=tq, bounded=bounded),
            grid=(t // seq, nb),
            in_specs=[
                col(0), col(nb), col(2 * nb), col(3 * nb),
                pl.BlockSpec((seq, 3 * LANES), lambda b, p: (b, 0)),
                pl.BlockSpec((1, LANES), lambda b, p: (0, 0)),
                pl.BlockSpec((1, LANES), lambda b, p: (0, 0)),
            ],
            out_specs=pl.BlockSpec((seq, LANES), lambda b, p: (b, p)),
            out_shape=jax.ShapeDtypeStruct((t, d), BF16),
            scratch_shapes=[pltpu.VMEM((seq // tq, 2 * tq, 2 * LANES), BF16),
                            pltpu.VMEM((seq, 2 * LANES), BF16)],
            compiler_params=_cparams("parallel", "parallel"),
            name="fox_attention" if bounded else "fox_attention_running_max",
        )(qkvg, qkvg, qkvg, qkvg, bias, qw, kw)

    bound = (dh ** 0.5 * LOG2E * SCORE_MARGIN) * jnp.max(jnp.abs(qw)) * jnp.max(jnp.abs(kw)) + 0.1
    fits = bound <= EXP2_SPAN - 4.0
    ref = jnp.where(fits, bound - jnp.maximum(0.0, 2.0 * bound - EXP2_SPAN), 0.0)
    bias = _fox_bias(flogit, jnp.full((1, LANES), ref, F32), seq, nb)
    return lax.cond(fits, lambda: call(True, bias), lambda: call(False, bias))


HGRN_PREP_ROWS = 256
HGRN_LOG2_SPAN = 100.0


def _hgrn_kernel(q_ref, f_ref, v_ref, g_ref, lb_ref, gw_ref, o_ref, qe_s, ke_s, kd_s, dec_s, *,
                 hpb, chunk):
    s_len = q_ref.shape[0]
    n_chunks = s_len // chunk
    nt = (((1,), (1,)), ((), ()))
    gw = gw_ref[...]

    pr = min(HGRN_PREP_ROWS, s_len)
    rr = lax.broadcasted_iota(I32, (pr, pr), 0)
    cc = lax.broadcasted_iota(I32, (pr, pr), 1)
    same = (rr // chunk) == (cc // chunk)
    cum_m = jnp.where(same & (rr >= cc), 1.0, 0.0).astype(BF16)
    tot_m = jnp.where(same, 1.0, 0.0).astype(BF16)
    lb_all = lb_ref[...]
    lowest = jnp.zeros((1, lb_all.shape[1]), F32)
    for r0 in range(0, s_len, pr):
        sig = _sigmoid(f_ref[r0:r0 + pr, :])
        terms = [t.astype(BF16) for t in _split3(jnp.log2(lb_all + (1.0 - lb_all) * sig))]
        b = sum(jnp.dot(cum_m, t, preferred_element_type=F32) for t in terms)
        tot = sum(jnp.dot(tot_m, t, preferred_element_type=F32) for t in terms)
        k = (1.0 - lb_all) * (1.0 - sig)
        q_raw = q_ref[r0:r0 + pr, :]
        qe_s[r0:r0 + pr, :] = (q_raw * _sigmoid(q_raw) * jnp.exp2(b)).astype(BF16)
        ke_s[r0:r0 + pr, :] = (k * jnp.exp2(-b)).astype(BF16)
        kd_s[r0:r0 + pr, :] = (k * jnp.exp2(tot - b)).astype(BF16)
        for c0 in range(0, pr, chunk):
            ci = (r0 + c0) // chunk
            dec_s[ci] = jnp.exp2(tot[c0:c0 + SUBLANES, :])
        lowest = jnp.minimum(lowest, jnp.min(tot, axis=0, keepdims=True))
    mild = jnp.min(lowest) >= -HGRN_LOG2_SPAN

    @pl.when(mild)
    def _():
        r32 = lax.broadcasted_iota(I32, (chunk, chunk), 0)
        c32 = lax.broadcasted_iota(I32, (chunk, chunk), 1)
        causal = r32 >= c32
        per_step = 4 if n_chunks % 4 == 0 else 2

        def fast_body(step, states):
            insts = [(ch, hd) for ch in range(per_step) for hd in range(hpb)]
            rows = {ch: pl.ds(pl.multiple_of((step * per_step + ch) * chunk, chunk), chunk)
                    for ch in range(per_step)}
            cols = {hd: slice(hd * LANES, (hd + 1) * LANES) for hd in range(hpb)}
            states = list(states)
            qe = {i: qe_s[rows[i[0]], cols[i[1]]] for i in insts}
            v = {i: v_ref[rows[i[0]], cols[i[1]]] for i in insts}
            a = {i: lax.dot_general(qe[i], ke_s[rows[i[0]], cols[i[1]]], nt,
                                    preferred_element_type=F32) for i in insts}
            upd = {i: jnp.dot(jnp.transpose(v[i]).astype(BF16), kd_s[rows[i[0]], cols[i[1]]],
                              preferred_element_type=F32) for i in insts}
            o = {}
            for ch in range(per_step):
                for hd in range(hpb):
                    i = (ch, hd)
                    o[i] = (jnp.dot(jnp.where(causal, a[i], 0.0).astype(BF16), v[i].astype(BF16),
                                    preferred_element_type=F32)
                            + lax.dot_general(qe[i], states[hd].astype(BF16), nt,
                                              preferred_element_type=F32))
                    dec = dec_s[step * per_step + ch, 0:1, cols[hd]]
                    states[hd] = dec * states[hd] + upd[i]
            for i in insts:
                g = g_ref[rows[i[0]], cols[i[1]]]
                ms = jnp.mean(o[i] * o[i], axis=-1, keepdims=True)
                y = o[i] * lax.rsqrt(ms + EPS) * gw * (g * _sigmoid(g))
                o_ref[rows[i[0]], cols[i[1]]] = y.astype(o_ref.dtype)
            return tuple(states)

        init = tuple(jnp.zeros((LANES, LANES), F32) for _ in range(hpb))
        lax.fori_loop(0, n_chunks // per_step, fast_body, init)

    pair = 2
    groups = chunk // SUBLANES
    off_w = (groups - 1) * chunk
    r = lax.broadcasted_iota(I32, (chunk, chunk), 0)
    c = lax.broadcasted_iota(I32, (chunk, chunk), 1)
    tri = jnp.where(r >= c, 1.0, 0.0).astype(BF16)
    ro = lax.broadcasted_iota(I32, (chunk, off_w), 0) // SUBLANES
    co = lax.broadcasted_iota(I32, (chunk, off_w), 1) // chunk
    keep = ro == co + 1
    sub = lax.broadcasted_iota(I32, (SUBLANES, LANES), 0)
    gw = gw_ref[...]
    nt = (((1,), (1,)), ((), ()))
    grp = lambda a, i: a[i * SUBLANES:(i + 1) * SUBLANES, :]
    zeros8 = jnp.zeros((SUBLANES, LANES), F32)

    def body(step, states):
        insts = [(ch, hd) for ch in range(pair) for hd in range(hpb)]
        rows = {ch: pl.ds(pl.multiple_of((step * pair + ch) * chunk, chunk), chunk)
                for ch in range(pair)}
        cols = {hd: slice(hd * LANES, (hd + 1) * LANES) for hd in range(hpb)}
        st = {}
        for ch, hd in insts:
            lb = lb_ref[:, cols[hd]]
            q_raw = q_ref[rows[ch], cols[hd]]
            sig = _sigmoid(f_ref[rows[ch], cols[hd]])
            d = dict(v=v_ref[rows[ch], cols[hd]], q=q_raw * _sigmoid(q_raw),
                     k=(1.0 - lb) * (1.0 - sig))
            lf = jnp.log(lb + (1.0 - lb) * sig)
            d["b"] = sum(jnp.dot(tri, part.astype(BF16), preferred_element_type=F32)
                         for part in _split3(lf))
            st[ch, hd] = d
        def from_state(ch, hd, state_t):
            d = st[ch, hd]
            qe = (d["q"] * jnp.exp(d["b"])).astype(BF16)
            d["o"] = lax.dot_general(qe, state_t.astype(BF16), nt, preferred_element_type=F32)
        for hd in range(hpb):
            from_state(0, hd, states[hd])
        for ch, hd in insts:
            d = st[ch, hd]
            b, q, k = d["b"], d["q"], d["k"]
            ends = [b[(i + 1) * SUBLANES - 1:(i + 1) * SUBLANES, :] for i in range(groups)]
            kt = [grp(k, i) * jnp.exp(ends[i] - grp(b, i)) for i in range(groups)]
            qh = jnp.concatenate(
                [zeros8] + [grp(q, i) * jnp.exp(grp(b, i) - ends[i - 1]) for i in range(1, groups)],
                axis=0).astype(BF16)
            kh = jnp.concatenate(
                [kt[j] * jnp.exp(ends[i - 1] - ends[j]) if j < i else zeros8
                 for i in range(1, groups) for j in range(groups)], axis=0).astype(BF16)
            d["a"] = lax.dot_general(qh, kh, nt, preferred_element_type=F32)
            d["ends"], d["kt"] = ends, kt
        def state_update(ch, hd, state_t):
            d = st[ch, hd]
            b_end = d["ends"][-1]
            ke = jnp.concatenate([d["kt"][j] * jnp.exp(b_end - d["ends"][j]) for j in range(groups)],
                                 axis=0).astype(BF16)
            upd = jnp.dot(jnp.transpose(d["v"]).astype(BF16), ke, preferred_element_type=F32)
            return jnp.exp(b_end) * state_t + upd
        states = [state_update(0, hd, states[hd]) for hd in range(hpb)]
        for ch, hd in insts:
            d = st[ch, hd]
            vb = d["v"].astype(BF16)
            v3 = jnp.concatenate([vb] * (groups - 1), axis=0)
            d["off"] = jnp.dot(jnp.where(keep, d["a"], 0.0).astype(BF16), v3,
                               preferred_element_type=F32)
        for hd in range(hpb):
            from_state(1, hd, states[hd])
        states = [state_update(1, hd, states[hd]) for hd in range(hpb)]
        for ch, hd in insts:
            d = st[ch, hd]
            o = d["o"] + d["off"]
            o_g = []
            for gi in range(groups):
                qg, kg, vg, bg = grp(d["q"], gi), grp(d["k"], gi), grp(d["v"], gi), grp(d["b"], gi)
                og = grp(o, gi)
                for s in range(SUBLANES):
                    live = sub >= s
                    e = jnp.exp(jnp.where(live, bg - bg[s:s + 1, :], 0.0))
                    p = jnp.where(live, qg * e * kg[s:s + 1, :], 0.0)
                    og = og + jnp.sum(p, axis=-1, keepdims=True) * vg[s:s + 1, :]
                o_g.append(og)
            o = jnp.concatenate(o_g, axis=0)
            g = g_ref[rows[ch], cols[hd]]
            ms = jnp.mean(o * o, axis=-1, keepdims=True)
            y = o * lax.rsqrt(ms + EPS) * gw * (g * _sigmoid(g))
            o_ref[rows[ch], cols[hd]] = y.astype(o_ref.dtype)
        return tuple(states)

    @pl.when(jnp.logical_not(mild))
    def _():
        init = tuple(jnp.zeros((LANES, LANES), F32) for _ in range(hpb))
        lax.fori_loop(0, s_len // (chunk * pair), body, init)


def _hgrn_mixer(proj, lower, gw, seq, d, hpb):
    t = proj.shape[0]
    w = hpb * LANES
    nb = d // w
    col = lambda off: pl.BlockSpec((seq, w), lambda b, p: (b, off + p))
    return pl.pallas_call(
        functools.partial(_hgrn_kernel, hpb=hpb, chunk=HGRN_CHUNK),
        grid=(t // seq, nb),
        in_specs=[
            col(0), col(nb), col(2 * nb), col(3 * nb),
            pl.BlockSpec((1, w), lambda b, p: (0, p)),
            pl.BlockSpec((1, LANES), lambda b, p: (0, 0)),
        ],
        out_specs=pl.BlockSpec((seq, w), lambda b, p: (b, p)),
        out_shape=jax.ShapeDtypeStruct((t, d), BF16),
        scratch_shapes=[pltpu.VMEM((seq, w), BF16)] * 3
        + [pltpu.VMEM((seq // HGRN_CHUNK, SUBLANES, w), F32)],
        compiler_params=_cparams("parallel", "parallel"),
        name="hgrn_mixer",
    )(proj, proj, proj, proj, lower, gw)


def _post_kernel(o_ref, x_ref, wo_ref, g1_ref, nw_ref, sh_ref, sc_ref, wr_ref, br_ref,
                 x1_ref, hp_ref, ri_ref, rw_ref, cnt_ref, run_s):
    i = pl.program_id(0)
    tm = x_ref.shape[0]

    @pl.when(i == 0)
    def _():
        run_s[...] = jnp.zeros_like(run_s)

    y = jnp.dot(o_ref[...], wo_ref[...], preferred_element_type=F32)
    x1 = x_ref[...] + g1_ref[...] * y
    x1_ref[...] = x1
    h2 = _norm_mod(x1, nw_ref[...], sc_ref[...], sh_ref[...])
    hp_ref[...] = _pack_halves(h2)
    logits = jnp.dot(h2.astype(BF16), wr_ref[...], preferred_element_type=F32) + br_ref[...]

    lane = lax.broadcasted_iota(I32, (tm, LANES), 1).astype(F32)
    work = logits
    idx, val = [], []
    for _ in range(TOP_K):
        mx = jnp.max(work, axis=-1, keepdims=True)
        ix = jnp.min(jnp.where(work == mx, lane, float(LANES)), axis=-1, keepdims=True)
        idx.append(ix)
        val.append(mx)
        work = jnp.where(lane == ix, -jnp.inf, work)
    ex = [jnp.exp(v - val[0]) for v in val]
    den = ex[0] + ex[1] + ex[2] + ex[3]
    wts = [e / den for e in ex]

    hot = jnp.zeros((tm, LANES), F32)
    for ix in idx:
        hot = hot + jnp.where(lane == ix, 1.0, 0.0)
    r = lax.broadcasted_iota(I32, (tm, tm), 0)
    c = lax.broadcasted_iota(I32, (tm, tm), 1)
    strict = jnp.where(r > c, 1.0, 0.0).astype(BF16)
    before = jnp.dot(strict, hot.astype(BF16), preferred_element_type=F32) + run_s[...]
    ranks = [jnp.sum(jnp.where(lane == ix, before, 0.0), axis=-1, keepdims=True) for ix in idx]
    run_s[...] = run_s[...] + jnp.sum(hot, axis=0, keepdims=True)

    ri = jnp.zeros((tm, LANES), F32)
    rw = jnp.zeros((tm, LANES), F32)
    for k in range(TOP_K):
        ri = jnp.where(lane == float(k), idx[k], ri)
        ri = jnp.where(lane == float(TOP_K + k), ranks[k], ri)
        rw = jnp.where(lane == float(k), wts[k], rw)
    ri_ref[...] = jnp.transpose(ri)[:2 * TOP_K, :].astype(I32)
    rw_ref[...] = rw
    cnt_ref[...] = run_s[...].astype(I32)


def _post_mixer(o, x, wo, mod, nw2, wr, br, seq, tm, row_off=0):
    t, d = o.shape
    per_seq = seq // tm
    tile_off = row_off // tm
    mod_spec = lambda j: pl.BlockSpec((None, 1, d), lambda i: (i // per_seq, 0, j))
    row = lambda w: pl.BlockSpec((tm, w), lambda i: (i, 0))
    return pl.pallas_call(
        _post_kernel,
        grid=(t // tm,),
        in_specs=[
            row(d), pl.BlockSpec((tm, d), lambda i: (i + tile_off, 0)),
            _resident((d, d), lambda i: (0, 0)),
            mod_spec(2),
            _resident((1, d), lambda i: (0, 0)),
            mod_spec(3), mod_spec(4),
            _resident((d, LANES), lambda i: (0, 0)),
            _resident((1, LANES), lambda i: (0, 0)),
        ],
        out_specs=[row(d), row(d // 2), pl.BlockSpec((2 * TOP_K, tm), lambda i: (0, i)), row(LANES),
                   pl.BlockSpec((1, LANES), lambda i: (0, 0))],
        out_shape=[
            jax.ShapeDtypeStruct((t, d), F32),
            jax.ShapeDtypeStruct((t, d // 2), U32),
            jax.ShapeDtypeStruct((2 * TOP_K, t), I32),
            jax.ShapeDtypeStruct((t, LANES), F32),
            jax.ShapeDtypeStruct((1, LANES), I32),
        ],
        scratch_shapes=[pltpu.VMEM((1, LANES), F32)],
        compiler_params=_cparams("arbitrary"),
        name="post_mixer",
    )(o, x, wo, mod, nw2, mod, mod, wr, br)


def _sc_mesh():
    return plsc.VectorSubcoreMesh(core_axis_name="c", subcore_axis_name="s")


def _sc_scatter_rows(x, posf, n_rows):
    t, w = x.shape
    workers = SC_CORES * SC_SUBCORES
    per_w = t // workers

    @functools.partial(
        pl.kernel, mesh=_sc_mesh(),
        out_type=jax.ShapeDtypeStruct((n_rows, w), x.dtype),
        scratch_types=[pltpu.VMEM((SC_ROWS,), I32)] * TOP_K
        + [pltpu.VMEM((SC_ROWS, w), x.dtype), pltpu.SemaphoreType.DMA])
    def scatter_kernel(x_hbm, pos_hbm, out_hbm, i0, i1, i2, i3, rows_v, sem):
        idxs = (i0, i1, i2, i3)
        wid = lax.axis_index("s") * SC_CORES + lax.axis_index("c")
        base = wid * per_w

        @pl.loop(0, per_w // SC_ROWS)
        def _(i):
            off = base + i * SC_ROWS
            for k in range(TOP_K):
                pltpu.sync_copy(pos_hbm.at[pl.ds(k * t + off, SC_ROWS)], idxs[k])
            pltpu.sync_copy(x_hbm.at[pl.ds(off, SC_ROWS)], rows_v)
            copies = [pltpu.async_copy(rows_v, out_hbm.at[idxs[k]], sem) for k in range(TOP_K)]
            for cp in copies:
                cp.wait()

    return scatter_kernel(x, posf)


def _sc_gather_rows(src, posf):
    n = posf.shape[0]
    w = src.shape[1]
    workers = SC_CORES * SC_SUBCORES
    per_w = n // workers

    @functools.partial(
        pl.kernel, mesh=_sc_mesh(),
        out_type=jax.ShapeDtypeStruct((n, w), src.dtype),
        scratch_types=[pltpu.VMEM((SC_ROWS,), I32), pltpu.VMEM((SC_ROWS, w), src.dtype),
                       pltpu.SemaphoreType.DMA])
    def gather_kernel(src_hbm, pos_hbm, out_hbm, idx_v, rows_v, sem):
        wid = lax.axis_index("s") * SC_CORES + lax.axis_index("c")
        base = wid * per_w

        @pl.loop(0, per_w // SC_ROWS)
        def _(i):
            off = base + i * SC_ROWS
            pltpu.sync_copy(pos_hbm.at[pl.ds(off, SC_ROWS)], idx_v)
            pltpu.async_copy(src_hbm.at[idx_v], rows_v, sem).wait()
            pltpu.sync_copy(rows_v, out_hbm.at[pl.ds(off, SC_ROWS)])

    return gather_kernel(src, posf)


GU_BLOCK = 2 * LANES
FF_CHUNK = 2 * LANES


def _deinterleave_matrix():
    sel = np.zeros((GU_BLOCK, GU_BLOCK), np.float32)
    j = np.arange(LANES)
    sel[2 * j, j] = 1.0
    sel[2 * j + 1, LANES + j] = 1.0
    return jnp.asarray(sel, BF16)


def _expert_kernel(be_ref, nu_ref, nxt_ref, x_ref, wgu_hbm, bg_ref, bu_ref, wd_hbm, bd_ref, sel_ref,
                   y_ref, wgu_f, wd_f, wgu_s, wd_s, sems, *, layer):
    i = pl.program_id(0)
    d, f2 = wgu_s.shape
    f = f2 // 2
    e_cur = be_ref[i]
    new_expert = jnp.logical_or(i == 0, e_cur != be_ref[jnp.maximum(i - 1, 0)])

    def weight_copies(e):
        return (pltpu.make_async_copy(wgu_hbm.at[layer, e], wgu_f, sems.at[0]),
                pltpu.make_async_copy(wd_hbm.at[layer, e], wd_f, sems.at[1]))

    @pl.when(jnp.logical_and(new_expert, i < nu_ref[0]))
    def _():
        @pl.when(i == 0)
        def _():
            for cp in weight_copies(e_cur):
                cp.start()

        for cp in weight_copies(e_cur):
            cp.wait()
        for c0 in range(0, f2, GU_BLOCK):
            blk = wgu_f[:, c0:c0 + GU_BLOCK].astype(BF16)
            wgu_s[:, c0:c0 + GU_BLOCK] = jnp.dot(
                blk, sel_ref[...], preferred_element_type=F32).astype(BF16)
        wd_s[...] = wd_f[...].astype(BF16)

        @pl.when(nxt_ref[e_cur] >= 0)
        def _():
            for cp in weight_copies(nxt_ref[e_cur]):
                cp.start()

    @pl.when(i < nu_ref[0])
    def _():
        half = x_ref.shape[1]
        lo, hi = _unpack_halves(x_ref[...])
        lo = lo.astype(BF16)
        hi = hi.astype(BF16)

        def hidden(c0):
            cols = slice(2 * c0, 2 * (c0 + FF_CHUNK))
            gu = (jnp.dot(lo, wgu_s[:half, cols], preferred_element_type=F32)
                  + jnp.dot(hi, wgu_s[half:, cols], preferred_element_type=F32))
            parts = []
            for b0 in range(0, FF_CHUNK, LANES):
                gate = gu[:, 2 * b0:2 * b0 + LANES] + bg_ref[0, :, c0 + b0:c0 + b0 + LANES]
                up = gu[:, 2 * b0 + LANES:2 * b0 + 2 * LANES] + bu_ref[0, :, c0 + b0:c0 + b0 + LANES]
                gate = jnp.minimum(gate, SWIGLU_LIMIT)
                up = jnp.clip(up, -SWIGLU_LIMIT, SWIGLU_LIMIT)
                parts.append((up + 1.0) * gate * _sigmoid(SWIGLU_ALPHA * gate))
            return jnp.concatenate(parts, axis=1).astype(BF16)

        chunks = list(range(0, f, FF_CHUNK))
        y = None
        h_next = hidden(chunks[0])
        for n, c0 in enumerate(chunks):
            h = h_next
            if n + 1 < len(chunks):
                h_next = hidden(chunks[n + 1])
            part = jnp.dot(h, wd_s[c0:c0 + FF_CHUNK, :], preferred_element_type=F32)
            y = part if y is None else y + part
        y_ref[...] = _pack_halves(y + bd_ref[0])

    @pl.when(i >= nu_ref[0])
    def _():
        y_ref[...] = jnp.zeros_like(y_ref)


def _expert_ffn(xs, blk_e, n_used, nxt_e, w_gu, bg, bu, w_down, bd, rb, layer):
    rows, half = xs.shape
    _, e, d, f2 = w_gu.shape
    f = f2 // 2
    n_blocks = rows // rb
    wspec = lambda shape: pl.BlockSpec(shape, lambda i, be, nu, nx: (be[i], 0, 0))
    return pl.pallas_call(
        functools.partial(_expert_kernel, layer=layer),
        grid_spec=pltpu.PrefetchScalarGridSpec(
            num_scalar_prefetch=3,
            grid=(n_blocks,),
            in_specs=[
                pl.BlockSpec((rb, half), lambda i, be, nu, nx: (i, 0)),
                pl.BlockSpec(memory_space=pl.ANY),
                wspec((1, 1, f)), wspec((1, 1, f)),
                pl.BlockSpec(memory_space=pl.ANY),
                wspec((1, 1, d)),
                _resident((GU_BLOCK, GU_BLOCK), lambda i, be, nu, nx: (0, 0)),
            ],
            out_specs=pl.BlockSpec((rb, d // 2), lambda i, be, nu, nx: (i, 0)),
            scratch_shapes=[pltpu.VMEM((d, f2), F32), pltpu.VMEM((f, d), F32),
                            pltpu.VMEM((d, f2), BF16), pltpu.VMEM((f, d), BF16),
                            pltpu.SemaphoreType.DMA((2,))],
        ),
        out_shape=jax.ShapeDtypeStruct((rows, d // 2), U32),
        compiler_params=_cparams("arbitrary"),
        name="expert_ffn",
    )(blk_e, n_used, nxt_e, xs, w_gu, bg, bu, w_down, bd, _deinterleave_matrix())


def _combine_kernel(x_ref, y0_ref, y1_ref, y2_ref, y3_ref, rw_ref, g2_ref, *dst_and_out):
    o_ref = dst_and_out[-1]
    half = y0_ref.shape[1]
    lo, hi = _moe_mix(x_ref, (y0_ref, y1_ref, y2_ref, y3_ref), rw_ref, g2_ref)
    o_ref[:, :half] = lo
    o_ref[:, half:] = hi


def _combine(x1, yt, rw, mod, seq, tm, out_rows=None, row_off=0, dst=None):
    t, d = x1.shape
    per_seq = seq // tm
    nt = t // tm
    tile_off = row_off // tm
    yspec = lambda k: pl.BlockSpec((tm, d // 2), lambda i: (k * nt + i, 0))
    in_specs = [
        pl.BlockSpec((tm, d), lambda i: (i, 0)),
        yspec(0), yspec(1), yspec(2), yspec(3),
        pl.BlockSpec((tm, LANES), lambda i: (i, 0)),
        pl.BlockSpec((None, 1, d), lambda i: (i // per_seq, 0, 5)),
    ]
    args = [x1, yt, yt, yt, yt, rw, mod]
    aliases = {}
    if dst is not None:
        in_specs.append(pl.BlockSpec(memory_space=pl.ANY))
        args.append(dst)
        aliases = {len(args) - 1: 0}
    return pl.pallas_call(
        _combine_kernel,
        grid=(nt,),
        in_specs=in_specs,
        out_specs=pl.BlockSpec((tm, d), lambda i: (i + tile_off, 0)),
        out_shape=jax.ShapeDtypeStruct((t if out_rows is None else out_rows, d), F32),
        input_output_aliases=aliases,
        compiler_params=_cparams("parallel"),
        name="moe_combine",
    )(*args)


POST_ROWS = 1024


def _row_tile(seq):
    return min(seq, 512)


def _moe_block_rows(t):
    return min(512, t * TOP_K // 8)


def kernel(x, c, fox_w_in, fox_b_f, fox_q_norm, fox_k_norm, fox_w_out, hgrn_w_in, hgrn_lb, hgrn_g_norm, hgrn_w_out, ada_w, ada_b, norm1_w, norm2_w, router_w, router_b, exp_w_gu, exp_b_gu, exp_w_down, exp_b_down):
    bsz, seq, d = x.shape
    t = bsz * seq
    depth = ada_w.shape[0]
    n_exp = router_w.shape[-1]
    f_dim = exp_w_down.shape[2]
    h_fox = fox_w_in.shape[-1] - 4 * d
    dh = d // h_fox
    tm = _row_tile(seq)
    n_split = 2 if bsz % 2 == 0 else 1
    bh, th = bsz // n_split, t // n_split
    rb = _moe_block_rows(th)
    n_blocks = (th * TOP_K) // rb + n_exp
    hgrn_hpb = 4 if d % (4 * LANES) == 0 else 2

    mod_all = _adaln_mod(c, ada_w, ada_b).reshape(depth, bsz, 1, -1)
    lb_soft = jax.nn.softmax(hgrn_lb.astype(F32), axis=0)
    lower = jnp.cumsum(lb_soft, axis=0) - lb_soft[0]
    experts = jnp.arange(n_exp, dtype=I32)

    xf = x.reshape(t, d)
    parts = [xf] * n_split
    for i in range(depth):
        j = i // 2
        nw1 = norm1_w[i].reshape(1, d)
        if i % 2 == 0:
            w_in = fox_w_in[j]
            w_main = w_in[:, :4 * d].astype(BF16)
            w_f = jnp.pad(w_in[:, 4 * d:], ((0, 0), (0, LANES - h_fox))).astype(BF16)
            b_f = jnp.pad(fox_b_f[j], (0, LANES - h_fox)).reshape(1, LANES)
            qw = jnp.tile(fox_q_norm[j], LANES // dh).reshape(1, LANES)
            kw = jnp.tile(fox_k_norm[j], LANES // dh).reshape(1, LANES)
            w_out = fox_w_out[j].astype(BF16)
        else:
            w_main = hgrn_w_in[j].astype(BF16)
            w_out = hgrn_w_out[j].astype(BF16)
        wr = jnp.pad(router_w[i], ((0, 0), (0, LANES - n_exp))).astype(BF16)
        br = jnp.pad(router_b[i], (0, LANES - n_exp), constant_values=MASK_VALUE).reshape(1, LANES)
        b_gu = exp_b_gu[i].reshape(n_exp, 1, f_dim, 2)

        def front(s, src):
            mod = mod_all[i, s * bh:(s + 1) * bh]
            off = s * th if i == 0 else 0
            moe = None
            xp = src
            if isinstance(src, tuple):
                xp, moe = src[0], src[1:]
            if i % 2 == 0:
                outs = _in_proj(xp, nw1, mod, w_main, BF16, seq, tm, w_f, b_f,
                                rows=th, row_off=off, moe=moe)
                o = _fox_attention(outs[0], outs[1], qw, kw, seq, d, dh)
            else:
                outs = _in_proj(xp, nw1, mod, w_main, F32, seq, tm, rows=th, row_off=off, moe=moe)
                o = _hgrn_mixer(outs[0], lower[j].reshape(1, d), hgrn_g_norm[j].reshape(1, HGRN_DIM),
                                seq, d, hgrn_hpb)
            if moe is not None:
                xp = outs[-1]
            x1, hp, ri, rw, cnt = _post_mixer(o, xp, w_out, mod, norm2_w[i].reshape(1, d), wr, br,
                                              seq, min(seq, POST_ROWS), row_off=off)
            counts = cnt[0, :n_exp]
            padded = (counts + rb - 1) // rb * rb
            pad_end = jnp.cumsum(padded)
            pad_start = pad_end - padded
            top_idx, rank = ri[:TOP_K], ri[TOP_K:]
            start = jnp.sum(jnp.where(top_idx[None] == experts[:, None, None],
                                      pad_start[:, None, None], 0), axis=0)
            posf = (start + rank).reshape(-1).astype(I32)
            blk_lo = jnp.arange(n_blocks, dtype=I32) * rb
            blk_e = jnp.minimum(jnp.sum(pad_end[None, :] <= blk_lo[:, None], axis=1),
                                n_exp - 1).astype(I32)
            n_used = (pad_end[-1] // rb).astype(I32).reshape(1)
            later = jnp.where((experts[None, :] > experts[:, None]) & (padded[None, :] > 0),
                              experts[None, :], n_exp)
            nxt_e = jnp.min(later, axis=1)
            nxt_e = jnp.where(nxt_e >= n_exp, -1, nxt_e).astype(I32)
            return dict(x1=x1, hp=hp, rw=rw, posf=posf, blk_e=blk_e, n_used=n_used, nxt_e=nxt_e,
                        mod=mod)

        def experts_of(st, xs):
            return _expert_ffn(xs, st["blk_e"], st["n_used"], st["nxt_e"], exp_w_gu,
                               b_gu[..., 0], b_gu[..., 1], exp_w_down,
                               exp_b_down[i].reshape(n_exp, 1, d), rb, i)

        def deferred(st, yt):
            return (st["x1"], yt, st["rw"], st["mod"])

        if n_split == 1:
            st = front(0, parts[0])
            xs = _sc_scatter_rows(st["hp"], st["posf"], n_blocks * rb)
            parts[0] = deferred(st, _sc_gather_rows(experts_of(st, xs), st["posf"]))
            continue

        tie = lax.optimization_barrier
        sa = front(0, parts[0])
        xs_a = _sc_scatter_rows(sa["hp"], sa["posf"], n_blocks * rb)
        src_b = parts[1]
        if isinstance(src_b, tuple):
            yt_b, _ = tie((src_b[1], sa["hp"]))
            src_b = (src_b[0], yt_b) + src_b[2:]
        else:
            src_b, _ = tie((src_b, sa["hp"]))
        sb = front(1, src_b)
        xs_a, _ = tie((xs_a, sb["hp"]))
        xs_b = _sc_scatter_rows(sb["hp"], sb["posf"], n_blocks * rb)
        ys_a = experts_of(sa, xs_a)
        xs_b, _ = tie((xs_b, ys_a))
        yt_a = _sc_gather_rows(ys_a, sa["posf"])
        ys_b = experts_of(sb, xs_b)
        yt_a, _ = tie((yt_a, ys_b))
        yt_b = _sc_gather_rows(ys_b, sb["posf"])
        parts = [deferred(sa, yt_a), deferred(sb, yt_b)]

    out = None
    for s_i, (x1, yt, rw, mod) in enumerate(parts):
        if n_split == 1:
            out = _combine(x1, yt, rw, mod, seq, tm)
        else:
            out = _combine(x1, yt, rw, mod, seq, tm, out_rows=t, row_off=s_i * th, dst=out)
    return out.reshape(bsz, seq, d)
```

```python
import functools

import jax
import jax.numpy as jnp
import numpy as np
from jax import lax
from jax.experimental import pallas as pl
from jax.experimental.pallas import tpu as pltpu
from jax.experimental.pallas import tpu_sc as plsc

F32 = jnp.float32
BF16 = jnp.bfloat16
U32 = jnp.uint32
I32 = jnp.int32

EPS = 1e-6
MASK_VALUE = -1e30
TOP_K = 4
HGRN_DIM = 128
HGRN_CHUNK = 32
SWIGLU_LIMIT = 7.0
SWIGLU_ALPHA = 1.702
LANES = 128
SUBLANES = 8
SC_CORES = 2
SC_SUBCORES = 16
SC_ROWS = 64
VMEM_LIMIT = 56 * 1024 * 1024
HIGHEST = lax.Precision.HIGHEST


def _cparams(*sem):
    return pltpu.CompilerParams(dimension_semantics=sem, vmem_limit_bytes=VMEM_LIMIT)


def _resident(shape, index_map):
    return pl.BlockSpec(shape, index_map, pipeline_mode=pl.Buffered(1))


def _sigmoid(x):
    return 1.0 / (1.0 + jnp.exp(-x))


def _bf16_bits(x):
    return lax.bitcast_convert_type(x.astype(BF16).astype(F32), U32)


def _pack_halves(x):
    n = x.shape[1] // 2
    lo = _bf16_bits(x[:, :n])
    hi = _bf16_bits(x[:, n:])
    return (hi & jnp.uint32(0xFFFF0000)) | (lo >> 16)


def _unpack_halves(w):
    lo = lax.bitcast_convert_type(w << 16, F32)
    hi = lax.bitcast_convert_type(w & jnp.uint32(0xFFFF0000), F32)
    return lo, hi


def _mod_kernel(c_ref, w_ref, b_ref, o_ref):
    c = c_ref[...]
    ca = c * _sigmoid(c)
    o_ref[0] = jnp.dot(ca, w_ref[0], preferred_element_type=F32, precision=HIGHEST) + b_ref[0]


def _adaln_mod(c, ada_w, ada_b):
    depth, d, n6 = ada_w.shape
    b = c.shape[0]
    nmod = n6 // d
    return pl.pallas_call(
        _mod_kernel,
        grid=(depth, nmod),
        in_specs=[
            pl.BlockSpec((b, d), lambda i, j: (0, 0)),
            pl.BlockSpec((1, d, d), lambda i, j: (i, 0, j)),
            pl.BlockSpec((1, 1, d), lambda i, j: (i, 0, j)),
        ],
        out_specs=pl.BlockSpec((1, b, d), lambda i, j: (i, 0, j)),
        out_shape=jax.ShapeDtypeStruct((depth, b, n6), F32),
        compiler_params=_cparams("parallel", "parallel"),
        name="adaln_mod",
    )(c, ada_w, ada_b.reshape(depth, 1, n6))


def _norm_mod(x, nw, sc, sh):
    ms = jnp.mean(x * x, axis=-1, keepdims=True)
    return x * lax.rsqrt(ms + EPS) * nw * (1.0 + sc) + sh


def _moe_mix(x_ref, y_refs, rw_ref, g2_ref):
    rw = rw_ref[...]
    lane = lax.broadcasted_iota(I32, rw.shape, 1)
    half = y_refs[0].shape[1]
    acc_lo = jnp.zeros(y_refs[0].shape, F32)
    acc_hi = jnp.zeros(y_refs[0].shape, F32)
    for k, y_ref in enumerate(y_refs):
        wk = jnp.sum(jnp.where(lane == k, rw, 0.0), axis=-1, keepdims=True)
        lo, hi = _unpack_halves(y_ref[...])
        acc_lo = acc_lo + wk * lo
        acc_hi = acc_hi + wk * hi
    g2 = g2_ref[...]
    return (x_ref[:, :half] + g2[:, :half] * acc_lo, x_ref[:, half:] + g2[:, half:] * acc_hi)


def _in_kernel(*refs, n_chunk, with_f, with_moe):
    n_in = 1 + (TOP_K + 2 if with_moe else 0) + 4 + (2 if with_f else 0)
    ins, outs = refs[:n_in], refs[n_in:]
    x_ref = ins[0]
    pos = 1
    if with_moe:
        lo, hi = _moe_mix(x_ref, ins[1:1 + TOP_K], ins[1 + TOP_K], ins[2 + TOP_K])
        x = jnp.concatenate([lo, hi], axis=1)
        outs[-1][...] = x
        pos += TOP_K + 2
    else:
        x = x_ref[...]
    nw_ref, sh_ref, sc_ref, w_ref = ins[pos:pos + 4]
    o_ref = outs[0]
    h = _norm_mod(x, nw_ref[...], sc_ref[...], sh_ref[...]).astype(BF16)
    if with_f:
        wf_ref, bf_ref = ins[pos + 4:pos + 6]
        of_ref = outs[1]
    n = o_ref.shape[1]
    for c0 in range(0, n, n_chunk):
        o_ref[:, c0:c0 + n_chunk] = jnp.dot(
            h, w_ref[:, c0:c0 + n_chunk], preferred_element_type=F32).astype(o_ref.dtype)
    if with_f:
        of_ref[...] = jnp.dot(h, wf_ref[...], preferred_element_type=F32) + bf_ref[...]


def _in_proj(x, nw, mod, w, out_dtype, seq, tm, wf=None, bf=None, rows=None, row_off=0, moe=None):
    d = x.shape[1]
    t = x.shape[0] if rows is None else rows
    n = w.shape[1]
    per_seq = seq // tm
    tile_off = row_off // tm
    nt = t // tm
    with_f = wf is not None
    mod_spec = lambda j: pl.BlockSpec((None, 1, d), lambda i: (i // per_seq, 0, j))
    in_specs = [pl.BlockSpec((tm, d), lambda i: (i + tile_off, 0))]
    args = [x]
    if moe is not None:
        yt, rw, mod_prev = moe
        in_specs += [pl.BlockSpec((tm, d // 2), lambda i, k=k: (k * nt + i, 0)) for k in range(TOP_K)]
        in_specs += [pl.BlockSpec((tm, LANES), lambda i: (i, 0)), mod_spec(5)]
        args += [yt] * TOP_K + [rw, mod_prev]
    in_specs += [_resident((1, d), lambda i: (0, 0)), mod_spec(0), mod_spec(1),
                 _resident((d, n), lambda i: (0, 0))]
    args += [nw, mod, mod, w]
    out_specs = [pl.BlockSpec((tm, n), lambda i: (i, 0))]
    out_shape = [jax.ShapeDtypeStruct((t, n), out_dtype)]
    if with_f:
        in_specs += [_resident((d, LANES), lambda i: (0, 0)), _resident((1, LANES), lambda i: (0, 0))]
        out_specs.append(pl.BlockSpec((tm, LANES), lambda i: (i, 0)))
        out_shape.append(jax.ShapeDtypeStruct((t, LANES), F32))
        args += [wf, bf]
    if moe is not None:
        out_specs.append(pl.BlockSpec((tm, d), lambda i: (i, 0)))
        out_shape.append(jax.ShapeDtypeStruct((t, d), F32))
    return pl.pallas_call(
        functools.partial(_in_kernel, n_chunk=min(n, 1024), with_f=with_f, with_moe=moe is not None),
        grid=(t // tm,),
        in_specs=in_specs,
        out_specs=out_specs,
        out_shape=out_shape,
        compiler_params=_cparams("parallel"),
        name="in_proj_fox" if with_f else "in_proj_hgrn",
    )(*args)


LOG2E = 1.4426950408889634
EXP2_SPAN = 120.0
SCORE_MARGIN = 1.01
BIAS_LANES = 16


def _split3(c):
    c1 = c.astype(BF16).astype(F32)
    r = c - c1
    c2 = r.astype(BF16).astype(F32)
    return c1, c2, r - c2


def _bias_layout(n_pairs):
    route =np.zeros((3, 3 * LANES, LANES), np.float32)
    const = np.zeros((3, 1, LANES), np.float32)
    for p in range(n_pairs):
        base = p * BIAS_LANES
        for i in range(3):
            const[0, 0, base + i] = 1.0
            for h in range(2):
                route[0, i * LANES + 2 * p + h, base + 3 + 3 * h + i] = -1.0
                route[1 + h, i * LANES + 2 * p + h, base + i] = 1.0
                const[1 + h, 0, base + 3 + 3 * h + i] = 1.0
    return jnp.asarray(route, BF16), jnp.asarray(const, F32)


def _cum_kernel(f_ref, ref_ref, route_ref, const_ref, o_ref, *, rows):
    s = f_ref.shape[0]
    r = lax.broadcasted_iota(I32, (rows, rows), 0)
    c = lax.broadcasted_iota(I32, (rows, rows), 1)
    tri = jnp.where(r >= c, 1.0, 0.0).astype(BF16)
    carry = jnp.zeros((1, LANES), F32)
    for r0 in range(0, s, rows):
        x = f_ref[r0:r0 + rows, :]
        lf = jnp.minimum(x, 0.0) - jnp.log(1.0 + jnp.exp(-jnp.abs(x)))
        cs = carry + sum(jnp.dot(tri, part.astype(BF16), preferred_element_type=F32)
                         for part in _split3(lf))
        carry = cs[rows - 1:rows, :]
        c2 = cs * LOG2E
        k_terms = jnp.concatenate(_split3(c2), axis=1).astype(BF16)
        q_terms = jnp.concatenate(_split3(c2 - ref_ref[...]), axis=1).astype(BF16)
        for j, terms in enumerate((k_terms, q_terms, q_terms)):
            routed = jnp.dot(terms, route_ref[j], preferred_element_type=F32) + const_ref[j]
            o_ref[r0:r0 + rows, j * LANES:(j + 1) * LANES] = routed.astype(o_ref.dtype)


def _fox_bias(flogit, ref, seq, n_pairs):
    t = flogit.shape[0]
    route, const = _bias_layout(n_pairs)
    return pl.pallas_call(
        functools.partial(_cum_kernel, rows=min(seq, 256)),
        grid=(t // seq,),
        in_specs=[pl.BlockSpec((seq, LANES), lambda b: (b, 0)),
                  pl.BlockSpec((1, LANES), lambda b: (0, 0)),
                  pl.BlockSpec((3, 3 * LANES, LANES), lambda b: (0, 0, 0)),
                  pl.BlockSpec((3, 1, LANES), lambda b: (0, 0, 0))],
        out_specs=pl.BlockSpec((seq, 3 * LANES), lambda b: (b, 0)),
        out_shape=jax.ShapeDtypeStruct((t, 3 * LANES), BF16),
        compiler_params=_cparams("parallel"),
        name="fox_bias",
    )(flogit, ref, route, const)


def _attn_kernel(q_ref, k_ref, v_ref, g_ref, bias_ref, qw_ref, kw_ref, o_ref, qs_s, ka_s, *,
                 dh, tq, bounded):
    s_len = q_ref.shape[0]
    nq = s_len // tq
    hp = pl.program_id(1)
    lane = lax.broadcasted_iota(I32, (1, LANES), 1)
    first = lane < dh

    def headnorm(z, w):
        z2 = z * z
        s0 = jnp.sum(jnp.where(first, z2, 0.0), axis=-1, keepdims=True)
        s1 = jnp.sum(jnp.where(first, 0.0, z2), axis=-1, keepdims=True)
        ms = jnp.where(first, s0, s1) * (1.0 / dh)
        return z * lax.rsqrt(ms + EPS) * w

    qn = headnorm(q_ref[...].astype(F32), qw_ref[...]) * (dh ** -0.5 * LOG2E)
    kn = headnorm(k_ref[...].astype(F32), kw_ref[...])
    mine = (lane // BIAS_LANES) == hp
    ka_s[:, :LANES] = kn.astype(BF16)
    ka_s[:, LANES:] = bias_ref[:, :LANES]
    q_lo = [jnp.where(first, qn, 0.0), jnp.where(first, 0.0, qn)]
    hq = tq // 2
    for qi in range(nq):
        for part_i in range(2):
            rows = slice(qi * tq + part_i * hq, qi * tq + (part_i + 1) * hq)
            for h in range(2):
                dst = slice((2 * part_i + h) * hq, (2 * part_i + h + 1) * hq)
                q_hi = bias_ref[rows, (1 + h) * LANES:(2 + h) * LANES].astype(F32)
                qs_s[qi, dst, :LANES] = q_lo[h][rows, :].astype(BF16)
                qs_s[qi, dst, LANES:] = jnp.where(mine, q_hi, 0.0).astype(BF16)

    nt = (((1,), (1,)), ((), ()))

    def causal(s, first_token):
        rr = lax.broadcasted_iota(I32, s.shape, 0)
        rr = jnp.where(rr >= hq, rr - hq, rr) + first_token
        cc = lax.broadcasted_iota(I32, s.shape, 1)
        return jnp.where(rr >= cc, s, MASK_VALUE)

    def finish(qi, halves):
        for part_i, (acc, l) in enumerate(halves):
            o2 = acc / l
            o = jnp.where(first, o2[:hq], o2[hq:])
            rows = slice(qi * tq + part_i * hq, qi * tq + (part_i + 1) * hq)
            g = g_ref[rows, :].astype(F32)
            o_ref[rows, :] = (o * _sigmoid(g)).astype(o_ref.dtype)

    def lane_blocks(e):
        cols = e[:, :LANES]
        for c0 in range(LANES, e.shape[1], LANES):
            cols = cols + e[:, c0:c0 + LANES]
        return cols

    if bounded:
        for qi in range(nq):
            acc, part = [None, None], [None, None]

            def add(i, pv, cols):
                acc[i] = pv if acc[i] is None else acc[i] + pv
                part[i] = cols if part[i] is None else part[i] + cols

            for t in range(qi):
                s = lax.dot_general(qs_s[qi], ka_s[t * tq:(t + 1) * tq, :], nt,
                                    preferred_element_type=F32)
                e = jnp.exp2(s)
                pv = jnp.dot(e.astype(BF16), v_ref[t * tq:(t + 1) * tq, :],
                             preferred_element_type=F32)
                cols = lane_blocks(e)
                add(0, pv[:tq], cols[:tq])
                add(1, pv[tq:], cols[tq:])
            k0 = qi * tq
            for i, n_keys in ((0, hq), (1, tq)):
                s = lax.dot_general(qs_s[qi, i * tq:(i + 1) * tq, :], ka_s[k0:k0 + n_keys, :], nt,
                                    preferred_element_type=F32)
                e = jnp.exp2(causal(s, i * hq))
                pv = jnp.dot(e.astype(BF16), v_ref[k0:k0 + n_keys, :], preferred_element_type=F32)
                add(i, pv, lane_blocks(e))
            finish(qi, [(acc[i], jnp.sum(part[i], axis=-1, keepdims=True)) for i in range(2)])
        return

    def causal_tile(s):
        return jnp.concatenate([causal(s[:tq], 0), causal(s[tq:], hq)], axis=0)

    def scores(q, off):
        return lax.dot_general(q, ka_s[pl.ds(off, tq), :], nt, preferred_element_type=F32)

    def update(s, off, carry, masked):
        m, l, acc = carry
        if masked:
            s = causal_tile(s)
        m_new = jnp.maximum(m, jnp.max(s, axis=-1, keepdims=True))
        p = jnp.exp2(s - m_new)
        alpha = jnp.exp2(m - m_new)
        l = alpha * l + jnp.sum(p, axis=-1, keepdims=True)
        acc = alpha * acc + jnp.dot(p.astype(BF16), v_ref[pl.ds(off, tq), :],
                                    preferred_element_type=F32)
        return m_new, l, acc

    for qi in range(nq):
        q = qs_s[qi]
        s = scores(q, 0)
        carry = (jnp.full((2 * tq, 1), MASK_VALUE, F32), jnp.zeros((2 * tq, 1), F32),
                 jnp.zeros((2 * tq, LANES), F32))
        if qi > 0:
            def body(j, c, q=q):
                off = pl.multiple_of(j * tq, tq)
                s_next = scores(q, off + tq)
                return (s_next,) + update(c[0], off, c[1:], False)

            s, *carry = lax.fori_loop(0, qi, body, (s,) + carry)
        m, l, acc = update(s, qi * tq, carry, True)
        finish(qi, [(acc[:tq], l[:tq]), (acc[tq:], l[tq:])])


def _fox_attention(qkvg, flogit, qw, kw, seq, d, dh):
    t = qkvg.shape[0]
    nb = d // LANES
    assert 2 * dh == LANES and nb * BIAS_LANES <= LANES
    tq = min(seq, 512)
    col = lambda off: pl.BlockSpec((seq, LANES), lambda b, p: (b, off + p))

    def call(bounded, bias):
        return pl.pallas_call(
            functools.partial(_attn_kernel, dh=dh, tq=tq, bounded=bounded),
            grid=(t // seq, nb),
            in_specs=[
                col(0), col(nb), col(2 * nb), col(3 * nb),
                pl.BlockSpec((seq, 3 * LANES), lambda b, p: (b, 0)),
                pl.BlockSpec((1, LANES), lambda b, p: (0, 0)),
                pl.BlockSpec((1, LANES), lambda b, p: (0, 0)),
            ],
            out_specs=pl.BlockSpec((seq, LANES), lambda b, p: (b, p)),
            out_shape=jax.ShapeDtypeStruct((t, d), BF16),
            scratch_shapes=[pltpu.VMEM((seq // tq, 2 * tq, 2 * LANES), BF16),
                            pltpu.VMEM((seq, 2 * LANES), BF16)],
            compiler_params=_cparams("parallel", "parallel"),
            name="fox_attention" if bounded else "fox_attention_running_max",
        )(qkvg, qkvg, qkvg, qkvg, bias, qw, kw)

    bound = (dh ** 0.5 * LOG2E * SCORE_MARGIN) * jnp.max(jnp.abs(qw)) * jnp.max(jnp.abs(kw)) + 0.1
    fits = bound <= EXP2_SPAN - 4.0
    ref = jnp.where(fits, bound - jnp.maximum(0.0, 2.0 * bound - EXP2_SPAN), 0.0)
    bias = _fox_bias(flogit, jnp.full((1, LANES), ref, F32), seq, nb)
    return lax.cond(fits, lambda: call(True, bias), lambda: call(False, bias))


HGRN_PREP_ROWS = 256
HGRN_LOG2_SPAN = 100.0


def _hgrn_kernel(q_ref, f_ref, v_ref, g_ref, lb_ref, gw_ref, o_ref, qe_s, ke_s, kd_s, dec_s, *,
                 hpb, chunk):
    s_len = q_ref.shape[0]
    n_chunks = s_len // chunk
    nt = (((1,), (1,)), ((), ()))
    gw = gw_ref[...]

    pr = min(HGRN_PREP_ROWS, s_len)
    rr = lax.broadcasted_iota(I32, (pr, pr), 0)
    cc = lax.broadcasted_iota(I32, (pr, pr), 1)
    same = (rr // chunk) == (cc // chunk)
    cum_m = jnp.where(same & (rr >= cc), 1.0, 0.0).astype(BF16)
    tot_m = jnp.where(same, 1.0, 0.0).astype(BF16)
    lb_all = lb_ref[...]
    lowest = jnp.zeros((1, lb_all.shape[1]), F32)
    for r0 in range(0, s_len, pr):
        sig = _sigmoid(f_ref[r0:r0 + pr, :])
        terms = [t.astype(BF16) for t in _split3(jnp.log2(lb_all + (1.0 - lb_all) * sig))]
        b = sum(jnp.dot(cum_m, t, preferred_element_type=F32) for t in terms)
        tot = sum(jnp.dot(tot_m, t, preferred_element_type=F32) for t in terms)
        k = (1.0 - lb_all) * (1.0 - sig)
        q_raw = q_ref[r0:r0 + pr, :]
        qe_s[r0:r0 + pr, :] = (q_raw * _sigmoid(q_raw) * jnp.exp2(b)).astype(BF16)
        ke_s[r0:r0 + pr, :] = (k * jnp.exp2(-b)).astype(BF16)
        kd_s[r0:r0 + pr, :] = (k * jnp.exp2(tot - b)).astype(BF16)
        for c0 in range(0, pr, chunk):
            ci = (r0 + c0) // chunk
            dec_s[ci] = jnp.exp2(tot[c0:c0 + SUBLANES, :])
        lowest = jnp.minimum(lowest, jnp.min(tot, axis=0, keepdims=True))
    mild = jnp.min(lowest) >= -HGRN_LOG2_SPAN

    @pl.when(mild)
    def _():
        r32 = lax.broadcasted_iota(I32, (chunk, chunk), 0)
        c32 = lax.broadcasted_iota(I32, (chunk, chunk), 1)
        causal = r32 >= c32
        per_step = 8 if n_chunks % 8 == 0 else 2

        def fast_body(step, states):
            insts = [(ch, hd) for ch in range(per_step) for hd in range(hpb)]
            rows = {ch: pl.ds(pl.multiple_of((step * per_step + ch) * chunk, chunk), chunk)
                    for ch in range(per_step)}
            cols = {hd: slice(hd * LANES, (hd + 1) * LANES) for hd in range(hpb)}
            states = list(states)
            qe = {i: qe_s[rows[i[0]], cols[i[1]]] for i in insts}
            v = {i: v_ref[rows[i[0]], cols[i[1]]] for i in insts}
            a = {i: lax.dot_general(qe[i], ke_s[rows[i[0]], cols[i[1]]], nt,
                                    preferred_element_type=F32) for i in insts}
            upd = {i: jnp.dot(jnp.transpose(v[i]).astype(BF16), kd_s[rows[i[0]], cols[i[1]]],
                              preferred_element_type=F32) for i in insts}
            o = {}
            for ch in range(per_step):
                for hd in range(hpb):
                    i = (ch, hd)
                    o[i] = (jnp.dot(jnp.where(causal, a[i], 0.0).astype(BF16), v[i].astype(BF16),
                                    preferred_element_type=F32)
                            + lax.dot_general(qe[i], states[hd].astype(BF16), nt,
                                              preferred_element_type=F32))
                    dec = dec_s[step * per_step + ch, 0:1, cols[hd]]
                    states[hd] = dec * states[hd] + upd[i]
            for i in insts:
                g = g_ref[rows[i[0]], cols[i[1]]]
                ms = jnp.mean(o[i] * o[i], axis=-1, keepdims=True)
                y = o[i] * lax.rsqrt(ms + EPS) * gw * (g * _sigmoid(g))
                o_ref[rows[i[0]], cols[i[1]]] = y.astype(o_ref.dtype)
            return tuple(states)

        init = tuple(jnp.zeros((LANES, LANES), F32) for _ in range(hpb))
        lax.fori_loop(0, n_chunks // per_step, fast_body, init)

    pair = 2
    groups = chunk // SUBLANES
    off_w = (groups - 1) * chunk
    r = lax.broadcasted_iota(I32, (chunk, chunk), 0)
    c = lax.broadcasted_iota(I32, (chunk, chunk), 1)
    tri = jnp.where(r >= c, 1.0, 0.0).astype(BF16)
    ro = lax.broadcasted_iota(I32, (chunk, off_w), 0) // SUBLANES
    co = lax.broadcasted_iota(I32, (chunk, off_w), 1) // chunk
    keep = ro == co + 1
    sub = lax.broadcasted_iota(I32, (SUBLANES, LANES), 0)
    gw = gw_ref[...]
    nt = (((1,), (1,)), ((), ()))
    grp = lambda a, i: a[i * SUBLANES:(i + 1) * SUBLANES, :]
    zeros8 = jnp.zeros((SUBLANES, LANES), F32)

    def body(step, states):
        insts = [(ch, hd) for ch in range(pair) for hd in range(hpb)]
        rows = {ch: pl.ds(pl.multiple_of((step * pair + ch) * chunk, chunk), chunk)
                for ch in range(pair)}
        cols = {hd: slice(hd * LANES, (hd + 1) * LANES) for hd in range(hpb)}
        st = {}
        for ch, hd in insts:
            lb = lb_ref[:, cols[hd]]
            q_raw = q_ref[rows[ch], cols[hd]]
            sig = _sigmoid(f_ref[rows[ch], cols[hd]])
            d = dict(v=v_ref[rows[ch], cols[hd]], q=q_raw * _sigmoid(q_raw),
                     k=(1.0 - lb) * (1.0 - sig))
            lf = jnp.log(lb + (1.0 - lb) * sig)
            d["b"] = sum(jnp.dot(tri, part.astype(BF16), preferred_element_type=F32)
                         for part in _split3(lf))
            st[ch, hd] = d
        def from_state(ch, hd, state_t):
            d = st[ch, hd]
            qe = (d["q"] * jnp.exp(d["b"])).astype(BF16)
            d["o"] = lax.dot_general(qe, state_t.astype(BF16), nt, preferred_element_type=F32)
        for hd in range(hpb):
            from_state(0, hd, states[hd])
        for ch, hd in insts:
            d = st[ch, hd]
            b, q, k = d["b"], d["q"], d["k"]
            ends = [b[(i + 1) * SUBLANES - 1:(i + 1) * SUBLANES, :] for i in range(groups)]
            kt = [grp(k, i) * jnp.exp(ends[i] - grp(b, i)) for i in range(groups)]
            qh = jnp.concatenate(
                [zeros8] + [grp(q, i) * jnp.exp(grp(b, i) - ends[i - 1]) for i in range(1, groups)],
                axis=0).astype(BF16)
            kh = jnp.concatenate(
                [kt[j] * jnp.exp(ends[i - 1] - ends[j]) if j < i else zeros8
                 for i in range(1, groups) for j in range(groups)], axis=0).astype(BF16)
            d["a"] = lax.dot_general(qh, kh, nt, preferred_element_type=F32)
            d["ends"], d["kt"] = ends, kt
        def state_update(ch, hd, state_t):
            d = st[ch, hd]
            b_end = d["ends"][-1]
            ke = jnp.concatenate([d["kt"][j] * jnp.exp(b_end - d["ends"][j]) for j in range(groups)],
                                 axis=0).astype(BF16)
            upd = jnp.dot(jnp.transpose(d["v"]).astype(BF16), ke, preferred_element_type=F32)
            return jnp.exp(b_end) * state_t + upd
        states = [state_update(0, hd, states[hd]) for hd in range(hpb)]
        for ch, hd in insts:
            d = st[ch, hd]
            vb = d["v"].astype(BF16)
            v3 = jnp.concatenate([vb] * (groups - 1), axis=0)
            d["off"] = jnp.dot(jnp.where(keep, d["a"], 0.0).astype(BF16), v3,
                               preferred_element_type=F32)
        for hd in range(hpb):
            from_state(1, hd, states[hd])
        states = [state_update(1, hd, states[hd]) for hd in range(hpb)]
        for ch, hd in insts:
            d = st[ch, hd]
            o = d["o"] + d["off"]
            o_g = []
            for gi in range(groups):
                qg, kg, vg, bg = grp(d["q"], gi), grp(d["k"], gi), grp(d["v"], gi), grp(d["b"], gi)
                og = grp(o, gi)
                for s in range(SUBLANES):
                    live = sub >= s
                    e = jnp.exp(jnp.where(live, bg - bg[s:s + 1, :], 0.0))
                    p = jnp.where(live, qg * e * kg[s:s + 1, :], 0.0)
                    og = og + jnp.sum(p, axis=-1, keepdims=True) * vg[s:s + 1, :]
                o_g.append(og)
            o = jnp.concatenate(o_g, axis=0)
            g = g_ref[rows[ch], cols[hd]]
            ms = jnp.mean(o * o, axis=-1, keepdims=True)
            y = o * lax.rsqrt(ms + EPS) * gw * (g * _sigmoid(g))
            o_ref[rows[ch], cols[hd]] = y.astype(o_ref.dtype)
        return tuple(states)

    @pl.when(jnp.logical_not(mild))
    def _():
        init = tuple(jnp.zeros((LANES, LANES), F32) for _ in range(hpb))
        lax.fori_loop(0, s_len // (chunk * pair), body, init)


def _hgrn_mixer(proj, lower, gw, seq, d, hpb):
    t = proj.shape[0]
    w = hpb * LANES
    nb = d // w
    col = lambda off: pl.BlockSpec((seq, w), lambda b, p: (b, off + p))
    return pl.pallas_call(
        functools.partial(_hgrn_kernel, hpb=hpb, chunk=HGRN_CHUNK),
        grid=(t // seq, nb),
        in_specs=[
            col(0), col(nb), col(2 * nb), col(3 * nb),
            pl.BlockSpec((1, w), lambda b, p: (0, p)),
            pl.BlockSpec((1, LANES), lambda b, p: (0, 0)),
        ],
        out_specs=pl.BlockSpec((seq, w), lambda b, p: (b, p)),
        out_shape=jax.ShapeDtypeStruct((t, d), BF16),
        scratch_shapes=[pltpu.VMEM((seq, w), BF16)] * 3
        + [pltpu.VMEM((seq // HGRN_CHUNK, SUBLANES, w), F32)],
        compiler_params=_cparams("parallel", "parallel"),
        name="hgrn_mixer",
    )(proj, proj, proj, proj, lower, gw)


def _post_kernel(o_ref, x_ref, wo_ref, g1_ref, nw_ref, sh_ref, sc_ref, wr_ref, br_ref,
                 x1_ref, hp_ref, ri_ref, rw_ref, cnt_ref, run_s):
    i = pl.program_id(0)
    tm = x_ref.shape[0]

    @pl.when(i == 0)
    def _():
        run_s[...] = jnp.zeros_like(run_s)

    y = jnp.dot(o_ref[...], wo_ref[...], preferred_element_type=F32)
    x1 = x_ref[...] + g1_ref[...] * y
    x1_ref[...] = x1
    h2 = _norm_mod(x1, nw_ref[...], sc_ref[...], sh_ref[...])
    hp_ref[...] = _pack_halves(h2)
    logits = jnp.dot(h2.astype(BF16), wr_ref[...], preferred_element_type=F32) + br_ref[...]

    lane = lax.broadcasted_iota(I32, (tm, LANES), 1).astype(F32)
    work = logits
    idx, val = [], []
    for _ in range(TOP_K):
        mx = jnp.max(work, axis=-1, keepdims=True)
        ix = jnp.min(jnp.where(work == mx, lane, float(LANES)), axis=-1, keepdims=True)
        idx.append(ix)
        val.append(mx)
        work = jnp.where(lane == ix, -jnp.inf, work)
    ex = [jnp.exp(v - val[0]) for v in val]
    den = ex[0] + ex[1] + ex[2] + ex[3]
    wts = [e / den for e in ex]

    hot = jnp.zeros((tm, LANES), F32)
    for ix in idx:
        hot = hot + jnp.where(lane == ix, 1.0, 0.0)
    r = lax.broadcasted_iota(I32, (tm, tm), 0)
    c = lax.broadcasted_iota(I32, (tm, tm), 1)
    strict = jnp.where(r > c, 1.0, 0.0).astype(BF16)
    before = jnp.dot(strict, hot.astype(BF16), preferred_element_type=F32) + run_s[...]
    ranks = [jnp.sum(jnp.where(lane == ix, before, 0.0), axis=-1, keepdims=True) for ix in idx]
    run_s[...] = run_s[...] + jnp.sum(hot, axis=0, keepdims=True)

    ri = jnp.zeros((tm, LANES), F32)
    rw = jnp.zeros((tm, LANES), F32)
    for k in range(TOP_K):
        ri = jnp.where(lane == float(k), idx[k], ri)
        ri = jnp.where(lane == float(TOP_K + k), ranks[k], ri)
        rw = jnp.where(lane == float(k), wts[k], rw)
    ri_ref[...] = jnp.transpose(ri)[:2 * TOP_K, :].astype(I32)
    rw_ref[...] = rw
    cnt_ref[...] = run_s[...].astype(I32)


def _post_mixer(o, x, wo, mod, nw2, wr, br, seq, tm, row_off=0):
    t, d = o.shape
    per_seq = seq // tm
    tile_off = row_off // tm
    mod_spec = lambda j: pl.BlockSpec((None, 1, d), lambda i: (i // per_seq, 0, j))
    row = lambda w: pl.BlockSpec((tm, w), lambda i: (i, 0))
    return pl.pallas_call(
        _post_kernel,
        grid=(t // tm,),
        in_specs=[
            row(d), pl.BlockSpec((tm, d), lambda i: (i + tile_off, 0)),
            _resident((d, d), lambda i: (0, 0)),
            mod_spec(2),
            _resident((1, d), lambda i: (0, 0)),
            mod_spec(3), mod_spec(4),
            _resident((d, LANES), lambda i: (0, 0)),
            _resident((1, LANES), lambda i: (0, 0)),
        ],
        out_specs=[row(d), row(d // 2), pl.BlockSpec((2 * TOP_K, tm), lambda i: (0, i)), row(LANES),
                   pl.BlockSpec((1, LANES), lambda i: (0, 0))],
        out_shape=[
            jax.ShapeDtypeStruct((t, d), F32),
            jax.ShapeDtypeStruct((t, d // 2), U32),
            jax.ShapeDtypeStruct((2 * TOP_K, t), I32),
            jax.ShapeDtypeStruct((t, LANES), F32),
            jax.ShapeDtypeStruct((1, LANES), I32),
        ],
        scratch_shapes=[pltpu.VMEM((1, LANES), F32)],
        compiler_params=_cparams("arbitrary"),
        name="post_mixer",
    )(o, x, wo, mod, nw2, mod, mod, wr, br)


def _sc_mesh():
    return plsc.VectorSubcoreMesh(core_axis_name="c", subcore_axis_name="s")


def _sc_scatter_rows(x, posf, n_rows):
    t, w = x.shape
    workers = SC_CORES * SC_SUBCORES
    per_w = t // workers

    @functools.partial(
        pl.kernel, mesh=_sc_mesh(),
        out_type=jax.ShapeDtypeStruct((n_rows, w), x.dtype),
        scratch_types=[pltpu.VMEM((SC_ROWS,), I32)] * TOP_K
        + [pltpu.VMEM((SC_ROWS, w), x.dtype), pltpu.SemaphoreType.DMA])
    def scatter_kernel(x_hbm, pos_hbm, out_hbm, i0, i1, i2, i3, rows_v, sem):
        idxs = (i0, i1, i2, i3)
        wid = lax.axis_index("s") * SC_CORES + lax.axis_index("c")
        base = wid * per_w

        @pl.loop(0, per_w // SC_ROWS)
        def _(i):
            off = base + i * SC_ROWS
            for k in range(TOP_K):
                pltpu.sync_copy(pos_hbm.at[pl.ds(k * t + off, SC_ROWS)], idxs[k])
            pltpu.sync_copy(x_hbm.at[pl.ds(off, SC_ROWS)], rows_v)
            copies = [pltpu.async_copy(rows_v, out_hbm.at[idxs[k]], sem) for k in range(TOP_K)]
            for cp in copies:
                cp.wait()

    return scatter_kernel(x, posf)


def _sc_gather_rows(src, posf):
    n = posf.shape[0]
    w = src.shape[1]
    workers = SC_CORES * SC_SUBCORES
    per_w = n // workers

    @functools.partial(
        pl.kernel, mesh=_sc_mesh(),
        out_type=jax.ShapeDtypeStruct((n, w), src.dtype),
        scratch_types=[pltpu.VMEM((SC_ROWS,), I32), pltpu.VMEM((SC_ROWS, w), src.dtype),
                       pltpu.SemaphoreType.DMA])
    def gather_kernel(src_hbm, pos_hbm, out_hbm, idx_v, rows_v, sem):
        wid = lax.axis_index("s") * SC_CORES + lax.axis_index("c")
        base = wid * per_w

        @pl.loop(0, per_w // SC_ROWS)
        def _(i):
            off = base + i * SC_ROWS
            pltpu.sync_copy(pos_hbm.at[pl.ds(off, SC_ROWS)], idx_v)
            pltpu.async_copy(src_hbm.at[idx_v], rows_v, sem).wait()
            pltpu.sync_copy(rows_v, out_hbm.at[pl.ds(off, SC_ROWS)])

    return gather_kernel(src, posf)


GU_BLOCK = 2 * LANES
FF_CHUNK = 2 * LANES


def _deinterleave_matrix():
    sel = np.zeros((GU_BLOCK, GU_BLOCK), np.float32)
    j = np.arange(LANES)
    sel[2 * j, j] = 1.0
    sel[2 * j + 1, LANES + j] = 1.0
    return jnp.asarray(sel, BF16)


def _expert_kernel(be_ref, nu_ref, nxt_ref, x_ref, wgu_hbm, bg_ref, bu_ref, wd_hbm, bd_ref, sel_ref,
                   y_ref, wgu_f, wd_f, wgu_s, wd_s, sems, *, layer):
    i = pl.program_id(0)
    d, f2 = wgu_s.shape
    f = f2 // 2
    e_cur = be_ref[i]
    new_expert = jnp.logical_or(i == 0, e_cur != be_ref[jnp.maximum(i - 1, 0)])

    def weight_copies(e):
        return (pltpu.make_async_copy(wgu_hbm.at[layer, e], wgu_f, sems.at[0]),
                pltpu.make_async_copy(wd_hbm.at[layer, e], wd_f, sems.at[1]))

    @pl.when(jnp.logical_and(new_expert, i < nu_ref[0]))
    def _():
        @pl.when(i == 0)
        def _():
            for cp in weight_copies(e_cur):
                cp.start()

        for cp in weight_copies(e_cur):
            cp.wait()
        for c0 in range(0, f2, GU_BLOCK):
            blk = wgu_f[:, c0:c0 + GU_BLOCK].astype(BF16)
            wgu_s[:, c0:c0 + GU_BLOCK] = jnp.dot(
                blk, sel_ref[...], preferred_element_type=F32).astype(BF16)
        wd_s[...] = wd_f[...].astype(BF16)

        @pl.when(nxt_ref[e_cur] >= 0)
        def _():
            for cp in weight_copies(nxt_ref[e_cur]):
                cp.start()

    @pl.when(i < nu_ref[0])
    def _():
        half = x_ref.shape[1]
        lo, hi = _unpack_halves(x_ref[...])
        lo = lo.astype(BF16)
        hi = hi.astype(BF16)

        def hidden(c0):
            cols = slice(2 * c0, 2 * (c0 + FF_CHUNK))
            gu = (jnp.dot(lo, wgu_s[:half, cols], preferred_element_type=F32)
                  + jnp.dot(hi, wgu_s[half:, cols], preferred_element_type=F32))
            parts = []
            for b0 in range(0, FF_CHUNK, LANES):
                gate = gu[:, 2 * b0:2 * b0 + LANES] + bg_ref[0, :, c0 + b0:c0 + b0 + LANES]
                up = gu[:, 2 * b0 + LANES:2 * b0 + 2 * LANES] + bu_ref[0, :, c0 + b0:c0 + b0 + LANES]
                gate = jnp.minimum(gate, SWIGLU_LIMIT)
                up = jnp.clip(up, -SWIGLU_LIMIT, SWIGLU_LIMIT)
                parts.append((up + 1.0) * gate * _sigmoid(SWIGLU_ALPHA * gate))
            return jnp.concatenate(parts, axis=1).astype(BF16)

        chunks = list(range(0, f, FF_CHUNK))
        y = None
        h_next = hidden(chunks[0])
        for n, c0 in enumerate(chunks):
            h = h_next
            if n + 1 < len(chunks):
                h_next = hidden(chunks[n + 1])
            part = jnp.dot(h, wd_s[c0:c0 + FF_CHUNK, :], preferred_element_type=F32)
            y = part if y is None else y + part
        y_ref[...] = _pack_halves(y + bd_ref[0])

    @pl.when(i >= nu_ref[0])
    def _():
        y_ref[...] = jnp.zeros_like(y_ref)


def _expert_ffn(xs, blk_e, n_used, nxt_e, w_gu, bg, bu, w_down, bd, rb, layer):
    rows, half = xs.shape
    _, e, d, f2 = w_gu.shape
    f = f2 // 2
    n_blocks = rows // rb
    wspec = lambda shape: pl.BlockSpec(shape, lambda i, be, nu, nx: (be[i], 0, 0))
    return pl.pallas_call(
        functools.partial(_expert_kernel, layer=layer),
        grid_spec=pltpu.PrefetchScalarGridSpec(
            num_scalar_prefetch=3,
            grid=(n_blocks,),
            in_specs=[
                pl.BlockSpec((rb, half), lambda i, be, nu, nx: (i, 0)),
                pl.BlockSpec(memory_space=pl.ANY),
                wspec((1, 1, f)), wspec((1, 1, f)),
                pl.BlockSpec(memory_space=pl.ANY),
                wspec((1, 1, d)),
                _resident((GU_BLOCK, GU_BLOCK), lambda i, be, nu, nx: (0, 0)),
            ],
            out_specs=pl.BlockSpec((rb, d // 2), lambda i, be, nu, nx: (i, 0)),
            scratch_shapes=[pltpu.VMEM((d, f2), F32), pltpu.VMEM((f, d), F32),
                            pltpu.VMEM((d, f2), BF16), pltpu.VMEM((f, d), BF16),
                            pltpu.SemaphoreType.DMA((2,))],
        ),
        out_shape=jax.ShapeDtypeStruct((rows, d // 2), U32),
        compiler_params=_cparams("arbitrary"),
        name="expert_ffn",
    )(blk_e, n_used, nxt_e, xs, w_gu, bg, bu, w_down, bd, _deinterleave_matrix())


def _combine_kernel(x_ref, y0_ref, y1_ref, y2_ref, y3_ref, rw_ref, g2_ref, *dst_and_out):
    o_ref = dst_and_out[-1]
    half = y0_ref.shape[1]
    lo, hi = _moe_mix(x_ref, (y0_ref, y1_ref, y2_ref, y3_ref), rw_ref, g2_ref)
    o_ref[:, :half] = lo
    o_ref[:, half:] = hi


def _combine(x1, yt, rw, mod, seq, tm, out_rows=None, row_off=0, dst=None):
    t, d = x1.shape
    per_seq = seq // tm
    nt = t // tm
    tile_off = row_off // tm
    yspec = lambda k: pl.BlockSpec((tm, d // 2), lambda i: (k * nt + i, 0))
    in_specs = [
        pl.BlockSpec((tm, d), lambda i: (i, 0)),
        yspec(0), yspec(1), yspec(2), yspec(3),
        pl.BlockSpec((tm, LANES), lambda i: (i, 0)),
        pl.BlockSpec((None, 1, d), lambda i: (i // per_seq, 0, 5)),
    ]
    args = [x1, yt, yt, yt, yt, rw, mod]
    aliases = {}
    if dst is not None:
        in_specs.append(pl.BlockSpec(memory_space=pl.ANY))
        args.append(dst)
        aliases = {len(args) - 1: 0}
    return pl.pallas_call(
        _combine_kernel,
        grid=(nt,),
        in_specs=in_specs,
        out_specs=pl.BlockSpec((tm, d), lambda i: (i + tile_off, 0)),
        out_shape=jax.ShapeDtypeStruct((t if out_rows is None else out_rows, d), F32),
        input_output_aliases=aliases,
        compiler_params=_cparams("parallel"),
        name="moe_combine",
    )(*args)


POST_ROWS = 1024


def _row_tile(seq):
    return min(seq, 512)


def _moe_block_rows(t):
    return min(512, t * TOP_K // 8)


def kernel(x, c, fox_w_in, fox_b_f, fox_q_norm, fox_k_norm, fox_w_out, hgrn_w_in, hgrn_lb, hgrn_g_norm, hgrn_w_out, ada_w, ada_b, norm1_w, norm2_w, router_w, router_b, exp_w_gu, exp_b_gu, exp_w_down, exp_b_down):
    bsz, seq, d = x.shape
    t = bsz * seq
    depth = ada_w.shape[0]
    n_exp = router_w.shape[-1]
    f_dim = exp_w_down.shape[2]
    h_fox = fox_w_in.shape[-1] - 4 * d
    dh = d // h_fox
    tm = _row_tile(seq)
    n_split = 2 if bsz % 2 == 0 else 1
    bh, th = bsz // n_split, t // n_split
    rb = _moe_block_rows(th)
    n_blocks = (th * TOP_K) // rb + n_exp
    hgrn_hpb = 4 if d % (4 * LANES) == 0 else 2

    mod_all = _adaln_mod(c, ada_w, ada_b).reshape(depth, bsz, 1, -1)
    lb_soft = jax.nn.softmax(hgrn_lb.astype(F32), axis=0)
    lower = jnp.cumsum(lb_soft, axis=0) - lb_soft[0]
    experts = jnp.arange(n_exp, dtype=I32)

    xf = x.reshape(t, d)
    parts = [xf] * n_split
    for i in range(depth):
        j = i // 2
        nw1 = norm1_w[i].reshape(1, d)
        if i % 2 == 0:
            w_in = fox_w_in[j]
            w_main = w_in[:, :4 * d].astype(BF16)
            w_f = jnp.pad(w_in[:, 4 * d:], ((0, 0), (0, LANES - h_fox))).astype(BF16)
            b_f = jnp.pad(fox_b_f[j], (0, LANES - h_fox)).reshape(1, LANES)
            qw = jnp.tile(fox_q_norm[j], LANES // dh).reshape(1, LANES)
            kw = jnp.tile(fox_k_norm[j], LANES // dh).reshape(1, LANES)
            w_out = fox_w_out[j].astype(BF16)
        else:
            w_main = hgrn_w_in[j].astype(BF16)
            w_out = hgrn_w_out[j].astype(BF16)
        wr = jnp.pad(router_w[i], ((0, 0), (0, LANES - n_exp))).astype(BF16)
        br = jnp.pad(router_b[i], (0, LANES - n_exp), constant_values=MASK_VALUE).reshape(1, LANES)
        b_gu = exp_b_gu[i].reshape(n_exp, 1, f_dim, 2)

        def front(s, src):
            mod = mod_all[i, s * bh:(s + 1) * bh]
            off = s * th if i == 0 else 0
            moe = None
            xp = src
            if isinstance(src, tuple):
                xp, moe = src[0], src[1:]
            if i % 2 == 0:
                outs = _in_proj(xp, nw1, mod, w_main, BF16, seq, tm, w_f, b_f,
                                rows=th, row_off=off, moe=moe)
                o = _fox_attention(outs[0], outs[1], qw, kw, seq, d, dh)
            else:
                outs = _in_proj(xp, nw1, mod, w_main, F32, seq, tm, rows=th, row_off=off, moe=moe)
                o = _hgrn_mixer(outs[0], lower[j].reshape(1, d), hgrn_g_norm[j].reshape(1, HGRN_DIM),
                                seq, d, hgrn_hpb)
            if moe is not None:
                xp = outs[-1]
            x1, hp, ri, rw, cnt = _post_mixer(o, xp, w_out, mod, norm2_w[i].reshape(1, d), wr, br,
                                              seq, min(seq, POST_ROWS), row_off=off)
            counts = cnt[0, :n_exp]
            padded = (counts + rb - 1) // rb * rb
            pad_end = jnp.cumsum(padded)
            pad_start = pad_end - padded
            top_idx, rank = ri[:TOP_K], ri[TOP_K:]
            start = jnp.sum(jnp.where(top_idx[None] == experts[:, None, None],
                                      pad_start[:, None, None], 0), axis=0)
            posf = (start + rank).reshape(-1).astype(I32)
            blk_lo = jnp.arange(n_blocks, dtype=I32) * rb
            blk_e = jnp.minimum(jnp.sum(pad_end[None, :] <= blk_lo[:, None], axis=1),
                                n_exp - 1).astype(I32)
            n_used = (pad_end[-1] // rb).astype(I32).reshape(1)
            later = jnp.where((experts[None, :] > experts[:, None]) & (padded[None, :] > 0),
                              experts[None, :], n_exp)
            nxt_e = jnp.min(later, axis=1)
            nxt_e = jnp.where(nxt_e >= n_exp, -1, nxt_e).astype(I32)
            return dict(x1=x1, hp=hp, rw=rw, posf=posf, blk_e=blk_e, n_used=n_used, nxt_e=nxt_e,
                        mod=mod)

        def experts_of(st, xs):
            return _expert_ffn(xs, st["blk_e"], st["n_used"], st["nxt_e"], exp_w_gu,
                               b_gu[..., 0], b_gu[..., 1], exp_w_down,
                               exp_b_down[i].reshape(n_exp, 1, d), rb, i)

        def deferred(st, yt):
            return (st["x1"], yt, st["rw"], st["mod"])

        if n_split == 1:
            st = front(0, parts[0])
            xs = _sc_scatter_rows(st["hp"], st["posf"], n_blocks * rb)
            parts[0] = deferred(st, _sc_gather_rows(experts_of(st, xs), st["posf"]))
            continue

        tie = lax.optimization_barrier
        sa = front(0, parts[0])
        xs_a = _sc_scatter_rows(sa["hp"], sa["posf"], n_blocks * rb)
        src_b = parts[1]
        if isinstance(src_b, tuple):
            yt_b, _ = tie((src_b[1], sa["hp"]))
            src_b = (src_b[0], yt_b) + src_b[2:]
        else:
            src_b, _ = tie((src_b, sa["hp"]))
        sb = front(1, src_b)
        xs_a, _ = tie((xs_a, sb["hp"]))
        xs_b = _sc_scatter_rows(sb["hp"], sb["posf"], n_blocks * rb)
        ys_a = experts_of(sa, xs_a)
        xs_b, _ = tie((xs_b, ys_a))
        yt_a = _sc_gather_rows(ys_a, sa["posf"])
        ys_b = experts_of(sb, xs_b)
        yt_a, _ = tie((yt_a, ys_b))
        yt_b = _sc_gather_rows(ys_b, sb["posf"])
        parts = [deferred(sa, yt_a), deferred(sb, yt_b)]

    out = None
    for s_i, (x1, yt, rw, mod) in enumerate(parts):
        if n_split == 1:
            out = _combine(x1, yt, rw, mod, seq, tm)
        else:
            out = _combine(x1, yt, rw, mod, seq, tm, out_rows=t, row_off=s_i * th, dst=out)
    return out.reshape(bsz, seq, d)
```

```python
import functools

import jax
import jax.numpy as jnp
import numpy as np
from jax import lax
from jax.experimental import pallas as pl
from jax.experimental.pallas import tpu as pltpu
from jax.experimental.pallas import tpu_sc as plsc

F32 = jnp.float32
BF16 = jnp.bfloat16
U32 = jnp.uint32
I32 = jnp.int32

EPS = 1e-6
MASK_VALUE = -1e30
TOP_K = 4
HGRN_DIM = 128
HGRN_CHUNK = 32
SWIGLU_LIMIT = 7.0
SWIGLU_ALPHA = 1.702
LANES = 128
SUBLANES = 8
SC_CORES = 2
SC_SUBCORES = 16
SC_ROWS = 64
VMEM_LIMIT = 56 * 1024 * 1024
HIGHEST = lax.Precision.HIGHEST


def _cparams(*sem):
    return pltpu.CompilerParams(dimension_semantics=sem, vmem_limit_bytes=VMEM_LIMIT)


def _resident(shape, index_map):
    return pl.BlockSpec(shape, index_map, pipeline_mode=pl.Buffered(1))


def _sigmoid(x):
    return 1.0 / (1.0 + jnp.exp(-x))


def _bf16_bits(x):
    return lax.bitcast_convert_type(x.astype(BF16).astype(F32), U32)


def _pack_halves(x):
    n = x.shape[1] // 2
    lo = _bf16_bits(x[:, :n])
    hi = _bf16_bits(x[:, n:])
    return (hi & jnp.uint32(0xFFFF0000)) | (lo >> 16)


def _unpack_halves(w):
    lo = lax.bitcast_convert_type(w << 16, F32)
    hi = lax.bitcast_convert_type(w & jnp.uint32(0xFFFF0000), F32)
    return lo, hi


def _mod_kernel(c_ref, w_ref, b_ref, o_ref):
    c = c_ref[...]
    ca = c * _sigmoid(c)
    o_ref[0] = jnp.dot(ca, w_ref[0], preferred_element_type=F32, precision=HIGHEST) + b_ref[0]


def _adaln_mod(c, ada_w, ada_b):
    depth, d, n6 = ada_w.shape
    b = c.shape[0]
    nmod = n6 // d
    return pl.pallas_call(
        _mod_kernel,
        grid=(depth, nmod),
        in_specs=[
            pl.BlockSpec((b, d), lambda i, j: (0, 0)),
            pl.BlockSpec((1, d, d), lambda i, j: (i, 0, j)),
            pl.BlockSpec((1, 1, d), lambda i, j: (i, 0, j)),
        ],
        out_specs=pl.BlockSpec((1, b, d), lambda i, j: (i, 0, j)),
        out_shape=jax.ShapeDtypeStruct((depth, b, n6), F32),
        compiler_params=_cparams("parallel", "parallel"),
        name="adaln_mod",
    )(c, ada_w, ada_b.reshape(depth, 1, n6))


def _norm_mod(x, nw, sc, sh):
    ms = jnp.mean(x * x, axis=-1, keepdims=True)
    return x * lax.rsqrt(ms + EPS) * nw * (1.0 + sc) + sh


def _moe_mix(x_ref, y_refs, rw_ref, g2_ref, rows=slice(None)):
    rw = rw_ref[rows, :]
    lane = lax.broadcasted_iota(I32, rw.shape, 1)
    half = y_refs[0].shape[1]
    acc_lo = acc_hi = None
    for k, y_ref in enumerate(y_refs):
        wk = jnp.sum(jnp.where(lane == k, rw, 0.0), axis=-1, keepdims=True)
        lo, hi = _unpack_halves(y_ref[rows, :])
        acc_lo = wk * lo if acc_lo is None else acc_lo + wk * lo
        acc_hi = wk * hi if acc_hi is None else acc_hi + wk * hi
    g2 = g2_ref[...]
    return (x_ref[rows, :half] + g2[:, :half] * acc_lo, x_ref[rows, half:] + g2[:, half:] * acc_hi)


def _in_kernel(*refs, n_chunk, with_f, with_moe, sub):
    n_in = 1 + (TOP_K + 2 if with_moe else 0) + 4 + (2 if with_f else 0)
    ins, outs = refs[:n_in], refs[n_in:]
    x_ref = ins[0]
    pos = 1 + (TOP_K + 2 if with_moe else 0)
    nw_ref, sh_ref, sc_ref, w_ref = ins[pos:pos + 4]
    o_ref = outs[0]
    if with_f:
        wf_ref, bf_ref = ins[pos + 4:pos + 6]
        of_ref = outs[1]
    n = o_ref.shape[1]
    for r0 in range(0, x_ref.shape[0], sub):
        rows = slice(r0, r0 + sub)
        if with_moe:
            lo, hi = _moe_mix(x_ref, ins[1:1 + TOP_K], ins[1 + TOP_K], ins[2 + TOP_K], rows)
            x = jnp.concatenate([lo, hi], axis=1)
            outs[-1][rows, :] = x
        else:
            x = x_ref[rows, :]
        h = _norm_mod(x, nw_ref[...], sc_ref[...], sh_ref[...]).astype(BF16)
        for c0 in range(0, n, n_chunk):
            o_ref[rows, c0:c0 + n_chunk] = jnp.dot(
                h, w_ref[:, c0:c0 + n_chunk], preferred_element_type=F32).astype(o_ref.dtype)
        if with_f:
            of_ref[rows, :] = jnp.dot(h, wf_ref[...], preferred_element_type=F32) + bf_ref[...]


def _in_proj(x, nw, mod, w, out_dtype, seq, tm, wf=None, bf=None, rows=None, row_off=0, moe=None):
    d = x.shape[1]
    t = x.shape[0] if rows is None else rows
    n = w.shape[1]
    per_seq = seq // tm
    tile_off = row_off // tm
    nt = t // tm
    with_f = wf is not None
    mod_spec = lambda j: pl.BlockSpec((None, 1, d), lambda i: (i // per_seq, 0, j))
    in_specs = [pl.BlockSpec((tm, d), lambda i: (i + tile_off, 0))]
    args = [x]
    if moe is not None:
        yt, rw, mod_prev = moe
        in_specs += [pl.BlockSpec((tm, d // 2), lambda i, k=k: (k * nt + i, 0)) for k in range(TOP_K)]
        in_specs += [pl.BlockSpec((tm, LANES), lambda i: (i, 0)), mod_spec(5)]
        args += [yt] * TOP_K + [rw, mod_prev]
    in_specs += [_resident((1, d), lambda i: (0, 0)), mod_spec(0), mod_spec(1),
                 _resident((d, n), lambda i: (0, 0))]
    args += [nw, mod, mod, w]
    out_specs = [pl.BlockSpec((tm, n), lambda i: (i, 0))]
    out_shape = [jax.ShapeDtypeStruct((t, n), out_dtype)]
    if with_f:
        in_specs += [_resident((d, LANES), lambda i: (0, 0)), _resident((1, LANES), lambda i: (0, 0))]
        out_specs.append(pl.BlockSpec((tm, LANES), lambda i: (i, 0)))
        out_shape.append(jax.ShapeDtypeStruct((t, LANES), F32))
        args += [wf, bf]
    if moe is not None:
        out_specs.append(pl.BlockSpec((tm, d), lambda i: (i, 0)))
        out_shape.append(jax.ShapeDtypeStruct((t, d), F32))
    return pl.pallas_call(
        functools.partial(_in_kernel, n_chunk=min(n, 1024), with_f=with_f, with_moe=moe is not None,
                          sub=min(tm, 256)),
        grid=(t // tm,),
        in_specs=in_specs,
        out_specs=out_specs,
        out_shape=out_shape,
        compiler_params=_cparams("parallel"),
        name="in_proj_fox" if with_f else "in_proj_hgrn",
    )(*args)


LOG2E = 1.4426950408889634
EXP2_SPAN = 120.0
SCORE_MARGIN = 1.01
BIAS_LANES = 16


def _split3(c):
    c1 = c.astype(BF16).astype(F32)
    r = c - c1
    c2 = r.astype(BF16).astype(F32)
    return c1, c2, r - c2


def _bias_layout(n_pairs):
    route =np.zeros((3, 3 * LANES, LANES), np.float32)
    const = np.zeros((3, 1, LANES), np.float32)
    for p in range(n_pairs):
        base = p * BIAS_LANES
        for i in range(3):
            const[0, 0, base + i] = 1.0
            for h in range(2):
                route[0, i * LANES + 2 * p + h, base + 3 + 3 * h + i] = -1.0
                route[1 + h, i * LANES + 2 * p + h, base + i] = 1.0
                const[1 + h, 0, base + 3 + 3 * h + i] = 1.0
    return jnp.asarray(route, BF16), jnp.asarray(const, F32)


def _cum_kernel(f_ref, ref_ref, route_ref, const_ref, o_ref, *, rows):
    s = f_ref.shape[0]
    r = lax.broadcasted_iota(I32, (rows, rows), 0)
    c = lax.broadcasted_iota(I32, (rows, rows), 1)
    tri = jnp.where(r >= c, 1.0, 0.0).astype(BF16)
    carry = jnp.zeros((1, LANES), F32)
    for r0 in range(0, s, rows):
        x = f_ref[r0:r0 + rows, :]
        lf = jnp.minimum(x, 0.0) - jnp.log(1.0 + jnp.exp(-jnp.abs(x)))
        cs = carry + sum(jnp.dot(tri, part.astype(BF16), preferred_element_type=F32)
                         for part in _split3(lf))
        carry = cs[rows - 1:rows, :]
        c2 = cs * LOG2E
        k_terms = jnp.concatenate(_split3(c2), axis=1).astype(BF16)
        q_terms = jnp.concatenate(_split3(c2 - ref_ref[...]), axis=1).astype(BF16)
        for j, terms in enumerate((k_terms, q_terms, q_terms)):
            routed = jnp.dot(terms, route_ref[j], preferred_element_type=F32) + const_ref[j]
            o_ref[r0:r0 + rows, j * LANES:(j + 1) * LANES] = routed.astype(o_ref.dtype)


def _fox_bias(flogit, ref, seq, n_pairs):
    t = flogit.shape[0]
    route, const = _bias_layout(n_pairs)
    return pl.pallas_call(
        functools.partial(_cum_kernel, rows=min(seq, 256)),
        grid=(t // seq,),
        in_specs=[pl.BlockSpec((seq, LANES), lambda b: (b, 0)),
                  pl.BlockSpec((1, LANES), lambda b: (0, 0)),
                  pl.BlockSpec((3, 3 * LANES, LANES), lambda b: (0, 0, 0)),
                  pl.BlockSpec((3, 1, LANES), lambda b: (0, 0, 0))],
        out_specs=pl.BlockSpec((seq, 3 * LANES), lambda b: (b, 0)),
        out_shape=jax.ShapeDtypeStruct((t, 3 * LANES), BF16),
        compiler_params=_cparams("parallel"),
        name="fox_bias",
    )(flogit, ref, route, const)


def _attn_kernel(q_ref, k_ref, v_ref, g_ref, bias_ref, qw_ref, kw_ref, o_ref, qs_s, ka_s, *,
                 dh, tq, bounded):
    s_len = q_ref.shape[0]
    nq = s_len // tq
    hp = pl.program_id(1)
    lane = lax.broadcasted_iota(I32, (1, LANES), 1)
    first = lane < dh

    def headnorm(z, w):
        z2 = z * z
        s0 = jnp.sum(jnp.where(first, z2, 0.0), axis=-1, keepdims=True)
        s1 = jnp.sum(jnp.where(first, 0.0, z2), axis=-1, keepdims=True)
        ms = jnp.where(first, s0, s1) * (1.0 / dh)
        return z * lax.rsqrt(ms + EPS) * w

    qn = headnorm(q_ref[...].astype(F32), qw_ref[...]) * (dh ** -0.5 * LOG2E)
    kn = headnorm(k_ref[...].astype(F32), kw_ref[...])
    mine = (lane // BIAS_LANES) == hp
    ka_s[:, :LANES] = kn.astype(BF16)
    ka_s[:, LANES:] = bias_ref[:, :LANES]
    q_lo = [jnp.where(first, qn, 0.0), jnp.where(first, 0.0, qn)]
    hq = tq // 2
    for qi in range(nq):
        for part_i in range(2):
            rows = slice(qi * tq + part_i * hq, qi * tq + (part_i + 1) * hq)
            for h in range(2):
                dst = slice((2 * part_i + h) * hq, (2 * part_i + h + 1) * hq)
                q_hi = bias_ref[rows, (1 + h) * LANES:(2 + h) * LANES].astype(F32)
                qs_s[qi, dst, :LANES] = q_lo[h][rows, :].astype(BF16)
                qs_s[qi, dst, LANES:] = jnp.where(mine, q_hi, 0.0).astype(BF16)

    nt = (((1,), (1,)), ((), ()))

    def causal(s, first_token):
        rr = lax.broadcasted_iota(I32, s.shape, 0)
        rr = jnp.where(rr >= hq, rr - hq, rr) + first_token
        cc = lax.broadcasted_iota(I32, s.shape, 1)
        return jnp.where(rr >= cc, s, MASK_VALUE)

    def finish(qi, halves):
        for part_i, (acc, l) in enumerate(halves):
            o2 = acc / l
            o = jnp.where(first, o2[:hq], o2[hq:])
            rows = slice(qi * tq + part_i * hq, qi * tq + (part_i + 1) * hq)
            g = g_ref[rows, :].astype(F32)
            o_ref[rows, :] = (o * _sigmoid(g)).astype(o_ref.dtype)

    def lane_blocks(e):
        cols = e[:, :LANES]
        for c0 in range(LANES, e.shape[1], LANES):
            cols = cols + e[:, c0:c0 + LANES]
        return cols

    if bounded:
        for qi in range(nq):
            acc, part = [None, None], [None, None]

            def add(i, pv, cols):
                acc[i] = pv if acc[i] is None else acc[i] + pv
                part[i] = cols if part[i] is None else part[i] + cols

            for t in range(qi):
                s = lax.dot_general(qs_s[qi], ka_s[t * tq:(t + 1) * tq, :], nt,
                                    preferred_element_type=F32)
                e = jnp.exp2(s)
                pv = jnp.dot(e.astype(BF16), v_ref[t * tq:(t + 1) * tq, :],
                             preferred_element_type=F32)
                cols = lane_blocks(e)
                add(0, pv[:tq], cols[:tq])
                add(1, pv[tq:], cols[tq:])
            k0 = qi * tq
            for i, n_keys in ((0, hq), (1, tq)):
                s = lax.dot_general(qs_s[qi, i * tq:(i + 1) * tq, :], ka_s[k0:k0 + n_keys, :], nt,
                                    preferred_element_type=F32)
                e = jnp.exp2(causal(s, i * hq))
                pv = jnp.dot(e.astype(BF16), v_ref[k0:k0 + n_keys, :], preferred_element_type=F32)
                add(i, pv, lane_blocks(e))
            finish(qi, [(acc[i], jnp.sum(part[i], axis=-1, keepdims=True)) for i in range(2)])
        return

    def causal_tile(s):
        return jnp.concatenate([causal(s[:tq], 0), causal(s[tq:], hq)], axis=0)

    def scores(q, off):
        return lax.dot_general(q, ka_s[pl.ds(off, tq), :], nt, preferred_element_type=F32)

    def update(s, off, carry, masked):
        m, l, acc = carry
        if masked:
            s = causal_tile(s)
        m_new = jnp.maximum(m, jnp.max(s, axis=-1, keepdims=True))
        p = jnp.exp2(s - m_new)
        alpha = jnp.exp2(m - m_new)
        l = alpha * l + jnp.sum(p, axis=-1, keepdims=True)
        acc = alpha * acc + jnp.dot(p.astype(BF16), v_ref[pl.ds(off, tq), :],
                                    preferred_element_type=F32)
        return m_new, l, acc

    for qi in range(nq):
        q = qs_s[qi]
        s = scores(q, 0)
        carry = (jnp.full((2 * tq, 1), MASK_VALUE, F32), jnp.zeros((2 * tq, 1), F32),
                 jnp.zeros((2 * tq, LANES), F32))
        if qi > 0:
            def body(j, c, q=q):
                off = pl.multiple_of(j * tq, tq)
                s_next = scores(q, off + tq)
                return (s_next,) + update(c[0], off, c[1:], False)

            s, *carry = lax.fori_loop(0, qi, body, (s,) + carry)
        m, l, acc = update(s, qi * tq, carry, True)
        finish(qi, [(acc[:tq], l[:tq]), (acc[tq:], l[tq:])])


def _fox_attention(qkvg, flogit, qw, kw, seq, d, dh):
    t = qkvg.shape[0]
    nb = d // LANES
    assert 2 * dh == LANES and nb * BIAS_LANES <= LANES
    tq = min(seq, 512)
    col = lambda off: pl.BlockSpec((seq, LANES), lambda b, p: (b, off + p))

    def call(bounded, bias):
        return pl.pallas_call(
            functools.partial(_attn_kernel, dh=dh, tq=tq, bounded=bounded),
            grid=(t // seq, nb),
            in_specs=[
                col(0), col(nb), col(2 * nb), col(3 * nb),
                pl.BlockSpec((seq, 3 * LANES), lambda b, p: (b, 0)),
                pl.BlockSpec((1, LANES), lambda b, p: (0, 0)),
                pl.BlockSpec((1, LANES), lambda b, p: (0, 0)),
            ],
            out_specs=pl.BlockSpec((seq, LANES), lambda b, p: (b, p)),
            out_shape=jax.ShapeDtypeStruct((t, d), BF16),
            scratch_shapes=[pltpu.VMEM((seq // tq, 2 * tq, 2 * LANES), BF16),
                            pltpu.VMEM((seq, 2 * LANES), BF16)],
            compiler_params=_cparams("parallel", "parallel"),
            name="fox_attention" if bounded else "fox_attention_running_max",
        )(qkvg, qkvg, qkvg, qkvg, bias, qw, kw)

    bound = (dh ** 0.5 * LOG2E * SCORE_MARGIN) * jnp.max(jnp.abs(qw)) * jnp.max(jnp.abs(kw)) + 0.1
    fits = bound <= EXP2_SPAN - 4.0
    ref = jnp.where(fits, bound - jnp.maximum(0.0, 2.0 * bound - EXP2_SPAN), 0.0)
    bias = _fox_bias(flogit, jnp.full((1, LANES), ref, F32), seq, nb)
    return lax.cond(fits, lambda: call(True, bias), lambda: call(False, bias))


HGRN_PREP_ROWS = 256
HGRN_LOG2_SPAN = 100.0


def _hgrn_kernel(q_ref, f_ref, v_ref, g_ref, lb_ref, gw_ref, o_ref, qe_s, ke_s, kd_s, dec_s, *,
                 hpb, chunk):
    s_len = q_ref.shape[0]
    n_chunks = s_len // chunk
    nt = (((1,), (1,)), ((), ()))
    gw = gw_ref[...]

    pr = min(HGRN_PREP_ROWS, s_len)
    rr = lax.broadcasted_iota(I32, (pr, pr), 0)
    cc = lax.broadcasted_iota(I32, (pr, pr), 1)
    same = (rr // chunk) == (cc // chunk)
    cum_m = jnp.where(same & (rr >= cc), 1.0, 0.0).astype(BF16)
    tot_m = jnp.where(same, 1.0, 0.0).astype(BF16)
    lb_all = lb_ref[...]
    lowest = jnp.zeros((1, lb_all.shape[1]), F32)
    for r0 in range(0, s_len, pr):
        sig = _sigmoid(f_ref[r0:r0 + pr, :])
        terms = [t.astype(BF16) for t in _split3(jnp.log2(lb_all + (1.0 - lb_all) * sig))]
        b = sum(jnp.dot(cum_m, t, preferred_element_type=F32) for t in terms)
        tot = sum(jnp.dot(tot_m, t, preferred_element_type=F32) for t in terms)
        k = (1.0 - lb_all) * (1.0 - sig)
        q_raw = q_ref[r0:r0 + pr, :]
        qe_s[r0:r0 + pr, :] = (q_raw * _sigmoid(q_raw) * jnp.exp2(b)).astype(BF16)
        ke_s[r0:r0 + pr, :] = (k * jnp.exp2(-b)).astype(BF16)
        kd_s[r0:r0 + pr, :] = (k * jnp.exp2(tot - b)).astype(BF16)
        for c0 in range(0, pr, chunk):
            ci = (r0 + c0) // chunk
            dec_s[ci] = jnp.exp2(tot[c0:c0 + SUBLANES, :])
        lowest = jnp.minimum(lowest, jnp.min(tot, axis=0, keepdims=True))
    mild = jnp.min(lowest) >= -HGRN_LOG2_SPAN

    @pl.when(mild)
    def _():
        r32 = lax.broadcasted_iota(I32, (chunk, chunk), 0)
        c32 = lax.broadcasted_iota(I32, (chunk, chunk), 1)
        causal = r32 >= c32
        per_step = 8 if n_chunks % 8 == 0 else 2

        def fast_body(step, states):
            insts = [(ch, hd) for ch in range(per_step) for hd in range(hpb)]
            rows = {ch: pl.ds(pl.multiple_of((step * per_step + ch) * chunk, chunk), chunk)
                    for ch in range(per_step)}
            cols = {hd: slice(hd * LANES, (hd + 1) * LANES) for hd in range(hpb)}
            states = list(states)
            qe = {i: qe_s[rows[i[0]], cols[i[1]]] for i in insts}
            v = {i: v_ref[rows[i[0]], cols[i[1]]] for i in insts}
            a = {i: lax.dot_general(qe[i], ke_s[rows[i[0]], cols[i[1]]], nt,
                                    preferred_element_type=F32) for i in insts}
            upd = {i: jnp.dot(jnp.transpose(v[i]).astype(BF16), kd_s[rows[i[0]], cols[i[1]]],
                              preferred_element_type=F32) for i in insts}
            o = {}
            for ch in range(per_step):
                for hd in range(hpb):
                    i = (ch, hd)
                    o[i] = (jnp.dot(jnp.where(causal, a[i], 0.0).astype(BF16), v[i].astype(BF16),
                                    preferred_element_type=F32)
                            + lax.dot_general(qe[i], states[hd].astype(BF16), nt,
                                              preferred_element_type=F32))
                    dec = dec_s[step * per_step + ch, 0:1, cols[hd]]
                    states[hd] = dec * states[hd] + upd[i]
            for i in insts:
                g = g_ref[rows[i[0]], cols[i[1]]]
                ms = jnp.mean(o[i] * o[i], axis=-1, keepdims=True)
                y = o[i] * lax.rsqrt(ms + EPS) * gw * (g * _sigmoid(g))
                o_ref[rows[i[0]], cols[i[1]]] = y.astype(o_ref.dtype)
            return tuple(states)

        init = tuple(jnp.zeros((LANES, LANES), F32) for _ in range(hpb))
        lax.fori_loop(0, n_chunks // per_step, fast_body, init)

    pair = 2
    groups = chunk // SUBLANES
    off_w = (groups - 1) * chunk
    r = lax.broadcasted_iota(I32, (chunk, chunk), 0)
    c = lax.broadcasted_iota(I32, (chunk, chunk), 1)
    tri = jnp.where(r >= c, 1.0, 0.0).astype(BF16)
    ro = lax.broadcasted_iota(I32, (chunk, off_w), 0) // SUBLANES
    co = lax.broadcasted_iota(I32, (chunk, off_w), 1) // chunk
    keep = ro == co + 1
    sub = lax.broadcasted_iota(I32, (SUBLANES, LANES), 0)
    gw = gw_ref[...]
    nt = (((1,), (1,)), ((), ()))
    grp = lambda a, i: a[i * SUBLANES:(i + 1) * SUBLANES, :]
    zeros8 = jnp.zeros((SUBLANES, LANES), F32)

    def body(step, states):
        insts = [(ch, hd) for ch in range(pair) for hd in range(hpb)]
        rows = {ch: pl.ds(pl.multiple_of((step * pair + ch) * chunk, chunk), chunk)
                for ch in range(pair)}
        cols = {hd: slice(hd * LANES, (hd + 1) * LANES) for hd in range(hpb)}
        st = {}
        for ch, hd in insts:
            lb = lb_ref[:, cols[hd]]
            q_raw = q_ref[rows[ch], cols[hd]]
            sig = _sigmoid(f_ref[rows[ch], cols[hd]])
            d = dict(v=v_ref[rows[ch], cols[hd]], q=q_raw * _sigmoid(q_raw),
                     k=(1.0 - lb) * (1.0 - sig))
            lf = jnp.log(lb + (1.0 - lb) * sig)
            d["b"] = sum(jnp.dot(tri, part.astype(BF16), preferred_element_type=F32)
                         for part in _split3(lf))
            st[ch, hd] = d
        def from_state(ch, hd, state_t):
            d = st[ch, hd]
            qe = (d["q"] * jnp.exp(d["b"])).astype(BF16)
            d["o"] = lax.dot_general(qe, state_t.astype(BF16), nt, preferred_element_type=F32)
        for hd in range(hpb):
            from_state(0, hd, states[hd])
        for ch, hd in insts:
            d = st[ch, hd]
            b, q, k = d["b"], d["q"], d["k"]
            ends = [b[(i + 1) * SUBLANES - 1:(i + 1) * SUBLANES, :] for i in range(groups)]
            kt = [grp(k, i) * jnp.exp(ends[i] - grp(b, i)) for i in range(groups)]
            qh = jnp.concatenate(
                [zeros8] + [grp(q, i) * jnp.exp(grp(b, i) - ends[i - 1]) for i in range(1, groups)],
                axis=0).astype(BF16)
            kh = jnp.concatenate(
                [kt[j] * jnp.exp(ends[i - 1] - ends[j]) if j < i else zeros8
                 for i in range(1, groups) for j in range(groups)], axis=0).astype(BF16)
            d["a"] = lax.dot_general(qh, kh, nt, preferred_element_type=F32)
            d["ends"], d["kt"] = ends, kt
        def state_update(ch, hd, state_t):
            d = st[ch, hd]
            b_end = d["ends"][-1]
            ke = jnp.concatenate([d["kt"][j] * jnp.exp(b_end - d["ends"][j]) for j in range(groups)],
                                 axis=0).astype(BF16)
            upd = jnp.dot(jnp.transpose(d["v"]).astype(BF16), ke, preferred_element_type=F32)
            return jnp.exp(b_end) * state_t + upd
        states = [state_update(0, hd, states[hd]) for hd in range(hpb)]
        for ch, hd in insts:
            d = st[ch, hd]
            vb = d["v"].astype(BF16)
            v3 = jnp.concatenate([vb] * (groups - 1), axis=0)
            d["off"] = jnp.dot(jnp.where(keep, d["a"], 0.0).astype(BF16), v3,
                               preferred_element_type=F32)
        for hd in range(hpb):
            from_state(1, hd, states[hd])
        states = [state_update(1, hd, states[hd]) for hd in range(hpb)]
        for ch, hd in insts:
            d = st[ch, hd]
            o = d["o"] + d["off"]
            o_g = []
            for gi in range(groups):
                qg, kg, vg, bg = grp(d["q"], gi), grp(d["k"], gi), grp(d["v"], gi), grp(d["b"], gi)
                og = grp(o, gi)
                for s in range(SUBLANES):
                    live = sub >= s
                    e = jnp.exp(jnp.where(live, bg - bg[s:s + 1, :], 0.0))
                    p = jnp.where(live, qg * e * kg[s:s + 1, :], 0.0)
                    og = og + jnp.sum(p, axis=-1, keepdims=True) * vg[s:s + 1, :]
                o_g.append(og)
            o = jnp.concatenate(o_g, axis=0)
            g = g_ref[rows[ch], cols[hd]]
            ms = jnp.mean(o * o, axis=-1, keepdims=True)
            y = o * lax.rsqrt(ms + EPS) * gw * (g * _sigmoid(g))
            o_ref[rows[ch], cols[hd]] = y.astype(o_ref.dtype)
        return tuple(states)

    @pl.when(jnp.logical_not(mild))
    def _():
        init = tuple(jnp.zeros((LANES, LANES), F32) for _ in range(hpb))
        lax.fori_loop(0, s_len // (chunk * pair), body, init)


def _hgrn_mixer(proj, lower, gw, seq, d, hpb):
    t = proj.shape[0]
    w = hpb * LANES
    nb = d // w
    col = lambda off: pl.BlockSpec((seq, w), lambda b, p: (b, off + p))
    return pl.pallas_call(
        functools.partial(_hgrn_kernel, hpb=hpb, chunk=HGRN_CHUNK),
        grid=(t // seq, nb),
        in_specs=[
            col(0), col(nb), col(2 * nb), col(3 * nb),
            pl.BlockSpec((1, w), lambda b, p: (0, p)),
            pl.BlockSpec((1, LANES), lambda b, p: (0, 0)),
        ],
        out_specs=pl.BlockSpec((seq, w), lambda b, p: (b, p)),
        out_shape=jax.ShapeDtypeStruct((t, d), BF16),
        scratch_shapes=[pltpu.VMEM((seq, w), BF16)] * 3
        + [pltpu.VMEM((seq // HGRN_CHUNK, SUBLANES, w), F32)],
        compiler_params=_cparams("parallel", "parallel"),
        name="hgrn_mixer",
    )(proj, proj, proj, proj, lower, gw)


def _post_kernel(o_ref, x_ref, wo_ref, g1_ref, nw_ref, sh_ref, sc_ref, wr_ref, br_ref,
                 x1_ref, hp_ref, ri_ref, rw_ref, cnt_ref, run_s):
    i = pl.program_id(0)
    tm = x_ref.shape[0]
    sub = min(tm, POST_SUB)
    subs = [slice(r0, r0 + sub) for r0 in range(0, tm, sub)]

    @pl.when(i == 0)
    def _():
        run_s[...] = jnp.zeros_like(run_s)

    lane = lax.broadcasted_iota(I32, (sub, LANES), 1).astype(F32)
    r = lax.broadcasted_iota(I32, (sub, sub), 0)
    c = lax.broadcasted_iota(I32, (sub, sub), 1)
    strict = jnp.where(r > c, 1.0, 0.0).astype(BF16)

    y = [jnp.dot(o_ref[rows, :], wo_ref[...], preferred_element_type=F32) for rows in subs]
    logits = []
    for rows, y_s in zip(subs, y):
        x1 = x_ref[rows, :] + g1_ref[...] * y_s
        x1_ref[rows, :] = x1
        h2 = _norm_mod(x1, nw_ref[...], sc_ref[...], sh_ref[...])
        hp_ref[rows, :] = _pack_halves(h2)
        logits.append(jnp.dot(h2.astype(BF16), wr_ref[...], preferred_element_type=F32) + br_ref[...])

    picks = []
    for work in logits:
        idx, val = [], []
        for _ in range(TOP_K):
            mx = jnp.max(work, axis=-1, keepdims=True)
            ix = jnp.min(jnp.where(work == mx, lane, float(LANES)), axis=-1, keepdims=True)
            idx.append(ix)
            val.append(mx)
            work = jnp.where(lane == ix, -jnp.inf, work)
        ex = [jnp.exp(v - val[0]) for v in val]
        den = ex[0] + ex[1] + ex[2] + ex[3]
        hot = jnp.zeros((sub, LANES), F32)
        for ix in idx:
            hot = hot + jnp.where(lane == ix, 1.0, 0.0)
        picks.append((idx, [e / den for e in ex], hot))

    inside = [jnp.dot(strict, hot.astype(BF16), preferred_element_type=F32) for _, _, hot in picks]
    run = run_s[...]
    for rows, (idx, wts, hot), before in zip(subs, picks, inside):
        before = before + run
        run = run + jnp.sum(hot, axis=0, keepdims=True)
        ranks = [jnp.sum(jnp.where(lane == ix, before, 0.0), axis=-1, keepdims=True) for ix in idx]
        ri = jnp.zeros((sub, LANES), F32)
        rw = jnp.zeros((sub, LANES), F32)
        for k in range(TOP_K):
            ri = jnp.where(lane == float(k), idx[k], ri)
            ri = jnp.where(lane == float(TOP_K + k), ranks[k], ri)
            rw = jnp.where(lane == float(k), wts[k], rw)
        ri_ref[:, rows] = jnp.transpose(ri)[:2 * TOP_K, :].astype(I32)
        rw_ref[rows, :] = rw
    run_s[...] = run
    cnt_ref[...] = run.astype(I32)


def _post_mixer(o, x, wo, mod, nw2, wr, br, seq, tm, row_off=0):
    t, d = o.shape
    per_seq = seq // tm
    tile_off = row_off // tm
    mod_spec = lambda j: pl.BlockSpec((None, 1, d), lambda i: (i // per_seq, 0, j))
    row = lambda w: pl.BlockSpec((tm, w), lambda i: (i, 0))
    return pl.pallas_call(
        _post_kernel,
        grid=(t // tm,),
        in_specs=[
            row(d), pl.BlockSpec((tm, d), lambda i: (i + tile_off, 0)),
            _resident((d, d), lambda i: (0, 0)),
            mod_spec(2),
            _resident((1, d), lambda i: (0, 0)),
            mod_spec(3), mod_spec(4),
            _resident((d, LANES), lambda i: (0, 0)),
            _resident((1, LANES), lambda i: (0, 0)),
        ],
        out_specs=[row(d), row(d // 2), pl.BlockSpec((2 * TOP_K, tm), lambda i: (0, i)), row(LANES),
                   pl.BlockSpec((1, LANES), lambda i: (0, 0))],
        out_shape=[
            jax.ShapeDtypeStruct((t, d), F32),
            jax.ShapeDtypeStruct((t, d // 2), U32),
            jax.ShapeDtypeStruct((2 * TOP_K, t), I32),
            jax.ShapeDtypeStruct((t, LANES), F32),
            jax.ShapeDtypeStruct((1, LANES), I32),
        ],
        scratch_shapes=[pltpu.VMEM((1, LANES), F32)],
        compiler_params=_cparams("arbitrary"),
        name="post_mixer",
    )(o, x, wo, mod, nw2, mod, mod, wr, br)


def _sc_mesh():
    return plsc.VectorSubcoreMesh(core_axis_name="c", subcore_axis_name="s")


def _sc_scatter_rows(x, posf, n_rows):
    t, w = x.shape
    workers = SC_CORES * SC_SUBCORES
    per_w = t // workers

    @functools.partial(
        pl.kernel, mesh=_sc_mesh(),
        out_type=jax.ShapeDtypeStruct((n_rows, w), x.dtype),
        scratch_types=[pltpu.VMEM((SC_ROWS,), I32)] * TOP_K
        + [pltpu.VMEM((SC_ROWS, w), x.dtype), pltpu.SemaphoreType.DMA])
    def scatter_kernel(x_hbm, pos_hbm, out_hbm, i0, i1, i2, i3, rows_v, sem):
        idxs = (i0, i1, i2, i3)
        wid = lax.axis_index("s") * SC_CORES + lax.axis_index("c")
        base = wid * per_w

        @pl.loop(0, per_w // SC_ROWS)
        def _(i):
            off = base + i * SC_ROWS
            for k in range(TOP_K):
                pltpu.sync_copy(pos_hbm.at[pl.ds(k * t + off, SC_ROWS)], idxs[k])
            pltpu.sync_copy(x_hbm.at[pl.ds(off, SC_ROWS)], rows_v)
            copies = [pltpu.async_copy(rows_v, out_hbm.at[idxs[k]], sem) for k in range(TOP_K)]
            for cp in copies:
                cp.wait()

    return scatter_kernel(x, posf)


def _sc_gather_rows(src, posf):
    n = posf.shape[0]
    w = src.shape[1]
    workers = SC_CORES * SC_SUBCORES
    per_w = n // workers

    @functools.partial(
        pl.kernel, mesh=_sc_mesh(),
        out_type=jax.ShapeDtypeStruct((n, w), src.dtype),
        scratch_types=[pltpu.VMEM((SC_ROWS,), I32), pltpu.VMEM((SC_ROWS, w), src.dtype),
                       pltpu.SemaphoreType.DMA])
    def gather_kernel(src_hbm, pos_hbm, out_hbm, idx_v, rows_v, sem):
        wid = lax.axis_index("s") * SC_CORES + lax.axis_index("c")
        base = wid * per_w

        @pl.loop(0, per_w // SC_ROWS)
        def _(i):
            off = base + i * SC_ROWS
            pltpu.sync_copy(pos_hbm.at[pl.ds(off, SC_ROWS)], idx_v)
            pltpu.async_copy(src_hbm.at[idx_v], rows_v, sem).wait()
            pltpu.sync_copy(rows_v, out_hbm.at[pl.ds(off, SC_ROWS)])

    return gather_kernel(src, posf)


GU_BLOCK = 2 * LANES
FF_CHUNK = 2 * LANES


def _deinterleave_matrix():
    sel = np.zeros((GU_BLOCK, GU_BLOCK), np.float32)
    j = np.arange(LANES)
    sel[2 * j, j] = 1.0
    sel[2 * j + 1, LANES + j] = 1.0
    return jnp.asarray(sel, BF16)


def _expert_kernel(be_ref, nu_ref, nxt_ref, x_ref, wgu_hbm, bg_ref, bu_ref, wd_hbm, bd_ref, sel_ref,
                   y_ref, wgu_f, wd_f, wgu_s, wd_s, sems, *, layer):
    i = pl.program_id(0)
    d, f2 = wgu_s.shape
    f = f2 // 2
    e_cur = be_ref[i]
    new_expert = jnp.logical_or(i == 0, e_cur != be_ref[jnp.maximum(i - 1, 0)])

    def weight_copies(e):
        return (pltpu.make_async_copy(wgu_hbm.at[layer, e], wgu_f, sems.at[0]),
                pltpu.make_async_copy(wd_hbm.at[layer, e], wd_f, sems.at[1]))

    @pl.when(jnp.logical_and(new_expert, i < nu_ref[0]))
    def _():
        @pl.when(i == 0)
        def _():
            for cp in weight_copies(e_cur):
                cp.start()

        for cp in weight_copies(e_cur):
            cp.wait()
        for c0 in range(0, f2, GU_BLOCK):
            blk = wgu_f[:, c0:c0 + GU_BLOCK].astype(BF16)
            wgu_s[:, c0:c0 + GU_BLOCK] = jnp.dot(
                blk, sel_ref[...], preferred_element_type=F32).astype(BF16)
        wd_s[...] = wd_f[...].astype(BF16)

        @pl.when(nxt_ref[e_cur] >= 0)
        def _():
            for cp in weight_copies(nxt_ref[e_cur]):
                cp.start()

    @pl.when(i < nu_ref[0])
    def _():
        half = x_ref.shape[1]
        lo, hi = _unpack_halves(x_ref[...])
        lo = lo.astype(BF16)
        hi = hi.astype(BF16)

        def hidden(c0):
            cols = slice(2 * c0, 2 * (c0 + FF_CHUNK))
            gu = (jnp.dot(lo, wgu_s[:half, cols], preferred_element_type=F32)
                  + jnp.dot(hi, wgu_s[half:, cols], preferred_element_type=F32))
            parts = []
            for b0 in range(0, FF_CHUNK, LANES):
                gate = gu[:, 2 * b0:2 * b0 + LANES] + bg_ref[0, :, c0 + b0:c0 + b0 + LANES]
                up = gu[:, 2 * b0 + LANES:2 * b0 + 2 * LANES] + bu_ref[0, :, c0 + b0:c0 + b0 + LANES]
                gate = jnp.minimum(gate, SWIGLU_LIMIT)
                up = jnp.clip(up, -SWIGLU_LIMIT, SWIGLU_LIMIT)
                parts.append((up + 1.0) * gate * _sigmoid(SWIGLU_ALPHA * gate))
            return jnp.concatenate(parts, axis=1).astype(BF16)

        chunks = list(range(0, f, FF_CHUNK))
        y = None
        h_next = hidden(chunks[0])
        for n, c0 in enumerate(chunks):
            h = h_next
            if n + 1 < len(chunks):
                h_next = hidden(chunks[n + 1])
            part = jnp.dot(h, wd_s[c0:c0 + FF_CHUNK, :], preferred_element_type=F32)
            y = part if y is None else y + part
        y_ref[...] = _pack_halves(y + bd_ref[0])

    @pl.when(i >= nu_ref[0])
    def _():
        y_ref[...] = jnp.zeros_like(y_ref)


def _expert_ffn(xs, blk_e, n_used, nxt_e, w_gu, bg, bu, w_down, bd, rb, layer):
    rows, half = xs.shape
    _, e, d, f2 = w_gu.shape
    f = f2 // 2
    n_blocks = rows // rb
    wspec = lambda shape: pl.BlockSpec(shape, lambda i, be, nu, nx: (be[i], 0, 0))
    return pl.pallas_call(
        functools.partial(_expert_kernel, layer=layer),
        grid_spec=pltpu.PrefetchScalarGridSpec(
            num_scalar_prefetch=3,
            grid=(n_blocks,),
            in_specs=[
                pl.BlockSpec((rb, half), lambda i, be, nu, nx: (i, 0)),
                pl.BlockSpec(memory_space=pl.ANY),
                wspec((1, 1, f)), wspec((1, 1, f)),
                pl.BlockSpec(memory_space=pl.ANY),
                wspec((1, 1, d)),
                _resident((GU_BLOCK, GU_BLOCK), lambda i, be, nu, nx: (0, 0)),
            ],
            out_specs=pl.BlockSpec((rb, d // 2), lambda i, be, nu, nx: (i, 0)),
            scratch_shapes=[pltpu.VMEM((d, f2), F32), pltpu.VMEM((f, d), F32),
                            pltpu.VMEM((d, f2), BF16), pltpu.VMEM((f, d), BF16),
                            pltpu.SemaphoreType.DMA((2,))],
        ),
        out_shape=jax.ShapeDtypeStruct((rows, d // 2), U32),
        compiler_params=_cparams("arbitrary"),
        name="expert_ffn",
    )(blk_e, n_used, nxt_e, xs, w_gu, bg, bu, w_down, bd, _deinterleave_matrix())


def _combine_kernel(x_ref, y0_ref, y1_ref, y2_ref, y3_ref, rw_ref, g2_ref, *dst_and_out):
    o_ref = dst_and_out[-1]
    half = y0_ref.shape[1]
    lo, hi = _moe_mix(x_ref, (y0_ref, y1_ref, y2_ref, y3_ref), rw_ref, g2_ref)
    o_ref[:, :half] = lo
    o_ref[:, half:] = hi


def _combine(x1, yt, rw, mod, seq, tm, out_rows=None, row_off=0, dst=None):
    t, d = x1.shape
    per_seq = seq // tm
    nt = t // tm
    tile_off = row_off // tm
    yspec = lambda k: pl.BlockSpec((tm, d // 2), lambda i: (k * nt + i, 0))
    in_specs = [
        pl.BlockSpec((tm, d), lambda i: (i, 0)),
        yspec(0), yspec(1), yspec(2), yspec(3),
        pl.BlockSpec((tm, LANES), lambda i: (i, 0)),
        pl.BlockSpec((None, 1, d), lambda i: (i // per_seq, 0, 5)),
    ]
    args = [x1, yt, yt, yt, yt, rw, mod]
    aliases = {}
    if dst is not None:
        in_specs.append(pl.BlockSpec(memory_space=pl.ANY))
        args.append(dst)
        aliases = {len(args) - 1: 0}
    return pl.pallas_call(
        _combine_kernel,
        grid=(nt,),
        in_specs=in_specs,
        out_specs=pl.BlockSpec((tm, d), lambda i: (i + tile_off, 0)),
        out_shape=jax.ShapeDtypeStruct((t if out_rows is None else out_rows, d), F32),
        input_output_aliases=aliases,
        compiler_params=_cparams("parallel"),
        name="moe_combine",
    )(*args)


POST_ROWS = 1024
POST_SUB = 256


def _row_tile(seq):
    return min(seq, 512)


def _moe_block_rows(t):
    return min(512, t * TOP_K // 8)


def kernel(x, c, fox_w_in, fox_b_f, fox_q_norm, fox_k_norm, fox_w_out, hgrn_w_in, hgrn_lb, hgrn_g_norm, hgrn_w_out, ada_w, ada_b, norm1_w, norm2_w, router_w, router_b, exp_w_gu, exp_b_gu, exp_w_down, exp_b_down):
    bsz, seq, d = x.shape
    t = bsz * seq
    depth = ada_w.shape[0]
    n_exp = router_w.shape[-1]
    f_dim = exp_w_down.shape[2]
    h_fox = fox_w_in.shape[-1] - 4 * d
    dh = d // h_fox
    tm = _row_tile(seq)
    n_split = 2 if bsz % 2 == 0 else 1
    bh, th = bsz // n_split, t // n_split
    rb = _moe_block_rows(th)
    n_blocks = (th * TOP_K) // rb + n_exp
    hgrn_hpb = 4 if d % (4 * LANES) == 0 else 2

    mod_all = _adaln_mod(c, ada_w, ada_b).reshape(depth, bsz, 1, -1)
    lb_soft = jax.nn.softmax(hgrn_lb.astype(F32), axis=0)
    lower = jnp.cumsum(lb_soft, axis=0) - lb_soft[0]
    experts = jnp.arange(n_exp, dtype=I32)

    xf = x.reshape(t, d)
    parts = [xf] * n_split
    for i in range(depth):
        j = i // 2
        nw1 = norm1_w[i].reshape(1, d)
        if i % 2 == 0:
            w_in = fox_w_in[j]
            w_main = w_in[:, :4 * d].astype(BF16)
            w_f = jnp.pad(w_in[:, 4 * d:], ((0, 0), (0, LANES - h_fox))).astype(BF16)
            b_f = jnp.pad(fox_b_f[j], (0, LANES - h_fox)).reshape(1, LANES)
            qw = jnp.tile(fox_q_norm[j], LANES // dh).reshape(1, LANES)
            kw = jnp.tile(fox_k_norm[j], LANES // dh).reshape(1, LANES)
            w_out = fox_w_out[j].astype(BF16)
        else:
            w_main = hgrn_w_in[j].astype(BF16)
            w_out = hgrn_w_out[j].astype(BF16)
        wr = jnp.pad(router_w[i], ((0, 0), (0, LANES - n_exp))).astype(BF16)
        br = jnp.pad(router_b[i], (0, LANES - n_exp), constant_values=MASK_VALUE).reshape(1, LANES)
        b_gu = exp_b_gu[i].reshape(n_exp, 1, f_dim, 2)

        def front(s, src):
            mod = mod_all[i, s * bh:(s + 1) * bh]
            off = s * th if i == 0 else 0
            moe = None
            xp = src
            if isinstance(src, tuple):
                xp, moe = src[0], src[1:]
            if i % 2 == 0:
                outs = _in_proj(xp, nw1, mod, w_main, BF16, seq, tm, w_f, b_f,
                                rows=th, row_off=off, moe=moe)
                o = _fox_attention(outs[0], outs[1], qw, kw, seq, d, dh)
            else:
                outs = _in_proj(xp, nw1, mod, w_main, F32, seq, tm, rows=th, row_off=off, moe=moe)
                o = _hgrn_mixer(outs[0], lower[j].reshape(1, d), hgrn_g_norm[j].reshape(1, HGRN_DIM),
                                seq, d, hgrn_hpb)
            if moe is not None:
                xp = outs[-1]
            x1, hp, ri, rw, cnt = _post_mixer(o, xp, w_out, mod, norm2_w[i].reshape(1, d), wr, br,
                                              seq, min(seq, POST_ROWS), row_off=off)
            counts = cnt[0, :n_exp]
            padded = (counts + rb - 1) // rb * rb
            pad_end = jnp.cumsum(padded)
            pad_start = pad_end - padded
            top_idx, rank = ri[:TOP_K], ri[TOP_K:]
            start = jnp.sum(jnp.where(top_idx[None] == experts[:, None, None],
                                      pad_start[:, None, None], 0), axis=0)
            posf = (start + rank).reshape(-1).astype(I32)
            blk_lo = jnp.arange(n_blocks, dtype=I32) * rb
            blk_e = jnp.minimum(jnp.sum(pad_end[None, :] <= blk_lo[:, None], axis=1),
                                n_exp - 1).astype(I32)
            n_used = (pad_end[-1] // rb).astype(I32).reshape(1)
            later = jnp.where((experts[None, :] > experts[:, None]) & (padded[None, :] > 0),
                              experts[None, :], n_exp)
            nxt_e = jnp.min(later, axis=1)
            nxt_e = jnp.where(nxt_e >= n_exp, -1, nxt_e).astype(I32)
            return dict(x1=x1, hp=hp, rw=rw, posf=posf, blk_e=blk_e, n_used=n_used, nxt_e=nxt_e,
                        mod=mod)

        def experts_of(st, xs):
            return _expert_ffn(xs, st["blk_e"], st["n_used"], st["nxt_e"], exp_w_gu,
                               b_gu[..., 0], b_gu[..., 1], exp_w_down,
                               exp_b_down[i].reshape(n_exp, 1, d), rb, i)

        def deferred(st, yt):
            return (st["x1"], yt, st["rw"], st["mod"])

        if n_split == 1:
            st = front(0, parts[0])
            xs = _sc_scatter_rows(st["hp"], st["posf"], n_blocks * rb)
            parts[0] = deferred(st, _sc_gather_rows(experts_of(st, xs), st["posf"]))
            continue

        tie = lax.optimization_barrier
        sa = front(0, parts[0])
        xs_a = _sc_scatter_rows(sa["hp"], sa["posf"], n_blocks * rb)
        src_b = parts[1]
        if isinstance(src_b, tuple):
            yt_b, _ = tie((src_b[1], sa["hp"]))
            src_b = (src_b[0], yt_b) + src_b[2:]
        else:
            src_b, _ = tie((src_b, sa["hp"]))
        sb = front(1, src_b)
        xs_a, _ = tie((xs_a, sb["hp"]))
        xs_b = _sc_scatter_rows(sb["hp"], sb["posf"], n_blocks * rb)
        ys_a = experts_of(sa, xs_a)
        xs_b, _ = tie((xs_b, ys_a))
        yt_a = _sc_gather_rows(ys_a, sa["posf"])
        ys_b = experts_of(sb, xs_b)
        yt_a, _ = tie((yt_a, ys_b))
        yt_b = _sc_gather_rows(ys_b, sb["posf"])
        parts = [deferred(sa, yt_a), deferred(sb, yt_b)]

    out = None
    for s_i, (x1, yt, rw, mod) in enumerate(parts):
        if n_split == 1:
            out = _combine(x1, yt, rw, mod, seq, tm)
        else:
            out = _combine(x1, yt, rw, mod, seq, tm, out_rows=t, row_off=s_i * th, dst=out)
    return out.reshape(bsz, seq, d)
```

```python
import functools

import jax
import jax.numpy as jnp
import numpy as np
from jax import lax
from jax.experimental import pallas as pl
from jax.experimental.pallas import tpu as pltpu
from jax.experimental.pallas import tpu_sc as plsc

F32 = jnp.float32
BF16 = jnp.bfloat16
U32 = jnp.uint32
I32 = jnp.int32

EPS = 1e-6
MASK_VALUE = -1e30
TOP_K = 4
HGRN_DIM = 128
HGRN_CHUNK = 32
SWIGLU_LIMIT = 7.0
SWIGLU_ALPHA = 1.702
LANES = 128
SUBLANES = 8
SC_CORES = 2
SC_SUBCORES = 16
SC_ROWS = 64
VMEM_LIMIT = 56 * 1024 * 1024
HIGHEST = lax.Precision.HIGHEST


def _cparams(*sem):
    return pltpu.CompilerParams(dimension_semantics=sem, vmem_limit_bytes=VMEM_LIMIT)


def _resident(shape, index_map):
    return pl.BlockSpec(shape, index_map, pipeline_mode=pl.Buffered(1))


def _sigmoid(x):
    return 1.0 / (1.0 + jnp.exp(-x))


def _bf16_bits(x):
    return lax.bitcast_convert_type(x.astype(BF16).astype(F32), U32)


def _pack_halves(x):
    n = x.shape[1] // 2
    lo = _bf16_bits(x[:, :n])
    hi = _bf16_bits(x[:, n:])
    return (hi & jnp.uint32(0xFFFF0000)) | (lo >> 16)


def _unpack_halves(w):
    lo = lax.bitcast_convert_type(w << 16, F32)
    hi = lax.bitcast_convert_type(w & jnp.uint32(0xFFFF0000), F32)
    return lo, hi


def _mod_kernel(c_ref, w_ref, b_ref, o_ref):
    c = c_ref[...]
    ca = c * _sigmoid(c)
    o_ref[0] = jnp.dot(ca, w_ref[0], preferred_element_type=F32, precision=HIGHEST) + b_ref[0]


def _adaln_mod(c, ada_w, ada_b):
    depth, d, n6 = ada_w.shape
    b = c.shape[0]
    nmod = n6 // d
    return pl.pallas_call(
        _mod_kernel,
        grid=(depth, nmod),
        in_specs=[
            pl.BlockSpec((b, d), lambda i, j: (0, 0)),
            pl.BlockSpec((1, d, d), lambda i, j: (i, 0, j)),
            pl.BlockSpec((1, 1, d), lambda i, j: (i, 0, j)),
        ],
        out_specs=pl.BlockSpec((1, b, d), lambda i, j: (i, 0, j)),
        out_shape=jax.ShapeDtypeStruct((depth, b, n6), F32),
        compiler_params=_cparams("parallel", "parallel"),
        name="adaln_mod",
    )(c, ada_w, ada_b.reshape(depth, 1, n6))


def _norm_mod(x, nw, sc, sh):
    ms = jnp.mean(x * x, axis=-1, keepdims=True)
    return x * lax.rsqrt(ms + EPS) * nw * (1.0 + sc) + sh


def _moe_mix(x_ref, y_refs, rw_ref, g2_ref, rows=slice(None)):
    rw = rw_ref[rows, :]
    lane = lax.broadcasted_iota(I32, rw.shape, 1)
    half = y_refs[0].shape[1]
    acc_lo = acc_hi = None
    for k, y_ref in enumerate(y_refs):
        wk = jnp.sum(jnp.where(lane == k, rw, 0.0), axis=-1, keepdims=True)
        lo, hi = _unpack_halves(y_ref[rows, :])
        acc_lo = wk * lo if acc_lo is None else acc_lo + wk * lo
        acc_hi = wk * hi if acc_hi is None else acc_hi + wk * hi
    g2 = g2_ref[...]
    return (x_ref[rows, :half] + g2[:, :half] * acc_lo, x_ref[rows, half:] + g2[:, half:] * acc_hi)


def _in_kernel(*refs, n_chunk, with_f, with_moe, sub):
    n_in = 1 + (TOP_K + 2 if with_moe else 0) + 4 + (2 if with_f else 0)
    ins, outs = refs[:n_in], refs[n_in:]
    x_ref = ins[0]
    pos = 1 + (TOP_K + 2 if with_moe else 0)
    nw_ref, sh_ref, sc_ref, w_ref = ins[pos:pos + 4]
    o_ref = outs[0]
    if with_f:
        wf_ref, bf_ref = ins[pos + 4:pos + 6]
        of_ref = outs[1]
    n = o_ref.shape[1]
    for r0 in range(0, x_ref.shape[0], sub):
        rows = slice(r0, r0 + sub)
        if with_moe:
            lo, hi = _moe_mix(x_ref, ins[1:1 + TOP_K], ins[1 + TOP_K], ins[2 + TOP_K], rows)
            x = jnp.concatenate([lo, hi], axis=1)
            outs[-1][rows, :] = x
        else:
            x = x_ref[rows, :]
        h = _norm_mod(x, nw_ref[...], sc_ref[...], sh_ref[...]).astype(BF16)
        for c0 in range(0, n, n_chunk):
            o_ref[rows, c0:c0 + n_chunk] = jnp.dot(
                h, w_ref[:, c0:c0 + n_chunk], preferred_element_type=F32).astype(o_ref.dtype)
        if with_f:
            of_ref[rows, :] = jnp.dot(h, wf_ref[...], preferred_element_type=F32) + bf_ref[...]


def _in_proj(x, nw, mod, w, out_dtype, seq, tm, wf=None, bf=None, rows=None, row_off=0, moe=None,
             name="in_proj"):
    d = x.shape[1]
    t = x.shape[0] if rows is None else rows
    n = w.shape[1]
    per_seq = seq // tm
    tile_off = row_off // tm
    nt = t // tm
    with_f = wf is not None
    mod_spec = lambda j: pl.BlockSpec((None, 1, d), lambda i: (i // per_seq, 0, j))
    in_specs = [pl.BlockSpec((tm, d), lambda i: (i + tile_off, 0))]
    args = [x]
    if moe is not None:
        yt, rw, mod_prev = moe
        in_specs += [pl.BlockSpec((tm, d // 2), lambda i, k=k: (k * nt + i, 0)) for k in range(TOP_K)]
        in_specs += [pl.BlockSpec((tm, LANES), lambda i: (i, 0)), mod_spec(5)]
        args += [yt] * TOP_K + [rw, mod_prev]
    in_specs += [_resident((1, d), lambda i: (0, 0)), mod_spec(0), mod_spec(1),
                 _resident((d, n), lambda i: (0, 0))]
    args += [nw, mod, mod, w]
    out_specs = [pl.BlockSpec((tm, n), lambda i: (i, 0))]
    out_shape = [jax.ShapeDtypeStruct((t, n), out_dtype)]
    if with_f:
        nf = wf.shape[1]
        in_specs += [_resident((d, nf), lambda i: (0, 0)), _resident((1, nf), lambda i: (0, 0))]
        out_specs.append(pl.BlockSpec((tm, nf), lambda i: (i, 0)))
        out_shape.append(jax.ShapeDtypeStruct((t, nf), F32))
        args += [wf, bf]
    if moe is not None:
        out_specs.append(pl.BlockSpec((tm, d), lambda i: (i, 0)))
        out_shape.append(jax.ShapeDtypeStruct((t, d), F32))
    return pl.pallas_call(
        functools.partial(_in_kernel, n_chunk=min(n, 1024), with_f=with_f, with_moe=moe is not None,
                          sub=min(tm, 256)),
        grid=(t // tm,),
        in_specs=in_specs,
        out_specs=out_specs,
        out_shape=out_shape,
        compiler_params=_cparams("parallel"),
        name=name,
    )(*args)


LOG2E = 1.4426950408889634
EXP2_SPAN = 120.0
SCORE_MARGIN = 1.01
BIAS_LANES = 16


def _split3(c):
    c1 = c.astype(BF16).astype(F32)
    r = c - c1
    c2 = r.astype(BF16).astype(F32)
    return c1, c2, r - c2


def _bias_layout(n_pairs):
    route =np.zeros((3, 3 * LANES, LANES), np.float32)
    const = np.zeros((3, 1, LANES), np.float32)
    for p in range(n_pairs):
        base = p * BIAS_LANES
        for i in range(3):
            const[0, 0, base + i] = 1.0
            for h in range(2):
                route[0, i * LANES + 2 * p + h, base + 3 + 3 * h + i] = -1.0
                route[1 + h, i * LANES + 2 * p + h, base + i] = 1.0
                const[1 + h, 0, base + 3 + 3 * h + i] = 1.0
    return jnp.asarray(route, BF16), jnp.asarray(const, F32)


def _cum_kernel(f_ref, ref_ref, route_ref, const_ref, o_ref, *, rows):
    s = f_ref.shape[0]
    r = lax.broadcasted_iota(I32, (rows, rows), 0)
    c = lax.broadcasted_iota(I32, (rows, rows), 1)
    tri = jnp.where(r >= c, 1.0, 0.0).astype(BF16)
    carry = jnp.zeros((1, LANES), F32)
    for r0 in range(0, s, rows):
        x = f_ref[r0:r0 + rows, :]
        lf = jnp.minimum(x, 0.0) - jnp.log(1.0 + jnp.exp(-jnp.abs(x)))
        cs = carry + sum(jnp.dot(tri, part.astype(BF16), preferred_element_type=F32)
                         for part in _split3(lf))
        carry = cs[rows - 1:rows, :]
        c2 = cs * LOG2E
        k_terms = jnp.concatenate(_split3(c2), axis=1).astype(BF16)
        q_terms = jnp.concatenate(_split3(c2 - ref_ref[...]), axis=1).astype(BF16)
        for j, terms in enumerate((k_terms, q_terms, q_terms)):
            routed = jnp.dot(terms, route_ref[j], preferred_element_type=F32) + const_ref[j]
            o_ref[r0:r0 + rows, j * LANES:(j + 1) * LANES] = routed.astype(o_ref.dtype)


def _fox_bias(flogit, ref, seq, n_pairs):
    t = flogit.shape[0]
    route, const = _bias_layout(n_pairs)
    return pl.pallas_call(
        functools.partial(_cum_kernel, rows=min(seq, 256)),
        grid=(t // seq,),
        in_specs=[pl.BlockSpec((seq, LANES), lambda b: (b, 0)),
                  pl.BlockSpec((1, LANES), lambda b: (0, 0)),
                  pl.BlockSpec((3, 3 * LANES, LANES), lambda b: (0, 0, 0)),
                  pl.BlockSpec((3, 1, LANES), lambda b: (0, 0, 0))],
        out_specs=pl.BlockSpec((seq, 3 * LANES), lambda b: (b, 0)),
        out_shape=jax.ShapeDtypeStruct((t, 3 * LANES), BF16),
        compiler_params=_cparams("parallel"),
        name="fox_bias",
    )(flogit, ref, route, const)


def _attn_kernel(q_ref, k_ref, v_ref, g_ref, bias_ref, qw_ref, kw_ref, o_ref, qs_s, ka_s, *,
                 dh, tq, bounded):
    s_len = q_ref.shape[0]
    nq = s_len // tq
    hp = pl.program_id(1)
    lane = lax.broadcasted_iota(I32, (1, LANES), 1)
    first = lane < dh

    def headnorm(z, w):
        z2 = z * z
        s0 = jnp.sum(jnp.where(first, z2, 0.0), axis=-1, keepdims=True)
        s1 = jnp.sum(jnp.where(first, 0.0, z2), axis=-1, keepdims=True)
        ms = jnp.where(first, s0, s1) * (1.0 / dh)
        return z * lax.rsqrt(ms + EPS) * w

    qn = headnorm(q_ref[...].astype(F32), qw_ref[...]) * (dh ** -0.5 * LOG2E)
    kn = headnorm(k_ref[...].astype(F32), kw_ref[...])
    mine = (lane // BIAS_LANES) == hp
    ka_s[:, :LANES] = kn.astype(BF16)
    ka_s[:, LANES:] = bias_ref[:, :LANES]
    q_lo = [jnp.where(first, qn, 0.0), jnp.where(first, 0.0, qn)]
    hq = tq // 2
    for qi in range(nq):
        for part_i in range(2):
            rows = slice(qi * tq + part_i * hq, qi * tq + (part_i + 1) * hq)
            for h in range(2):
                dst = slice((2 * part_i + h) * hq, (2 * part_i + h + 1) * hq)
                q_hi = bias_ref[rows, (1 + h) * LANES:(2 + h) * LANES].astype(F32)
                qs_s[qi, dst, :LANES] = q_lo[h][rows, :].astype(BF16)
                qs_s[qi, dst, LANES:] = jnp.where(mine, q_hi, 0.0).astype(BF16)

    nt = (((1,), (1,)), ((), ()))

    def causal(s, first_token):
        rr = lax.broadcasted_iota(I32, s.shape, 0)
        rr = jnp.where(rr >= hq, rr - hq, rr) + first_token
        cc = lax.broadcasted_iota(I32, s.shape, 1)
        return jnp.where(rr >= cc, s, MASK_VALUE)

    def finish(qi, halves):
        for part_i, (acc, l) in enumerate(halves):
            o2 = acc / l
            o = jnp.where(first, o2[:hq], o2[hq:])
            rows = slice(qi * tq + part_i * hq, qi * tq + (part_i + 1) * hq)
            g = g_ref[rows, :].astype(F32)
            o_ref[rows, :] = (o * _sigmoid(g)).astype(o_ref.dtype)

    def lane_blocks(e):
        cols = e[:, :LANES]
        for c0 in range(LANES, e.shape[1], LANES):
            cols = cols + e[:, c0:c0 + LANES]
        return cols

    if bounded:
        for qi in range(nq):
            acc, part = [None, None], [None, None]

            def add(i, pv, cols):
                acc[i] = pv if acc[i] is None else acc[i] + pv
                part[i] = cols if part[i] is None else part[i] + cols

            for t in range(qi):
                s = lax.dot_general(qs_s[qi], ka_s[t * tq:(t + 1) * tq, :], nt,
                                    preferred_element_type=F32)
                e = jnp.exp2(s)
                pv = jnp.dot(e.astype(BF16), v_ref[t * tq:(t + 1) * tq, :],
                             preferred_element_type=F32)
                cols = lane_blocks(e)
                add(0, pv[:tq], cols[:tq])
                add(1, pv[tq:], cols[tq:])
            k0 = qi * tq
            for i, n_keys in ((0, hq), (1, tq)):
                s = lax.dot_general(qs_s[qi, i * tq:(i + 1) * tq, :], ka_s[k0:k0 + n_keys, :], nt,
                                    preferred_element_type=F32)
                e = jnp.exp2(causal(s, i * hq))
                pv = jnp.dot(e.astype(BF16), v_ref[k0:k0 + n_keys, :], preferred_element_type=F32)
                add(i, pv, lane_blocks(e))
            finish(qi, [(acc[i], jnp.sum(part[i], axis=-1, keepdims=True)) for i in range(2)])
        return

    def causal_tile(s):
        return jnp.concatenate([causal(s[:tq], 0), causal(s[tq:], hq)], axis=0)

    def scores(q, off):
        return lax.dot_general(q, ka_s[pl.ds(off, tq), :], nt, preferred_element_type=F32)

    def update(s, off, carry, masked):
        m, l, acc = carry
        if masked:
            s = causal_tile(s)
        m_new = jnp.maximum(m, jnp.max(s, axis=-1, keepdims=True))
        p = jnp.exp2(s - m_new)
        alpha = jnp.exp2(m - m_new)
        l = alpha * l + jnp.sum(p, axis=-1, keepdims=True)
        acc = alpha * acc + jnp.dot(p.astype(BF16), v_ref[pl.ds(off, tq), :],
                                    preferred_element_type=F32)
        return m_new, l, acc

    for qi in range(nq):
        q = qs_s[qi]
        s = scores(q, 0)
        carry = (jnp.full((2 * tq, 1), MASK_VALUE, F32), jnp.zeros((2 * tq, 1), F32),
                 jnp.zeros((2 * tq, LANES), F32))
        if qi > 0:
            def body(j, c, q=q):
                off = pl.multiple_of(j * tq, tq)
                s_next = scores(q, off + tq)
                return (s_next,) + update(c[0], off, c[1:], False)

            s, *carry = lax.fori_loop(0, qi, body, (s,) + carry)
        m, l, acc = update(s, qi * tq, carry, True)
        finish(qi, [(acc[:tq], l[:tq]), (acc[tq:], l[tq:])])


def _fox_attention(qkvg, flogit, qw, kw, seq, d, dh):
    t = qkvg.shape[0]
    nb = d // LANES
    assert 2 * dh == LANES and nb * BIAS_LANES <= LANES
    tq = min(seq, 512)
    col = lambda off: pl.BlockSpec((seq, LANES), lambda b, p: (b, off + p))

    def call(bounded, bias):
        return pl.pallas_call(
            functools.partial(_attn_kernel, dh=dh, tq=tq, bounded=bounded),
            grid=(t // seq, nb),
            in_specs=[
                col(0), col(nb), col(2 * nb), col(3 * nb),
                pl.BlockSpec((seq, 3 * LANES), lambda b, p: (b, 0)),
                pl.BlockSpec((1, LANES), lambda b, p: (0, 0)),
                pl.BlockSpec((1, LANES), lambda b, p: (0, 0)),
            ],
            out_specs=pl.BlockSpec((seq, LANES), lambda b, p: (b, p)),
            out_shape=jax.ShapeDtypeStruct((t, d), BF16),
            scratch_shapes=[pltpu.VMEM((seq // tq, 2 * tq, 2 * LANES), BF16),
                            pltpu.VMEM((seq, 2 * LANES), BF16)],
            compiler_params=_cparams("parallel", "parallel"),
            name="fox_attention" if bounded else "fox_attention_running_max",
        )(qkvg, qkvg, qkvg, qkvg, bias, qw, kw)

    bound = (dh ** 0.5 * LOG2E * SCORE_MARGIN) * jnp.max(jnp.abs(qw)) * jnp.max(jnp.abs(kw)) + 0.1
    fits = bound <= EXP2_SPAN - 4.0
    ref = jnp.where(fits, bound - jnp.maximum(0.0, 2.0 * bound - EXP2_SPAN), 0.0)
    bias = _fox_bias(flogit, jnp.full((1, LANES), ref, F32), seq, nb)
    return lax.cond(fits, lambda: call(True, bias), lambda: call(False, bias))


HGRN_PREP_ROWS = 256
HGRN_LOG2_SPAN = 100.0


def _hgrn_kernel(q_ref, f_ref, v_ref, g_ref, lb_ref, gw_ref, o_ref, qe_s, ke_s, kd_s, dec_s, *,
                 hpb, chunk):
    s_len = q_ref.shape[0]
    n_chunks = s_len // chunk
    nt = (((1,), (1,)), ((), ()))
    gw = gw_ref[...]

    pr = min(HGRN_PREP_ROWS, s_len)
    rr = lax.broadcasted_iota(I32, (pr, pr), 0)
    cc = lax.broadcasted_iota(I32, (pr, pr), 1)
    same = (rr // chunk) == (cc // chunk)
    cum_m = jnp.where(same & (rr >= cc), 1.0, 0.0).astype(BF16)
    tot_m = jnp.where(same, 1.0, 0.0).astype(BF16)
    lb_all = lb_ref[...]
    lowest = jnp.zeros((1, lb_all.shape[1]), F32)
    for r0 in range(0, s_len, pr):
        sig = _sigmoid(f_ref[r0:r0 + pr, :])
        terms = [t.astype(BF16) for t in _split3(jnp.log2(lb_all + (1.0 - lb_all) * sig))]
        b = sum(jnp.dot(cum_m, t, preferred_element_type=F32) for t in terms)
        tot = sum(jnp.dot(tot_m, t, preferred_element_type=F32) for t in terms)
        k = (1.0 - lb_all) * (1.0 - sig)
        q_raw = q_ref[r0:r0 + pr, :].astype(F32)
        qe_s[r0:r0 + pr, :] = (q_raw * _sigmoid(q_raw) * jnp.exp2(b)).astype(BF16)
        ke_s[r0:r0 + pr, :] = (k * jnp.exp2(-b)).astype(BF16)
        kd_s[r0:r0 + pr, :] = (k * jnp.exp2(tot - b)).astype(BF16)
        for c0 in range(0, pr, chunk):
            ci = (r0 + c0) // chunk
            dec_s[ci] = jnp.exp2(tot[c0:c0 + SUBLANES, :])
        lowest = jnp.minimum(lowest, jnp.min(tot, axis=0, keepdims=True))
    mild = jnp.min(lowest) >= -HGRN_LOG2_SPAN

    @pl.when(mild)
    def _():
        r32 = lax.broadcasted_iota(I32, (chunk, chunk), 0)
        c32 = lax.broadcasted_iota(I32, (chunk, chunk), 1)
        causal = r32 >= c32
        per_step = 8 if n_chunks % 8 == 0 else 2

        def fast_body(step, states):
            insts = [(ch, hd) for ch in range(per_step) for hd in range(hpb)]
            rows = {ch: pl.ds(pl.multiple_of((step * per_step + ch) * chunk, chunk), chunk)
                    for ch in range(per_step)}
            cols = {hd: slice(hd * LANES, (hd + 1) * LANES) for hd in range(hpb)}
            states = list(states)
            qe = {i: qe_s[rows[i[0]], cols[i[1]]] for i in insts}
            v = {i: v_ref[rows[i[0]], cols[i[1]]].astype(F32) for i in insts}
            a = {i: lax.dot_general(qe[i], ke_s[rows[i[0]], cols[i[1]]], nt,
                                    preferred_element_type=F32) for i in insts}
            upd = {i: jnp.dot(jnp.transpose(v[i]).astype(BF16), kd_s[rows[i[0]], cols[i[1]]],
                              preferred_element_type=F32) for i in insts}
            o = {}
            for ch in range(per_step):
                for hd in range(hpb):
                    i = (ch, hd)
                    o[i] = (jnp.dot(jnp.where(causal, a[i], 0.0).astype(BF16), v[i].astype(BF16),
                                    preferred_element_type=F32)
                            + lax.dot_general(qe[i], states[hd].astype(BF16), nt,
                                              preferred_element_type=F32))
                    dec = dec_s[step * per_step + ch, 0:1, cols[hd]]
                    states[hd] = dec * states[hd] + upd[i]
            for i in insts:
                g = g_ref[rows[i[0]], cols[i[1]]].astype(F32)
                ms = jnp.mean(o[i] * o[i], axis=-1, keepdims=True)
                y = o[i] * lax.rsqrt(ms + EPS) * gw * (g * _sigmoid(g))
                o_ref[rows[i[0]], cols[i[1]]] = y.astype(o_ref.dtype)
            return tuple(states)

        init = tuple(jnp.zeros((LANES, LANES), F32) for _ in range(hpb))
        lax.fori_loop(0, n_chunks // per_step, fast_body, init)

    pair = 2
    groups = chunk // SUBLANES
    off_w = (groups - 1) * chunk
    r = lax.broadcasted_iota(I32, (chunk, chunk), 0)
    c = lax.broadcasted_iota(I32, (chunk, chunk), 1)
    tri = jnp.where(r >= c, 1.0, 0.0).astype(BF16)
    ro = lax.broadcasted_iota(I32, (chunk, off_w), 0) // SUBLANES
    co = lax.broadcasted_iota(I32, (chunk, off_w), 1) // chunk
    keep = ro == co + 1
    sub = lax.broadcasted_iota(I32, (SUBLANES, LANES), 0)
    gw = gw_ref[...]
    nt = (((1,), (1,)), ((), ()))
    grp = lambda a, i: a[i * SUBLANES:(i + 1) * SUBLANES, :]
    zeros8 = jnp.zeros((SUBLANES, LANES), F32)

    def body(step, states):
        insts = [(ch, hd) for ch in range(pair) for hd in range(hpb)]
        rows = {ch: pl.ds(pl.multiple_of((step * pair + ch) * chunk, chunk), chunk)
                for ch in range(pair)}
        cols = {hd: slice(hd * LANES, (hd + 1) * LANES) for hd in range(hpb)}
        st = {}
        for ch, hd in insts:
            lb = lb_ref[:, cols[hd]]
            q_raw = q_ref[rows[ch], cols[hd]].astype(F32)
            sig = _sigmoid(f_ref[rows[ch], cols[hd]])
            d = dict(v=v_ref[rows[ch], cols[hd]].astype(F32), q=q_raw * _sigmoid(q_raw),
                     k=(1.0 - lb) * (1.0 - sig))
            lf = jnp.log(lb + (1.0 - lb) * sig)
            d["b"] = sum(jnp.dot(tri, part.astype(BF16), preferred_element_type=F32)
                         for part in _split3(lf))
            st[ch, hd] = d
        def from_state(ch, hd, state_t):
            d = st[ch, hd]
            qe = (d["q"] * jnp.exp(d["b"])).astype(BF16)
            d["o"] = lax.dot_general(qe, state_t.astype(BF16), nt, preferred_element_type=F32)
        for hd in range(hpb):
            from_state(0, hd, states[hd])
        for ch, hd in insts:
            d = st[ch, hd]
            b, q, k = d["b"], d["q"], d["k"]
            ends = [b[(i + 1) * SUBLANES - 1:(i + 1) * SUBLANES, :] for i in range(groups)]
            kt = [grp(k, i) * jnp.exp(ends[i] - grp(b, i)) for i in range(groups)]
            qh = jnp.concatenate(
                [zeros8] + [grp(q, i) * jnp.exp(grp(b, i) - ends[i - 1]) for i in range(1, groups)],
                axis=0).astype(BF16)
            kh = jnp.concatenate(
                [kt[j] * jnp.exp(ends[i - 1] - ends[j]) if j < i else zeros8
                 for i in range(1, groups) for j in range(groups)], axis=0).astype(BF16)
            d["a"] = lax.dot_general(qh, kh, nt, preferred_element_type=F32)
            d["ends"], d["kt"] = ends, kt
        def state_update(ch, hd, state_t):
            d = st[ch, hd]
            b_end = d["ends"][-1]
            ke = jnp.concatenate([d["kt"][j] * jnp.exp(b_end - d["ends"][j]) for j in range(groups)],
                                 axis=0).astype(BF16)
            upd = jnp.dot(jnp.transpose(d["v"]).astype(BF16), ke, preferred_element_type=F32)
            return jnp.exp(b_end) * state_t + upd
        states = [state_update(0, hd, states[hd]) for hd in range(hpb)]
        for ch, hd in insts:
            d = st[ch, hd]
            vb = d["v"].astype(BF16)
            v3 = jnp.concatenate([vb] * (groups - 1), axis=0)
            d["off"] = jnp.dot(jnp.where(keep, d["a"], 0.0).astype(BF16), v3,
                               preferred_element_type=F32)
        for hd in range(hpb):
            from_state(1, hd, states[hd])
        states = [state_update(1, hd, states[hd]) for hd in range(hpb)]
        for ch, hd in insts:
            d = st[ch, hd]
            o = d["o"] + d["off"]
            o_g = []
            for gi in range(groups):
                qg, kg, vg, bg = grp(d["q"], gi), grp(d["k"], gi), grp(d["v"], gi), grp(d["b"], gi)
                og = grp(o, gi)
                for s in range(SUBLANES):
                    live = sub >= s
                    e = jnp.exp(jnp.where(live, bg - bg[s:s + 1, :], 0.0))
                    p = jnp.where(live, qg * e * kg[s:s + 1, :], 0.0)
                    og = og + jnp.sum(p, axis=-1, keepdims=True) * vg[s:s + 1, :]
                o_g.append(og)
            o = jnp.concatenate(o_g, axis=0)
            g = g_ref[rows[ch], cols[hd]].astype(F32)
            ms = jnp.mean(o * o, axis=-1, keepdims=True)
            y = o * lax.rsqrt(ms + EPS) * gw * (g * _sigmoid(g))
            o_ref[rows[ch], cols[hd]] = y.astype(o_ref.dtype)
        return tuple(states)

    @pl.when(jnp.logical_not(mild))
    def _():
        init = tuple(jnp.zeros((LANES, LANES), F32) for _ in range(hpb))
        lax.fori_loop(0, s_len // (chunk * pair), body, init)


def _hgrn_mixer(qvg, fgate, lower, gw, seq, d, hpb):
    t = qvg.shape[0]
    w = hpb * LANES
    nb = d // w
    col = lambda off: pl.BlockSpec((seq, w), lambda b, p: (b, off + p))
    return pl.pallas_call(
        functools.partial(_hgrn_kernel, hpb=hpb, chunk=HGRN_CHUNK),
        grid=(t // seq, nb),
        in_specs=[
            col(0), col(0), col(nb), col(2 * nb),
            pl.BlockSpec((1, w), lambda b, p: (0, p)),
            pl.BlockSpec((1, LANES), lambda b, p: (0, 0)),
        ],
        out_specs=pl.BlockSpec((seq, w), lambda b, p: (b, p)),
        out_shape=jax.ShapeDtypeStruct((t, d), BF16),
        scratch_shapes=[pltpu.VMEM((seq, w), BF16)] * 3
        + [pltpu.VMEM((seq // HGRN_CHUNK, SUBLANES, w), F32)],
        compiler_params=_cparams("parallel", "parallel"),
        name="hgrn_mixer",
    )(qvg, fgate, qvg, qvg, lower, gw)


def _post_kernel(o_ref, x_ref, wo_ref, g1_ref, nw_ref, sh_ref, sc_ref, wr_ref, br_ref,
                 x1_ref, hp_ref, ri_ref, rw_ref, cnt_ref, run_s):
    i = pl.program_id(0)
    tm = x_ref.shape[0]
    sub = min(tm, POST_SUB)
    subs = [slice(r0, r0 + sub) for r0 in range(0, tm, sub)]

    @pl.when(i == 0)
    def _():
        run_s[...] = jnp.zeros_like(run_s)

    lane = lax.broadcasted_iota(I32, (sub, LANES), 1).astype(F32)
    r = lax.broadcasted_iota(I32, (sub, sub), 0)
    c = lax.broadcasted_iota(I32, (sub, sub), 1)
    strict = jnp.where(r > c, 1.0, 0.0).astype(BF16)

    y = [jnp.dot(o_ref[rows, :], wo_ref[...], preferred_element_type=F32) for rows in subs]
    logits = []
    for rows, y_s in zip(subs, y):
        x1 = x_ref[rows, :] + g1_ref[...] * y_s
        x1_ref[rows, :] = x1
        h2 = _norm_mod(x1, nw_ref[...], sc_ref[...], sh_ref[...])
        hp_ref[rows, :] = _pack_halves(h2)
        logits.append(jnp.dot(h2.astype(BF16), wr_ref[...], preferred_element_type=F32) + br_ref[...])

    picks = []
    for work in logits:
        idx, val = [], []
        for _ in range(TOP_K):
            mx = jnp.max(work, axis=-1, keepdims=True)
            ix = jnp.min(jnp.where(work == mx, lane, float(LANES)), axis=-1, keepdims=True)
            idx.append(ix)
            val.append(mx)
            work = jnp.where(lane == ix, -jnp.inf, work)
        ex = [jnp.exp(v - val[0]) for v in val]
        den = ex[0] + ex[1] + ex[2] + ex[3]
        hot = jnp.zeros((sub, LANES), F32)
        for ix in idx:
            hot = hot + jnp.where(lane == ix, 1.0, 0.0)
        picks.append((idx, [e / den for e in ex], hot))

    inside = [jnp.dot(strict, hot.astype(BF16), preferred_element_type=F32) for _, _, hot in picks]
    run = run_s[...]
    for rows, (idx, wts, hot), before in zip(subs, picks, inside):
        before = before + run
        run = run + jnp.sum(hot, axis=0, keepdims=True)
        ranks = [jnp.sum(jnp.where(lane == ix, before, 0.0), axis=-1, keepdims=True) for ix in idx]
        ri = jnp.zeros((sub, LANES), F32)
        rw = jnp.zeros((sub, LANES), F32)
        for k in range(TOP_K):
            ri = jnp.where(lane == float(k), idx[k], ri)
            ri = jnp.where(lane == float(TOP_K + k), ranks[k], ri)
            rw = jnp.where(lane == float(k), wts[k], rw)
        ri_ref[:, rows] = jnp.transpose(ri)[:2 * TOP_K, :].astype(I32)
        rw_ref[rows, :] = rw
    run_s[...] = run
    cnt_ref[...] = run.astype(I32)


def _post_mixer(o, x, wo, mod, nw2, wr, br, seq, tm, row_off=0):
    t, d = o.shape
    per_seq = seq // tm
    tile_off = row_off // tm
    mod_spec = lambda j: pl.BlockSpec((None, 1, d), lambda i: (i // per_seq, 0, j))
    row = lambda w: pl.BlockSpec((tm, w), lambda i: (i, 0))
    return pl.pallas_call(
        _post_kernel,
        grid=(t // tm,),
        in_specs=[
            row(d), pl.BlockSpec((tm, d), lambda i: (i + tile_off, 0)),
            _resident((d, d), lambda i: (0, 0)),
            mod_spec(2),
            _resident((1, d), lambda i: (0, 0)),
            mod_spec(3), mod_spec(4),
            _resident((d, LANES), lambda i: (0, 0)),
            _resident((1, LANES), lambda i: (0, 0)),
        ],
        out_specs=[row(d), row(d // 2), pl.BlockSpec((2 * TOP_K, tm), lambda i: (0, i)), row(LANES),
                   pl.BlockSpec((1, LANES), lambda i: (0, 0))],
        out_shape=[
            jax.ShapeDtypeStruct((t, d), F32),
            jax.ShapeDtypeStruct((t, d // 2), U32),
            jax.ShapeDtypeStruct((2 * TOP_K, t), I32),
            jax.ShapeDtypeStruct((t, LANES), F32),
            jax.ShapeDtypeStruct((1, LANES), I32),
        ],
        scratch_shapes=[pltpu.VMEM((1, LANES), F32)],
        compiler_params=_cparams("arbitrary"),
        name="post_mixer",
    )(o, x, wo, mod, nw2, mod, mod, wr, br)


def _sc_mesh():
    return plsc.VectorSubcoreMesh(core_axis_name="c", subcore_axis_name="s")


def _sc_scatter_rows(x, posf, n_rows):
    t, w = x.shape
    workers = SC_CORES * SC_SUBCORES
    per_w = t // workers

    @functools.partial(
        pl.kernel, mesh=_sc_mesh(),
        out_type=jax.ShapeDtypeStruct((n_rows, w), x.dtype),
        scratch_types=[pltpu.VMEM((SC_ROWS,), I32)] * TOP_K
        + [pltpu.VMEM((SC_ROWS, w), x.dtype), pltpu.SemaphoreType.DMA])
    def scatter_kernel(x_hbm, pos_hbm, out_hbm, i0, i1, i2, i3, rows_v, sem):
        idxs = (i0, i1, i2, i3)
        wid = lax.axis_index("s") * SC_CORES + lax.axis_index("c")
        base = wid * per_w

        @pl.loop(0, per_w // SC_ROWS)
        def _(i):
            off = base + i * SC_ROWS
            for k in range(TOP_K):
                pltpu.sync_copy(pos_hbm.at[pl.ds(k * t + off, SC_ROWS)], idxs[k])
            pltpu.sync_copy(x_hbm.at[pl.ds(off, SC_ROWS)], rows_v)
            copies = [pltpu.async_copy(rows_v, out_hbm.at[idxs[k]], sem) for k in range(TOP_K)]
            for cp in copies:
                cp.wait()

    return scatter_kernel(x, posf)


def _sc_gather_rows(src, posf):
    n = posf.shape[0]
    w = src.shape[1]
    workers = SC_CORES * SC_SUBCORES
    per_w = n // workers

    @functools.partial(
        pl.kernel, mesh=_sc_mesh(),
        out_type=jax.ShapeDtypeStruct((n, w), src.dtype),
        scratch_types=[pltpu.VMEM((SC_ROWS,), I32), pltpu.VMEM((SC_ROWS, w), src.dtype),
                       pltpu.SemaphoreType.DMA])
    def gather_kernel(src_hbm, pos_hbm, out_hbm, idx_v, rows_v, sem):
        wid = lax.axis_index("s") * SC_CORES + lax.axis_index("c")
        base = wid * per_w

        @pl.loop(0, per_w // SC_ROWS)
        def _(i):
            off = base + i * SC_ROWS
            pltpu.sync_copy(pos_hbm.at[pl.ds(off, SC_ROWS)], idx_v)
            pltpu.async_copy(src_hbm.at[idx_v], rows_v, sem).wait()
            pltpu.sync_copy(rows_v, out_hbm.at[pl.ds(off, SC_ROWS)])

    return gather_kernel(src, posf)


GU_BLOCK = 2 * LANES
FF_CHUNK = 2 * LANES


def _deinterleave_matrix():
    sel = np.zeros((GU_BLOCK, GU_BLOCK), np.float32)
    j = np.arange(LANES)
    sel[2 * j, j] = 1.0
    sel[2 * j + 1, LANES + j] = 1.0
    return jnp.asarray(sel, BF16)


def _expert_kernel(be_ref, nu_ref, nxt_ref, x_ref, wgu_hbm, bg_ref, bu_ref, wd_hbm, bd_ref, sel_ref,
                   y_ref, wgu_f, wd_f, wgu_s, wd_s, sems, *, layer):
    i = pl.program_id(0)
    d, f2 = wgu_s.shape
    f = f2 // 2
    e_cur = be_ref[i]
    new_expert = jnp.logical_or(i == 0, e_cur != be_ref[jnp.maximum(i - 1, 0)])

    def weight_copies(e):
        return (pltpu.make_async_copy(wgu_hbm.at[layer, e], wgu_f, sems.at[0]),
                pltpu.make_async_copy(wd_hbm.at[layer, e], wd_f, sems.at[1]))

    @pl.when(jnp.logical_and(new_expert, i < nu_ref[0]))
    def _():
        @pl.when(i == 0)
        def _():
            for cp in weight_copies(e_cur):
                cp.start()

        for cp in weight_copies(e_cur):
            cp.wait()
        for c0 in range(0, f2, GU_BLOCK):
            blk = wgu_f[:, c0:c0 + GU_BLOCK].astype(BF16)
            wgu_s[:, c0:c0 + GU_BLOCK] = jnp.dot(
                blk, sel_ref[...], preferred_element_type=F32).astype(BF16)
        wd_s[...] = wd_f[...].astype(BF16)

        @pl.when(nxt_ref[e_cur] >= 0)
        def _():
            for cp in weight_copies(nxt_ref[e_cur]):
                cp.start()

    @pl.when(i < nu_ref[0])
    def _():
        half = x_ref.shape[1]
        lo, hi = _unpack_halves(x_ref[...])
        lo = lo.astype(BF16)
        hi = hi.astype(BF16)

        def hidden(c0):
            cols = slice(2 * c0, 2 * (c0 + FF_CHUNK))
            gu = (jnp.dot(lo, wgu_s[:half, cols], preferred_element_type=F32)
                  + jnp.dot(hi, wgu_s[half:, cols], preferred_element_type=F32))
            parts = []
            for b0 in range(0, FF_CHUNK, LANES):
                gate = gu[:, 2 * b0:2 * b0 + LANES] + bg_ref[0, :, c0 + b0:c0 + b0 + LANES]
                up = gu[:, 2 * b0 + LANES:2 * b0 + 2 * LANES] + bu_ref[0, :, c0 + b0:c0 + b0 + LANES]
                gate = jnp.minimum(gate, SWIGLU_LIMIT)
                up = jnp.clip(up, -SWIGLU_LIMIT, SWIGLU_LIMIT)
                parts.append((up + 1.0) * gate * _sigmoid(SWIGLU_ALPHA * gate))
            return jnp.concatenate(parts, axis=1).astype(BF16)

        chunks = list(range(0, f, FF_CHUNK))
        y = None
        h_next = hidden(chunks[0])
        for n, c0 in enumerate(chunks):
            h = h_next
            if n + 1 < len(chunks):
                h_next = hidden(chunks[n + 1])
            part = jnp.dot(h, wd_s[c0:c0 + FF_CHUNK, :], preferred_element_type=F32)
            y = part if y is None else y + part
        y_ref[...] = _pack_halves(y + bd_ref[0])

    @pl.when(i >= nu_ref[0])
    def _():
        y_ref[...] = jnp.zeros_like(y_ref)


def _expert_ffn(xs, blk_e, n_used, nxt_e, w_gu, bg, bu, w_down, bd, rb, layer):
    rows, half = xs.shape
    _, e, d, f2 = w_gu.shape
    f = f2 // 2
    n_blocks = rows // rb
    wspec = lambda shape: pl.BlockSpec(shape, lambda i, be, nu, nx: (be[i], 0, 0))
    return pl.pallas_call(
        functools.partial(_expert_kernel, layer=layer),
        grid_spec=pltpu.PrefetchScalarGridSpec(
            num_scalar_prefetch=3,
            grid=(n_blocks,),
            in_specs=[
                pl.BlockSpec((rb, half), lambda i, be, nu, nx: (i, 0)),
                pl.BlockSpec(memory_space=pl.ANY),
                wspec((1, 1, f)), wspec((1, 1, f)),
                pl.BlockSpec(memory_space=pl.ANY),
                wspec((1, 1, d)),
                _resident((GU_BLOCK, GU_BLOCK), lambda i, be, nu, nx: (0, 0)),
            ],
            out_specs=pl.BlockSpec((rb, d // 2), lambda i, be, nu, nx: (i, 0)),
            scratch_shapes=[pltpu.VMEM((d, f2), F32), pltpu.VMEM((f, d), F32),
                            pltpu.VMEM((d, f2), BF16), pltpu.VMEM((f, d), BF16),
                            pltpu.SemaphoreType.DMA((2,))],
        ),
        out_shape=jax.ShapeDtypeStruct((rows, d // 2), U32),
        compiler_params=_cparams("arbitrary"),
        name="expert_ffn",
    )(blk_e, n_used, nxt_e, xs, w_gu, bg, bu, w_down, bd, _deinterleave_matrix())


def _combine_kernel(x_ref, y0_ref, y1_ref, y2_ref, y3_ref, rw_ref, g2_ref, *dst_and_out):
    o_ref = dst_and_out[-1]
    half = y0_ref.shape[1]
    lo, hi = _moe_mix(x_ref, (y0_ref, y1_ref, y2_ref, y3_ref), rw_ref, g2_ref)
    o_ref[:, :half] = lo
    o_ref[:, half:] = hi


def _combine(x1, yt, rw, mod, seq, tm, out_rows=None, row_off=0, dst=None):
    t, d = x1.shape
    per_seq = seq // tm
    nt = t // tm
    tile_off = row_off // tm
    yspec = lambda k: pl.BlockSpec((tm, d // 2), lambda i: (k * nt + i, 0))
    in_specs = [
        pl.BlockSpec((tm, d), lambda i: (i, 0)),
        yspec(0), yspec(1), yspec(2), yspec(3),
        pl.BlockSpec((tm, LANES), lambda i: (i, 0)),
        pl.BlockSpec((None, 1, d), lambda i: (i // per_seq, 0, 5)),
    ]
    args = [x1, yt, yt, yt, yt, rw, mod]
    aliases = {}
    if dst is not None:
        in_specs.append(pl.BlockSpec(memory_space=pl.ANY))
        args.append(dst)
        aliases = {len(args) - 1: 0}
    return pl.pallas_call(
        _combine_kernel,
        grid=(nt,),
        in_specs=in_specs,
        out_specs=pl.BlockSpec((tm, d), lambda i: (i + tile_off, 0)),
        out_shape=jax.ShapeDtypeStruct((t if out_rows is None else out_rows, d), F32),
        input_output_aliases=aliases,
        compiler_params=_cparams("parallel"),
        name="moe_combine",
    )(*args)


POST_ROWS = 1024
POST_SUB = 256


def _row_tile(seq):
    return min(seq, 512)


def _moe_block_rows(t):
    return min(512, t * TOP_K // 8)


def kernel(x, c, fox_w_in, fox_b_f, fox_q_norm, fox_k_norm, fox_w_out, hgrn_w_in, hgrn_lb, hgrn_g_norm, hgrn_w_out, ada_w, ada_b, norm1_w, norm2_w, router_w, router_b, exp_w_gu, exp_b_gu, exp_w_down, exp_b_down):
    bsz, seq, d = x.shape
    t = bsz * seq
    depth = ada_w.shape[0]
    n_exp = router_w.shape[-1]
    f_dim = exp_w_down.shape[2]
    h_fox = fox_w_in.shape[-1] - 4 * d
    dh = d // h_fox
    tm = _row_tile(seq)
    n_split = 2 if bsz % 2 == 0 else 1
    bh, th = bsz // n_split, t // n_split
    rb = _moe_block_rows(th)
    n_blocks = (th * TOP_K) // rb + n_exp
    hgrn_hpb = 4 if d % (4 * LANES) == 0 else 2

    mod_all = _adaln_mod(c, ada_w, ada_b).reshape(depth, bsz, 1, -1)
    lb_soft = jax.nn.softmax(hgrn_lb.astype(F32), axis=0)
    lower = jnp.cumsum(lb_soft, axis=0) - lb_soft[0]
    experts = jnp.arange(n_exp, dtype=I32)

    xf = x.reshape(t, d)
    parts = [xf] * n_split
    for i in range(depth):
        j = i // 2
        nw1 = norm1_w[i].reshape(1, d)
        if i % 2 == 0:
            w_in = fox_w_in[j]
            w_main = w_in[:, :4 * d].astype(BF16)
            w_f = jnp.pad(w_in[:, 4 * d:], ((0, 0), (0, LANES - h_fox))).astype(BF16)
            b_f = jnp.pad(fox_b_f[j], (0, LANES - h_fox)).reshape(1, LANES)
            qw = jnp.tile(fox_q_norm[j], LANES // dh).reshape(1, LANES)
            kw = jnp.tile(fox_k_norm[j], LANES // dh).reshape(1, LANES)
            w_out = fox_w_out[j].astype(BF16)
        else:
            w_h = hgrn_w_in[j]
            w_main = jnp.concatenate([w_h[:, :d], w_h[:, 2 * d:]], axis=1).astype(BF16)
            w_f = w_h[:, d:2 * d].astype(BF16)
            b_f = jnp.zeros((1, d), F32)
            w_out = hgrn_w_out[j].astype(BF16)
        wr = jnp.pad(router_w[i], ((0, 0), (0, LANES - n_exp))).astype(BF16)
        br = jnp.pad(router_b[i], (0, LANES - n_exp), constant_values=MASK_VALUE).reshape(1, LANES)
        b_gu = exp_b_gu[i].reshape(n_exp, 1, f_dim, 2)

        def front(s, src):
            mod = mod_all[i, s * bh:(s + 1) * bh]
            off = s * th if i == 0 else 0
            moe = None
            xp = src
            if isinstance(src, tuple):
                xp, moe = src[0], src[1:]
            if i % 2 == 0:
                outs = _in_proj(xp, nw1, mod, w_main, BF16, seq, tm, w_f, b_f,
                                rows=th, row_off=off, moe=moe, name="in_proj_fox")
                o = _fox_attention(outs[0], outs[1], qw, kw, seq, d, dh)
            else:
                outs = _in_proj(xp, nw1, mod, w_main, BF16, seq, tm, w_f, b_f,
                                rows=th, row_off=off, moe=moe, name="in_proj_hgrn")
                o = _hgrn_mixer(outs[0], outs[1], lower[j].reshape(1, d),
                                hgrn_g_norm[j].reshape(1, HGRN_DIM), seq, d, hgrn_hpb)
            if moe is not None:
                xp = outs[-1]
            x1, hp, ri, rw, cnt = _post_mixer(o, xp, w_out, mod, norm2_w[i].reshape(1, d), wr, br,
                                              seq, min(seq, POST_ROWS), row_off=off)
            counts = cnt[0, :n_exp]
            padded = (counts + rb - 1) // rb * rb
            pad_end = jnp.cumsum(padded)
            pad_start = pad_end - padded
            top_idx, rank = ri[:TOP_K], ri[TOP_K:]
            start = jnp.sum(jnp.where(top_idx[None] == experts[:, None, None],
                                      pad_start[:, None, None], 0), axis=0)
            posf = (start + rank).reshape(-1).astype(I32)
            blk_lo = jnp.arange(n_blocks, dtype=I32) * rb
            blk_e = jnp.minimum(jnp.sum(pad_end[None, :] <= blk_lo[:, None], axis=1),
                                n_exp - 1).astype(I32)
            n_used = (pad_end[-1] // rb).astype(I32).reshape(1)
            later = jnp.where((experts[None, :] > experts[:, None]) & (padded[None, :] > 0),
                              experts[None, :], n_exp)
            nxt_e = jnp.min(later, axis=1)
            nxt_e = jnp.where(nxt_e >= n_exp, -1, nxt_e).astype(I32)
            return dict(x1=x1, hp=hp, rw=rw, posf=posf, blk_e=blk_e, n_used=n_used, nxt_e=nxt_e,
                        mod=mod)

        def experts_of(st, xs):
            return _expert_ffn(xs, st["blk_e"], st["n_used"], st["nxt_e"], exp_w_gu,
                               b_gu[..., 0], b_gu[..., 1], exp_w_down,
                               exp_b_down[i].reshape(n_exp, 1, d), rb, i)

        def deferred(st, yt):
            return (st["x1"], yt, st["rw"], st["mod"])

        if n_split == 1:
            st = front(0, parts[0])
            xs = _sc_scatter_rows(st["hp"], st["posf"], n_blocks * rb)
            parts[0] = deferred(st, _sc_gather_rows(experts_of(st, xs), st["posf"]))
            continue

        tie = lax.optimization_barrier
        sa = front(0, parts[0])
        xs_a = _sc_scatter_rows(sa["hp"], sa["posf"], n_blocks * rb)
        src_b = parts[1]
        if isinstance(src_b, tuple):
            yt_b, _ = tie((src_b[1], sa["hp"]))
            src_b = (src_b[0], yt_b) + src_b[2:]
        else:
            src_b, _ = tie((src_b, sa["hp"]))
        sb = front(1, src_b)
        xs_a, _ = tie((xs_a, sb["hp"]))
        xs_b = _sc_scatter_rows(sb["hp"], sb["posf"], n_blocks * rb)
        ys_a = experts_of(sa, xs_a)
        xs_b, _ = tie((xs_b, ys_a))
        yt_a = _sc_gather_rows(ys_a, sa["posf"])
        ys_b = experts_of(sb, xs_b)
        yt_a, _ = tie((yt_a, ys_b))
        yt_b = _sc_gather_rows(ys_b, sb["posf"])
        parts = [deferred(sa, yt_a), deferred(sb, yt_b)]

    out = None
    for s_i, (x1, yt, rw, mod) in enumerate(parts):
        if n_split == 1:
            out = _combine(x1, yt, rw, mod, seq, tm)
        else:
            out = _combine(x1, yt, rw, mod, seq, tm, out_rows=t, row_off=s_i * th, dst=out)
    return out.reshape(bsz, seq, d)
```

```python
import functools

import jax
import jax.numpy as jnp
import numpy as np
from jax import lax
from jax.experimental import pallas as pl
from jax.experimental.pallas import tpu as pltpu
from jax.experimental.pallas import tpu_sc as plsc

F32 = jnp.float32
BF16 = jnp.bfloat16
U32 = jnp.uint32
I32 = jnp.int32

EPS = 1e-6
MASK_VALUE = -1e30
TOP_K = 4
HGRN_DIM = 128
HGRN_CHUNK = 32
SWIGLU_LIMIT = 7.0
SWIGLU_ALPHA = 1.702
LANES = 128
SUBLANES = 8
SC_CORES = 2
SC_SUBCORES = 16
SC_ROWS = 64
VMEM_LIMIT = 56 * 1024 * 1024
HIGHEST = lax.Precision.HIGHEST


def _cparams(*sem):
    return pltpu.CompilerParams(dimension_semantics=sem, vmem_limit_bytes=VMEM_LIMIT)


def _resident(shape, index_map):
    return pl.BlockSpec(shape, index_map, pipeline_mode=pl.Buffered(1))


def _sigmoid(x):
    return 1.0 / (1.0 + jnp.exp(-x))


def _bf16_bits(x):
    return lax.bitcast_convert_type(x.astype(BF16).astype(F32), U32)


def _pack_halves(x):
    n = x.shape[1] // 2
    lo = _bf16_bits(x[:, :n])
    hi = _bf16_bits(x[:, n:])
    return (hi & jnp.uint32(0xFFFF0000)) | (lo >> 16)


def _unpack_halves(w):
    lo = lax.bitcast_convert_type(w << 16, F32)
    hi = lax.bitcast_convert_type(w & jnp.uint32(0xFFFF0000), F32)
    return lo, hi


def _mod_kernel(c_ref, w_ref, b_ref, o_ref):
    c = c_ref[...]
    ca = c * _sigmoid(c)
    o_ref[0] = jnp.dot(ca, w_ref[0], preferred_element_type=F32, precision=HIGHEST) + b_ref[0]


def _adaln_mod(c, ada_w, ada_b):
    depth, d, n6 = ada_w.shape
    b = c.shape[0]
    nmod = n6 // d
    return pl.pallas_call(
        _mod_kernel,
        grid=(depth, nmod),
        in_specs=[
            pl.BlockSpec((b, d), lambda i, j: (0, 0)),
            pl.BlockSpec((1, d, d), lambda i, j: (i, 0, j)),
            pl.BlockSpec((1, 1, d), lambda i, j: (i, 0, j)),
        ],
        out_specs=pl.BlockSpec((1, b, d), lambda i, j: (i, 0, j)),
        out_shape=jax.ShapeDtypeStruct((depth, b, n6), F32),
        compiler_params=_cparams("parallel", "parallel"),
        name="adaln_mod",
    )(c, ada_w, ada_b.reshape(depth, 1, n6))


def _norm_mod(x, nw, sc, sh):
    ms = jnp.mean(x * x, axis=-1, keepdims=True)
    return x * lax.rsqrt(ms + EPS) * nw * (1.0 + sc) + sh


def _moe_mix(x_ref, y_refs, rw_ref, g2_ref, rows=slice(None)):
    rw = rw_ref[rows, :]
    lane = lax.broadcasted_iota(I32, rw.shape, 1)
    half = y_refs[0].shape[1]
    acc_lo = acc_hi = None
    for k, y_ref in enumerate(y_refs):
        wk = jnp.sum(jnp.where(lane == k, rw, 0.0), axis=-1, keepdims=True)
        lo, hi = _unpack_halves(y_ref[rows, :])
        acc_lo = wk * lo if acc_lo is None else acc_lo + wk * lo
        acc_hi = wk * hi if acc_hi is None else acc_hi + wk * hi
    g2 = g2_ref[...]
    return (x_ref[rows, :half] + g2[:, :half] * acc_lo, x_ref[rows, half:] + g2[:, half:] * acc_hi)


def _in_kernel(*refs, n_chunk, with_f, with_moe, sub):
    n_in = 1 + (TOP_K + 2 if with_moe else 0) + 4 + (2 if with_f else 0)
    ins, outs = refs[:n_in], refs[n_in:]
    x_ref = ins[0]
    pos = 1 + (TOP_K + 2 if with_moe else 0)
    nw_ref, sh_ref, sc_ref, w_ref = ins[pos:pos + 4]
    o_ref = outs[0]
    if with_f:
        wf_ref, bf_ref = ins[pos + 4:pos + 6]
        of_ref = outs[1]
    n = o_ref.shape[1]
    for r0 in range(0, x_ref.shape[0], sub):
        rows = slice(r0, r0 + sub)
        if with_moe:
            lo, hi = _moe_mix(x_ref, ins[1:1 + TOP_K], ins[1 + TOP_K], ins[2 + TOP_K], rows)
            x = jnp.concatenate([lo, hi], axis=1)
            outs[-1][rows, :] = x
        else:
            x = x_ref[rows, :]
        h = _norm_mod(x, nw_ref[...], sc_ref[...], sh_ref[...]).astype(BF16)
        for c0 in range(0, n, n_chunk):
            o_ref[rows, c0:c0 + n_chunk] = jnp.dot(
                h, w_ref[:, c0:c0 + n_chunk], preferred_element_type=F32).astype(o_ref.dtype)
        if with_f:
            of_ref[rows, :] = jnp.dot(h, wf_ref[...], preferred_element_type=F32) + bf_ref[...]


def _in_proj(x, nw, mod, w, out_dtype, seq, tm, wf=None, bf=None, rows=None, row_off=0, moe=None,
             name="in_proj"):
    d = x.shape[1]
    t = x.shape[0] if rows is None else rows
    n = w.shape[1]
    per_seq = seq // tm
    tile_off = row_off // tm
    nt = t // tm
    with_f = wf is not None
    mod_spec = lambda j: pl.BlockSpec((None, 1, d), lambda i: (i // per_seq, 0, j))
    in_specs = [pl.BlockSpec((tm, d), lambda i: (i + tile_off, 0))]
    args = [x]
    if moe is not None:
        yt, rw, mod_prev = moe
        in_specs += [pl.BlockSpec((tm, d // 2), lambda i, k=k: (k * nt + i, 0)) for k in range(TOP_K)]
        in_specs += [pl.BlockSpec((tm, LANES), lambda i: (i, 0)), mod_spec(5)]
        args += [yt] * TOP_K + [rw, mod_prev]
    in_specs += [_resident((1, d), lambda i: (0, 0)), mod_spec(0), mod_spec(1),
                 _resident((d, n), lambda i: (0, 0))]
    args += [nw, mod, mod, w]
    out_specs = [pl.BlockSpec((tm, n), lambda i: (i, 0))]
    out_shape = [jax.ShapeDtypeStruct((t, n), out_dtype)]
    if with_f:
        nf = wf.shape[1]
        in_specs += [_resident((d, nf), lambda i: (0, 0)), _resident((1, nf), lambda i: (0, 0))]
        out_specs.append(pl.BlockSpec((tm, nf), lambda i: (i, 0)))
        out_shape.append(jax.ShapeDtypeStruct((t, nf), F32))
        args += [wf, bf]
    if moe is not None:
        out_specs.append(pl.BlockSpec((tm, d), lambda i: (i, 0)))
        out_shape.append(jax.ShapeDtypeStruct((t, d), F32))
    return pl.pallas_call(
        functools.partial(_in_kernel, n_chunk=min(n, 1024), with_f=with_f, with_moe=moe is not None,
                          sub=min(tm, 256)),
        grid=(t // tm,),
        in_specs=in_specs,
        out_specs=out_specs,
        out_shape=out_shape,
        compiler_params=_cparams("parallel"),
        name=name,
    )(*args)


LOG2E = 1.4426950408889634
EXP2_SPAN = 120.0
SCORE_MARGIN = 1.01
BIAS_LANES = 16


def _split3(c):
    c1 = c.astype(BF16).astype(F32)
    r = c - c1
    c2 = r.astype(BF16).astype(F32)
    return c1, c2, r - c2


def _bias_layout(n_pairs):
    route =np.zeros((3, 3 * LANES, LANES), np.float32)
    const = np.zeros((3, 1, LANES), np.float32)
    for p in range(n_pairs):
        base = p * BIAS_LANES
        for i in range(3):
            const[0, 0, base + i] = 1.0
            for h in range(2):
                route[0, i * LANES + 2 * p + h, base + 3 + 3 * h + i] = -1.0
                route[1 + h, i * LANES + 2 * p + h, base + i] = 1.0
                const[1 + h, 0, base + 3 + 3 * h + i] = 1.0
    return jnp.asarray(route, BF16), jnp.asarray(const, F32)


def _cum_kernel(f_ref, ref_ref, route_ref, const_ref, o_ref, *, rows):
    s = f_ref.shape[0]
    r = lax.broadcasted_iota(I32, (rows, rows), 0)
    c = lax.broadcasted_iota(I32, (rows, rows), 1)
    tri = jnp.where(r >= c, 1.0, 0.0).astype(BF16)
    carry = jnp.zeros((1, LANES), F32)
    for r0 in range(0, s, rows):
        x = f_ref[r0:r0 + rows, :]
        lf = jnp.minimum(x, 0.0) - jnp.log(1.0 + jnp.exp(-jnp.abs(x)))
        cs = carry + sum(jnp.dot(tri, part.astype(BF16), preferred_element_type=F32)
                         for part in _split3(lf))
        carry = cs[rows - 1:rows, :]
        c2 = cs * LOG2E
        k_terms = jnp.concatenate(_split3(c2), axis=1).astype(BF16)
        q_terms = jnp.concatenate(_split3(c2 - ref_ref[...]), axis=1).astype(BF16)
        for j, terms in enumerate((k_terms, q_terms, q_terms)):
            routed = jnp.dot(terms, route_ref[j], preferred_element_type=F32) + const_ref[j]
            o_ref[r0:r0 + rows, j * LANES:(j + 1) * LANES] = routed.astype(o_ref.dtype)


def _fox_bias(flogit, ref, seq, n_pairs):
    t = flogit.shape[0]
    route, const = _bias_layout(n_pairs)
    return pl.pallas_call(
        functools.partial(_cum_kernel, rows=min(seq, 256)),
        grid=(t // seq,),
        in_specs=[pl.BlockSpec((seq, LANES), lambda b: (b, 0)),
                  pl.BlockSpec((1, LANES), lambda b: (0, 0)),
                  pl.BlockSpec((3, 3 * LANES, LANES), lambda b: (0, 0, 0)),
                  pl.BlockSpec((3, 1, LANES), lambda b: (0, 0, 0))],
        out_specs=pl.BlockSpec((seq, 3 * LANES), lambda b: (b, 0)),
        out_shape=jax.ShapeDtypeStruct((t, 3 * LANES), BF16),
        compiler_params=_cparams("parallel"),
        name="fox_bias",
    )(flogit, ref, route, const)


def _attn_kernel(q_ref, k_ref, v_ref, g_ref, bias_ref, qw_ref, kw_ref, o_ref, qs_s, ka_s, vt_s, *,
                 dh, tq, bounded):
    s_len = q_ref.shape[0]
    nq = s_len // tq
    hp = pl.program_id(1)
    lane = lax.broadcasted_iota(I32, (1, LANES), 1)
    first = lane < dh

    def headnorm(z, w):
        z2 = z * z
        s0 = jnp.sum(jnp.where(first, z2, 0.0), axis=-1, keepdims=True)
        s1 = jnp.sum(jnp.where(first, 0.0, z2), axis=-1, keepdims=True)
        ms = jnp.where(first, s0, s1) * (1.0 / dh)
        return z * lax.rsqrt(ms + EPS) * w

    qn = headnorm(q_ref[...].astype(F32), qw_ref[...]) * (dh ** -0.5 * LOG2E)
    kn = headnorm(k_ref[...].astype(F32), kw_ref[...])
    mine = (lane // BIAS_LANES) == hp
    ka_s[:, :LANES] = kn.astype(BF16)
    ka_s[:, LANES:] = bias_ref[:, :LANES]
    q_lo = [jnp.where(first, qn, 0.0), jnp.where(first, 0.0, qn)]
    hq = tq // 2
    for qi in range(nq):
        for part_i in range(2):
            rows = slice(qi * tq + part_i * hq, qi * tq + (part_i + 1) * hq)
            for h in range(2):
                dst = slice((2 * part_i + h) * hq, (2 * part_i + h + 1) * hq)
                q_hi = bias_ref[rows, (1 + h) * LANES:(2 + h) * LANES].astype(F32)
                qs_s[qi, dst, :LANES] = q_lo[h][rows, :].astype(BF16)
                qs_s[qi, dst, LANES:] = jnp.where(mine, q_hi, 0.0).astype(BF16)

    nt = (((1,), (1,)), ((), ()))

    def causal(s, first_token):
        rr = lax.broadcasted_iota(I32, s.shape, 0)
        rr = jnp.where(rr >= hq, rr - hq, rr) + first_token
        cc = lax.broadcasted_iota(I32, s.shape, 1)
        return jnp.where(rr >= cc, s, MASK_VALUE)

    def finish(qi, halves):
        for part_i, (acc, l) in enumerate(halves):
            o2 = acc / l
            o = jnp.where(first, o2[:hq], o2[hq:])
            rows = slice(qi * tq + part_i * hq, qi * tq + (part_i + 1) * hq)
            g = g_ref[rows, :].astype(F32)
            o_ref[rows, :] = (o * _sigmoid(g)).astype(o_ref.dtype)

    if bounded:
        vt_s[...] = jnp.transpose(v_ref[...].astype(F32)).astype(BF16)
        dv_first = lax.broadcasted_iota(I32, (LANES, 1), 0) < dh

        def causal_t(s_t, first_token):
            kk = lax.broadcasted_iota(I32, s_t.shape, 0)
            qq = lax.broadcasted_iota(I32, s_t.shape, 1)
            qq = jnp.where(qq >= hq, qq - hq, qq) + first_token
            return jnp.where(qq >= kk, s_t, MASK_VALUE)

        for qi in range(nq):
            acc, part = [None, None], [None, None]

            def add(i, pv_t, sums):
                acc[i] = pv_t if acc[i] is None else acc[i] + pv_t
                part[i] = sums if part[i] is None else part[i] + sums

            k0 = qi * tq
            tiles = [(slice(t * tq, (t + 1) * tq), slice(0, 2 * tq), None) for t in range(qi)]
            tiles += [(slice(k0, k0 + hq), slice(0, tq), 0), (slice(k0, k0 + tq), slice(tq, 2 * tq), 1)]

            def scores_t(tile):
                keys, q_rows, _ = tile
                return lax.dot_general(ka_s[keys, :], qs_s[qi, q_rows, :], nt,
                                       preferred_element_type=F32)

            s_next = scores_t(tiles[0])
            for n, (keys, _, half) in enumerate(tiles):
                s_t = s_next
                if n + 1 < len(tiles):
                    s_next = scores_t(tiles[n + 1])
                e_t = jnp.exp2(s_t if half is None else causal_t(s_t, half * hq))
                pv_t = jnp.dot(vt_s[:, keys], e_t.astype(BF16), preferred_element_type=F32)
                sums = jnp.sum(e_t, axis=0, keepdims=True)
                if half is None:
                    add(0, pv_t[:, :tq], sums[:, :tq])
                    add(1, pv_t[:, tq:], sums[:, tq:])
                else:
                    add(half, pv_t, sums)
            for i in range(2):
                o_t = acc[i] / part[i]
                o = jnp.transpose(jnp.where(dv_first, o_t[:, :hq], o_t[:, hq:]))
                rows = slice(qi * tq + i * hq, qi * tq + (i + 1) * hq)
                g = g_ref[rows, :].astype(F32)
                o_ref[rows, :] = (o * _sigmoid(g)).astype(o_ref.dtype)
        return

    def causal_tile(s):
        return jnp.concatenate([causal(s[:tq], 0), causal(s[tq:], hq)], axis=0)

    def scores(q, off):
        return lax.dot_general(q, ka_s[pl.ds(off, tq), :], nt, preferred_element_type=F32)

    def update(s, off, carry, masked):
        m, l, acc = carry
        if masked:
            s = causal_tile(s)
        m_new = jnp.maximum(m, jnp.max(s, axis=-1, keepdims=True))
        p = jnp.exp2(s - m_new)
        alpha = jnp.exp2(m - m_new)
        l = alpha * l + jnp.sum(p, axis=-1, keepdims=True)
        acc = alpha * acc + jnp.dot(p.astype(BF16), v_ref[pl.ds(off, tq), :],
                                    preferred_element_type=F32)
        return m_new, l, acc

    for qi in range(nq):
        q = qs_s[qi]
        s = scores(q, 0)
        carry = (jnp.full((2 * tq, 1), MASK_VALUE, F32), jnp.zeros((2 * tq, 1), F32),
                 jnp.zeros((2 * tq, LANES), F32))
        if qi > 0:
            def body(j, c, q=q):
                off = pl.multiple_of(j * tq, tq)
                s_next = scores(q, off + tq)
                return (s_next,) + update(c[0], off, c[1:], False)

            s, *carry = lax.fori_loop(0, qi, body, (s,) + carry)
        m, l, acc = update(s, qi * tq, carry, True)
        finish(qi, [(acc[:tq], l[:tq]), (acc[tq:], l[tq:])])


def _fox_attention(qkvg, flogit, qw, kw, seq, d, dh):
    t = qkvg.shape[0]
    nb = d // LANES
    assert 2 * dh == LANES and nb * BIAS_LANES <= LANES
    tq = min(seq, 512)
    col = lambda off: pl.BlockSpec((seq, LANES), lambda b, p: (b, off + p))

    def call(bounded, bias):
        return pl.pallas_call(
            functools.partial(_attn_kernel, dh=dh, tq=tq, bounded=bounded),
            grid=(t // seq, nb),
            in_specs=[
                col(0), col(nb), col(2 * nb), col(3 * nb),
                pl.BlockSpec((seq, 3 * LANES), lambda b, p: (b, 0)),
                pl.BlockSpec((1, LANES), lambda b, p: (0, 0)),
                pl.BlockSpec((1, LANES), lambda b, p: (0, 0)),
            ],
            out_specs=pl.BlockSpec((seq, LANES), lambda b, p: (b, p)),
            out_shape=jax.ShapeDtypeStruct((t, d), BF16),
            scratch_shapes=[pltpu.VMEM((seq // tq, 2 * tq, 2 * LANES), BF16),
                            pltpu.VMEM((seq, 2 * LANES), BF16),
                            pltpu.VMEM((LANES, seq), BF16)],
            compiler_params=_cparams("parallel", "parallel"),
            name="fox_attention" if bounded else "fox_attention_running_max",
        )(qkvg, qkvg, qkvg, qkvg, bias, qw, kw)

    bound = (dh ** 0.5 * LOG2E * SCORE_MARGIN) * jnp.max(jnp.abs(qw)) * jnp.max(jnp.abs(kw)) + 0.1
    fits = bound <= EXP2_SPAN - 4.0
    ref = jnp.where(fits, bound - jnp.maximum(0.0, 2.0 * bound - EXP2_SPAN), 0.0)
    bias = _fox_bias(flogit, jnp.full((1, LANES), ref, F32), seq, nb)
    return lax.cond(fits, lambda: call(True, bias), lambda: call(False, bias))


HGRN_PREP_ROWS = 256
HGRN_LOG2_SPAN = 100.0


def _hgrn_kernel(q_ref, f_ref, v_ref, g_ref, lb_ref, gw_ref, o_ref, qe_s, ke_s, kd_s, dec_s, *,
                 hpb, chunk):
    s_len = q_ref.shape[0]
    n_chunks = s_len // chunk
    nt = (((1,), (1,)), ((), ()))
    gw = gw_ref[...]

    pr = min(HGRN_PREP_ROWS, s_len)
    rr = lax.broadcasted_iota(I32, (pr, pr), 0)
    cc = lax.broadcasted_iota(I32, (pr, pr), 1)
    same = (rr // chunk) == (cc // chunk)
    cum_m = jnp.where(same & (rr >= cc), 1.0, 0.0).astype(BF16)
    tot_m = jnp.where(same, 1.0, 0.0).astype(BF16)
    lb_all = lb_ref[...]
    lowest = jnp.zeros((1, lb_all.shape[1]), F32)
    for r0 in range(0, s_len, pr):
        sig = _sigmoid(f_ref[r0:r0 + pr, :])
        terms = [t.astype(BF16) for t in _split3(jnp.log2(lb_all + (1.0 - lb_all) * sig))]
        b = sum(jnp.dot(cum_m, t, preferred_element_type=F32) for t in terms)
        tot = sum(jnp.dot(tot_m, t, preferred_element_type=F32) for t in terms)
        k = (1.0 - lb_all) * (1.0 - sig)
        q_raw = q_ref[r0:r0 + pr, :].astype(F32)
        qe_s[r0:r0 + pr, :] = (q_raw * _sigmoid(q_raw) * jnp.exp2(b)).astype(BF16)
        ke_s[r0:r0 + pr, :] = (k * jnp.exp2(-b)).astype(BF16)
        kd_s[r0:r0 + pr, :] = (k * jnp.exp2(tot - b)).astype(BF16)
        for c0 in range(0, pr, chunk):
            ci = (r0 + c0) // chunk
            dec_s[ci] = jnp.exp2(tot[c0:c0 + SUBLANES, :])
        lowest = jnp.minimum(lowest, jnp.min(tot, axis=0, keepdims=True))
    mild = jnp.min(lowest) >= -HGRN_LOG2_SPAN

    @pl.when(mild)
    def _():
        r32 = lax.broadcasted_iota(I32, (chunk, chunk), 0)
        c32 = lax.broadcasted_iota(I32, (chunk, chunk), 1)
        causal = r32 >= c32
        per_step = 8 if n_chunks % 8 == 0 else 2

        def fast_body(step, states):
            insts = [(ch, hd) for ch in range(per_step) for hd in range(hpb)]
            rows = {ch: pl.ds(pl.multiple_of((step * per_step + ch) * chunk, chunk), chunk)
                    for ch in range(per_step)}
            cols = {hd: slice(hd * LANES, (hd + 1) * LANES) for hd in range(hpb)}
            states = list(states)
            qe = {i: qe_s[rows[i[0]], cols[i[1]]] for i in insts}
            v = {i: v_ref[rows[i[0]], cols[i[1]]].astype(F32) for i in insts}
            a = {i: lax.dot_general(qe[i], ke_s[rows[i[0]], cols[i[1]]], nt,
                                    preferred_element_type=F32) for i in insts}
            upd = {i: jnp.dot(jnp.transpose(v[i]).astype(BF16), kd_s[rows[i[0]], cols[i[1]]],
                              preferred_element_type=F32) for i in insts}
            o = {}
            for ch in range(per_step):
                for hd in range(hpb):
                    i = (ch, hd)
                    o[i] = (jnp.dot(jnp.where(causal, a[i], 0.0).astype(BF16), v[i].astype(BF16),
                                    preferred_element_type=F32)
                            + lax.dot_general(qe[i], states[hd].astype(BF16), nt,
                                              preferred_element_type=F32))
                    dec = dec_s[step * per_step + ch, 0:1, cols[hd]]
                    states[hd] = dec * states[hd] + upd[i]
            for i in insts:
                g = g_ref[rows[i[0]], cols[i[1]]].astype(F32)
                ms = jnp.mean(o[i] * o[i], axis=-1, keepdims=True)
                y = o[i] * lax.rsqrt(ms + EPS) * gw * (g * _sigmoid(g))
                o_ref[rows[i[0]], cols[i[1]]] = y.astype(o_ref.dtype)
            return tuple(states)

        init = tuple(jnp.zeros((LANES, LANES), F32) for _ in range(hpb))
        lax.fori_loop(0, n_chunks // per_step, fast_body, init)

    pair = 2
    groups = chunk // SUBLANES
    off_w = (groups - 1) * chunk
    r = lax.broadcasted_iota(I32, (chunk, chunk), 0)
    c = lax.broadcasted_iota(I32, (chunk, chunk), 1)
    tri = jnp.where(r >= c, 1.0, 0.0).astype(BF16)
    ro = lax.broadcasted_iota(I32, (chunk, off_w), 0) // SUBLANES
    co = lax.broadcasted_iota(I32, (chunk, off_w), 1) // chunk
    keep = ro == co + 1
    sub = lax.broadcasted_iota(I32, (SUBLANES, LANES), 0)
    gw = gw_ref[...]
    nt = (((1,), (1,)), ((), ()))
    grp = lambda a, i: a[i * SUBLANES:(i + 1) * SUBLANES, :]
    zeros8 = jnp.zeros((SUBLANES, LANES), F32)

    def body(step, states):
        insts = [(ch, hd) for ch in range(pair) for hd in range(hpb)]
        rows = {ch: pl.ds(pl.multiple_of((step * pair + ch) * chunk, chunk), chunk)
                for ch in range(pair)}
        cols = {hd: slice(hd * LANES, (hd + 1) * LANES) for hd in range(hpb)}
        st = {}
        for ch, hd in insts:
            lb = lb_ref[:, cols[hd]]
            q_raw = q_ref[rows[ch], cols[hd]].astype(F32)
            sig = _sigmoid(f_ref[rows[ch], cols[hd]])
            d = dict(v=v_ref[rows[ch], cols[hd]].astype(F32), q=q_raw * _sigmoid(q_raw),
                     k=(1.0 - lb) * (1.0 - sig))
            lf = jnp.log(lb + (1.0 - lb) * sig)
            d["b"] = sum(jnp.dot(tri, part.astype(BF16), preferred_element_type=F32)
                         for part in _split3(lf))
            st[ch, hd] = d
        def from_state(ch, hd, state_t):
            d = st[ch, hd]
            qe = (d["q"] * jnp.exp(d["b"])).astype(BF16)
            d["o"] = lax.dot_general(qe, state_t.astype(BF16), nt, preferred_element_type=F32)
        for hd in range(hpb):
            from_state(0, hd, states[hd])
        for ch, hd in insts:
            d = st[ch, hd]
            b, q, k = d["b"], d["q"], d["k"]
            ends = [b[(i + 1) * SUBLANES - 1:(i + 1) * SUBLANES, :] for i in range(groups)]
            kt = [grp(k, i) * jnp.exp(ends[i] - grp(b, i)) for i in range(groups)]
            qh = jnp.concatenate(
                [zeros8] + [grp(q, i) * jnp.exp(grp(b, i) - ends[i - 1]) for i in range(1, groups)],
                axis=0).astype(BF16)
            kh = jnp.concatenate(
                [kt[j] * jnp.exp(ends[i - 1] - ends[j]) if j < i else zeros8
                 for i in range(1, groups) for j in range(groups)], axis=0).astype(BF16)
            d["a"] = lax.dot_general(qh, kh, nt, preferred_element_type=F32)
            d["ends"], d["kt"] = ends, kt
        def state_update(ch, hd, state_t):
            d = st[ch, hd]
            b_end = d["ends"][-1]
            ke = jnp.concatenate([d["kt"][j] * jnp.exp(b_end - d["ends"][j]) for j in range(groups)],
                                 axis=0).astype(BF16)
            upd = jnp.dot(jnp.transpose(d["v"]).astype(BF16), ke, preferred_element_type=F32)
            return jnp.exp(b_end) * state_t + upd
        states = [state_update(0, hd, states[hd]) for hd in range(hpb)]
        for ch, hd in insts:
            d = st[ch, hd]
            vb = d["v"].astype(BF16)
            v3 = jnp.concatenate([vb] * (groups - 1), axis=0)
            d["off"] = jnp.dot(jnp.where(keep, d["a"], 0.0).astype(BF16), v3,
                               preferred_element_type=F32)
        for hd in range(hpb):
            from_state(1, hd, states[hd])
        states = [state_update(1, hd, states[hd]) for hd in range(hpb)]
        for ch, hd in insts:
            d = st[ch, hd]
            o = d["o"] + d["off"]
            o_g = []
            for gi in range(groups):
                qg, kg, vg, bg = grp(d["q"], gi), grp(d["k"], gi), grp(d["v"], gi), grp(d["b"], gi)
                og = grp(o, gi)
                for s in range(SUBLANES):
                    live = sub >= s
                    e = jnp.exp(jnp.where(live, bg - bg[s:s + 1, :], 0.0))
                    p = jnp.where(live, qg * e * kg[s:s + 1, :], 0.0)
                    og = og + jnp.sum(p, axis=-1, keepdims=True) * vg[s:s + 1, :]
                o_g.append(og)
            o = jnp.concatenate(o_g, axis=0)
            g = g_ref[rows[ch], cols[hd]].astype(F32)
            ms = jnp.mean(o * o, axis=-1, keepdims=True)
            y = o * lax.rsqrt(ms + EPS) * gw * (g * _sigmoid(g))
            o_ref[rows[ch], cols[hd]] = y.astype(o_ref.dtype)
        return tuple(states)

    @pl.when(jnp.logical_not(mild))
    def _():
        init = tuple(jnp.zeros((LANES, LANES), F32) for _ in range(hpb))
        lax.fori_loop(0, s_len // (chunk * pair), body, init)


def _hgrn_mixer(qvg, fgate, lower, gw, seq, d, hpb):
    t = qvg.shape[0]
    w = hpb * LANES
    nb = d // w
    col = lambda off: pl.BlockSpec((seq, w), lambda b, p: (b, off + p))
    return pl.pallas_call(
        functools.partial(_hgrn_kernel, hpb=hpb, chunk=HGRN_CHUNK),
        grid=(t // seq, nb),
        in_specs=[
            col(0), col(0), col(nb), col(2 * nb),
            pl.BlockSpec((1, w), lambda b, p: (0, p)),
            pl.BlockSpec((1, LANES), lambda b, p: (0, 0)),
        ],
        out_specs=pl.BlockSpec((seq, w), lambda b, p: (b, p)),
        out_shape=jax.ShapeDtypeStruct((t, d), BF16),
        scratch_shapes=[pltpu.VMEM((seq, w), BF16)] * 3
        + [pltpu.VMEM((seq // HGRN_CHUNK, SUBLANES, w), F32)],
        compiler_params=_cparams("parallel", "parallel"),
        name="hgrn_mixer",
    )(qvg, fgate, qvg, qvg, lower, gw)


def _post_kernel(o_ref, x_ref, wo_ref, g1_ref, nw_ref, sh_ref, sc_ref, wr_ref, br_ref,
                 x1_ref, hp_ref, ri_ref, rw_ref, cnt_ref, run_s):
    i = pl.program_id(0)
    tm = x_ref.shape[0]
    sub = min(tm, POST_SUB)
    subs = [slice(r0, r0 + sub) for r0 in range(0, tm, sub)]

    @pl.when(i == 0)
    def _():
        run_s[...] = jnp.zeros_like(run_s)

    lane = lax.broadcasted_iota(I32, (sub, LANES), 1).astype(F32)
    r = lax.broadcasted_iota(I32, (sub, sub), 0)
    c = lax.broadcasted_iota(I32, (sub, sub), 1)
    strict = jnp.where(r > c, 1.0, 0.0).astype(BF16)

    y = [jnp.dot(o_ref[rows, :], wo_ref[...], preferred_element_type=F32) for rows in subs]
    logits = []
    for rows, y_s in zip(subs, y):
        x1 = x_ref[rows, :] + g1_ref[...] * y_s
        x1_ref[rows, :] = x1
        h2 = _norm_mod(x1, nw_ref[...], sc_ref[...], sh_ref[...])
        hp_ref[rows, :] = _pack_halves(h2)
        logits.append(jnp.dot(h2.astype(BF16), wr_ref[...], preferred_element_type=F32) + br_ref[...])

    picks = []
    for work in logits:
        idx, val = [], []
        for _ in range(TOP_K):
            mx = jnp.max(work, axis=-1, keepdims=True)
            ix = jnp.min(jnp.where(work == mx, lane, float(LANES)), axis=-1, keepdims=True)
            idx.append(ix)
            val.append(mx)
            work = jnp.where(lane == ix, -jnp.inf, work)
        ex = [jnp.exp(v - val[0]) for v in val]
        den = ex[0] + ex[1] + ex[2] + ex[3]
        hot = jnp.zeros((sub, LANES), F32)
        for ix in idx:
            hot = hot + jnp.where(lane == ix, 1.0, 0.0)
        picks.append((idx, [e / den for e in ex], hot))

    inside = [jnp.dot(strict, hot.astype(BF16), preferred_element_type=F32) for _, _, hot in picks]
    run = run_s[...]
    for rows, (idx, wts, hot), before in zip(subs, picks, inside):
        before = before + run
        run = run + jnp.sum(hot, axis=0, keepdims=True)
        ranks = [jnp.sum(jnp.where(lane == ix, before, 0.0), axis=-1, keepdims=True) for ix in idx]
        ri = jnp.zeros((sub, LANES), F32)
        rw = jnp.zeros((sub, LANES), F32)
        for k in range(TOP_K):
            ri = jnp.where(lane == float(k), idx[k], ri)
            ri = jnp.where(lane == float(TOP_K + k), ranks[k], ri)
            rw = jnp.where(lane == float(k), wts[k], rw)
        ri_ref[:, rows] = jnp.transpose(ri)[:2 * TOP_K, :].astype(I32)
        rw_ref[rows, :] = rw
    run_s[...] = run
    cnt_ref[...] = run.astype(I32)


def _post_mixer(o, x, wo, mod, nw2, wr, br, seq, tm, row_off=0):
    t, d = o.shape
    per_seq = seq // tm
    tile_off = row_off // tm
    mod_spec = lambda j: pl.BlockSpec((None, 1, d), lambda i: (i // per_seq, 0, j))
    row = lambda w: pl.BlockSpec((tm, w), lambda i: (i, 0))
    return pl.pallas_call(
        _post_kernel,
        grid=(t // tm,),
        in_specs=[
            row(d), pl.BlockSpec((tm, d), lambda i: (i + tile_off, 0)),
            _resident((d, d), lambda i: (0, 0)),
            mod_spec(2),
            _resident((1, d), lambda i: (0, 0)),
            mod_spec(3), mod_spec(4),
            _resident((d, LANES), lambda i: (0, 0)),
            _resident((1, LANES), lambda i: (0, 0)),
        ],
        out_specs=[row(d), row(d // 2), pl.BlockSpec((2 * TOP_K, tm), lambda i: (0, i)), row(LANES),
                   pl.BlockSpec((1, LANES), lambda i: (0, 0))],
        out_shape=[
            jax.ShapeDtypeStruct((t, d), F32),
            jax.ShapeDtypeStruct((t, d // 2), U32),
            jax.ShapeDtypeStruct((2 * TOP_K, t), I32),
            jax.ShapeDtypeStruct((t, LANES), F32),
            jax.ShapeDtypeStruct((1, LANES), I32),
        ],
        scratch_shapes=[pltpu.VMEM((1, LANES), F32)],
        compiler_params=_cparams("arbitrary"),
        name="post_mixer",
    )(o, x, wo, mod, nw2, mod, mod, wr, br)


def _sc_mesh():
    return plsc.VectorSubcoreMesh(core_axis_name="c", subcore_axis_name="s")


def _sc_scatter_rows(x, posf, n_rows):
    t, w = x.shape
    workers = SC_CORES * SC_SUBCORES
    per_w = t // workers

    @functools.partial(
        pl.kernel, mesh=_sc_mesh(),
        out_type=jax.ShapeDtypeStruct((n_rows, w), x.dtype),
        scratch_types=[pltpu.VMEM((SC_ROWS,), I32)] * TOP_K
        + [pltpu.VMEM((SC_ROWS, w), x.dtype), pltpu.SemaphoreType.DMA])
    def scatter_kernel(x_hbm, pos_hbm, out_hbm, i0, i1, i2, i3, rows_v, sem):
        idxs = (i0, i1, i2, i3)
        wid = lax.axis_index("s") * SC_CORES + lax.axis_index("c")
        base = wid * per_w

        @pl.loop(0, per_w // SC_ROWS)
        def _(i):
            off = base + i * SC_ROWS
            for k in range(TOP_K):
                pltpu.sync_copy(pos_hbm.at[pl.ds(k * t + off, SC_ROWS)], idxs[k])
            pltpu.sync_copy(x_hbm.at[pl.ds(off, SC_ROWS)], rows_v)
            copies = [pltpu.async_copy(rows_v, out_hbm.at[idxs[k]], sem) for k in range(TOP_K)]
            for cp in copies:
                cp.wait()

    return scatter_kernel(x, posf)


def _sc_gather_rows(src, posf):
    n = posf.shape[0]
    w = src.shape[1]
    workers = SC_CORES * SC_SUBCORES
    per_w = n // workers

    @functools.partial(
        pl.kernel, mesh=_sc_mesh(),
        out_type=jax.ShapeDtypeStruct((n, w), src.dtype),
        scratch_types=[pltpu.VMEM((SC_ROWS,), I32), pltpu.VMEM((SC_ROWS, w), src.dtype),
                       pltpu.SemaphoreType.DMA])
    def gather_kernel(src_hbm, pos_hbm, out_hbm, idx_v, rows_v, sem):
        wid = lax.axis_index("s") * SC_CORES + lax.axis_index("c")
        base = wid * per_w

        @pl.loop(0, per_w // SC_ROWS)
        def _(i):
            off = base + i * SC_ROWS
            pltpu.sync_copy(pos_hbm.at[pl.ds(off, SC_ROWS)], idx_v)
            pltpu.async_copy(src_hbm.at[idx_v], rows_v, sem).wait()
            pltpu.sync_copy(rows_v, out_hbm.at[pl.ds(off, SC_ROWS)])

    return gather_kernel(src, posf)


GU_BLOCK = 2 * LANES
FF_CHUNK = 2 * LANES


def _deinterleave_matrix():
    sel = np.zeros((GU_BLOCK, GU_BLOCK), np.float32)
    j = np.arange(LANES)
    sel[2 * j, j] = 1.0
    sel[2 * j + 1, LANES + j] = 1.0
    return jnp.asarray(sel, BF16)


def _expert_kernel(be_ref, nu_ref, nxt_ref, x_ref, wgu_hbm, bg_ref, bu_ref, wd_hbm, bd_ref, sel_ref,
                   y_ref, wgu_f, wd_f, wgu_s, wd_s, sems, *, layer):
    i = pl.program_id(0)
    d, f2 = wgu_s.shape
    f = f2 // 2
    e_cur = be_ref[i]
    new_expert = jnp.logical_or(i == 0, e_cur != be_ref[jnp.maximum(i - 1, 0)])

    def weight_copies(e):
        return (pltpu.make_async_copy(wgu_hbm.at[layer, e], wgu_f, sems.at[0]),
                pltpu.make_async_copy(wd_hbm.at[layer, e], wd_f, sems.at[1]))

    @pl.when(jnp.logical_and(new_expert, i < nu_ref[0]))
    def _():
        @pl.when(i == 0)
        def _():
            for cp in weight_copies(e_cur):
                cp.start()

        for cp in weight_copies(e_cur):
            cp.wait()
        for c0 in range(0, f2, GU_BLOCK):
            blk = wgu_f[:, c0:c0 + GU_BLOCK].astype(BF16)
            wgu_s[:, c0:c0 + GU_BLOCK] = jnp.dot(
                blk, sel_ref[...], preferred_element_type=F32).astype(BF16)
        wd_s[...] = wd_f[...].astype(BF16)

        @pl.when(nxt_ref[e_cur] >= 0)
        def _():
            for cp in weight_copies(nxt_ref[e_cur]):
                cp.start()

    @pl.when(i < nu_ref[0])
    def _():
        half = x_ref.shape[1]
        lo, hi = _unpack_halves(x_ref[...])
        lo = lo.astype(BF16)
        hi = hi.astype(BF16)

        def hidden(c0):
            cols = slice(2 * c0, 2 * (c0 + FF_CHUNK))
            gu = (jnp.dot(lo, wgu_s[:half, cols], preferred_element_type=F32)
                  + jnp.dot(hi, wgu_s[half:, cols], preferred_element_type=F32))
            parts = []
            for b0 in range(0, FF_CHUNK, LANES):
                gate = gu[:, 2 * b0:2 * b0 + LANES] + bg_ref[0, :, c0 + b0:c0 + b0 + LANES]
                up = gu[:, 2 * b0 + LANES:2 * b0 + 2 * LANES] + bu_ref[0, :, c0 + b0:c0 + b0 + LANES]
                gate = jnp.minimum(gate, SWIGLU_LIMIT)
                up = jnp.clip(up, -SWIGLU_LIMIT, SWIGLU_LIMIT)
                parts.append((up + 1.0) * gate * _sigmoid(SWIGLU_ALPHA * gate))
            return jnp.concatenate(parts, axis=1).astype(BF16)

        chunks = list(range(0, f, FF_CHUNK))
        y = None
        h_next = hidden(chunks[0])
        for n, c0 in enumerate(chunks):
            h = h_next
            if n + 1 < len(chunks):
                h_next = hidden(chunks[n + 1])
            part = jnp.dot(h, wd_s[c0:c0 + FF_CHUNK, :], preferred_element_type=F32)
            y = part if y is None else y + part
        y_ref[...] = _pack_halves(y + bd_ref[0])

    @pl.when(i >= nu_ref[0])
    def _():
        y_ref[...] = jnp.zeros_like(y_ref)


def _expert_ffn(xs, blk_e, n_used, nxt_e, w_gu, bg, bu, w_down, bd, rb, layer):
    rows, half = xs.shape
    _, e, d, f2 = w_gu.shape
    f = f2 // 2
    n_blocks = rows // rb
    wspec = lambda shape: pl.BlockSpec(shape, lambda i, be, nu, nx: (be[i], 0, 0))
    return pl.pallas_call(
        functools.partial(_expert_kernel, layer=layer),
        grid_spec=pltpu.PrefetchScalarGridSpec(
            num_scalar_prefetch=3,
            grid=(n_blocks,),
            in_specs=[
                pl.BlockSpec((rb, half), lambda i, be, nu, nx: (i, 0)),
                pl.BlockSpec(memory_space=pl.ANY),
                wspec((1, 1, f)), wspec((1, 1, f)),
                pl.BlockSpec(memory_space=pl.ANY),
                wspec((1, 1, d)),
                _resident((GU_BLOCK, GU_BLOCK), lambda i, be, nu, nx: (0, 0)),
            ],
            out_specs=pl.BlockSpec((rb, d // 2), lambda i, be, nu, nx: (i, 0)),
            scratch_shapes=[pltpu.VMEM((d, f2), F32), pltpu.VMEM((f, d), F32),
                            pltpu.VMEM((d, f2), BF16), pltpu.VMEM((f, d), BF16),
                            pltpu.SemaphoreType.DMA((2,))],
        ),
        out_shape=jax.ShapeDtypeStruct((rows, d // 2), U32),
        compiler_params=_cparams("arbitrary"),
        name="expert_ffn",
    )(blk_e, n_used, nxt_e, xs, w_gu, bg, bu, w_down, bd, _deinterleave_matrix())


def _combine_kernel(x_ref, y0_ref, y1_ref, y2_ref, y3_ref, rw_ref, g2_ref, *dst_and_out):
    o_ref = dst_and_out[-1]
    half = y0_ref.shape[1]
    lo, hi = _moe_mix(x_ref, (y0_ref, y1_ref, y2_ref, y3_ref), rw_ref, g2_ref)
    o_ref[:, :half] = lo
    o_ref[:, half:] = hi


def _combine(x1, yt, rw, mod, seq, tm, out_rows=None, row_off=0, dst=None):
    t, d = x1.shape
    per_seq = seq // tm
    nt = t // tm
    tile_off = row_off // tm
    yspec = lambda k: pl.BlockSpec((tm, d // 2), lambda i: (k * nt + i, 0))
    in_specs = [
        pl.BlockSpec((tm, d), lambda i: (i, 0)),
        yspec(0), yspec(1), yspec(2), yspec(3),
        pl.BlockSpec((tm, LANES), lambda i: (i, 0)),
        pl.BlockSpec((None, 1, d), lambda i: (i // per_seq, 0, 5)),
    ]
    args = [x1, yt, yt, yt, yt, rw, mod]
    aliases = {}
    if dst is not None:
        in_specs.append(pl.BlockSpec(memory_space=pl.ANY))
        args.append(dst)
        aliases = {len(args) - 1: 0}
    return pl.pallas_call(
        _combine_kernel,
        grid=(nt,),
        in_specs=in_specs,
        out_specs=pl.BlockSpec((tm, d), lambda i: (i + tile_off, 0)),
        out_shape=jax.ShapeDtypeStruct((t if out_rows is None else out_rows, d), F32),
        input_output_aliases=aliases,
        compiler_params=_cparams("parallel"),
        name="moe_combine",
    )(*args)


POST_ROWS = 1024
POST_SUB = 256


def _row_tile(seq):
    return min(seq, 512)


def _moe_block_rows(t):
    return min(512, t * TOP_K // 8)


def kernel(x, c, fox_w_in, fox_b_f, fox_q_norm, fox_k_norm, fox_w_out, hgrn_w_in, hgrn_lb, hgrn_g_norm, hgrn_w_out, ada_w, ada_b, norm1_w, norm2_w, router_w, router_b, exp_w_gu, exp_b_gu, exp_w_down, exp_b_down):
    bsz, seq, d = x.shape
    t = bsz * seq
    depth = ada_w.shape[0]
    n_exp = router_w.shape[-1]
    f_dim = exp_w_down.shape[2]
    h_fox = fox_w_in.shape[-1] - 4 * d
    dh = d // h_fox
    tm = _row_tile(seq)
    n_split = 2 if bsz % 2 == 0 else 1
    bh, th = bsz // n_split, t // n_split
    rb = _moe_block_rows(th)
    n_blocks = (th * TOP_K) // rb + n_exp
    hgrn_hpb = 4 if d % (4 * LANES) == 0 else 2

    mod_all = _adaln_mod(c, ada_w, ada_b).reshape(depth, bsz, 1, -1)
    lb_soft = jax.nn.softmax(hgrn_lb.astype(F32), axis=0)
    lower = jnp.cumsum(lb_soft, axis=0) - lb_soft[0]
    experts = jnp.arange(n_exp, dtype=I32)

    xf = x.reshape(t, d)
    parts = [xf] * n_split
    for i in range(depth):
        j = i // 2
        nw1 = norm1_w[i].reshape(1, d)
        if i % 2 == 0:
            w_in = fox_w_in[j]
            w_main = w_in[:, :4 * d].astype(BF16)
            w_f = jnp.pad(w_in[:, 4 * d:], ((0, 0), (0, LANES - h_fox))).astype(BF16)
            b_f = jnp.pad(fox_b_f[j], (0, LANES - h_fox)).reshape(1, LANES)
            qw = jnp.tile(fox_q_norm[j], LANES // dh).reshape(1, LANES)
            kw = jnp.tile(fox_k_norm[j], LANES // dh).reshape(1, LANES)
            w_out = fox_w_out[j].astype(BF16)
        else:
            w_h = hgrn_w_in[j]
            w_main = jnp.concatenate([w_h[:, :d], w_h[:, 2 * d:]], axis=1).astype(BF16)
            w_f = w_h[:, d:2 * d].astype(BF16)
            b_f = jnp.zeros((1, d), F32)
            w_out = hgrn_w_out[j].astype(BF16)
        wr = jnp.pad(router_w[i], ((0, 0), (0, LANES - n_exp))).astype(BF16)
        br = jnp.pad(router_b[i], (0, LANES - n_exp), constant_values=MASK_VALUE).reshape(1, LANES)
        b_gu = exp_b_gu[i].reshape(n_exp, 1, f_dim, 2)

        def front(s, src):
            mod = mod_all[i, s * bh:(s + 1) * bh]
            off = s * th if i == 0 else 0
            moe = None
            xp = src
            if isinstance(src, tuple):
                xp, moe = src[0], src[1:]
            if i % 2 == 0:
                outs = _in_proj(xp, nw1, mod, w_main, BF16, seq, tm, w_f, b_f,
                                rows=th, row_off=off, moe=moe, name="in_proj_fox")
                o = _fox_attention(outs[0], outs[1], qw, kw, seq, d, dh)
            else:
                outs = _in_proj(xp, nw1, mod, w_main, BF16, seq, tm, w_f, b_f,
                                rows=th, row_off=off, moe=moe, name="in_proj_hgrn")
                o = _hgrn_mixer(outs[0], outs[1], lower[j].reshape(1, d),
                                hgrn_g_norm[j].reshape(1, HGRN_DIM), seq, d, hgrn_hpb)
            if moe is not None:
                xp = outs[-1]
            x1, hp, ri, rw, cnt = _post_mixer(o, xp, w_out, mod, norm2_w[i].reshape(1, d), wr, br,
                                              seq, min(seq, POST_ROWS), row_off=off)
            counts = cnt[0, :n_exp]
            padded = (counts + rb - 1) // rb * rb
            pad_end = jnp.cumsum(padded)
            pad_start = pad_end - padded
            top_idx, rank = ri[:TOP_K], ri[TOP_K:]
            start = jnp.sum(jnp.where(top_idx[None] == experts[:, None, None],
                                      pad_start[:, None, None], 0), axis=0)
            posf = (start + rank).reshape(-1).astype(I32)
            blk_lo = jnp.arange(n_blocks, dtype=I32) * rb
            blk_e = jnp.minimum(jnp.sum(pad_end[None, :] <= blk_lo[:, None], axis=1),
                                n_exp - 1).astype(I32)
            n_used = (pad_end[-1] // rb).astype(I32).reshape(1)
            later = jnp.where((experts[None, :] > experts[:, None]) & (padded[None, :] > 0),
                              experts[None, :], n_exp)
            nxt_e = jnp.min(later, axis=1)
            nxt_e = jnp.where(nxt_e >= n_exp, -1, nxt_e).astype(I32)
            return dict(x1=x1, hp=hp, rw=rw, posf=posf, blk_e=blk_e, n_used=n_used, nxt_e=nxt_e,
                        mod=mod)

        def experts_of(st, xs):
            return _expert_ffn(xs, st["blk_e"], st["n_used"], st["nxt_e"], exp_w_gu,
                               b_gu[..., 0], b_gu[..., 1], exp_w_down,
                               exp_b_down[i].reshape(n_exp, 1, d), rb, i)

        def deferred(st, yt):
            return (st["x1"], yt, st["rw"], st["mod"])

        if n_split == 1:
            st = front(0, parts[0])
            xs = _sc_scatter_rows(st["hp"], st["posf"], n_blocks * rb)
            parts[0] = deferred(st, _sc_gather_rows(experts_of(st, xs), st["posf"]))
            continue

        tie = lax.optimization_barrier
        sa = front(0, parts[0])
        xs_a = _sc_scatter_rows(sa["hp"], sa["posf"], n_blocks * rb)
        src_b = parts[1]
        if isinstance(src_b, tuple):
            yt_b, _ = tie((src_b[1], sa["hp"]))
            src_b = (src_b[0], yt_b) + src_b[2:]
        else:
            src_b, _ = tie((src_b, sa["hp"]))
        sb = front(1, src_b)
        xs_a, _ = tie((xs_a, sb["hp"]))
        xs_b = _sc_scatter_rows(sb["hp"], sb["posf"], n_blocks * rb)
        ys_a = experts_of(sa, xs_a)
        xs_b, _ = tie((xs_b, ys_a))
        yt_a = _sc_gather_rows(ys_a, sa["posf"])
        ys_b = experts_of(sb, xs_b)
        yt_a, _ = tie((yt_a, ys_b))
        yt_b = _sc_gather_rows(ys_b, sb["posf"])
        parts = [deferred(sa, yt_a), deferred(sb, yt_b)]

    out = None
    for s_i, (x1, yt, rw, mod) in enumerate(parts):
        if n_split == 1:
            out = _combine(x1, yt, rw, mod, seq, tm)
        else:
            out = _combine(x1, yt, rw, mod, seq, tm, out_rows=t, row_off=s_i * th, dst=out)
    return out.reshape(bsz, seq, d)
```

```python
import functools

import jax
import jax.numpy as jnp
import numpy as np
from jax import lax
from jax.experimental import pallas as pl
from jax.experimental.pallas import tpu as pltpu
from jax.experimental.pallas import tpu_sc as plsc

F32 = jnp.float32
BF16 = jnp.bfloat16
U32 = jnp.uint32
I32 = jnp.int32

EPS = 1e-6
MASK_VALUE = -1e30
TOP_K = 4
HGRN_DIM = 128
HGRN_CHUNK = 32
SWIGLU_LIMIT = 7.0
SWIGLU_ALPHA = 1.702
LANES = 128
SUBLANES = 8
SC_CORES = 2
SC_SUBCORES = 16
SC_ROWS = 64
VMEM_LIMIT = 56 * 1024 * 1024
HIGHEST = lax.Precision.HIGHEST


def _cparams(*sem):
    return pltpu.CompilerParams(dimension_semantics=sem, vmem_limit_bytes=VMEM_LIMIT)


def _resident(shape, index_map):
    return pl.BlockSpec(shape, index_map, pipeline_mode=pl.Buffered(1))


def _sigmoid(x):
    return 1.0 / (1.0 + jnp.exp(-x))


def _bf16_bits(x):
    return lax.bitcast_convert_type(x.astype(BF16).astype(F32), U32)


def _pack_halves(x):
    n = x.shape[1] // 2
    lo = _bf16_bits(x[:, :n])
    hi = _bf16_bits(x[:, n:])
    return (hi & jnp.uint32(0xFFFF0000)) | (lo >> 16)


def _unpack_halves(w):
    lo = lax.bitcast_convert_type(w << 16, F32)
    hi = lax.bitcast_convert_type(w & jnp.uint32(0xFFFF0000), F32)
    return lo, hi


def _mod_kernel(c_ref, w_ref, b_ref, o_ref):
    c = c_ref[...]
    ca = c * _sigmoid(c)
    o_ref[0] = jnp.dot(ca, w_ref[0], preferred_element_type=F32, precision=HIGHEST) + b_ref[0]


def _adaln_mod(c, ada_w, ada_b):
    depth, d, n6 = ada_w.shape
    b = c.shape[0]
    nmod = n6 // d
    return pl.pallas_call(
        _mod_kernel,
        grid=(depth, nmod),
        in_specs=[
            pl.BlockSpec((b, d), lambda i, j: (0, 0)),
            pl.BlockSpec((1, d, d), lambda i, j: (i, 0, j)),
            pl.BlockSpec((1, 1, d), lambda i, j: (i, 0, j)),
        ],
        out_specs=pl.BlockSpec((1, b, d), lambda i, j: (i, 0, j)),
        out_shape=jax.ShapeDtypeStruct((depth, b, n6), F32),
        compiler_params=_cparams("parallel", "parallel"),
        name="adaln_mod",
    )(c, ada_w, ada_b.reshape(depth, 1, n6))


def _norm_mod(x, nw, sc, sh):
    ms = jnp.mean(x * x, axis=-1, keepdims=True)
    return x * lax.rsqrt(ms + EPS) * nw * (1.0 + sc) + sh


def _moe_mix(x_ref, y_refs, rw_ref, g2_ref, rows=slice(None)):
    rw = rw_ref[rows, :]
    lane = lax.broadcasted_iota(I32, rw.shape, 1)
    half = y_refs[0].shape[1]
    acc_lo = acc_hi = None
    for k, y_ref in enumerate(y_refs):
        wk = jnp.sum(jnp.where(lane == k, rw, 0.0), axis=-1, keepdims=True)
        lo, hi = _unpack_halves(y_ref[rows, :])
        acc_lo = wk * lo if acc_lo is None else acc_lo + wk * lo
        acc_hi = wk * hi if acc_hi is None else acc_hi + wk * hi
    g2 = g2_ref[...]
    return (x_ref[rows, :half] + g2[:, :half] * acc_lo, x_ref[rows, half:] + g2[:, half:] * acc_hi)


def _in_kernel(*refs, n_chunk, with_f, with_moe, sub):
    n_in = 1 + (TOP_K + 2 if with_moe else 0) + 4 + (2 if with_f else 0)
    ins, outs = refs[:n_in], refs[n_in:]
    x_ref = ins[0]
    pos = 1 + (TOP_K + 2 if with_moe else 0)
    nw_ref, sh_ref, sc_ref, w_ref = ins[pos:pos + 4]
    o_ref = outs[0]
    if with_f:
        wf_ref, bf_ref = ins[pos + 4:pos + 6]
        of_ref = outs[1]
    n = w_ref.shape[1]
    for r0 in range(0, x_ref.shape[0], sub):
        rows = slice(r0, r0 + sub)
        if with_moe:
            lo, hi = _moe_mix(x_ref, ins[1:1 + TOP_K], ins[1 + TOP_K], ins[2 + TOP_K], rows)
            x = jnp.concatenate([lo, hi], axis=1)
            outs[-1][rows, :] = x
        else:
            x = x_ref[rows, :]
        h = _norm_mod(x, nw_ref[...], sc_ref[...], sh_ref[...]).astype(BF16)
        for c0 in range(0, n, n_chunk):
            res = jnp.dot(h, w_ref[:, c0:c0 + n_chunk],
                          preferred_element_type=F32).astype(o_ref.dtype)
            if len(o_ref.shape) == 2:
                o_ref[rows, c0:c0 + n_chunk] = res
            else:
                for j in range(n_chunk // LANES):
                    o_ref[c0 // LANES + j, rows, :] = res[:, j * LANES:(j + 1) * LANES]
        if with_f:
            of_ref[rows, :] = jnp.dot(h, wf_ref[...], preferred_element_type=F32) + bf_ref[...]


def _in_proj(x, nw, mod, w, out_dtype, seq, tm, wf=None, bf=None, rows=None, row_off=0, moe=None,
             name="in_proj", col_blocks=False):
    d = x.shape[1]
    t = x.shape[0] if rows is None else rows
    n = w.shape[1]
    per_seq = seq // tm
    tile_off = row_off // tm
    nt = t // tm
    with_f = wf is not None
    mod_spec = lambda j: pl.BlockSpec((None, 1, d), lambda i: (i // per_seq, 0, j))
    in_specs = [pl.BlockSpec((tm, d), lambda i: (i + tile_off, 0))]
    args = [x]
    if moe is not None:
        yt, rw, mod_prev = moe
        in_specs += [pl.BlockSpec((tm, d // 2), lambda i, k=k: (k * nt + i, 0)) for k in range(TOP_K)]
        in_specs += [pl.BlockSpec((tm, LANES), lambda i: (i, 0)), mod_spec(5)]
        args += [yt] * TOP_K + [rw, mod_prev]
    in_specs += [_resident((1, d), lambda i: (0, 0)), mod_spec(0), mod_spec(1),
                 _resident((d, n), lambda i: (0, 0))]
    args += [nw, mod, mod, w]
    if col_blocks:
        out_specs = [pl.BlockSpec((n // LANES, tm, LANES), lambda i: (0, i, 0))]
        out_shape = [jax.ShapeDtypeStruct((n // LANES, t, LANES), out_dtype)]
    else:
        out_specs = [pl.BlockSpec((tm, n), lambda i: (i, 0))]
        out_shape = [jax.ShapeDtypeStruct((t, n), out_dtype)]
    if with_f:
        nf = wf.shape[1]
        in_specs += [_resident((d, nf), lambda i: (0, 0)), _resident((1, nf), lambda i: (0, 0))]
        out_specs.append(pl.BlockSpec((tm, nf), lambda i: (i, 0)))
        out_shape.append(jax.ShapeDtypeStruct((t, nf), F32))
        args += [wf, bf]
    if moe is not None:
        out_specs.append(pl.BlockSpec((tm, d), lambda i: (i, 0)))
        out_shape.append(jax.ShapeDtypeStruct((t, d), F32))
    return pl.pallas_call(
        functools.partial(_in_kernel, n_chunk=min(n, 1024), with_f=with_f, with_moe=moe is not None,
                          sub=min(tm, 256)),
        grid=(t // tm,),
        in_specs=in_specs,
        out_specs=out_specs,
        out_shape=out_shape,
        compiler_params=_cparams("parallel"),
        name=name,
    )(*args)


LOG2E = 1.4426950408889634
EXP2_SPAN = 120.0
SCORE_MARGIN = 1.01
BIAS_LANES = 16


def _split3(c):
    c1 = c.astype(BF16).astype(F32)
    r = c - c1
    c2 = r.astype(BF16).astype(F32)
    return c1, c2, r - c2


def _bias_layout(n_pairs):
    route =np.zeros((3, 3 * LANES, LANES), np.float32)
    const = np.zeros((3, 1, LANES), np.float32)
    for p in range(n_pairs):
        base = p * BIAS_LANES
        for i in range(3):
            const[0, 0, base + i] = 1.0
            for h in range(2):
                route[0, i * LANES + 2 * p + h, base + 3 + 3 * h + i] = -1.0
                route[1 + h, i * LANES + 2 * p + h, base + i] = 1.0
                const[1 + h, 0, base + 3 + 3 * h + i] = 1.0
    return jnp.asarray(route, BF16), jnp.asarray(const, F32)


def _cum_kernel(f_ref, ref_ref, route_ref, const_ref, o_ref, *, rows):
    s = f_ref.shape[0]
    r = lax.broadcasted_iota(I32, (rows, rows), 0)
    c = lax.broadcasted_iota(I32, (rows, rows), 1)
    tri = jnp.where(r >= c, 1.0, 0.0).astype(BF16)
    carry = jnp.zeros((1, LANES), F32)
    for r0 in range(0, s, rows):
        x = f_ref[r0:r0 + rows, :]
        lf = jnp.minimum(x, 0.0) - jnp.log(1.0 + jnp.exp(-jnp.abs(x)))
        cs = carry + sum(jnp.dot(tri, part.astype(BF16), preferred_element_type=F32)
                         for part in _split3(lf))
        carry = cs[rows - 1:rows, :]
        c2 = cs * LOG2E
        k_terms = jnp.concatenate(_split3(c2), axis=1).astype(BF16)
        q_terms = jnp.concatenate(_split3(c2 - ref_ref[...]), axis=1).astype(BF16)
        for j, terms in enumerate((k_terms, q_terms, q_terms)):
            routed = jnp.dot(terms, route_ref[j], preferred_element_type=F32) + const_ref[j]
            o_ref[r0:r0 + rows, j * LANES:(j + 1) * LANES] = routed.astype(o_ref.dtype)


def _fox_bias(flogit, ref, seq, n_pairs):
    t = flogit.shape[0]
    route, const = _bias_layout(n_pairs)
    return pl.pallas_call(
        functools.partial(_cum_kernel, rows=min(seq, 256)),
        grid=(t // seq,),
        in_specs=[pl.BlockSpec((seq, LANES), lambda b: (b, 0)),
                  pl.BlockSpec((1, LANES), lambda b: (0, 0)),
                  pl.BlockSpec((3, 3 * LANES, LANES), lambda b: (0, 0, 0)),
                  pl.BlockSpec((3, 1, LANES), lambda b: (0, 0, 0))],
        out_specs=pl.BlockSpec((seq, 3 * LANES), lambda b: (b, 0)),
        out_shape=jax.ShapeDtypeStruct((t, 3 * LANES), BF16),
        compiler_params=_cparams("parallel"),
        name="fox_bias",
    )(flogit, ref, route, const)


def _attn_kernel(q_ref, k_ref, v_ref, g_ref, bias_ref, qw_ref, kw_ref, o_ref, qs_s, ka_s, vt_s, *,
                 dh, tq, bounded):
    s_len = q_ref.shape[0]
    nq = s_len // tq
    hp = pl.program_id(1)
    lane = lax.broadcasted_iota(I32, (1, LANES), 1)
    first = lane < dh

    def headnorm(z, w):
        z2 = z * z
        s0 = jnp.sum(jnp.where(first, z2, 0.0), axis=-1, keepdims=True)
        s1 = jnp.sum(jnp.where(first, 0.0, z2), axis=-1, keepdims=True)
        ms = jnp.where(first, s0, s1) * (1.0 / dh)
        return z * lax.rsqrt(ms + EPS) * w

    qn = headnorm(q_ref[...].astype(F32), qw_ref[...]) * (dh ** -0.5 * LOG2E)
    kn = headnorm(k_ref[...].astype(F32), kw_ref[...])
    mine = (lane // BIAS_LANES) == hp
    ka_s[:, :LANES] = kn.astype(BF16)
    ka_s[:, LANES:] = bias_ref[:, :LANES]
    q_lo = [jnp.where(first, qn, 0.0), jnp.where(first, 0.0, qn)]
    hq = tq // 2
    for qi in range(nq):
        for part_i in range(2):
            rows = slice(qi * tq + part_i * hq, qi * tq + (part_i + 1) * hq)
            for h in range(2):
                dst = slice((2 * part_i + h) * hq, (2 * part_i + h + 1) * hq)
                q_hi = bias_ref[rows, (1 + h) * LANES:(2 + h) * LANES].astype(F32)
                qs_s[qi, dst, :LANES] = q_lo[h][rows, :].astype(BF16)
                qs_s[qi, dst, LANES:] = jnp.where(mine, q_hi, 0.0).astype(BF16)

    nt = (((1,), (1,)), ((), ()))

    def causal(s, first_token):
        rr = lax.broadcasted_iota(I32, s.shape, 0)
        rr = jnp.where(rr >= hq, rr - hq, rr) + first_token
        cc = lax.broadcasted_iota(I32, s.shape, 1)
        return jnp.where(rr >= cc, s, MASK_VALUE)

    def finish(qi, halves):
        for part_i, (acc, l) in enumerate(halves):
            o2 = acc / l
            o = jnp.where(first, o2[:hq], o2[hq:])
            rows = slice(qi * tq + part_i * hq, qi * tq + (part_i + 1) * hq)
            g = g_ref[rows, :].astype(F32)
            o_ref[rows, :] = (o * _sigmoid(g)).astype(o_ref.dtype)

    if bounded:
        vt_s[...] = jnp.transpose(v_ref[...].astype(F32)).astype(BF16)
        dv_first = lax.broadcasted_iota(I32, (LANES, 1), 0) < dh

        def causal_t(s_t, first_token):
            kk = lax.broadcasted_iota(I32, s_t.shape, 0)
            qq = lax.broadcasted_iota(I32, s_t.shape, 1)
            qq = jnp.where(qq >= hq, qq - hq, qq) + first_token
            return jnp.where(qq >= kk, s_t, MASK_VALUE)

        for qi in range(nq):
            acc, part = [None, None], [None, None]

            def add(i, pv_t, sums):
                acc[i] = pv_t if acc[i] is None else acc[i] + pv_t
                part[i] = sums if part[i] is None else part[i] + sums

            k0 = qi * tq
            tiles = [(slice(t * tq, (t + 1) * tq), slice(0, 2 * tq), None) for t in range(qi)]
            tiles += [(slice(k0, k0 + hq), slice(0, tq), 0), (slice(k0, k0 + tq), slice(tq, 2 * tq), 1)]

            def scores_t(tile):
                keys, q_rows, _ = tile
                return lax.dot_general(ka_s[keys, :], qs_s[qi, q_rows, :], nt,
                                       preferred_element_type=F32)

            s_next = scores_t(tiles[0])
            for n, (keys, _, half) in enumerate(tiles):
                s_t = s_next
                if n + 1 < len(tiles):
                    s_next = scores_t(tiles[n + 1])
                e_t = jnp.exp2(s_t if half is None else causal_t(s_t, half * hq))
                pv_t = jnp.dot(vt_s[:, keys], e_t.astype(BF16), preferred_element_type=F32)
                sums = jnp.sum(e_t, axis=0, keepdims=True)
                if half is None:
                    add(0, pv_t[:, :tq], sums[:, :tq])
                    add(1, pv_t[:, tq:], sums[:, tq:])
                else:
                    add(half, pv_t, sums)
            for i in range(2):
                o_t = acc[i] / part[i]
                o = jnp.transpose(jnp.where(dv_first, o_t[:, :hq], o_t[:, hq:]))
                rows = slice(qi * tq + i * hq, qi * tq + (i + 1) * hq)
                g = g_ref[rows, :].astype(F32)
                o_ref[rows, :] = (o * _sigmoid(g)).astype(o_ref.dtype)
        return

    def causal_tile(s):
        return jnp.concatenate([causal(s[:tq], 0), causal(s[tq:], hq)], axis=0)

    def scores(q, off):
        return lax.dot_general(q, ka_s[pl.ds(off, tq), :], nt, preferred_element_type=F32)

    def update(s, off, carry, masked):
        m, l, acc = carry
        if masked:
            s = causal_tile(s)
        m_new = jnp.maximum(m, jnp.max(s, axis=-1, keepdims=True))
        p = jnp.exp2(s - m_new)
        alpha = jnp.exp2(m - m_new)
        l = alpha * l + jnp.sum(p, axis=-1, keepdims=True)
        acc = alpha * acc + jnp.dot(p.astype(BF16), v_ref[pl.ds(off, tq), :],
                                    preferred_element_type=F32)
        return m_new, l, acc

    for qi in range(nq):
        q = qs_s[qi]
        s = scores(q, 0)
        carry = (jnp.full((2 * tq, 1), MASK_VALUE, F32), jnp.zeros((2 * tq, 1), F32),
                 jnp.zeros((2 * tq, LANES), F32))
        if qi > 0:
            def body(j, c, q=q):
                off = pl.multiple_of(j * tq, tq)
                s_next = scores(q, off + tq)
                return (s_next,) + update(c[0], off, c[1:], False)

            s, *carry = lax.fori_loop(0, qi, body, (s,) + carry)
        m, l, acc = update(s, qi * tq, carry, True)
        finish(qi, [(acc[:tq], l[:tq]), (acc[tq:], l[tq:])])


def _fox_attention(qkvg, flogit, qw, kw, seq, d, dh):
    t = qkvg.shape[1]
    nb = d // LANES
    assert 2 * dh == LANES and nb * BIAS_LANES <= LANES
    tq = min(seq, 512)
    col = lambda off: pl.BlockSpec((None, seq, LANES), lambda b, p: (off + p, b, 0))

    def call(bounded, bias):
        return pl.pallas_call(
            functools.partial(_attn_kernel, dh=dh, tq=tq, bounded=bounded),
            grid=(t // seq, nb),
            in_specs=[
                col(0), col(nb), col(2 * nb), col(3 * nb),
                pl.BlockSpec((seq, 3 * LANES), lambda b, p: (b, 0)),
                pl.BlockSpec((1, LANES), lambda b, p: (0, 0)),
                pl.BlockSpec((1, LANES), lambda b, p: (0, 0)),
            ],
            out_specs=pl.BlockSpec((seq, LANES), lambda b, p: (b, p)),
            out_shape=jax.ShapeDtypeStruct((t, d), BF16),
            scratch_shapes=[pltpu.VMEM((seq // tq, 2 * tq, 2 * LANES), BF16),
                            pltpu.VMEM((seq, 2 * LANES), BF16),
                            pltpu.VMEM((LANES, seq), BF16)],
            compiler_params=_cparams("parallel", "parallel"),
            name="fox_attention" if bounded else "fox_attention_running_max",
        )(qkvg, qkvg, qkvg, qkvg, bias, qw, kw)

    bound = (dh ** 0.5 * LOG2E * SCORE_MARGIN) * jnp.max(jnp.abs(qw)) * jnp.max(jnp.abs(kw)) + 0.1
    fits = bound <= EXP2_SPAN - 4.0
    ref = jnp.where(fits, bound - jnp.maximum(0.0, 2.0 * bound - EXP2_SPAN), 0.0)
    bias = _fox_bias(flogit, jnp.full((1, LANES), ref, F32), seq, nb)
    return lax.cond(fits, lambda: call(True, bias), lambda: call(False, bias))


HGRN_PREP_ROWS = 256
HGRN_LOG2_SPAN = 100.0


def _hgrn_kernel(q_ref, f_ref, v_ref, g_ref, lb_ref, gw_ref, o_ref, qe_s, ke_s, kd_s, dec_s, *,
                 hpb, chunk):
    s_len = q_ref.shape[0]
    n_chunks = s_len // chunk
    nt = (((1,), (1,)), ((), ()))
    gw = gw_ref[...]

    pr = min(HGRN_PREP_ROWS, s_len)
    rr = lax.broadcasted_iota(I32, (pr, pr), 0)
    cc = lax.broadcasted_iota(I32, (pr, pr), 1)
    same = (rr // chunk) == (cc // chunk)
    cum_m = jnp.where(same & (rr >= cc), 1.0, 0.0).astype(BF16)
    tot_m = jnp.where(same, 1.0, 0.0).astype(BF16)
    lb_all = lb_ref[...]
    lowest = jnp.zeros((1, lb_all.shape[1]), F32)
    for r0 in range(0, s_len, pr):
        sig = _sigmoid(f_ref[r0:r0 + pr, :])
        terms = [t.astype(BF16) for t in _split3(jnp.log2(lb_all + (1.0 - lb_all) * sig))]
        b = sum(jnp.dot(cum_m, t, preferred_element_type=F32) for t in terms)
        tot = sum(jnp.dot(tot_m, t, preferred_element_type=F32) for t in terms)
        k = (1.0 - lb_all) * (1.0 - sig)
        q_raw = q_ref[r0:r0 + pr, :].astype(F32)
        qe_s[r0:r0 + pr, :] = (q_raw * _sigmoid(q_raw) * jnp.exp2(b)).astype(BF16)
        ke_s[r0:r0 + pr, :] = (k * jnp.exp2(-b)).astype(BF16)
        kd_s[r0:r0 + pr, :] = (k * jnp.exp2(tot - b)).astype(BF16)
        for c0 in range(0, pr, chunk):
            ci = (r0 + c0) // chunk
            dec_s[ci] = jnp.exp2(tot[c0:c0 + SUBLANES, :])
        lowest = jnp.minimum(lowest, jnp.min(tot, axis=0, keepdims=True))
    mild = jnp.min(lowest) >= -HGRN_LOG2_SPAN

    @pl.when(mild)
    def _():
        r32 = lax.broadcasted_iota(I32, (chunk, chunk), 0)
        c32 = lax.broadcasted_iota(I32, (chunk, chunk), 1)
        causal = r32 >= c32
        per_step = 8 if n_chunks % 8 == 0 else 2

        def fast_body(step, states):
            insts = [(ch, hd) for ch in range(per_step) for hd in range(hpb)]
            rows = {ch: pl.ds(pl.multiple_of((step * per_step + ch) * chunk, chunk), chunk)
                    for ch in range(per_step)}
            cols = {hd: slice(hd * LANES, (hd + 1) * LANES) for hd in range(hpb)}
            states = list(states)
            qe = {i: qe_s[rows[i[0]], cols[i[1]]] for i in insts}
            v = {i: v_ref[rows[i[0]], cols[i[1]]].astype(F32) for i in insts}
            a = {i: lax.dot_general(qe[i], ke_s[rows[i[0]], cols[i[1]]], nt,
                                    preferred_element_type=F32) for i in insts}
            upd = {i: jnp.dot(jnp.transpose(v[i]).astype(BF16), kd_s[rows[i[0]], cols[i[1]]],
                              preferred_element_type=F32) for i in insts}
            o = {}
            for ch in range(per_step):
                for hd in range(hpb):
                    i = (ch, hd)
                    o[i] = (jnp.dot(jnp.where(causal, a[i], 0.0).astype(BF16), v[i].astype(BF16),
                                    preferred_element_type=F32)
                            + lax.dot_general(qe[i], states[hd].astype(BF16), nt,
                                              preferred_element_type=F32))
                    dec = dec_s[step * per_step + ch, 0:1, cols[hd]]
                    states[hd] = dec * states[hd] + upd[i]
            for i in insts:
                g = g_ref[rows[i[0]], cols[i[1]]].astype(F32)
                ms = jnp.mean(o[i] * o[i], axis=-1, keepdims=True)
                y = o[i] * lax.rsqrt(ms + EPS) * gw * (g * _sigmoid(g))
                o_ref[rows[i[0]], cols[i[1]]] = y.astype(o_ref.dtype)
            return tuple(states)

        init = tuple(jnp.zeros((LANES, LANES), F32) for _ in range(hpb))
        lax.fori_loop(0, n_chunks // per_step, fast_body, init)

    pair = 2
    groups = chunk // SUBLANES
    off_w = (groups - 1) * chunk
    r = lax.broadcasted_iota(I32, (chunk, chunk), 0)
    c = lax.broadcasted_iota(I32, (chunk, chunk), 1)
    tri = jnp.where(r >= c, 1.0, 0.0).astype(BF16)
    ro = lax.broadcasted_iota(I32, (chunk, off_w), 0) // SUBLANES
    co = lax.broadcasted_iota(I32, (chunk, off_w), 1) // chunk
    keep = ro == co + 1
    sub = lax.broadcasted_iota(I32, (SUBLANES, LANES), 0)
    gw = gw_ref[...]
    nt = (((1,), (1,)), ((), ()))
    grp = lambda a, i: a[i * SUBLANES:(i + 1) * SUBLANES, :]
    zeros8 = jnp.zeros((SUBLANES, LANES), F32)

    def body(step, states):
        insts = [(ch, hd) for ch in range(pair) for hd in range(hpb)]
        rows = {ch: pl.ds(pl.multiple_of((step * pair + ch) * chunk, chunk), chunk)
                for ch in range(pair)}
        cols = {hd: slice(hd * LANES, (hd + 1) * LANES) for hd in range(hpb)}
        st = {}
        for ch, hd in insts:
            lb = lb_ref[:, cols[hd]]
            q_raw = q_ref[rows[ch], cols[hd]].astype(F32)
            sig = _sigmoid(f_ref[rows[ch], cols[hd]])
            d = dict(v=v_ref[rows[ch], cols[hd]].astype(F32), q=q_raw * _sigmoid(q_raw),
                     k=(1.0 - lb) * (1.0 - sig))
            lf = jnp.log(lb + (1.0 - lb) * sig)
            d["b"] = sum(jnp.dot(tri, part.astype(BF16), preferred_element_type=F32)
                         for part in _split3(lf))
            st[ch, hd] = d
        def from_state(ch, hd, state_t):
            d = st[ch, hd]
            qe = (d["q"] * jnp.exp(d["b"])).astype(BF16)
            d["o"] = lax.dot_general(qe, state_t.astype(BF16), nt, preferred_element_type=F32)
        for hd in range(hpb):
            from_state(0, hd, states[hd])
        for ch, hd in insts:
            d = st[ch, hd]
            b, q, k = d["b"], d["q"], d["k"]
            ends = [b[(i + 1) * SUBLANES - 1:(i + 1) * SUBLANES, :] for i in range(groups)]
            kt = [grp(k, i) * jnp.exp(ends[i] - grp(b, i)) for i in range(groups)]
            qh = jnp.concatenate(
                [zeros8] + [grp(q, i) * jnp.exp(grp(b, i) - ends[i - 1]) for i in range(1, groups)],
                axis=0).astype(BF16)
            kh = jnp.concatenate(
                [kt[j] * jnp.exp(ends[i - 1] - ends[j]) if j < i else zeros8
                 for i in range(1, groups) for j in range(groups)], axis=0).astype(BF16)
            d["a"] = lax.dot_general(qh, kh, nt, preferred_element_type=F32)
            d["ends"], d["kt"] = ends, kt
        def state_update(ch, hd, state_t):
            d = st[ch, hd]
            b_end = d["ends"][-1]
            ke = jnp.concatenate([d["kt"][j] * jnp.exp(b_end - d["ends"][j]) for j in range(groups)],
                                 axis=0).astype(BF16)
            upd = jnp.dot(jnp.transpose(d["v"]).astype(BF16), ke, preferred_element_type=F32)
            return jnp.exp(b_end) * state_t + upd
        states = [state_update(0, hd, states[hd]) for hd in range(hpb)]
        for ch, hd in insts:
            d = st[ch, hd]
            vb = d["v"].astype(BF16)
            v3 = jnp.concatenate([vb] * (groups - 1), axis=0)
            d["off"] = jnp.dot(jnp.where(keep, d["a"], 0.0).astype(BF16), v3,
                               preferred_element_type=F32)
        for hd in range(hpb):
            from_state(1, hd, states[hd])
        states = [state_update(1, hd, states[hd]) for hd in range(hpb)]
        for ch, hd in insts:
            d = st[ch, hd]
            o = d["o"] + d["off"]
            o_g = []
            for gi in range(groups):
                qg, kg, vg, bg = grp(d["q"], gi), grp(d["k"], gi), grp(d["v"], gi), grp(d["b"], gi)
                og = grp(o, gi)
                for s in range(SUBLANES):
                    live = sub >= s
                    e = jnp.exp(jnp.where(live, bg - bg[s:s + 1, :], 0.0))
                    p = jnp.where(live, qg * e * kg[s:s + 1, :], 0.0)
                    og = og + jnp.sum(p, axis=-1, keepdims=True) * vg[s:s + 1, :]
                o_g.append(og)
            o = jnp.concatenate(o_g, axis=0)
            g = g_ref[rows[ch], cols[hd]].astype(F32)
            ms = jnp.mean(o * o, axis=-1, keepdims=True)
            y = o * lax.rsqrt(ms + EPS) * gw * (g * _sigmoid(g))
            o_ref[rows[ch], cols[hd]] = y.astype(o_ref.dtype)
        return tuple(states)

    @pl.when(jnp.logical_not(mild))
    def _():
        init = tuple(jnp.zeros((LANES, LANES), F32) for _ in range(hpb))
        lax.fori_loop(0, s_len // (chunk * pair), body, init)


def _hgrn_mixer(qvg, fgate, lower, gw, seq, d, hpb):
    t = qvg.shape[0]
    w = hpb * LANES
    nb = d // w
    col = lambda off: pl.BlockSpec((seq, w), lambda b, p: (b, off + p))
    return pl.pallas_call(
        functools.partial(_hgrn_kernel, hpb=hpb, chunk=HGRN_CHUNK),
        grid=(t // seq, nb),
        in_specs=[
            col(0), col(0), col(nb), col(2 * nb),
            pl.BlockSpec((1, w), lambda b, p: (0, p)),
            pl.BlockSpec((1, LANES), lambda b, p: (0, 0)),
        ],
        out_specs=pl.BlockSpec((seq, w), lambda b, p: (b, p)),
        out_shape=jax.ShapeDtypeStruct((t, d), BF16),
        scratch_shapes=[pltpu.VMEM((seq, w), BF16)] * 3
        + [pltpu.VMEM((seq // HGRN_CHUNK, SUBLANES, w), F32)],
        compiler_params=_cparams("parallel", "parallel"),
        name="hgrn_mixer",
    )(qvg, fgate, qvg, qvg, lower, gw)


def _post_kernel(o_ref, x_ref, wo_ref, g1_ref, nw_ref, sh_ref, sc_ref, wr_ref, br_ref,
                 x1_ref, hp_ref, ri_ref, rw_ref, cnt_ref, run_s):
    i = pl.program_id(0)
    tm = x_ref.shape[0]
    sub = min(tm, POST_SUB)
    subs = [slice(r0, r0 + sub) for r0 in range(0, tm, sub)]

    @pl.when(i == 0)
    def _():
        run_s[...] = jnp.zeros_like(run_s)

    lane = lax.broadcasted_iota(I32, (sub, LANES), 1).astype(F32)
    r = lax.broadcasted_iota(I32, (sub, sub), 0)
    c = lax.broadcasted_iota(I32, (sub, sub), 1)
    strict = jnp.where(r > c, 1.0, 0.0).astype(BF16)

    y = [jnp.dot(o_ref[rows, :], wo_ref[...], preferred_element_type=F32) for rows in subs]
    logits = []
    for rows, y_s in zip(subs, y):
        x1 = x_ref[rows, :] + g1_ref[...] * y_s
        x1_ref[rows, :] = x1
        h2 = _norm_mod(x1, nw_ref[...], sc_ref[...], sh_ref[...])
        hp_ref[rows, :] = _pack_halves(h2)
        logits.append(jnp.dot(h2.astype(BF16), wr_ref[...], preferred_element_type=F32) + br_ref[...])

    picks = []
    for work in logits:
        idx, val = [], []
        for _ in range(TOP_K):
            mx = jnp.max(work, axis=-1, keepdims=True)
            ix = jnp.min(jnp.where(work == mx, lane, float(LANES)), axis=-1, keepdims=True)
            idx.append(ix)
            val.append(mx)
            work = jnp.where(lane == ix, -jnp.inf, work)
        ex = [jnp.exp(v - val[0]) for v in val]
        den = ex[0] + ex[1] + ex[2] + ex[3]
        hot = jnp.zeros((sub, LANES), F32)
        for ix in idx:
            hot = hot + jnp.where(lane == ix, 1.0, 0.0)
        picks.append((idx, [e / den for e in ex], hot))

    inside = [jnp.dot(strict, hot.astype(BF16), preferred_element_type=F32) for _, _, hot in picks]
    run = run_s[...]
    for rows, (idx, wts, hot), before in zip(subs, picks, inside):
        before = before + run
        run = run + jnp.sum(hot, axis=0, keepdims=True)
        ranks = [jnp.sum(jnp.where(lane == ix, before, 0.0), axis=-1, keepdims=True) for ix in idx]
        ri = jnp.zeros((sub, LANES), F32)
        rw = jnp.zeros((sub, LANES), F32)
        for k in range(TOP_K):
            ri = jnp.where(lane == float(k), idx[k], ri)
            ri = jnp.where(lane == float(TOP_K + k), ranks[k], ri)
            rw = jnp.where(lane == float(k), wts[k], rw)
        ri_ref[:, rows] = jnp.transpose(ri)[:2 * TOP_K, :].astype(I32)
        rw_ref[rows, :] = rw
    run_s[...] = run
    cnt_ref[...] = run.astype(I32)


def _post_mixer(o, x, wo, mod, nw2, wr, br, seq, tm, row_off=0):
    t, d = o.shape
    per_seq = seq // tm
    tile_off = row_off // tm
    mod_spec = lambda j: pl.BlockSpec((None, 1, d), lambda i: (i // per_seq, 0, j))
    row = lambda w: pl.BlockSpec((tm, w), lambda i: (i, 0))
    return pl.pallas_call(
        _post_kernel,
        grid=(t // tm,),
        in_specs=[
            row(d), pl.BlockSpec((tm, d), lambda i: (i + tile_off, 0)),
            _resident((d, d), lambda i: (0, 0)),
            mod_spec(2),
            _resident((1, d), lambda i: (0, 0)),
            mod_spec(3), mod_spec(4),
            _resident((d, LANES), lambda i: (0, 0)),
            _resident((1, LANES), lambda i: (0, 0)),
        ],
        out_specs=[row(d), row(d // 2), pl.BlockSpec((2 * TOP_K, tm), lambda i: (0, i)), row(LANES),
                   pl.BlockSpec((1, LANES), lambda i: (0, 0))],
        out_shape=[
            jax.ShapeDtypeStruct((t, d), F32),
            jax.ShapeDtypeStruct((t, d // 2), U32),
            jax.ShapeDtypeStruct((2 * TOP_K, t), I32),
            jax.ShapeDtypeStruct((t, LANES), F32),
            jax.ShapeDtypeStruct((1, LANES), I32),
        ],
        scratch_shapes=[pltpu.VMEM((1, LANES), F32)],
        compiler_params=_cparams("arbitrary"),
        name="post_mixer",
    )(o, x, wo, mod, nw2, mod, mod, wr, br)


def _sc_mesh():
    return plsc.VectorSubcoreMesh(core_axis_name="c", subcore_axis_name="s")


def _sc_scatter_rows(x, posf, n_rows):
    t, w = x.shape
    workers = SC_CORES * SC_SUBCORES
    per_w = t // workers

    @functools.partial(
        pl.kernel, mesh=_sc_mesh(),
        out_type=jax.ShapeDtypeStruct((n_rows, w), x.dtype),
        scratch_types=[pltpu.VMEM((SC_ROWS,), I32)] * TOP_K
        + [pltpu.VMEM((SC_ROWS, w), x.dtype), pltpu.SemaphoreType.DMA])
    def scatter_kernel(x_hbm, pos_hbm, out_hbm, i0, i1, i2, i3, rows_v, sem):
        idxs = (i0, i1, i2, i3)
        wid = lax.axis_index("s") * SC_CORES + lax.axis_index("c")
        base = wid * per_w

        @pl.loop(0, per_w // SC_ROWS)
        def _(i):
            off = base + i * SC_ROWS
            for k in range(TOP_K):
                pltpu.sync_copy(pos_hbm.at[pl.ds(k * t + off, SC_ROWS)], idxs[k])
            pltpu.sync_copy(x_hbm.at[pl.ds(off, SC_ROWS)], rows_v)
            copies = [pltpu.async_copy(rows_v, out_hbm.at[idxs[k]], sem) for k in range(TOP_K)]
            for cp in copies:
                cp.wait()

    return scatter_kernel(x, posf)


def _sc_gather_rows(src, posf):
    n = posf.shape[0]
    w = src.shape[1]
    workers = SC_CORES * SC_SUBCORES
    per_w = n // workers

    @functools.partial(
        pl.kernel, mesh=_sc_mesh(),
        out_type=jax.ShapeDtypeStruct((n, w), src.dtype),
        scratch_types=[pltpu.VMEM((SC_ROWS,), I32), pltpu.VMEM((SC_ROWS, w), src.dtype),
                       pltpu.SemaphoreType.DMA])
    def gather_kernel(src_hbm, pos_hbm, out_hbm, idx_v, rows_v, sem):
        wid = lax.axis_index("s") * SC_CORES + lax.axis_index("c")
        base = wid * per_w

        @pl.loop(0, per_w // SC_ROWS)
        def _(i):
            off = base + i * SC_ROWS
            pltpu.sync_copy(pos_hbm.at[pl.ds(off, SC_ROWS)], idx_v)
            pltpu.async_copy(src_hbm.at[idx_v], rows_v, sem).wait()
            pltpu.sync_copy(rows_v, out_hbm.at[pl.ds(off, SC_ROWS)])

    return gather_kernel(src, posf)


GU_BLOCK = 2 * LANES
FF_CHUNK = 2 * LANES


def _deinterleave_matrix():
    sel = np.zeros((GU_BLOCK, GU_BLOCK), np.float32)
    j = np.arange(LANES)
    sel[2 * j, j] = 1.0
    sel[2 * j + 1, LANES + j] = 1.0
    return jnp.asarray(sel, BF16)


def _expert_kernel(be_ref, nu_ref, nxt_ref, x_ref, wgu_hbm, bg_ref, bu_ref, wd_hbm, bd_ref, sel_ref,
                   y_ref, wgu_f, wd_f, wgu_s, wd_s, sems, *, layer):
    i = pl.program_id(0)
    d, f2 = wgu_s.shape
    f = f2 // 2
    e_cur = be_ref[i]
    new_expert = jnp.logical_or(i == 0, e_cur != be_ref[jnp.maximum(i - 1, 0)])

    def weight_copies(e):
        return (pltpu.make_async_copy(wgu_hbm.at[layer, e], wgu_f, sems.at[0]),
                pltpu.make_async_copy(wd_hbm.at[layer, e], wd_f, sems.at[1]))

    @pl.when(jnp.logical_and(new_expert, i < nu_ref[0]))
    def _():
        @pl.when(i == 0)
        def _():
            for cp in weight_copies(e_cur):
                cp.start()

        for cp in weight_copies(e_cur):
            cp.wait()
        for c0 in range(0, f2, GU_BLOCK):
            blk = wgu_f[:, c0:c0 + GU_BLOCK].astype(BF16)
            wgu_s[:, c0:c0 + GU_BLOCK] = jnp.dot(
                blk, sel_ref[...], preferred_element_type=F32).astype(BF16)
        wd_s[...] = wd_f[...].astype(BF16)

        @pl.when(nxt_ref[e_cur] >= 0)
        def _():
            for cp in weight_copies(nxt_ref[e_cur]):
                cp.start()

    @pl.when(i < nu_ref[0])
    def _():
        half = x_ref.shape[1]
        lo, hi = _unpack_halves(x_ref[...])
        lo = lo.astype(BF16)
        hi = hi.astype(BF16)

        def hidden(c0):
            cols = slice(2 * c0, 2 * (c0 + FF_CHUNK))
            gu = (jnp.dot(lo, wgu_s[:half, cols], preferred_element_type=F32)
                  + jnp.dot(hi, wgu_s[half:, cols], preferred_element_type=F32))
            parts = []
            for b0 in range(0, FF_CHUNK, LANES):
                gate = gu[:, 2 * b0:2 * b0 + LANES] + bg_ref[0, :, c0 + b0:c0 + b0 + LANES]
                up = gu[:, 2 * b0 + LANES:2 * b0 + 2 * LANES] + bu_ref[0, :, c0 + b0:c0 + b0 + LANES]
                gate = jnp.minimum(gate, SWIGLU_LIMIT)
                up = jnp.clip(up, -SWIGLU_LIMIT, SWIGLU_LIMIT)
                parts.append((up + 1.0) * gate * _sigmoid(SWIGLU_ALPHA * gate))
            return jnp.concatenate(parts, axis=1).astype(BF16)

        chunks = list(range(0, f, FF_CHUNK))
        y = None
        h_next = hidden(chunks[0])
        for n, c0 in enumerate(chunks):
            h = h_next
            if n + 1 < len(chunks):
                h_next = hidden(chunks[n + 1])
            part = jnp.dot(h, wd_s[c0:c0 + FF_CHUNK, :], preferred_element_type=F32)
            y = part if y is None else y + part
        y_ref[...] = _pack_halves(y + bd_ref[0])

    @pl.when(i >= nu_ref[0])
    def _():
        y_ref[...] = jnp.zeros_like(y_ref)


def _expert_ffn(xs, blk_e, n_used, nxt_e, w_gu, bg, bu, w_down, bd, rb, layer):
    rows, half = xs.shape
    _, e, d, f2 = w_gu.shape
    f = f2 // 2
    n_blocks = rows // rb
    wspec = lambda shape: pl.BlockSpec(shape, lambda i, be, nu, nx: (be[i], 0, 0))
    return pl.pallas_call(
        functools.partial(_expert_kernel, layer=layer),
        grid_spec=pltpu.PrefetchScalarGridSpec(
            num_scalar_prefetch=3,
            grid=(n_blocks,),
            in_specs=[
                pl.BlockSpec((rb, half), lambda i, be, nu, nx: (i, 0)),
                pl.BlockSpec(memory_space=pl.ANY),
                wspec((1, 1, f)), wspec((1, 1, f)),
                pl.BlockSpec(memory_space=pl.ANY),
                wspec((1, 1, d)),
                _resident((GU_BLOCK, GU_BLOCK), lambda i, be, nu, nx: (0, 0)),
            ],
            out_specs=pl.BlockSpec((rb, d // 2), lambda i, be, nu, nx: (i, 0)),
            scratch_shapes=[pltpu.VMEM((d, f2), F32), pltpu.VMEM((f, d), F32),
                            pltpu.VMEM((d, f2), BF16), pltpu.VMEM((f, d), BF16),
                            pltpu.SemaphoreType.DMA((2,))],
        ),
        out_shape=jax.ShapeDtypeStruct((rows, d // 2), U32),
        compiler_params=_cparams("arbitrary"),
        name="expert_ffn",
    )(blk_e, n_used, nxt_e, xs, w_gu, bg, bu, w_down, bd, _deinterleave_matrix())


def _combine_kernel(x_ref, y0_ref, y1_ref, y2_ref, y3_ref, rw_ref, g2_ref, *dst_and_out):
    o_ref = dst_and_out[-1]
    half = y0_ref.shape[1]
    lo, hi = _moe_mix(x_ref, (y0_ref, y1_ref, y2_ref, y3_ref), rw_ref, g2_ref)
    o_ref[:, :half] = lo
    o_ref[:, half:] = hi


def _combine(x1, yt, rw, mod, seq, tm, out_rows=None, row_off=0, dst=None):
    t, d = x1.shape
    per_seq = seq // tm
    nt = t // tm
    tile_off = row_off // tm
    yspec = lambda k: pl.BlockSpec((tm, d // 2), lambda i: (k * nt + i, 0))
    in_specs = [
        pl.BlockSpec((tm, d), lambda i: (i, 0)),
        yspec(0), yspec(1), yspec(2), yspec(3),
        pl.BlockSpec((tm, LANES), lambda i: (i, 0)),
        pl.BlockSpec((None, 1, d), lambda i: (i // per_seq, 0, 5)),
    ]
    args = [x1, yt, yt, yt, yt, rw, mod]
    aliases = {}
    if dst is not None:
        in_specs.append(pl.BlockSpec(memory_space=pl.ANY))
        args.append(dst)
        aliases = {len(args) - 1: 0}
    return pl.pallas_call(
        _combine_kernel,
        grid=(nt,),
        in_specs=in_specs,
        out_specs=pl.BlockSpec((tm, d), lambda i: (i + tile_off, 0)),
        out_shape=jax.ShapeDtypeStruct((t if out_rows is None else out_rows, d), F32),
        input_output_aliases=aliases,
        compiler_params=_cparams("parallel"),
        name="moe_combine",
    )(*args)


POST_ROWS = 1024
POST_SUB = 256


def _row_tile(seq):
    return min(seq, 512)


def _moe_block_rows(t):
    return min(512, t * TOP_K // 8)


def kernel(x, c, fox_w_in, fox_b_f, fox_q_norm, fox_k_norm, fox_w_out, hgrn_w_in, hgrn_lb, hgrn_g_norm, hgrn_w_out, ada_w, ada_b, norm1_w, norm2_w, router_w, router_b, exp_w_gu, exp_b_gu, exp_w_down, exp_b_down):
    bsz, seq, d = x.shape
    t = bsz * seq
    depth = ada_w.shape[0]
    n_exp = router_w.shape[-1]
    f_dim = exp_w_down.shape[2]
    h_fox = fox_w_in.shape[-1] - 4 * d
    dh = d // h_fox
    tm = _row_tile(seq)
    n_split = 2 if bsz % 2 == 0 else 1
    bh, th = bsz // n_split, t // n_split
    rb = _moe_block_rows(th)
    n_blocks = (th * TOP_K) // rb + n_exp
    hgrn_hpb = 4 if d % (4 * LANES) == 0 else 2

    mod_all = _adaln_mod(c, ada_w, ada_b).reshape(depth, bsz, 1, -1)
    lb_soft = jax.nn.softmax(hgrn_lb.astype(F32), axis=0)
    lower = jnp.cumsum(lb_soft, axis=0) - lb_soft[0]
    experts = jnp.arange(n_exp, dtype=I32)

    xf = x.reshape(t, d)
    parts = [xf] * n_split
    for i in range(depth):
        j = i // 2
        nw1 = norm1_w[i].reshape(1, d)
        if i % 2 == 0:
            w_in = fox_w_in[j]
            w_main = w_in[:, :4 * d].astype(BF16)
            w_f = jnp.pad(w_in[:, 4 * d:], ((0, 0), (0, LANES - h_fox))).astype(BF16)
            b_f = jnp.pad(fox_b_f[j], (0, LANES - h_fox)).reshape(1, LANES)
            qw = jnp.tile(fox_q_norm[j], LANES // dh).reshape(1, LANES)
            kw = jnp.tile(fox_k_norm[j], LANES // dh).reshape(1, LANES)
            w_out = fox_w_out[j].astype(BF16)
        else:
            w_h = hgrn_w_in[j]
            w_main = jnp.concatenate([w_h[:, :d], w_h[:, 2 * d:]], axis=1).astype(BF16)
            w_f = w_h[:, d:2 * d].astype(BF16)
            b_f = jnp.zeros((1, d), F32)
            w_out = hgrn_w_out[j].astype(BF16)
        wr = jnp.pad(router_w[i], ((0, 0), (0, LANES - n_exp))).astype(BF16)
        br = jnp.pad(router_b[i], (0, LANES - n_exp), constant_values=MASK_VALUE).reshape(1, LANES)
        b_gu = exp_b_gu[i].reshape(n_exp, 1, f_dim, 2)

        def front(s, src):
            mod = mod_all[i, s * bh:(s + 1) * bh]
            off = s * th if i == 0 else 0
            moe = None
            xp = src
            if isinstance(src, tuple):
                xp, moe = src[0], src[1:]
            if i % 2 == 0:
                outs = _in_proj(xp, nw1, mod, w_main, BF16, seq, tm, w_f, b_f,
                                rows=th, row_off=off, moe=moe, name="in_proj_fox", col_blocks=True)
                o = _fox_attention(outs[0], outs[1], qw, kw, seq, d, dh)
            else:
                outs = _in_proj(xp, nw1, mod, w_main, BF16, seq, tm, w_f, b_f,
                                rows=th, row_off=off, moe=moe, name="in_proj_hgrn")
                o = _hgrn_mixer(outs[0], outs[1], lower[j].reshape(1, d),
                                hgrn_g_norm[j].reshape(1, HGRN_DIM), seq, d, hgrn_hpb)
            if moe is not None:
                xp = outs[-1]
            x1, hp, ri, rw, cnt = _post_mixer(o, xp, w_out, mod, norm2_w[i].reshape(1, d), wr, br,
                                              seq, min(seq, POST_ROWS), row_off=off)
            counts = cnt[0, :n_exp]
            padded = (counts + rb - 1) // rb * rb
            pad_end = jnp.cumsum(padded)
            pad_start = pad_end - padded
            top_idx, rank = ri[:TOP_K], ri[TOP_K:]
            start = jnp.sum(jnp.where(top_idx[None] == experts[:, None, None],
                                      pad_start[:, None, None], 0), axis=0)
            posf = (start + rank).reshape(-1).astype(I32)
            blk_lo = jnp.arange(n_blocks, dtype=I32) * rb
            blk_e = jnp.minimum(jnp.sum(pad_end[None, :] <= blk_lo[:, None], axis=1),
                                n_exp - 1).astype(I32)
            n_used = (pad_end[-1] // rb).astype(I32).reshape(1)
            later = jnp.where((experts[None, :] > experts[:, None]) & (padded[None, :] > 0),
                              experts[None, :], n_exp)
            nxt_e = jnp.min(later, axis=1)
            nxt_e = jnp.where(nxt_e >= n_exp, -1, nxt_e).astype(I32)
            return dict(x1=x1, hp=hp, rw=rw, posf=posf, blk_e=blk_e, n_used=n_used, nxt_e=nxt_e,
                        mod=mod)

        def experts_of(st, xs):
            return _expert_ffn(xs, st["blk_e"], st["n_used"], st["nxt_e"], exp_w_gu,
                               b_gu[..., 0], b_gu[..., 1], exp_w_down,
                               exp_b_down[i].reshape(n_exp, 1, d), rb, i)

        def deferred(st, yt):
            return (st["x1"], yt, st["rw"], st["mod"])

        if n_split == 1:
            st = front(0, parts[0])
            xs = _sc_scatter_rows(st["hp"], st["posf"], n_blocks * rb)
            parts[0] = deferred(st, _sc_gather_rows(experts_of(st, xs), st["posf"]))
            continue

        tie = lax.optimization_barrier
        sa = front(0, parts[0])
        xs_a = _sc_scatter_rows(sa["hp"], sa["posf"], n_blocks * rb)
        src_b = parts[1]
        if isinstance(src_b, tuple):
            yt_b, _ = tie((src_b[1], sa["hp"]))
            src_b = (src_b[0], yt_b) + src_b[2:]
        else:
            src_b, _ = tie((src_b, sa["hp"]))
        sb = front(1, src_b)
        xs_a, _ = tie((xs_a, sb["hp"]))
        xs_b = _sc_scatter_rows(sb["hp"], sb["posf"], n_blocks * rb)
        ys_a = experts_of(sa, xs_a)
        xs_b, _ = tie((xs_b, ys_a))
        yt_a = _sc_gather_rows(ys_a, sa["posf"])
        ys_b = experts_of(sb, xs_b)
        yt_a, _ = tie((yt_a, ys_b))
        yt_b = _sc_gather_rows(ys_b, sb["posf"])
        parts = [deferred(sa, yt_a), deferred(sb, yt_b)]

    out = None
    for s_i, (x1, yt, rw, mod) in enumerate(parts):
        if n_split == 1:
            out = _combine(x1, yt, rw, mod, seq, tm)
        else:
            out = _combine(x1, yt, rw, mod, seq, tm, out_rows=t, row_off=s_i * th, dst=out)
    return out.reshape(bsz, seq, d)
```

```python
import functools

import jax
import jax.numpy as jnp
import numpy as np
from jax import lax
from jax.experimental import pallas as pl
from jax.experimental.pallas import tpu as pltpu
from jax.experimental.pallas import tpu_sc as plsc

F32 = jnp.float32
BF16 = jnp.bfloat16
U32 = jnp.uint32
I32 = jnp.int32

EPS = 1e-6
MASK_VALUE = -1e30
TOP_K = 4
HGRN_DIM = 128
HGRN_CHUNK = 32
SWIGLU_LIMIT = 7.0
SWIGLU_ALPHA = 1.702
LANES = 128
SUBLANES = 8
SC_CORES = 2
SC_SUBCORES = 16
SC_ROWS = 64
VMEM_LIMIT = 56 * 1024 * 1024
HIGHEST = lax.Precision.HIGHEST


def _cparams(*sem):
    return pltpu.CompilerParams(dimension_semantics=sem, vmem_limit_bytes=VMEM_LIMIT)


def _resident(shape, index_map):
    return pl.BlockSpec(shape, index_map, pipeline_mode=pl.Buffered(1))


def _sigmoid(x):
    return 1.0 / (1.0 + jnp.exp(-x))


def _bf16_bits(x):
    return lax.bitcast_convert_type(x.astype(BF16).astype(F32), U32)


def _pack_halves(x):
    n = x.shape[1] // 2
    lo = _bf16_bits(x[:, :n])
    hi = _bf16_bits(x[:, n:])
    return (hi & jnp.uint32(0xFFFF0000)) | (lo >> 16)


def _unpack_halves(w):
    lo = lax.bitcast_convert_type(w << 16, F32)
    hi = lax.bitcast_convert_type(w & jnp.uint32(0xFFFF0000), F32)
    return lo, hi


def _mod_kernel(c_ref, w_ref, b_ref, o_ref):
    c = c_ref[...]
    ca = c * _sigmoid(c)
    o_ref[0] = jnp.dot(ca, w_ref[0], preferred_element_type=F32, precision=HIGHEST) + b_ref[0]


def _adaln_mod(c, ada_w, ada_b):
    depth, d, n6 = ada_w.shape
    b = c.shape[0]
    nmod = n6 // d
    return pl.pallas_call(
        _mod_kernel,
        grid=(depth, nmod),
        in_specs=[
            pl.BlockSpec((b, d), lambda i, j: (0, 0)),
            pl.BlockSpec((1, d, d), lambda i, j: (i, 0, j)),
            pl.BlockSpec((1, 1, d), lambda i, j: (i, 0, j)),
        ],
        out_specs=pl.BlockSpec((1, b, d), lambda i, j: (i, 0, j)),
        out_shape=jax.ShapeDtypeStruct((depth, b, n6), F32),
        compiler_params=_cparams("parallel", "parallel"),
        name="adaln_mod",
    )(c, ada_w, ada_b.reshape(depth, 1, n6))


def _norm_mod(x, nw, sc, sh):
    ms = jnp.mean(x * x, axis=-1, keepdims=True)
    return x * lax.rsqrt(ms + EPS) * nw * (1.0 + sc) + sh


def _moe_mix(x_ref, y_refs, rw_ref, g2_ref, rows=slice(None)):
    rw = rw_ref[rows, :]
    lane = lax.broadcasted_iota(I32, rw.shape, 1)
    half = y_refs[0].shape[1]
    acc_lo = acc_hi = None
    for k, y_ref in enumerate(y_refs):
        wk = jnp.sum(jnp.where(lane == k, rw, 0.0), axis=-1, keepdims=True)
        lo, hi = _unpack_halves(y_ref[rows, :])
        acc_lo = wk * lo if acc_lo is None else acc_lo + wk * lo
        acc_hi = wk * hi if acc_hi is None else acc_hi + wk * hi
    g2 = g2_ref[...]
    return (x_ref[rows, :half] + g2[:, :half] * acc_lo, x_ref[rows, half:] + g2[:, half:] * acc_hi)


def _in_kernel(*refs, n_chunk, with_f, with_moe, sub, silu_cols):
    n_in = 1 + (TOP_K + 2 if with_moe else 0) + 4 + (2 if with_f else 0) + (1 if silu_cols else 0)
    ins, outs = refs[:n_in], refs[n_in:]
    x_ref = ins[0]
    pos = 1 + (TOP_K + 2 if with_moe else 0)
    nw_ref, sh_ref, sc_ref, w_ref = ins[pos:pos + 4]
    o_ref = outs[0]
    if with_f:
        wf_ref, bf_ref = ins[pos + 4:pos + 6]
        of_ref = outs[1]
    n = o_ref.shape[1]
    for r0 in range(0, x_ref.shape[0], sub):
        rows = slice(r0, r0 + sub)
        if with_moe:
            lo, hi = _moe_mix(x_ref, ins[1:1 + TOP_K], ins[1 + TOP_K], ins[2 + TOP_K], rows)
            x = jnp.concatenate([lo, hi], axis=1)
            outs[-1][rows, :] = x
        else:
            x = x_ref[rows, :]
        h = _norm_mod(x, nw_ref[...], sc_ref[...], sh_ref[...]).astype(BF16)
        for c0 in range(0, n, n_chunk):
            res = jnp.dot(h, w_ref[:, c0:c0 + n_chunk], preferred_element_type=F32)
            if c0 < silu_cols:
                res = res * _sigmoid(res)
            o_ref[rows, c0:c0 + n_chunk] = res.astype(o_ref.dtype)
        if with_f:
            second = jnp.dot(h, wf_ref[...], preferred_element_type=F32) + bf_ref[...]
            if silu_cols:
                lb = ins[pos + 6][...]
                second = lb + (1.0 - lb) * _sigmoid(second)
            of_ref[rows, :] = second


def _in_proj(x, nw, mod, w, out_dtype, seq, tm, wf=None, bf=None, rows=None, row_off=0, moe=None,
             name="in_proj", gate_lb=None):
    d = x.shape[1]
    t = x.shape[0] if rows is None else rows
    n = w.shape[1]
    per_seq = seq // tm
    tile_off = row_off // tm
    nt = t // tm
    with_f = wf is not None
    mod_spec = lambda j: pl.BlockSpec((None, 1, d), lambda i: (i // per_seq, 0, j))
    in_specs = [pl.BlockSpec((tm, d), lambda i: (i + tile_off, 0))]
    args = [x]
    if moe is not None:
        yt, rw, mod_prev = moe
        in_specs += [pl.BlockSpec((tm, d // 2), lambda i, k=k: (k * nt + i, 0)) for k in range(TOP_K)]
        in_specs += [pl.BlockSpec((tm, LANES), lambda i: (i, 0)), mod_spec(5)]
        args += [yt] * TOP_K + [rw, mod_prev]
    in_specs += [_resident((1, d), lambda i: (0, 0)), mod_spec(0), mod_spec(1),
                 _resident((d, n), lambda i: (0, 0))]
    args += [nw, mod, mod, w]
    out_specs = [pl.BlockSpec((tm, n), lambda i: (i, 0))]
    out_shape = [jax.ShapeDtypeStruct((t, n), out_dtype)]
    if with_f:
        nf = wf.shape[1]
        in_specs += [_resident((d, nf), lambda i: (0, 0)), _resident((1, nf), lambda i: (0, 0))]
        out_specs.append(pl.BlockSpec((tm, nf), lambda i: (i, 0)))
        out_shape.append(jax.ShapeDtypeStruct((t, nf), F32))
        args += [wf, bf]
        if gate_lb is not None:
            in_specs.append(_resident((1, nf), lambda i: (0, 0)))
            args.append(gate_lb)
    silu_cols = d if gate_lb is not None else 0
    if moe is not None:
        out_specs.append(pl.BlockSpec((tm, d), lambda i: (i, 0)))
        out_shape.append(jax.ShapeDtypeStruct((t, d), F32))
    return pl.pallas_call(
        functools.partial(_in_kernel, n_chunk=d if silu_cols else min(n, 1024), with_f=with_f,
                          with_moe=moe is not None, sub=min(tm, 256), silu_cols=silu_cols),
        grid=(t // tm,),
        in_specs=in_specs,
        out_specs=out_specs,
        out_shape=out_shape,
        compiler_params=_cparams("parallel"),
        name=name,
    )(*args)


LOG2E = 1.4426950408889634
EXP2_SPAN = 120.0
SCORE_MARGIN = 1.01
BIAS_LANES = 16


def _split3(c):
    c1 = c.astype(BF16).astype(F32)
    r = c - c1
    c2 = r.astype(BF16).astype(F32)
    return c1, c2, r - c2


def _bias_layout(n_pairs):
    route =np.zeros((3, 3 * LANES, LANES), np.float32)
    const = np.zeros((3, 1, LANES), np.float32)
    for p in range(n_pairs):
        base = p * BIAS_LANES
        for i in range(3):
            const[0, 0, base + i] = 1.0
            for h in range(2):
                route[0, i * LANES + 2 * p + h, base + 3 + 3 * h + i] = -1.0
                route[1 + h, i * LANES + 2 * p + h, base + i] = 1.0
                const[1 + h, 0, base + 3 + 3 * h + i] = 1.0
    return jnp.asarray(route, BF16), jnp.asarray(const, F32)


def _cum_kernel(f_ref, ref_ref, route_ref, const_ref, o_ref, *, rows):
    s = f_ref.shape[0]
    r = lax.broadcasted_iota(I32, (rows, rows), 0)
    c = lax.broadcasted_iota(I32, (rows, rows), 1)
    tri = jnp.where(r >= c, 1.0, 0.0).astype(BF16)
    carry = jnp.zeros((1, LANES), F32)
    for r0 in range(0, s, rows):
        x = f_ref[r0:r0 + rows, :]
        lf = jnp.minimum(x, 0.0) - jnp.log(1.0 + jnp.exp(-jnp.abs(x)))
        cs = carry + sum(jnp.dot(tri, part.astype(BF16), preferred_element_type=F32)
                         for part in _split3(lf))
        carry = cs[rows - 1:rows, :]
        c2 = cs * LOG2E
        k_terms = jnp.concatenate(_split3(c2), axis=1).astype(BF16)
        q_terms = jnp.concatenate(_split3(c2 - ref_ref[...]), axis=1).astype(BF16)
        for j, terms in enumerate((k_terms, q_terms, q_terms)):
            routed = jnp.dot(terms, route_ref[j], preferred_element_type=F32) + const_ref[j]
            o_ref[r0:r0 + rows, j * LANES:(j + 1) * LANES] = routed.astype(o_ref.dtype)


def _fox_bias(flogit, ref, seq, n_pairs):
    t = flogit.shape[0]
    route, const = _bias_layout(n_pairs)
    return pl.pallas_call(
        functools.partial(_cum_kernel, rows=min(seq, 256)),
        grid=(t // seq,),
        in_specs=[pl.BlockSpec((seq, LANES), lambda b: (b, 0)),
                  pl.BlockSpec((1, LANES), lambda b: (0, 0)),
                  pl.BlockSpec((3, 3 * LANES, LANES), lambda b: (0, 0, 0)),
                  pl.BlockSpec((3, 1, LANES), lambda b: (0, 0, 0))],
        out_specs=pl.BlockSpec((seq, 3 * LANES), lambda b: (b, 0)),
        out_shape=jax.ShapeDtypeStruct((t, 3 * LANES), BF16),
        compiler_params=_cparams("parallel"),
        name="fox_bias",
    )(flogit, ref, route, const)


def _attn_kernel(q_ref, k_ref, v_ref, g_ref, bias_ref, qw_ref, kw_ref, o_ref, qs_s, ka_s, *,
                 dh, tq, bounded):
    s_len = q_ref.shape[0]
    nq = s_len // tq
    hp = pl.program_id(1)
    lane = lax.broadcasted_iota(I32, (1, LANES), 1)
    first = lane < dh

    def headnorm(z, w):
        z2 = z * z
        s0 = jnp.sum(jnp.where(first, z2, 0.0), axis=-1, keepdims=True)
        s1 = jnp.sum(jnp.where(first, 0.0, z2), axis=-1, keepdims=True)
        ms = jnp.where(first, s0, s1) * (1.0 / dh)
        return z * lax.rsqrt(ms + EPS) * w

    qn = headnorm(q_ref[...].astype(F32), qw_ref[...]) * (dh ** -0.5 * LOG2E)
    kn = headnorm(k_ref[...].astype(F32), kw_ref[...])
    mine = (lane // BIAS_LANES) == hp
    ka_s[:, :LANES] = kn.astype(BF16)
    ka_s[:, LANES:] = bias_ref[:, :LANES]
    q_lo = [jnp.where(first, qn, 0.0), jnp.where(first, 0.0, qn)]
    hq = tq // 2
    for qi in range(nq):
        for part_i in range(2):
            rows = slice(qi * tq + part_i * hq, qi * tq + (part_i + 1) * hq)
            for h in range(2):
                dst = slice((2 * part_i + h) * hq, (2 * part_i + h + 1) * hq)
                q_hi = bias_ref[rows, (1 + h) * LANES:(2 + h) * LANES].astype(F32)
                qs_s[qi, dst, :LANES] = q_lo[h][rows, :].astype(BF16)
                qs_s[qi, dst, LANES:] = jnp.where(mine, q_hi, 0.0).astype(BF16)

    nt = (((1,), (1,)), ((), ()))

    def causal(s, first_token):
        rr = lax.broadcasted_iota(I32, s.shape, 0)
        rr = jnp.where(rr >= hq, rr - hq, rr) + first_token
        cc = lax.broadcasted_iota(I32, s.shape, 1)
        return jnp.where(rr >= cc, s, MASK_VALUE)

    def finish(qi, halves):
        for part_i, (acc, l) in enumerate(halves):
            o2 = acc / l
            o = jnp.where(first, o2[:hq], o2[hq:])
            rows = slice(qi * tq + part_i * hq, qi * tq + (part_i + 1) * hq)
            g = g_ref[rows, :].astype(F32)
            o_ref[rows, :] = (o * _sigmoid(g)).astype(o_ref.dtype)

    def lane_blocks(e):
        cols = e[:, :LANES]
        for c0 in range(LANES, e.shape[1], LANES):
            cols = cols + e[:, c0:c0 + LANES]
        return cols

    if bounded:
        for qi in range(nq):
            acc, part = [None, None], [None, None]

            def add(i, pv, cols):
                acc[i] = pv if acc[i] is None else acc[i] + pv
                part[i] = cols if part[i] is None else part[i] + cols

            for t in range(qi):
                s = lax.dot_general(qs_s[qi], ka_s[t * tq:(t + 1) * tq, :], nt,
                                    preferred_element_type=F32)
                e = jnp.exp2(s)
                pv = jnp.dot(e.astype(BF16), v_ref[t * tq:(t + 1) * tq, :],
                             preferred_element_type=F32)
                cols = lane_blocks(e)
                add(0, pv[:tq], cols[:tq])
                add(1, pv[tq:], cols[tq:])
            k0 = qi * tq
            for i, n_keys in ((0, hq), (1, tq)):
                s = lax.dot_general(qs_s[qi, i * tq:(i + 1) * tq, :], ka_s[k0:k0 + n_keys, :], nt,
                                    preferred_element_type=F32)
                e = jnp.exp2(causal(s, i * hq))
                pv = jnp.dot(e.astype(BF16), v_ref[k0:k0 + n_keys, :], preferred_element_type=F32)
                add(i, pv, lane_blocks(e))
            finish(qi, [(acc[i], jnp.sum(part[i], axis=-1, keepdims=True)) for i in range(2)])
        return

    def causal_tile(s):
        return jnp.concatenate([causal(s[:tq], 0), causal(s[tq:], hq)], axis=0)

    def scores(q, off):
        return lax.dot_general(q, ka_s[pl.ds(off, tq), :], nt, preferred_element_type=F32)

    def update(s, off, carry, masked):
        m, l, acc = carry
        if masked:
            s = causal_tile(s)
        m_new = jnp.maximum(m, jnp.max(s, axis=-1, keepdims=True))
        p = jnp.exp2(s - m_new)
        alpha = jnp.exp2(m - m_new)
        l = alpha * l + jnp.sum(p, axis=-1, keepdims=True)
        acc = alpha * acc + jnp.dot(p.astype(BF16), v_ref[pl.ds(off, tq), :],
                                    preferred_element_type=F32)
        return m_new, l, acc

    for qi in range(nq):
        q = qs_s[qi]
        s = scores(q, 0)
        carry = (jnp.full((2 * tq, 1), MASK_VALUE, F32), jnp.zeros((2 * tq, 1), F32),
                 jnp.zeros((2 * tq, LANES), F32))
        if qi > 0:
            def body(j, c, q=q):
                off = pl.multiple_of(j * tq, tq)
                s_next = scores(q, off + tq)
                return (s_next,) + update(c[0], off, c[1:], False)

            s, *carry = lax.fori_loop(0, qi, body, (s,) + carry)
        m, l, acc = update(s, qi * tq, carry, True)
        finish(qi, [(acc[:tq], l[:tq]), (acc[tq:], l[tq:])])


def _fox_attention(qkvg, flogit, qw, kw, seq, d, dh):
    t = qkvg.shape[0]
    nb = d // LANES
    assert 2 * dh == LANES and nb * BIAS_LANES <= LANES
    tq = min(seq, 512)
    col = lambda off: pl.BlockSpec((seq, LANES), lambda b, p: (b, off + p))

    def call(bounded, bias):
        return pl.pallas_call(
            functools.partial(_attn_kernel, dh=dh, tq=tq, bounded=bounded),
            grid=(t // seq, nb),
            in_specs=[
                col(0), col(nb), col(2 * nb), col(3 * nb),
                pl.BlockSpec((seq, 3 * LANES), lambda b, p: (b, 0)),
                pl.BlockSpec((1, LANES), lambda b, p: (0, 0)),
                pl.BlockSpec((1, LANES), lambda b, p: (0, 0)),
            ],
            out_specs=pl.BlockSpec((seq, LANES), lambda b, p: (b, p)),
            out_shape=jax.ShapeDtypeStruct((t, d), BF16),
            scratch_shapes=[pltpu.VMEM((seq // tq, 2 * tq, 2 * LANES), BF16),
                            pltpu.VMEM((seq, 2 * LANES), BF16)],
            compiler_params=_cparams("parallel", "parallel"),
            name="fox_attention" if bounded else "fox_attention_running_max",
        )(qkvg, qkvg, qkvg, qkvg, bias, qw, kw)

    bound = (dh ** 0.5 * LOG2E * SCORE_MARGIN) * jnp.max(jnp.abs(qw)) * jnp.max(jnp.abs(kw)) + 0.1
    fits = bound <= EXP2_SPAN - 4.0
    ref = jnp.where(fits, bound - jnp.maximum(0.0, 2.0 * bound - EXP2_SPAN), 0.0)
    bias = _fox_bias(flogit, jnp.full((1, LANES), ref, F32), seq, nb)
    return lax.cond(fits, lambda: call(True, bias), lambda: call(False, bias))


HGRN_PREP_ROWS = 256
HGRN_LOG2_SPAN = 100.0


def _hgrn_kernel(q_ref, f_ref, v_ref, g_ref, gw_ref, o_ref, qe_s, ke_s, kd_s, dec_s, *,
                 hpb, chunk):
    s_len = q_ref.shape[0]
    n_chunks = s_len // chunk
    nt = (((1,), (1,)), ((), ()))
    gw = gw_ref[...]

    pr = min(HGRN_PREP_ROWS, s_len)
    rr = lax.broadcasted_iota(I32, (pr, pr), 0)
    cc = lax.broadcasted_iota(I32, (pr, pr), 1)
    same = (rr // chunk) == (cc // chunk)
    cum_m = jnp.where(same & (rr >= cc), 1.0, 0.0).astype(BF16)
    tot_m = jnp.where(same, 1.0, 0.0).astype(BF16)
    lowest = jnp.zeros((1, q_ref.shape[1]), F32)
    for r0 in range(0, s_len, pr):
        f = f_ref[r0:r0 + pr, :]
        terms = [t.astype(BF16) for t in _split3(jnp.log2(f))]
        b = sum(jnp.dot(cum_m, t, preferred_element_type=F32) for t in terms)
        tot = sum(jnp.dot(tot_m, t, preferred_element_type=F32) for t in terms)
        k = 1.0 - f
        qe_s[r0:r0 + pr, :] = (q_ref[r0:r0 + pr, :].astype(F32) * jnp.exp2(b)).astype(BF16)
        ke_s[r0:r0 + pr, :] = (k * jnp.exp2(-b)).astype(BF16)
        kd_s[r0:r0 + pr, :] = (k * jnp.exp2(tot - b)).astype(BF16)
        for c0 in range(0, pr, chunk):
            ci = (r0 + c0) // chunk
            dec_s[ci] = jnp.exp2(tot[c0:c0 + SUBLANES, :])
        lowest = jnp.minimum(lowest, jnp.min(tot, axis=0, keepdims=True))
    mild = jnp.min(lowest) >= -HGRN_LOG2_SPAN

    @pl.when(mild)
    def _():
        r32 = lax.broadcasted_iota(I32, (chunk, chunk), 0)
        c32 = lax.broadcasted_iota(I32, (chunk, chunk), 1)
        causal = r32 >= c32
        per_step = 8 if n_chunks % 8 == 0 else 2

        def fast_body(step, states):
            insts = [(ch, hd) for ch in range(per_step) for hd in range(hpb)]
            rows = {ch: pl.ds(pl.multiple_of((step * per_step + ch) * chunk, chunk), chunk)
                    for ch in range(per_step)}
            cols = {hd: slice(hd * LANES, (hd + 1) * LANES) for hd in range(hpb)}
            states = list(states)
            qe = {i: qe_s[rows[i[0]], cols[i[1]]] for i in insts}
            v = {i: v_ref[rows[i[0]], cols[i[1]]].astype(F32) for i in insts}
            a = {i: lax.dot_general(qe[i], ke_s[rows[i[0]], cols[i[1]]], nt,
                                    preferred_element_type=F32) for i in insts}
            upd = {i: jnp.dot(jnp.transpose(v[i]).astype(BF16), kd_s[rows[i[0]], cols[i[1]]],
                              preferred_element_type=F32) for i in insts}
            o = {}
            for ch in range(per_step):
                for hd in range(hpb):
                    i = (ch, hd)
                    o[i] = (jnp.dot(jnp.where(causal, a[i], 0.0).astype(BF16), v[i].astype(BF16),
                                    preferred_element_type=F32)
                            + lax.dot_general(qe[i], states[hd].astype(BF16), nt,
                                              preferred_element_type=F32))
                    dec = dec_s[step * per_step + ch, 0:1, cols[hd]]
                    states[hd] = dec * states[hd] + upd[i]
            for i in insts:
                g = g_ref[rows[i[0]], cols[i[1]]].astype(F32)
                ms = jnp.mean(o[i] * o[i], axis=-1, keepdims=True)
                y = o[i] * lax.rsqrt(ms + EPS) * gw * (g * _sigmoid(g))
                o_ref[rows[i[0]], cols[i[1]]] = y.astype(o_ref.dtype)
            return tuple(states)

        init = tuple(jnp.zeros((LANES, LANES), F32) for _ in range(hpb))
        lax.fori_loop(0, n_chunks // per_step, fast_body, init)

    pair = 2
    groups = chunk // SUBLANES
    off_w = (groups - 1) * chunk
    r = lax.broadcasted_iota(I32, (chunk, chunk), 0)
    c = lax.broadcasted_iota(I32, (chunk, chunk), 1)
    tri = jnp.where(r >= c, 1.0, 0.0).astype(BF16)
    ro = lax.broadcasted_iota(I32, (chunk, off_w), 0) // SUBLANES
    co = lax.broadcasted_iota(I32, (chunk, off_w), 1) // chunk
    keep = ro == co + 1
    sub = lax.broadcasted_iota(I32, (SUBLANES, LANES), 0)
    gw = gw_ref[...]
    nt = (((1,), (1,)), ((), ()))
    grp = lambda a, i: a[i * SUBLANES:(i + 1) * SUBLANES, :]
    zeros8 = jnp.zeros((SUBLANES, LANES), F32)

    def body(step, states):
        insts = [(ch, hd) for ch in range(pair) for hd in range(hpb)]
        rows = {ch: pl.ds(pl.multiple_of((step * pair + ch) * chunk, chunk), chunk)
                for ch in range(pair)}
        cols = {hd: slice(hd * LANES, (hd + 1) * LANES) for hd in range(hpb)}
        st = {}
        for ch, hd in insts:
            f = f_ref[rows[ch], cols[hd]]
            d = dict(v=v_ref[rows[ch], cols[hd]].astype(F32),
                     q=q_ref[rows[ch], cols[hd]].astype(F32), k=1.0 - f)
            lf = jnp.log(f)
            d["b"] = sum(jnp.dot(tri, part.astype(BF16), preferred_element_type=F32)
                         for part in _split3(lf))
            st[ch, hd] = d
        def from_state(ch, hd, state_t):
            d = st[ch, hd]
            qe = (d["q"] * jnp.exp(d["b"])).astype(BF16)
            d["o"] = lax.dot_general(qe, state_t.astype(BF16), nt, preferred_element_type=F32)
        for hd in range(hpb):
            from_state(0, hd, states[hd])
        for ch, hd in insts:
            d = st[ch, hd]
            b, q, k = d["b"], d["q"], d["k"]
            ends = [b[(i + 1) * SUBLANES - 1:(i + 1) * SUBLANES, :] for i in range(groups)]
            kt = [grp(k, i) * jnp.exp(ends[i] - grp(b, i)) for i in range(groups)]
            qh = jnp.concatenate(
                [zeros8] + [grp(q, i) * jnp.exp(grp(b, i) - ends[i - 1]) for i in range(1, groups)],
                axis=0).astype(BF16)
            kh = jnp.concatenate(
                [kt[j] * jnp.exp(ends[i - 1] - ends[j]) if j < i else zeros8
                 for i in range(1, groups) for j in range(groups)], axis=0).astype(BF16)
            d["a"] = lax.dot_general(qh, kh, nt, preferred_element_type=F32)
            d["ends"], d["kt"] = ends, kt
        def state_update(ch, hd, state_t):
            d = st[ch, hd]
            b_end = d["ends"][-1]
            ke = jnp.concatenate([d["kt"][j] * jnp.exp(b_end - d["ends"][j]) for j in range(groups)],
                                 axis=0).astype(BF16)
            upd = jnp.dot(jnp.transpose(d["v"]).astype(BF16), ke, preferred_element_type=F32)
            return jnp.exp(b_end) * state_t + upd
        states = [state_update(0, hd, states[hd]) for hd in range(hpb)]
        for ch, hd in insts:
            d = st[ch, hd]
            vb = d["v"].astype(BF16)
            v3 = jnp.concatenate([vb] * (groups - 1), axis=0)
            d["off"] = jnp.dot(jnp.where(keep, d["a"], 0.0).astype(BF16), v3,
                               preferred_element_type=F32)
        for hd in range(hpb):
            from_state(1, hd, states[hd])
        states = [state_update(1, hd, states[hd]) for hd in range(hpb)]
        for ch, hd in insts:
            d = st[ch, hd]
            o = d["o"] + d["off"]
            o_g = []
            for gi in range(groups):
                qg, kg, vg, bg = grp(d["q"], gi), grp(d["k"], gi), grp(d["v"], gi), grp(d["b"], gi)
                og = grp(o, gi)
                for s in range(SUBLANES):
                    live = sub >= s
                    e = jnp.exp(jnp.where(live, bg - bg[s:s + 1, :], 0.0))
                    p = jnp.where(live, qg * e * kg[s:s + 1, :], 0.0)
                    og = og + jnp.sum(p, axis=-1, keepdims=True) * vg[s:s + 1, :]
                o_g.append(og)
            o = jnp.concatenate(o_g, axis=0)
            g = g_ref[rows[ch], cols[hd]].astype(F32)
            ms = jnp.mean(o * o, axis=-1, keepdims=True)
            y = o * lax.rsqrt(ms + EPS) * gw * (g * _sigmoid(g))
            o_ref[rows[ch], cols[hd]] = y.astype(o_ref.dtype)
        return tuple(states)

    @pl.when(jnp.logical_not(mild))
    def _():
        init = tuple(jnp.zeros((LANES, LANES), F32) for _ in range(hpb))
        lax.fori_loop(0, s_len // (chunk * pair), body, init)


def _hgrn_mixer(qvg, fgate, gw, seq, d, hpb):
    t = qvg.shape[0]
    w = hpb * LANES
    nb = d // w
    col = lambda off: pl.BlockSpec((seq, w), lambda b, p: (b, off + p))
    return pl.pallas_call(
        functools.partial(_hgrn_kernel, hpb=hpb, chunk=HGRN_CHUNK),
        grid=(t // seq, nb),
        in_specs=[
            col(0), col(0), col(nb), col(2 * nb),
            pl.BlockSpec((1, LANES), lambda b, p: (0, 0)),
        ],
        out_specs=pl.BlockSpec((seq, w), lambda b, p: (b, p)),
        out_shape=jax.ShapeDtypeStruct((t, d), BF16),
        scratch_shapes=[pltpu.VMEM((seq, w), BF16)] * 3
        + [pltpu.VMEM((seq // HGRN_CHUNK, SUBLANES, w), F32)],
        compiler_params=_cparams("parallel", "parallel"),
        name="hgrn_mixer",
    )(qvg, fgate, qvg, qvg, gw)


def _post_kernel(o_ref, x_ref, wo_ref, g1_ref, nw_ref, sh_ref, sc_ref, wr_ref, br_ref,
                 x1_ref, hp_ref, ri_ref, rw_ref, cnt_ref, run_s):
    i = pl.program_id(0)
    tm = x_ref.shape[0]
    sub = min(tm, POST_SUB)
    subs = [slice(r0, r0 + sub) for r0 in range(0, tm, sub)]

    @pl.when(i == 0)
    def _():
        run_s[...] = jnp.zeros_like(run_s)

    lane = lax.broadcasted_iota(I32, (sub, LANES), 1).astype(F32)
    r = lax.broadcasted_iota(I32, (sub, sub), 0)
    c = lax.broadcasted_iota(I32, (sub, sub), 1)
    strict = jnp.where(r > c, 1.0, 0.0).astype(BF16)

    y = [jnp.dot(o_ref[rows, :], wo_ref[...], preferred_element_type=F32) for rows in subs]
    logits = []
    for rows, y_s in zip(subs, y):
        x1 = x_ref[rows, :] + g1_ref[...] * y_s
        x1_ref[rows, :] = x1
        h2 = _norm_mod(x1, nw_ref[...], sc_ref[...], sh_ref[...])
        hp_ref[rows, :] = _pack_halves(h2)
        logits.append(jnp.dot(h2.astype(BF16), wr_ref[...], preferred_element_type=F32) + br_ref[...])

    picks = []
    for work in logits:
        idx, val = [], []
        for _ in range(TOP_K):
            mx = jnp.max(work, axis=-1, keepdims=True)
            ix = jnp.min(jnp.where(work == mx, lane, float(LANES)), axis=-1, keepdims=True)
            idx.append(ix)
            val.append(mx)
            work = jnp.where(lane == ix, -jnp.inf, work)
        ex = [jnp.exp(v - val[0]) for v in val]
        den = ex[0] + ex[1] + ex[2] + ex[3]
        hot = jnp.zeros((sub, LANES), F32)
        for ix in idx:
            hot = hot + jnp.where(lane == ix, 1.0, 0.0)
        picks.append((idx, [e / den for e in ex], hot))

    inside = [jnp.dot(strict, hot.astype(BF16), preferred_element_type=F32) for _, _, hot in picks]
    run = run_s[...]
    for rows, (idx, wts, hot), before in zip(subs, picks, inside):
        before = before + run
        run = run + jnp.sum(hot, axis=0, keepdims=True)
        ranks = [jnp.sum(jnp.where(lane == ix, before, 0.0), axis=-1, keepdims=True) for ix in idx]
        ri = jnp.zeros((sub, LANES), F32)
        rw = jnp.zeros((sub, LANES), F32)
        for k in range(TOP_K):
            ri = jnp.where(lane == float(k), idx[k], ri)
            ri = jnp.where(lane == float(TOP_K + k), ranks[k], ri)
            rw = jnp.where(lane == float(k), wts[k], rw)
        ri_ref[:, rows] = jnp.transpose(ri)[:2 * TOP_K, :].astype(I32)
        rw_ref[rows, :] = rw
    run_s[...] = run
    cnt_ref[...] = run.astype(I32)


def _post_mixer(o, x, wo, mod, nw2, wr, br, seq, tm, row_off=0):
    t, d = o.shape
    per_seq = seq // tm
    tile_off = row_off // tm
    mod_spec = lambda j: pl.BlockSpec((None, 1, d), lambda i: (i // per_seq, 0, j))
    row = lambda w: pl.BlockSpec((tm, w), lambda i: (i, 0))
    return pl.pallas_call(
        _post_kernel,
        grid=(t // tm,),
        in_specs=[
            row(d), pl.BlockSpec((tm, d), lambda i: (i + tile_off, 0)),
            _resident((d, d), lambda i: (0, 0)),
            mod_spec(2),
            _resident((1, d), lambda i: (0, 0)),
            mod_spec(3), mod_spec(4),
            _resident((d, LANES), lambda i: (0, 0)),
            _resident((1, LANES), lambda i: (0, 0)),
        ],
        out_specs=[row(d), row(d // 2), pl.BlockSpec((2 * TOP_K, tm), lambda i: (0, i)), row(LANES),
                   pl.BlockSpec((1, LANES), lambda i: (0, 0))],
        out_shape=[
            jax.ShapeDtypeStruct((t, d), F32),
            jax.ShapeDtypeStruct((t, d // 2), U32),
            jax.ShapeDtypeStruct((2 * TOP_K, t), I32),
            jax.ShapeDtypeStruct((t, LANES), F32),
            jax.ShapeDtypeStruct((1, LANES), I32),
        ],
        scratch_shapes=[pltpu.VMEM((1, LANES), F32)],
        compiler_params=_cparams("arbitrary"),
        name="post_mixer",
    )(o, x, wo, mod, nw2, mod, mod, wr, br)


def _sc_mesh():
    return plsc.VectorSubcoreMesh(core_axis_name="c", subcore_axis_name="s")


def _sc_scatter_rows(x, posf, n_rows):
    t, w = x.shape
    workers = SC_CORES * SC_SUBCORES
    per_w = t // workers

    @functools.partial(
        pl.kernel, mesh=_sc_mesh(),
        out_type=jax.ShapeDtypeStruct((n_rows, w), x.dtype),
        scratch_types=[pltpu.VMEM((SC_ROWS,), I32)] * TOP_K
        + [pltpu.VMEM((SC_ROWS, w), x.dtype), pltpu.SemaphoreType.DMA])
    def scatter_kernel(x_hbm, pos_hbm, out_hbm, i0, i1, i2, i3, rows_v, sem):
        idxs = (i0, i1, i2, i3)
        wid = lax.axis_index("s") * SC_CORES + lax.axis_index("c")
        base = wid * per_w

        @pl.loop(0, per_w // SC_ROWS)
        def _(i):
            off = base + i * SC_ROWS
            for k in range(TOP_K):
                pltpu.sync_copy(pos_hbm.at[pl.ds(k * t + off, SC_ROWS)], idxs[k])
            pltpu.sync_copy(x_hbm.at[pl.ds(off, SC_ROWS)], rows_v)
            copies = [pltpu.async_copy(rows_v, out_hbm.at[idxs[k]], sem) for k in range(TOP_K)]
            for cp in copies:
                cp.wait()

    return scatter_kernel(x, posf)


def _sc_gather_rows(src, posf):
    n = posf.shape[0]
    w = src.shape[1]
    workers = SC_CORES * SC_SUBCORES
    per_w = n // workers

    @functools.partial(
        pl.kernel, mesh=_sc_mesh(),
        out_type=jax.ShapeDtypeStruct((n, w), src.dtype),
        scratch_types=[pltpu.VMEM((SC_ROWS,), I32), pltpu.VMEM((SC_ROWS, w), src.dtype),
                       pltpu.SemaphoreType.DMA])
    def gather_kernel(src_hbm, pos_hbm, out_hbm, idx_v, rows_v, sem):
        wid = lax.axis_index("s") * SC_CORES + lax.axis_index("c")
        base = wid * per_w

        @pl.loop(0, per_w // SC_ROWS)
        def _(i):
            off = base + i * SC_ROWS
            pltpu.sync_copy(pos_hbm.at[pl.ds(off, SC_ROWS)], idx_v)
            pltpu.async_copy(src_hbm.at[idx_v], rows_v, sem).wait()
            pltpu.sync_copy(rows_v, out_hbm.at[pl.ds(off, SC_ROWS)])

    return gather_kernel(src, posf)


GU_BLOCK = 2 * LANES
FF_CHUNK = 2 * LANES


def _deinterleave_matrix():
    sel = np.zeros((GU_BLOCK, GU_BLOCK), np.float32)
    j = np.arange(LANES)
    sel[2 * j, j] = 1.0
    sel[2 * j + 1, LANES + j] = 1.0
    return jnp.asarray(sel, BF16)


def _expert_kernel(be_ref, nu_ref, nxt_ref, x_ref, wgu_hbm, bg_ref, bu_ref, wd_hbm, bd_ref, sel_ref,
                   y_ref, wgu_f, wd_f, wgu_s, wd_s, sems, *, layer):
    i = pl.program_id(0)
    d, f2 = wgu_s.shape
    f = f2 // 2
    e_cur = be_ref[i]
    new_expert = jnp.logical_or(i == 0, e_cur != be_ref[jnp.maximum(i - 1, 0)])

    def weight_copies(e):
        return (pltpu.make_async_copy(wgu_hbm.at[layer, e], wgu_f, sems.at[0]),
                pltpu.make_async_copy(wd_hbm.at[layer, e], wd_f, sems.at[1]))

    @pl.when(jnp.logical_and(new_expert, i < nu_ref[0]))
    def _():
        @pl.when(i == 0)
        def _():
            for cp in weight_copies(e_cur):
                cp.start()

        for cp in weight_copies(e_cur):
            cp.wait()
        for c0 in range(0, f2, GU_BLOCK):
            blk = wgu_f[:, c0:c0 + GU_BLOCK].astype(BF16)
            wgu_s[:, c0:c0 + GU_BLOCK] = jnp.dot(
                blk, sel_ref[...], preferred_element_type=F32).astype(BF16)
        wd_s[...] = wd_f[...].astype(BF16)

        @pl.when(nxt_ref[e_cur] >= 0)
        def _():
            for cp in weight_copies(nxt_ref[e_cur]):
                cp.start()

    @pl.when(i < nu_ref[0])
    def _():
        half = x_ref.shape[1]
        lo, hi = _unpack_halves(x_ref[...])
        lo = lo.astype(BF16)
        hi = hi.astype(BF16)

        def hidden(c0):
            cols = slice(2 * c0, 2 * (c0 + FF_CHUNK))
            gu = (jnp.dot(lo, wgu_s[:half, cols], preferred_element_type=F32)
                  + jnp.dot(hi, wgu_s[half:, cols], preferred_element_type=F32))
            parts = []
            for b0 in range(0, FF_CHUNK, LANES):
                gate = gu[:, 2 * b0:2 * b0 + LANES] + bg_ref[0, :, c0 + b0:c0 + b0 + LANES]
                up = gu[:, 2 * b0 + LANES:2 * b0 + 2 * LANES] + bu_ref[0, :, c0 + b0:c0 + b0 + LANES]
                gate = jnp.minimum(gate, SWIGLU_LIMIT)
                up = jnp.clip(up, -SWIGLU_LIMIT, SWIGLU_LIMIT)
                parts.append((up + 1.0) * gate * _sigmoid(SWIGLU_ALPHA * gate))
            return jnp.concatenate(parts, axis=1).astype(BF16)

        chunks = list(range(0, f, FF_CHUNK))
        y = None
        h_next = hidden(chunks[0])
        for n, c0 in enumerate(chunks):
            h = h_next
            if n + 1 < len(chunks):
                h_next = hidden(chunks[n + 1])
            part = jnp.dot(h, wd_s[c0:c0 + FF_CHUNK, :], preferred_element_type=F32)
            y = part if y is None else y + part
        y_ref[...] = _pack_halves(y + bd_ref[0])

    @pl.when(i >= nu_ref[0])
    def _():
        y_ref[...] = jnp.zeros_like(y_ref)


def _expert_ffn(xs, blk_e, n_used, nxt_e, w_gu, bg, bu, w_down, bd, rb, layer):
    rows, half = xs.shape
    _, e, d, f2 = w_gu.shape
    f = f2 // 2
    n_blocks = rows // rb
    wspec = lambda shape: pl.BlockSpec(shape, lambda i, be, nu, nx: (be[i], 0, 0))
    return pl.pallas_call(
        functools.partial(_expert_kernel, layer=layer),
        grid_spec=pltpu.PrefetchScalarGridSpec(
            num_scalar_prefetch=3,
            grid=(n_blocks,),
            in_specs=[
                pl.BlockSpec((rb, half), lambda i, be, nu, nx: (i, 0)),
                pl.BlockSpec(memory_space=pl.ANY),
                wspec((1, 1, f)), wspec((1, 1, f)),
                pl.BlockSpec(memory_space=pl.ANY),
                wspec((1, 1, d)),
                _resident((GU_BLOCK, GU_BLOCK), lambda i, be, nu, nx: (0, 0)),
            ],
            out_specs=pl.BlockSpec((rb, d // 2), lambda i, be, nu, nx: (i, 0)),
            scratch_shapes=[pltpu.VMEM((d, f2), F32), pltpu.VMEM((f, d), F32),
                            pltpu.VMEM((d, f2), BF16), pltpu.VMEM((f, d), BF16),
                            pltpu.SemaphoreType.DMA((2,))],
        ),
        out_shape=jax.ShapeDtypeStruct((rows, d // 2), U32),
        compiler_params=_cparams("arbitrary"),
        name="expert_ffn",
    )(blk_e, n_used, nxt_e, xs, w_gu, bg, bu, w_down, bd, _deinterleave_matrix())


def _combine_kernel(x_ref, y0_ref, y1_ref, y2_ref, y3_ref, rw_ref, g2_ref, *dst_and_out):
    o_ref = dst_and_out[-1]
    half = y0_ref.shape[1]
    lo, hi = _moe_mix(x_ref, (y0_ref, y1_ref, y2_ref, y3_ref), rw_ref, g2_ref)
    o_ref[:, :half] = lo
    o_ref[:, half:] = hi


def _combine(x1, yt, rw, mod, seq, tm, out_rows=None, row_off=0, dst=None):
    t, d = x1.shape
    per_seq = seq // tm
    nt = t // tm
    tile_off = row_off // tm
    yspec = lambda k: pl.BlockSpec((tm, d // 2), lambda i: (k * nt + i, 0))
    in_specs = [
        pl.BlockSpec((tm, d), lambda i: (i, 0)),
        yspec(0), yspec(1), yspec(2), yspec(3),
        pl.BlockSpec((tm, LANES), lambda i: (i, 0)),
        pl.BlockSpec((None, 1, d), lambda i: (i // per_seq, 0, 5)),
    ]
    args = [x1, yt, yt, yt, yt, rw, mod]
    aliases = {}
    if dst is not None:
        in_specs.append(pl.BlockSpec(memory_space=pl.ANY))
        args.append(dst)
        aliases = {len(args) - 1: 0}
    return pl.pallas_call(
        _combine_kernel,
        grid=(nt,),
        in_specs=in_specs,
        out_specs=pl.BlockSpec((tm, d), lambda i: (i + tile_off, 0)),
        out_shape=jax.ShapeDtypeStruct((t if out_rows is None else out_rows, d), F32),
        input_output_aliases=aliases,
        compiler_params=_cparams("parallel"),
        name="moe_combine",
    )(*args)


POST_ROWS = 1024
POST_SUB = 256


def _row_tile(seq):
    return min(seq, 512)


def _moe_block_rows(t):
    return min(512, t * TOP_K // 8)


def kernel(x, c, fox_w_in, fox_b_f, fox_q_norm, fox_k_norm, fox_w_out, hgrn_w_in, hgrn_lb, hgrn_g_norm, hgrn_w_out, ada_w, ada_b, norm1_w, norm2_w, router_w, router_b, exp_w_gu, exp_b_gu, exp_w_down, exp_b_down):
    bsz, seq, d = x.shape
    t = bsz * seq
    depth = ada_w.shape[0]
    n_exp = router_w.shape[-1]
    f_dim = exp_w_down.shape[2]
    h_fox = fox_w_in.shape[-1] - 4 * d
    dh = d // h_fox
    tm = _row_tile(seq)
    n_split = 2 if bsz % 2 == 0 else 1
    bh, th = bsz // n_split, t // n_split
    rb = _moe_block_rows(th)
    n_blocks = (th * TOP_K) // rb + n_exp
    hgrn_hpb = 4 if d % (4 * LANES) == 0 else 2

    mod_all = _adaln_mod(c, ada_w, ada_b).reshape(depth, bsz, 1, -1)
    lb_soft = jax.nn.softmax(hgrn_lb.astype(F32), axis=0)
    lower = jnp.cumsum(lb_soft, axis=0) - lb_soft[0]
    experts = jnp.arange(n_exp, dtype=I32)

    xf = x.reshape(t, d)
    parts = [xf] * n_split
    for i in range(depth):
        j = i // 2
        nw1 = norm1_w[i].reshape(1, d)
        if i % 2 == 0:
            w_in = fox_w_in[j]
            w_main = w_in[:, :4 * d].astype(BF16)
            w_f = jnp.pad(w_in[:, 4 * d:], ((0, 0), (0, LANES - h_fox))).astype(BF16)
            b_f = jnp.pad(fox_b_f[j], (0, LANES - h_fox)).reshape(1, LANES)
            qw = jnp.tile(fox_q_norm[j], LANES // dh).reshape(1, LANES)
            kw = jnp.tile(fox_k_norm[j], LANES // dh).reshape(1, LANES)
            w_out = fox_w_out[j].astype(BF16)
        else:
            w_h = hgrn_w_in[j]
            w_main = jnp.concatenate([w_h[:, :d], w_h[:, 2 * d:]], axis=1).astype(BF16)
            w_f = w_h[:, d:2 * d].astype(BF16)
            b_f = jnp.zeros((1, d), F32)
            w_out = hgrn_w_out[j].astype(BF16)
        wr = jnp.pad(router_w[i], ((0, 0), (0, LANES - n_exp))).astype(BF16)
        br = jnp.pad(router_b[i], (0, LANES - n_exp), constant_values=MASK_VALUE).reshape(1, LANES)
        b_gu = exp_b_gu[i].reshape(n_exp, 1, f_dim, 2)

        def front(s, src):
            mod = mod_all[i, s * bh:(s + 1) * bh]
            off = s * th if i == 0 else 0
            moe = None
            xp = src
            if isinstance(src, tuple):
                xp, moe = src[0], src[1:]
            if i % 2 == 0:
                outs = _in_proj(xp, nw1, mod, w_main, BF16, seq, tm, w_f, b_f,
                                rows=th, row_off=off, moe=moe, name="in_proj_fox")
                o = _fox_attention(outs[0], outs[1], qw, kw, seq, d, dh)
            else:
                outs = _in_proj(xp, nw1, mod, w_main, BF16, seq, tm, w_f, b_f,
                                rows=th, row_off=off, moe=moe, name="in_proj_hgrn",
                                gate_lb=lower[j].reshape(1, d))
                o = _hgrn_mixer(outs[0], outs[1], hgrn_g_norm[j].reshape(1, HGRN_DIM), seq, d,
                                hgrn_hpb)
            if moe is not None:
                xp = outs[-1]
            x1, hp, ri, rw, cnt = _post_mixer(o, xp, w_out, mod, norm2_w[i].reshape(1, d), wr, br,
                                              seq, min(seq, POST_ROWS), row_off=off)
            counts = cnt[0, :n_exp]
            padded = (counts + rb - 1) // rb * rb
            pad_end = jnp.cumsum(padded)
            pad_start = pad_end - padded
            top_idx, rank = ri[:TOP_K], ri[TOP_K:]
            start = jnp.sum(jnp.where(top_idx[None] == experts[:, None, None],
                                      pad_start[:, None, None], 0), axis=0)
            posf = (start + rank).reshape(-1).astype(I32)
            blk_lo = jnp.arange(n_blocks, dtype=I32) * rb
            blk_e = jnp.minimum(jnp.sum(pad_end[None, :] <= blk_lo[:, None], axis=1),
                                n_exp - 1).astype(I32)
            n_used = (pad_end[-1] // rb).astype(I32).reshape(1)
            later = jnp.where((experts[None, :] > experts[:, None]) & (padded[None, :] > 0),
                              experts[None, :], n_exp)
            nxt_e = jnp.min(later, axis=1)
            nxt_e = jnp.where(nxt_e >= n_exp, -1, nxt_e).astype(I32)
            return dict(x1=x1, hp=hp, rw=rw, posf=posf, blk_e=blk_e, n_used=n_used, nxt_e=nxt_e,
                        mod=mod)

        def experts_of(st, xs):
            return _expert_ffn(xs, st["blk_e"], st["n_used"], st["nxt_e"], exp_w_gu,
                               b_gu[..., 0], b_gu[..., 1], exp_w_down,
                               exp_b_down[i].reshape(n_exp, 1, d), rb, i)

        def deferred(st, yt):
            return (st["x1"], yt, st["rw"], st["mod"])

        if n_split == 1:
            st = front(0, parts[0])
            xs = _sc_scatter_rows(st["hp"], st["posf"], n_blocks * rb)
            parts[0] = deferred(st, _sc_gather_rows(experts_of(st, xs), st["posf"]))
            continue

        tie = lax.optimization_barrier
        sa = front(0, parts[0])
        xs_a = _sc_scatter_rows(sa["hp"], sa["posf"], n_blocks * rb)
        src_b = parts[1]
        if isinstance(src_b, tuple):
            yt_b, _ = tie((src_b[1], sa["hp"]))
            src_b = (src_b[0], yt_b) + src_b[2:]
        else:
            src_b, _ = tie((src_b, sa["hp"]))
        sb = front(1, src_b)
        xs_a, _ = tie((xs_a, sb["hp"]))
        xs_b = _sc_scatter_rows(sb["hp"], sb["posf"], n_blocks * rb)
        ys_a = experts_of(sa, xs_a)
        xs_b, _ = tie((xs_b, ys_a))
        yt_a = _sc_gather_rows(ys_a, sa["posf"])
        ys_b = experts_of(sb, xs_b)
        yt_a, _ = tie((yt_a, ys_b))
        yt_b = _sc_gather_rows(ys_b, sb["posf"])
        parts = [deferred(sa, yt_a), deferred(sb, yt_b)]

    out = None
    for s_i, (x1, yt, rw, mod) in enumerate(parts):
        if n_split == 1:
            out = _combine(x1, yt, rw, mod, seq, tm)
        else:
            out = _combine(x1, yt, rw, mod, seq, tm, out_rows=t, row_off=s_i * th, dst=out)
    return out.reshape(bsz, seq, d)
```

```python
import functools

import jax
import jax.numpy as jnp
import numpy as np
from jax import lax
from jax.experimental import pallas as pl
from jax.experimental.pallas import tpu as pltpu
from jax.experimental.pallas import tpu_sc as plsc

F32 = jnp.float32
BF16 = jnp.bfloat16
U32 = jnp.uint32
I32 = jnp.int32

EPS = 1e-6
MASK_VALUE = -1e30
TOP_K = 4
HGRN_DIM = 128
HGRN_CHUNK = 32
SWIGLU_LIMIT = 7.0
SWIGLU_ALPHA = 1.702
LANES = 128
SUBLANES = 8
SC_CORES = 2
SC_SUBCORES = 16
SC_ROWS = 64
VMEM_LIMIT = 56 * 1024 * 1024
HIGHEST = lax.Precision.HIGHEST


def _cparams(*sem):
    return pltpu.CompilerParams(dimension_semantics=sem, vmem_limit_bytes=VMEM_LIMIT)


def _resident(shape, index_map):
    return pl.BlockSpec(shape, index_map, pipeline_mode=pl.Buffered(1))


def _sigmoid(x):
    return 1.0 / (1.0 + jnp.exp(-x))


def _bf16_bits(x):
    return lax.bitcast_convert_type(x.astype(BF16).astype(F32), U32)


def _pack_halves(x):
    n = x.shape[1] // 2
    lo = _bf16_bits(x[:, :n])
    hi = _bf16_bits(x[:, n:])
    return (hi & jnp.uint32(0xFFFF0000)) | (lo >> 16)


def _unpack_halves(w):
    lo = lax.bitcast_convert_type(w << 16, F32)
    hi = lax.bitcast_convert_type(w & jnp.uint32(0xFFFF0000), F32)
    return lo, hi


def _mod_kernel(c_ref, w_ref, b_ref, o_ref):
    c = c_ref[...]
    ca = c * _sigmoid(c)
    o_ref[0] = jnp.dot(ca, w_ref[0], preferred_element_type=F32, precision=HIGHEST) + b_ref[0]


def _adaln_mod(c, ada_w, ada_b):
    depth, d, n6 = ada_w.shape
    b = c.shape[0]
    nmod = n6 // d
    return pl.pallas_call(
        _mod_kernel,
        grid=(depth, nmod),
        in_specs=[
            pl.BlockSpec((b, d), lambda i, j: (0, 0)),
            pl.BlockSpec((1, d, d), lambda i, j: (i, 0, j)),
            pl.BlockSpec((1, 1, d), lambda i, j: (i, 0, j)),
        ],
        out_specs=pl.BlockSpec((1, b, d), lambda i, j: (i, 0, j)),
        out_shape=jax.ShapeDtypeStruct((depth, b, n6), F32),
        compiler_params=_cparams("parallel", "parallel"),
        name="adaln_mod",
    )(c, ada_w, ada_b.reshape(depth, 1, n6))


def _norm_mod(x, nw, sc, sh):
    ms = jnp.mean(x * x, axis=-1, keepdims=True)
    return x * lax.rsqrt(ms + EPS) * nw * (1.0 + sc) + sh


def _moe_mix(x_ref, y_refs, rw_ref, g2_ref, rows=slice(None)):
    rw = rw_ref[rows, :]
    lane = lax.broadcasted_iota(I32, rw.shape, 1)
    half = y_refs[0].shape[1]
    acc_lo = acc_hi = None
    for k, y_ref in enumerate(y_refs):
        wk = jnp.sum(jnp.where(lane == k, rw, 0.0), axis=-1, keepdims=True)
        lo, hi = _unpack_halves(y_ref[rows, :])
        acc_lo = wk * lo if acc_lo is None else acc_lo + wk * lo
        acc_hi = wk * hi if acc_hi is None else acc_hi + wk * hi
    g2 = g2_ref[...]
    return (x_ref[rows, :half] + g2[:, :half] * acc_lo, x_ref[rows, half:] + g2[:, half:] * acc_hi)


def _in_kernel(*refs, n_chunk, with_f, with_moe, sub, silu_cols):
    n_in = 1 + (TOP_K + 2 if with_moe else 0) + 4 + (2 if with_f else 0) + (1 if silu_cols else 0)
    ins, outs = refs[:n_in], refs[n_in:]
    x_ref = ins[0]
    pos = 1 + (TOP_K + 2 if with_moe else 0)
    nw_ref, sh_ref, sc_ref, w_ref = ins[pos:pos + 4]
    o_ref = outs[0]
    if with_f:
        wf_ref, bf_ref = ins[pos + 4:pos + 6]
        of_ref = outs[1]
    n = o_ref.shape[1]
    for r0 in range(0, x_ref.shape[0], sub):
        rows = slice(r0, r0 + sub)
        if with_moe:
            lo, hi = _moe_mix(x_ref, ins[1:1 + TOP_K], ins[1 + TOP_K], ins[2 + TOP_K], rows)
            x = jnp.concatenate([lo, hi], axis=1)
            outs[-1][rows, :] = x
        else:
            x = x_ref[rows, :]
        h = _norm_mod(x, nw_ref[...], sc_ref[...], sh_ref[...]).astype(BF16)
        for c0 in range(0, n, n_chunk):
            res = jnp.dot(h, w_ref[:, c0:c0 + n_chunk], preferred_element_type=F32)
            if c0 < silu_cols:
                res = res * _sigmoid(res)
            o_ref[rows, c0:c0 + n_chunk] = res.astype(o_ref.dtype)
        if with_f:
            second = jnp.dot(h, wf_ref[...], preferred_element_type=F32) + bf_ref[...]
            if silu_cols:
                lb = ins[pos + 6][...]
                second = lb + (1.0 - lb) * _sigmoid(second)
            of_ref[rows, :] = second


def _in_proj(x, nw, mod, w, out_dtype, seq, tm, wf=None, bf=None, rows=None, row_off=0, moe=None,
             name="in_proj", gate_lb=None):
    d = x.shape[1]
    t = x.shape[0] if rows is None else rows
    n = w.shape[1]
    per_seq = seq // tm
    tile_off = row_off // tm
    nt = t // tm
    with_f = wf is not None
    mod_spec = lambda j: pl.BlockSpec((None, 1, d), lambda i: (i // per_seq, 0, j))
    in_specs = [pl.BlockSpec((tm, d), lambda i: (i + tile_off, 0))]
    args = [x]
    if moe is not None:
        yt, rw, mod_prev = moe
        in_specs += [pl.BlockSpec((tm, d // 2), lambda i, k=k: (k * nt + i, 0)) for k in range(TOP_K)]
        in_specs += [pl.BlockSpec((tm, LANES), lambda i: (i, 0)), mod_spec(5)]
        args += [yt] * TOP_K + [rw, mod_prev]
    in_specs += [_resident((1, d), lambda i: (0, 0)), mod_spec(0), mod_spec(1),
                 _resident((d, n), lambda i: (0, 0))]
    args += [nw, mod, mod, w]
    out_specs = [pl.BlockSpec((tm, n), lambda i: (i, 0))]
    out_shape = [jax.ShapeDtypeStruct((t, n), out_dtype)]
    if with_f:
        nf = wf.shape[1]
        in_specs += [_resident((d, nf), lambda i: (0, 0)), _resident((1, nf), lambda i: (0, 0))]
        out_specs.append(pl.BlockSpec((tm, nf), lambda i: (i, 0)))
        out_shape.append(jax.ShapeDtypeStruct((t, nf), F32))
        args += [wf, bf]
        if gate_lb is not None:
            in_specs.append(_resident((1, nf), lambda i: (0, 0)))
            args.append(gate_lb)
    silu_cols = d if gate_lb is not None else 0
    if moe is not None:
        out_specs.append(pl.BlockSpec((tm, d), lambda i: (i, 0)))
        out_shape.append(jax.ShapeDtypeStruct((t, d), F32))
    return pl.pallas_call(
        functools.partial(_in_kernel, n_chunk=d if silu_cols else min(n, 1024), with_f=with_f,
                          with_moe=moe is not None, sub=min(tm, 256), silu_cols=silu_cols),
        grid=(t // tm,),
        in_specs=in_specs,
        out_specs=out_specs,
        out_shape=out_shape,
        compiler_params=_cparams("parallel"),
        name=name,
    )(*args)


LOG2E = 1.4426950408889634
EXP2_SPAN = 120.0
SCORE_MARGIN = 1.01
BIAS_LANES = 16


def _split3(c):
    c1 = c.astype(BF16).astype(F32)
    r = c - c1
    c2 = r.astype(BF16).astype(F32)
    return c1, c2, r - c2


def _bias_layout(n_pairs):
    route =np.zeros((3, 3 * LANES, LANES), np.float32)
    const = np.zeros((3, 1, LANES), np.float32)
    for p in range(n_pairs):
        base = p * BIAS_LANES
        for i in range(3):
            const[0, 0, base + i] = 1.0
            for h in range(2):
                route[0, i * LANES + 2 * p + h, base + 3 + 3 * h + i] = -1.0
                route[1 + h, i * LANES + 2 * p + h, base + i] = 1.0
                const[1 + h, 0, base + 3 + 3 * h + i] = 1.0
    return jnp.asarray(route, BF16), jnp.asarray(const, F32)


def _cum_kernel(f_ref, ref_ref, route_ref, const_ref, o_ref, *, rows):
    s = f_ref.shape[0]
    r = lax.broadcasted_iota(I32, (rows, rows), 0)
    c = lax.broadcasted_iota(I32, (rows, rows), 1)
    tri = jnp.where(r >= c, 1.0, 0.0).astype(BF16)
    carry = jnp.zeros((1, LANES), F32)
    for r0 in range(0, s, rows):
        x = f_ref[r0:r0 + rows, :]
        lf = jnp.minimum(x, 0.0) - jnp.log(1.0 + jnp.exp(-jnp.abs(x)))
        cs = carry + sum(jnp.dot(tri, part.astype(BF16), preferred_element_type=F32)
                         for part in _split3(lf))
        carry = cs[rows - 1:rows, :]
        c2 = cs * LOG2E
        k_terms = jnp.concatenate(_split3(c2), axis=1).astype(BF16)
        q_terms = jnp.concatenate(_split3(c2 - ref_ref[...]), axis=1).astype(BF16)
        for j, terms in enumerate((k_terms, q_terms, q_terms)):
            routed = jnp.dot(terms, route_ref[j], preferred_element_type=F32) + const_ref[j]
            o_ref[r0:r0 + rows, j * LANES:(j + 1) * LANES] = routed.astype(o_ref.dtype)


def _fox_bias(flogit, ref, seq, n_pairs):
    t = flogit.shape[0]
    route, const = _bias_layout(n_pairs)
    return pl.pallas_call(
        functools.partial(_cum_kernel, rows=min(seq, 256)),
        grid=(t // seq,),
        in_specs=[pl.BlockSpec((seq, LANES), lambda b: (b, 0)),
                  pl.BlockSpec((1, LANES), lambda b: (0, 0)),
                  pl.BlockSpec((3, 3 * LANES, LANES), lambda b: (0, 0, 0)),
                  pl.BlockSpec((3, 1, LANES), lambda b: (0, 0, 0))],
        out_specs=pl.BlockSpec((seq, 3 * LANES), lambda b: (b, 0)),
        out_shape=jax.ShapeDtypeStruct((t, 3 * LANES), BF16),
        compiler_params=_cparams("parallel"),
        name="fox_bias",
    )(flogit, ref, route, const)


def _attn_kernel(q_ref, k_ref, v_ref, g_ref, bias_ref, qw_ref, kw_ref, o_ref, qs_s, ka_s, *,
                 dh, tq, bounded):
    s_len = q_ref.shape[0]
    nq = s_len // tq
    hp = pl.program_id(1)
    lane = lax.broadcasted_iota(I32, (1, LANES), 1)
    first = lane < dh

    def headnorm(z, w):
        z2 = z * z
        s0 = jnp.sum(jnp.where(first, z2, 0.0), axis=-1, keepdims=True)
        s1 = jnp.sum(jnp.where(first, 0.0, z2), axis=-1, keepdims=True)
        ms = jnp.where(first, s0, s1) * (1.0 / dh)
        return z * lax.rsqrt(ms + EPS) * w

    qn = headnorm(q_ref[...].astype(F32), qw_ref[...]) * (dh ** -0.5 * LOG2E)
    kn = headnorm(k_ref[...].astype(F32), kw_ref[...])
    mine = (lane // BIAS_LANES) == hp
    ka_s[:, :LANES] = kn.astype(BF16)
    ka_s[:, LANES:] = bias_ref[:, :LANES]
    q_lo = [jnp.where(first, qn, 0.0), jnp.where(first, 0.0, qn)]
    hq = tq // 2
    for qi in range(nq):
        for part_i in range(2):
            rows = slice(qi * tq + part_i * hq, qi * tq + (part_i + 1) * hq)
            for h in range(2):
                dst = slice((2 * part_i + h) * hq, (2 * part_i + h + 1) * hq)
                q_hi = bias_ref[rows, (1 + h) * LANES:(2 + h) * LANES].astype(F32)
                qs_s[qi, dst, :LANES] = q_lo[h][rows, :].astype(BF16)
                qs_s[qi, dst, LANES:] = jnp.where(mine, q_hi, 0.0).astype(BF16)

    nt = (((1,), (1,)), ((), ()))

    def causal(s, first_token):
        rr = lax.broadcasted_iota(I32, s.shape, 0)
        rr = jnp.where(rr >= hq, rr - hq, rr) + first_token
        cc = lax.broadcasted_iota(I32, s.shape, 1)
        return jnp.where(rr >= cc, s, MASK_VALUE)

    def finish(qi, halves):
        for part_i, (acc, l) in enumerate(halves):
            o2 = acc / l
            o = jnp.where(first, o2[:hq], o2[hq:])
            rows = slice(qi * tq + part_i * hq, qi * tq + (part_i + 1) * hq)
            g = g_ref[rows, :].astype(F32)
            o_ref[rows, :] = (o * _sigmoid(g)).astype(o_ref.dtype)

    def lane_blocks(e):
        cols = e[:, :LANES]
        for c0 in range(LANES, e.shape[1], LANES):
            cols = cols + e[:, c0:c0 + LANES]
        return cols

    if bounded:
        for qi in range(nq):
            acc, part = [None, None], [None, None]

            def add(i, pv, cols):
                acc[i] = pv if acc[i] is None else acc[i] + pv
                part[i] = cols if part[i] is None else part[i] + cols

            for t in range(qi):
                s = lax.dot_general(qs_s[qi], ka_s[t * tq:(t + 1) * tq, :], nt,
                                    preferred_element_type=F32)
                e = jnp.exp2(s)
                pv = jnp.dot(e.astype(BF16), v_ref[t * tq:(t + 1) * tq, :],
                             preferred_element_type=F32)
                cols = lane_blocks(e)
                add(0, pv[:tq], cols[:tq])
                add(1, pv[tq:], cols[tq:])
            k0 = qi * tq
            for i, n_keys in ((0, hq), (1, tq)):
                s = lax.dot_general(qs_s[qi, i * tq:(i + 1) * tq, :], ka_s[k0:k0 + n_keys, :], nt,
                                    preferred_element_type=F32)
                e = jnp.exp2(causal(s, i * hq))
                pv = jnp.dot(e.astype(BF16), v_ref[k0:k0 + n_keys, :], preferred_element_type=F32)
                add(i, pv, lane_blocks(e))
            finish(qi, [(acc[i], jnp.sum(part[i], axis=-1, keepdims=True)) for i in range(2)])
        return

    def causal_tile(s):
        return jnp.concatenate([causal(s[:tq], 0), causal(s[tq:], hq)], axis=0)

    def scores(q, off):
        return lax.dot_general(q, ka_s[pl.ds(off, tq), :], nt, preferred_element_type=F32)

    def update(s, off, carry, masked):
        m, l, acc = carry
        if masked:
            s = causal_tile(s)
        m_new = jnp.maximum(m, jnp.max(s, axis=-1, keepdims=True))
        p = jnp.exp2(s - m_new)
        alpha = jnp.exp2(m - m_new)
        l = alpha * l + jnp.sum(p, axis=-1, keepdims=True)
        acc = alpha * acc + jnp.dot(p.astype(BF16), v_ref[pl.ds(off, tq), :],
                                    preferred_element_type=F32)
        return m_new, l, acc

    for qi in range(nq):
        q = qs_s[qi]
        s = scores(q, 0)
        carry = (jnp.full((2 * tq, 1), MASK_VALUE, F32), jnp.zeros((2 * tq, 1), F32),
                 jnp.zeros((2 * tq, LANES), F32))
        if qi > 0:
            def body(j, c, q=q):
                off = pl.multiple_of(j * tq, tq)
                s_next = scores(q, off + tq)
                return (s_next,) + update(c[0], off, c[1:], False)

            s, *carry = lax.fori_loop(0, qi, body, (s,) + carry)
        m, l, acc = update(s, qi * tq, carry, True)
        finish(qi, [(acc[:tq], l[:tq]), (acc[tq:], l[tq:])])


def _fox_attention(qkvg, flogit, qw, kw, seq, d, dh):
    t = qkvg.shape[0]
    nb = d // LANES
    assert 2 * dh == LANES and nb * BIAS_LANES <= LANES
    tq = min(seq, 512)
    col = lambda off: pl.BlockSpec((seq, LANES), lambda b, p: (b, off + p))

    def call(bounded, bias):
        return pl.pallas_call(
            functools.partial(_attn_kernel, dh=dh, tq=tq, bounded=bounded),
            grid=(t // seq, nb),
            in_specs=[
                col(0), col(nb), col(2 * nb), col(3 * nb),
                pl.BlockSpec((seq, 3 * LANES), lambda b, p: (b, 0)),
                pl.BlockSpec((1, LANES), lambda b, p: (0, 0)),
                pl.BlockSpec((1, LANES), lambda b, p: (0, 0)),
            ],
            out_specs=pl.BlockSpec((seq, LANES), lambda b, p: (b, p)),
            out_shape=jax.ShapeDtypeStruct((t, d), BF16),
            scratch_shapes=[pltpu.VMEM((seq // tq, 2 * tq, 2 * LANES), BF16),
                            pltpu.VMEM((seq, 2 * LANES), BF16)],
            compiler_params=_cparams("parallel", "parallel"),
            name="fox_attention" if bounded else "fox_attention_running_max",
        )(qkvg, qkvg, qkvg, qkvg, bias, qw, kw)

    bound = (dh ** 0.5 * LOG2E * SCORE_MARGIN) * jnp.max(jnp.abs(qw)) * jnp.max(jnp.abs(kw)) + 0.1
    fits = bound <= EXP2_SPAN - 4.0
    ref = jnp.where(fits, bound - jnp.maximum(0.0, 2.0 * bound - EXP2_SPAN), 0.0)
    bias = _fox_bias(flogit, jnp.full((1, LANES), ref, F32), seq, nb)
    return lax.cond(fits, lambda: call(True, bias), lambda: call(False, bias))


HGRN_PREP_ROWS = 256
HGRN_LOG2_SPAN = 100.0


def _hgrn_kernel(q_ref, f_ref, v_ref, g_ref, gw_ref, o_ref, qe_s, ke_s, kd_s, dec_s, *,
                 hpb, chunk):
    s_len = q_ref.shape[0]
    n_chunks = s_len // chunk
    nt = (((1,), (1,)), ((), ()))
    gw = gw_ref[...]

    pr = min(HGRN_PREP_ROWS, s_len)
    rr = lax.broadcasted_iota(I32, (pr, pr), 0)
    cc = lax.broadcasted_iota(I32, (pr, pr), 1)
    same = (rr // chunk) == (cc // chunk)
    cum_m = jnp.where(same & (rr >= cc), 1.0, 0.0).astype(BF16)
    tot_m = jnp.where(same, 1.0, 0.0).astype(BF16)
    lowest = jnp.zeros((1, q_ref.shape[1]), F32)
    for r0 in range(0, s_len, pr):
        f = f_ref[r0:r0 + pr, :]
        terms = [t.astype(BF16) for t in _split3(jnp.log2(f))]
        b = sum(jnp.dot(cum_m, t, preferred_element_type=F32) for t in terms)
        tot = sum(jnp.dot(tot_m, t, preferred_element_type=F32) for t in terms)
        k = 1.0 - f
        qe_s[r0:r0 + pr, :] = (q_ref[r0:r0 + pr, :].astype(F32) * jnp.exp2(b)).astype(BF16)
        ke_s[r0:r0 + pr, :] = (k * jnp.exp2(-b)).astype(BF16)
        kd_s[r0:r0 + pr, :] = (k * jnp.exp2(tot - b)).astype(BF16)
        for c0 in range(0, pr, chunk):
            ci = (r0 + c0) // chunk
            dec_s[ci] = jnp.exp2(tot[c0:c0 + SUBLANES, :])
        lowest = jnp.minimum(lowest, jnp.min(tot, axis=0, keepdims=True))
    mild = jnp.min(lowest) >= -HGRN_LOG2_SPAN

    @pl.when(mild)
    def _():
        r32 = lax.broadcasted_iota(I32, (chunk, chunk), 0)
        c32 = lax.broadcasted_iota(I32, (chunk, chunk), 1)
        causal = r32 >= c32
        per_step = 16 if n_chunks % 16 == 0 else 2

        def fast_body(step, states):
            insts = [(ch, hd) for ch in range(per_step) for hd in range(hpb)]
            rows = {ch: pl.ds(pl.multiple_of((step * per_step + ch) * chunk, chunk), chunk)
                    for ch in range(per_step)}
            cols = {hd: slice(hd * LANES, (hd + 1) * LANES) for hd in range(hpb)}
            states = list(states)
            qe = {i: qe_s[rows[i[0]], cols[i[1]]] for i in insts}
            v = {i: v_ref[rows[i[0]], cols[i[1]]].astype(F32) for i in insts}
            a = {i: lax.dot_general(qe[i], ke_s[rows[i[0]], cols[i[1]]], nt,
                                    preferred_element_type=F32) for i in insts}
            upd = {i: jnp.dot(jnp.transpose(v[i]).astype(BF16), kd_s[rows[i[0]], cols[i[1]]],
                              preferred_element_type=F32) for i in insts}
            o = {}
            for ch in range(per_step):
                for hd in range(hpb):
                    i = (ch, hd)
                    o[i] = (jnp.dot(jnp.where(causal, a[i], 0.0).astype(BF16), v[i].astype(BF16),
                                    preferred_element_type=F32)
                            + lax.dot_general(qe[i], states[hd].astype(BF16), nt,
                                              preferred_element_type=F32))
                    dec = dec_s[step * per_step + ch, 0:1, cols[hd]]
                    states[hd] = dec * states[hd] + upd[i]
            for i in insts:
                g = g_ref[rows[i[0]], cols[i[1]]].astype(F32)
                ms = jnp.mean(o[i] * o[i], axis=-1, keepdims=True)
                y = o[i] * lax.rsqrt(ms + EPS) * gw * (g * _sigmoid(g))
                o_ref[rows[i[0]], cols[i[1]]] = y.astype(o_ref.dtype)
            return tuple(states)

        init = tuple(jnp.zeros((LANES, LANES), F32) for _ in range(hpb))
        lax.fori_loop(0, n_chunks // per_step, fast_body, init)

    pair = 2
    groups = chunk // SUBLANES
    off_w = (groups - 1) * chunk
    r = lax.broadcasted_iota(I32, (chunk, chunk), 0)
    c = lax.broadcasted_iota(I32, (chunk, chunk), 1)
    tri = jnp.where(r >= c, 1.0, 0.0).astype(BF16)
    ro = lax.broadcasted_iota(I32, (chunk, off_w), 0) // SUBLANES
    co = lax.broadcasted_iota(I32, (chunk, off_w), 1) // chunk
    keep = ro == co + 1
    sub = lax.broadcasted_iota(I32, (SUBLANES, LANES), 0)
    gw = gw_ref[...]
    nt = (((1,), (1,)), ((), ()))
    grp = lambda a, i: a[i * SUBLANES:(i + 1) * SUBLANES, :]
    zeros8 = jnp.zeros((SUBLANES, LANES), F32)

    def body(step, states):
        insts = [(ch, hd) for ch in range(pair) for hd in range(hpb)]
        rows = {ch: pl.ds(pl.multiple_of((step * pair + ch) * chunk, chunk), chunk)
                for ch in range(pair)}
        cols = {hd: slice(hd * LANES, (hd + 1) * LANES) for hd in range(hpb)}
        st = {}
        for ch, hd in insts:
            f = f_ref[rows[ch], cols[hd]]
            d = dict(v=v_ref[rows[ch], cols[hd]].astype(F32),
                     q=q_ref[rows[ch], cols[hd]].astype(F32), k=1.0 - f)
            lf = jnp.log(f)
            d["b"] = sum(jnp.dot(tri, part.astype(BF16), preferred_element_type=F32)
                         for part in _split3(lf))
            st[ch, hd] = d
        def from_state(ch, hd, state_t):
            d = st[ch, hd]
            qe = (d["q"] * jnp.exp(d["b"])).astype(BF16)
            d["o"] = lax.dot_general(qe, state_t.astype(BF16), nt, preferred_element_type=F32)
        for hd in range(hpb):
            from_state(0, hd, states[hd])
        for ch, hd in insts:
            d = st[ch, hd]
            b, q, k = d["b"], d["q"], d["k"]
            ends = [b[(i + 1) * SUBLANES - 1:(i + 1) * SUBLANES, :] for i in range(groups)]
            kt = [grp(k, i) * jnp.exp(ends[i] - grp(b, i)) for i in range(groups)]
            qh = jnp.concatenate(
                [zeros8] + [grp(q, i) * jnp.exp(grp(b, i) - ends[i - 1]) for i in range(1, groups)],
                axis=0).astype(BF16)
            kh = jnp.concatenate(
                [kt[j] * jnp.exp(ends[i - 1] - ends[j]) if j < i else zeros8
                 for i in range(1, groups) for j in range(groups)], axis=0).astype(BF16)
            d["a"] = lax.dot_general(qh, kh, nt, preferred_element_type=F32)
            d["ends"], d["kt"] = ends, kt
        def state_update(ch, hd, state_t):
            d = st[ch, hd]
            b_end = d["ends"][-1]
            ke = jnp.concatenate([d["kt"][j] * jnp.exp(b_end - d["ends"][j]) for j in range(groups)],
                                 axis=0).astype(BF16)
            upd = jnp.dot(jnp.transpose(d["v"]).astype(BF16), ke, preferred_element_type=F32)
            return jnp.exp(b_end) * state_t + upd
        states = [state_update(0, hd, states[hd]) for hd in range(hpb)]
        for ch, hd in insts:
            d = st[ch, hd]
            vb = d["v"].astype(BF16)
            v3 = jnp.concatenate([vb] * (groups - 1), axis=0)
            d["off"] = jnp.dot(jnp.where(keep, d["a"], 0.0).astype(BF16), v3,
                               preferred_element_type=F32)
        for hd in range(hpb):
            from_state(1, hd, states[hd])
        states = [state_update(1, hd, states[hd]) for hd in range(hpb)]
        for ch, hd in insts:
            d = st[ch, hd]
            o = d["o"] + d["off"]
            o_g = []
            for gi in range(groups):
                qg, kg, vg, bg = grp(d["q"], gi), grp(d["k"], gi), grp(d["v"], gi), grp(d["b"], gi)
                og = grp(o, gi)
                for s in range(SUBLANES):
                    live = sub >= s
                    e = jnp.exp(jnp.where(live, bg - bg[s:s + 1, :], 0.0))
                    p = jnp.where(live, qg * e * kg[s:s + 1, :], 0.0)
                    og = og + jnp.sum(p, axis=-1, keepdims=True) * vg[s:s + 1, :]
                o_g.append(og)
            o = jnp.concatenate(o_g, axis=0)
            g = g_ref[rows[ch], cols[hd]].astype(F32)
            ms = jnp.mean(o * o, axis=-1, keepdims=True)
            y = o * lax.rsqrt(ms + EPS) * gw * (g * _sigmoid(g))
            o_ref[rows[ch], cols[hd]] = y.astype(o_ref.dtype)
        return tuple(states)

    @pl.when(jnp.logical_not(mild))
    def _():
        init = tuple(jnp.zeros((LANES, LANES), F32) for _ in range(hpb))
        lax.fori_loop(0, s_len // (chunk * pair), body, init)


def _hgrn_mixer(qvg, fgate, gw, seq, d, hpb):
    t = qvg.shape[0]
    w = hpb * LANES
    nb = d // w
    col = lambda off: pl.BlockSpec((seq, w), lambda b, p: (b, off + p))
    return pl.pallas_call(
        functools.partial(_hgrn_kernel, hpb=hpb, chunk=HGRN_CHUNK),
        grid=(t // seq, nb),
        in_specs=[
            col(0), col(0), col(nb), col(2 * nb),
            pl.BlockSpec((1, LANES), lambda b, p: (0, 0)),
        ],
        out_specs=pl.BlockSpec((seq, w), lambda b, p: (b, p)),
        out_shape=jax.ShapeDtypeStruct((t, d), BF16),
        scratch_shapes=[pltpu.VMEM((seq, w), BF16)] * 3
        + [pltpu.VMEM((seq // HGRN_CHUNK, SUBLANES, w), F32)],
        compiler_params=_cparams("parallel", "parallel"),
        name="hgrn_mixer",
    )(qvg, fgate, qvg, qvg, gw)


def _post_kernel(o_ref, x_ref, wo_ref, g1_ref, nw_ref, sh_ref, sc_ref, wr_ref, br_ref,
                 x1_ref, hp_ref, ri_ref, rw_ref, cnt_ref, run_s):
    i = pl.program_id(0)
    tm = x_ref.shape[0]
    sub = min(tm, POST_SUB)
    subs = [slice(r0, r0 + sub) for r0 in range(0, tm, sub)]

    @pl.when(i == 0)
    def _():
        run_s[...] = jnp.zeros_like(run_s)

    lane = lax.broadcasted_iota(I32, (sub, LANES), 1).astype(F32)
    r = lax.broadcasted_iota(I32, (sub, sub), 0)
    c = lax.broadcasted_iota(I32, (sub, sub), 1)
    strict = jnp.where(r > c, 1.0, 0.0).astype(BF16)

    y = [jnp.dot(o_ref[rows, :], wo_ref[...], preferred_element_type=F32) for rows in subs]
    logits = []
    for rows, y_s in zip(subs, y):
        x1 = x_ref[rows, :] + g1_ref[...] * y_s
        x1_ref[rows, :] = x1
        h2 = _norm_mod(x1, nw_ref[...], sc_ref[...], sh_ref[...])
        hp_ref[rows, :] = _pack_halves(h2)
        logits.append(jnp.dot(h2.astype(BF16), wr_ref[...], preferred_element_type=F32) + br_ref[...])

    picks = []
    for work in logits:
        idx, val = [], []
        for _ in range(TOP_K):
            mx = jnp.max(work, axis=-1, keepdims=True)
            ix = jnp.min(jnp.where(work == mx, lane, float(LANES)), axis=-1, keepdims=True)
            idx.append(ix)
            val.append(mx)
            work = jnp.where(lane == ix, -jnp.inf, work)
        ex = [jnp.exp(v - val[0]) for v in val]
        den = ex[0] + ex[1] + ex[2] + ex[3]
        hot = jnp.zeros((sub, LANES), F32)
        for ix in idx:
            hot = hot + jnp.where(lane == ix, 1.0, 0.0)
        picks.append((idx, [e / den for e in ex], hot))

    inside = [jnp.dot(strict, hot.astype(BF16), preferred_element_type=F32) for _, _, hot in picks]
    run = run_s[...]
    for rows, (idx, wts, hot), before in zip(subs, picks, inside):
        before = before + run
        run = run + jnp.sum(hot, axis=0, keepdims=True)
        ranks = [jnp.sum(jnp.where(lane == ix, before, 0.0), axis=-1, keepdims=True) for ix in idx]
        ri = jnp.zeros((sub, LANES), F32)
        rw = jnp.zeros((sub, LANES), F32)
        for k in range(TOP_K):
            ri = jnp.where(lane == float(k), idx[k], ri)
            ri = jnp.where(lane == float(TOP_K + k), ranks[k], ri)
            rw = jnp.where(lane == float(k), wts[k], rw)
        ri_ref[:, rows] = jnp.transpose(ri)[:2 * TOP_K, :].astype(I32)
        rw_ref[rows, :] = rw
    run_s[...] = run
    cnt_ref[...] = run.astype(I32)


def _post_mixer(o, x, wo, mod, nw2, wr, br, seq, tm, row_off=0):
    t, d = o.shape
    per_seq = seq // tm
    tile_off = row_off // tm
    mod_spec = lambda j: pl.BlockSpec((None, 1, d), lambda i: (i // per_seq, 0, j))
    row = lambda w: pl.BlockSpec((tm, w), lambda i: (i, 0))
    return pl.pallas_call(
        _post_kernel,
        grid=(t // tm,),
        in_specs=[
            row(d), pl.BlockSpec((tm, d), lambda i: (i + tile_off, 0)),
            _resident((d, d), lambda i: (0, 0)),
            mod_spec(2),
            _resident((1, d), lambda i: (0, 0)),
            mod_spec(3), mod_spec(4),
            _resident((d, LANES), lambda i: (0, 0)),
            _resident((1, LANES), lambda i: (0, 0)),
        ],
        out_specs=[row(d), row(d // 2), pl.BlockSpec((2 * TOP_K, tm), lambda i: (0, i)), row(LANES),
                   pl.BlockSpec((1, LANES), lambda i: (0, 0))],
        out_shape=[
            jax.ShapeDtypeStruct((t, d), F32),
            jax.ShapeDtypeStruct((t, d // 2), U32),
            jax.ShapeDtypeStruct((2 * TOP_K, t), I32),
            jax.ShapeDtypeStruct((t, LANES), F32),
            jax.ShapeDtypeStruct((1, LANES), I32),
        ],
        scratch_shapes=[pltpu.VMEM((1, LANES), F32)],
        compiler_params=_cparams("arbitrary"),
        name="post_mixer",
    )(o, x, wo, mod, nw2, mod, mod, wr, br)


def _sc_mesh():
    return plsc.VectorSubcoreMesh(core_axis_name="c", subcore_axis_name="s")


def _sc_scatter_rows(x, posf, n_rows):
    t, w = x.shape
    workers = SC_CORES * SC_SUBCORES
    per_w = t // workers

    @functools.partial(
        pl.kernel, mesh=_sc_mesh(),
        out_type=jax.ShapeDtypeStruct((n_rows, w), x.dtype),
        scratch_types=[pltpu.VMEM((SC_ROWS,), I32)] * TOP_K
        + [pltpu.VMEM((SC_ROWS, w), x.dtype), pltpu.SemaphoreType.DMA])
    def scatter_kernel(x_hbm, pos_hbm, out_hbm, i0, i1, i2, i3, rows_v, sem):
        idxs = (i0, i1, i2, i3)
        wid = lax.axis_index("s") * SC_CORES + lax.axis_index("c")
        base = wid * per_w

        @pl.loop(0, per_w // SC_ROWS)
        def _(i):
            off = base + i * SC_ROWS
            for k in range(TOP_K):
                pltpu.sync_copy(pos_hbm.at[pl.ds(k * t + off, SC_ROWS)], idxs[k])
            pltpu.sync_copy(x_hbm.at[pl.ds(off, SC_ROWS)], rows_v)
            copies = [pltpu.async_copy(rows_v, out_hbm.at[idxs[k]], sem) for k in range(TOP_K)]
            for cp in copies:
                cp.wait()

    return scatter_kernel(x, posf)


def _sc_gather_rows(src, posf):
    n = posf.shape[0]
    w = src.shape[1]
    workers = SC_CORES * SC_SUBCORES
    per_w = n // workers

    @functools.partial(
        pl.kernel, mesh=_sc_mesh(),
        out_type=jax.ShapeDtypeStruct((n, w), src.dtype),
        scratch_types=[pltpu.VMEM((SC_ROWS,), I32), pltpu.VMEM((SC_ROWS, w), src.dtype),
                       pltpu.SemaphoreType.DMA])
    def gather_kernel(src_hbm, pos_hbm, out_hbm, idx_v, rows_v, sem):
        wid = lax.axis_index("s") * SC_CORES + lax.axis_index("c")
        base = wid * per_w

        @pl.loop(0, per_w // SC_ROWS)
        def _(i):
            off = base + i * SC_ROWS
            pltpu.sync_copy(pos_hbm.at[pl.ds(off, SC_ROWS)], idx_v)
            pltpu.async_copy(src_hbm.at[idx_v], rows_v, sem).wait()
            pltpu.sync_copy(rows_v, out_hbm.at[pl.ds(off, SC_ROWS)])

    return gather_kernel(src, posf)


GU_BLOCK = 2 * LANES
FF_CHUNK = 2 * LANES


def _deinterleave_matrix():
    sel = np.zeros((GU_BLOCK, GU_BLOCK), np.float32)
    j = np.arange(LANES)
    sel[2 * j, j] = 1.0
    sel[2 * j + 1, LANES + j] = 1.0
    return jnp.asarray(sel, BF16)


def _expert_kernel(be_ref, nu_ref, nxt_ref, x_ref, wgu_hbm, bg_ref, bu_ref, wd_hbm, bd_ref, sel_ref,
                   y_ref, wgu_f, wd_f, wgu_s, wd_s, sems, *, layer):
    i = pl.program_id(0)
    d, f2 = wgu_s.shape
    f = f2 // 2
    e_cur = be_ref[i]
    new_expert = jnp.logical_or(i == 0, e_cur != be_ref[jnp.maximum(i - 1, 0)])

    def weight_copies(e):
        return (pltpu.make_async_copy(wgu_hbm.at[layer, e], wgu_f, sems.at[0]),
                pltpu.make_async_copy(wd_hbm.at[layer, e], wd_f, sems.at[1]))

    @pl.when(jnp.logical_and(new_expert, i < nu_ref[0]))
    def _():
        @pl.when(i == 0)
        def _():
            for cp in weight_copies(e_cur):
                cp.start()

        for cp in weight_copies(e_cur):
            cp.wait()
        for c0 in range(0, f2, GU_BLOCK):
            blk = wgu_f[:, c0:c0 + GU_BLOCK].astype(BF16)
            wgu_s[:, c0:c0 + GU_BLOCK] = jnp.dot(
                blk, sel_ref[...], preferred_element_type=F32).astype(BF16)
        wd_s[...] = wd_f[...].astype(BF16)

        @pl.when(nxt_ref[e_cur] >= 0)
        def _():
            for cp in weight_copies(nxt_ref[e_cur]):
                cp.start()

    @pl.when(i < nu_ref[0])
    def _():
        half = x_ref.shape[1]
        lo, hi = _unpack_halves(x_ref[...])
        lo = lo.astype(BF16)
        hi = hi.astype(BF16)

        def hidden(c0):
            cols = slice(2 * c0, 2 * (c0 + FF_CHUNK))
            gu = (jnp.dot(lo, wgu_s[:half, cols], preferred_element_type=F32)
                  + jnp.dot(hi, wgu_s[half:, cols], preferred_element_type=F32))
            parts = []
            for b0 in range(0, FF_CHUNK, LANES):
                gate = gu[:, 2 * b0:2 * b0 + LANES] + bg_ref[0, :, c0 + b0:c0 + b0 + LANES]
                up = gu[:, 2 * b0 + LANES:2 * b0 + 2 * LANES] + bu_ref[0, :, c0 + b0:c0 + b0 + LANES]
                gate = jnp.minimum(gate, SWIGLU_LIMIT)
                up = jnp.clip(up, -SWIGLU_LIMIT, SWIGLU_LIMIT)
                parts.append((up + 1.0) * gate * _sigmoid(SWIGLU_ALPHA * gate))
            return jnp.concatenate(parts, axis=1).astype(BF16)

        chunks = list(range(0, f, FF_CHUNK))
        y = None
        h_next = hidden(chunks[0])
        for n, c0 in enumerate(chunks):
            h = h_next
            if n + 1 < len(chunks):
                h_next = hidden(chunks[n + 1])
            part = jnp.dot(h, wd_s[c0:c0 + FF_CHUNK, :], preferred_element_type=F32)
            y = part if y is None else y + part
        y_ref[...] = _pack_halves(y + bd_ref[0])

    @pl.when(i >= nu_ref[0])
    def _():
        y_ref[...] = jnp.zeros_like(y_ref)


def _expert_ffn(xs, blk_e, n_used, nxt_e, w_gu, bg, bu, w_down, bd, rb, layer):
    rows, half = xs.shape
    _, e, d, f2 = w_gu.shape
    f = f2 // 2
    n_blocks = rows // rb
    wspec = lambda shape: pl.BlockSpec(shape, lambda i, be, nu, nx: (be[i], 0, 0))
    return pl.pallas_call(
        functools.partial(_expert_kernel, layer=layer),
        grid_spec=pltpu.PrefetchScalarGridSpec(
            num_scalar_prefetch=3,
            grid=(n_blocks,),
            in_specs=[
                pl.BlockSpec((rb, half), lambda i, be, nu, nx: (i, 0)),
                pl.BlockSpec(memory_space=pl.ANY),
                wspec((1, 1, f)), wspec((1, 1, f)),
                pl.BlockSpec(memory_space=pl.ANY),
                wspec((1, 1, d)),
                _resident((GU_BLOCK, GU_BLOCK), lambda i, be, nu, nx: (0, 0)),
            ],
            out_specs=pl.BlockSpec((rb, d // 2), lambda i, be, nu, nx: (i, 0)),
            scratch_shapes=[pltpu.VMEM((d, f2), F32), pltpu.VMEM((f, d), F32),
                            pltpu.VMEM((d, f2), BF16), pltpu.VMEM((f, d), BF16),
                            pltpu.SemaphoreType.DMA((2,))],
        ),
        out_shape=jax.ShapeDtypeStruct((rows, d // 2), U32),
        compiler_params=_cparams("arbitrary"),
        name="expert_ffn",
    )(blk_e, n_used, nxt_e, xs, w_gu, bg, bu, w_down, bd, _deinterleave_matrix())


def _combine_kernel(x_ref, y0_ref, y1_ref, y2_ref, y3_ref, rw_ref, g2_ref, *dst_and_out):
    o_ref = dst_and_out[-1]
    half = y0_ref.shape[1]
    lo, hi = _moe_mix(x_ref, (y0_ref, y1_ref, y2_ref, y3_ref), rw_ref, g2_ref)
    o_ref[:, :half] = lo
    o_ref[:, half:] = hi


def _combine(x1, yt, rw, mod, seq, tm, out_rows=None, row_off=0, dst=None):
    t, d = x1.shape
    per_seq = seq // tm
    nt = t // tm
    tile_off = row_off // tm
    yspec = lambda k: pl.BlockSpec((tm, d // 2), lambda i: (k * nt + i, 0))
    in_specs = [
        pl.BlockSpec((tm, d), lambda i: (i, 0)),
        yspec(0), yspec(1), yspec(2), yspec(3),
        pl.BlockSpec((tm, LANES), lambda i: (i, 0)),
        pl.BlockSpec((None, 1, d), lambda i: (i // per_seq, 0, 5)),
    ]
    args = [x1, yt, yt, yt, yt, rw, mod]
    aliases = {}
    if dst is not None:
        in_specs.append(pl.BlockSpec(memory_space=pl.ANY))
        args.append(dst)
        aliases = {len(args) - 1: 0}
    return pl.pallas_call(
        _combine_kernel,
        grid=(nt,),
        in_specs=in_specs,
        out_specs=pl.BlockSpec((tm, d), lambda i: (i + tile_off, 0)),
        out_shape=jax.ShapeDtypeStruct((t if out_rows is None else out_rows, d), F32),
        input_output_aliases=aliases,
        compiler_params=_cparams("parallel"),
        name="moe_combine",
    )(*args)


POST_ROWS = 1024
POST_SUB = 256


def _row_tile(seq):
    return min(seq, 512)


def _moe_block_rows(t):
    return min(512, t * TOP_K // 8)


def kernel(x, c, fox_w_in, fox_b_f, fox_q_norm, fox_k_norm, fox_w_out, hgrn_w_in, hgrn_lb, hgrn_g_norm, hgrn_w_out, ada_w, ada_b, norm1_w, norm2_w, router_w, router_b, exp_w_gu, exp_b_gu, exp_w_down, exp_b_down):
    bsz, seq, d = x.shape
    t = bsz * seq
    depth = ada_w.shape[0]
    n_exp = router_w.shape[-1]
    f_dim = exp_w_down.shape[2]
    h_fox = fox_w_in.shape[-1] - 4 * d
    dh = d // h_fox
    tm = _row_tile(seq)
    n_split = 2 if bsz % 2 == 0 else 1
    bh, th = bsz // n_split, t // n_split
    rb = _moe_block_rows(th)
    n_blocks = (th * TOP_K) // rb + n_exp
    hgrn_hpb = 4 if d % (4 * LANES) == 0 else 2

    mod_all = _adaln_mod(c, ada_w, ada_b).reshape(depth, bsz, 1, -1)
    lb_soft = jax.nn.softmax(hgrn_lb.astype(F32), axis=0)
    lower = jnp.cumsum(lb_soft, axis=0) - lb_soft[0]
    experts = jnp.arange(n_exp, dtype=I32)

    xf = x.reshape(t, d)
    parts = [xf] * n_split
    for i in range(depth):
        j = i // 2
        nw1 = norm1_w[i].reshape(1, d)
        if i % 2 == 0:
            w_in = fox_w_in[j]
            w_main = w_in[:, :4 * d].astype(BF16)
            w_f = jnp.pad(w_in[:, 4 * d:], ((0, 0), (0, LANES - h_fox))).astype(BF16)
            b_f = jnp.pad(fox_b_f[j], (0, LANES - h_fox)).reshape(1, LANES)
            qw = jnp.tile(fox_q_norm[j], LANES // dh).reshape(1, LANES)
            kw = jnp.tile(fox_k_norm[j], LANES // dh).reshape(1, LANES)
            w_out = fox_w_out[j].astype(BF16)
        else:
            w_h = hgrn_w_in[j]
            w_main = jnp.concatenate([w_h[:, :d], w_h[:, 2 * d:]], axis=1).astype(BF16)
            w_f = w_h[:, d:2 * d].astype(BF16)
            b_f = jnp.zeros((1, d), F32)
            w_out = hgrn_w_out[j].astype(BF16)
        wr = jnp.pad(router_w[i], ((0, 0), (0, LANES - n_exp))).astype(BF16)
        br = jnp.pad(router_b[i], (0, LANES - n_exp), constant_values=MASK_VALUE).reshape(1, LANES)
        b_gu = exp_b_gu[i].reshape(n_exp, 1, f_dim, 2)

        def front(s, src):
            mod = mod_all[i, s * bh:(s + 1) * bh]
            off = s * th if i == 0 else 0
            moe = None
            xp = src
            if isinstance(src, tuple):
                xp, moe = src[0], src[1:]
            if i % 2 == 0:
                outs = _in_proj(xp, nw1, mod, w_main, BF16, seq, tm, w_f, b_f,
                                rows=th, row_off=off, moe=moe, name="in_proj_fox")
                o = _fox_attention(outs[0], outs[1], qw, kw, seq, d, dh)
            else:
                outs = _in_proj(xp, nw1, mod, w_main, BF16, seq, tm, w_f, b_f,
                                rows=th, row_off=off, moe=moe, name="in_proj_hgrn",
                                gate_lb=lower[j].reshape(1, d))
                o = _hgrn_mixer(outs[0], outs[1], hgrn_g_norm[j].reshape(1, HGRN_DIM), seq, d,
                                hgrn_hpb)
            if moe is not None:
                xp = outs[-1]
            x1, hp, ri, rw, cnt = _post_mixer(o, xp, w_out, mod, norm2_w[i].reshape(1, d), wr, br,
                                              seq, min(seq, POST_ROWS), row_off=off)
            counts = cnt[0, :n_exp]
            padded = (counts + rb - 1) // rb * rb
            pad_end = jnp.cumsum(padded)
            pad_start = pad_end - padded
            top_idx, rank = ri[:TOP_K], ri[TOP_K:]
            start = jnp.sum(jnp.where(top_idx[None] == experts[:, None, None],
                                      pad_start[:, None, None], 0), axis=0)
            posf = (start + rank).reshape(-1).astype(I32)
            blk_lo = jnp.arange(n_blocks, dtype=I32) * rb
            blk_e = jnp.minimum(jnp.sum(pad_end[None, :] <= blk_lo[:, None], axis=1),
                                n_exp - 1).astype(I32)
            n_used = (pad_end[-1] // rb).astype(I32).reshape(1)
            later = jnp.where((experts[None, :] > experts[:, None]) & (padded[None, :] > 0),
                              experts[None, :], n_exp)
            nxt_e = jnp.min(later, axis=1)
            nxt_e = jnp.where(nxt_e >= n_exp, -1, nxt_e).astype(I32)
            return dict(x1=x1, hp=hp, rw=rw, posf=posf, blk_e=blk_e, n_used=n_used, nxt_e=nxt_e,
                        mod=mod)

        def experts_of(st, xs):
            return _expert_ffn(xs, st["blk_e"], st["n_used"], st["nxt_e"], exp_w_gu,
                               b_gu[..., 0], b_gu[..., 1], exp_w_down,
                               exp_b_down[i].reshape(n_exp, 1, d), rb, i)

        def deferred(st, yt):
            return (st["x1"], yt, st["rw"], st["mod"])

        if n_split == 1:
            st = front(0, parts[0])
            xs = _sc_scatter_rows(st["hp"], st["posf"], n_blocks * rb)
            parts[0] = deferred(st, _sc_gather_rows(experts_of(st, xs), st["posf"]))
            continue

        tie = lax.optimization_barrier
        sa = front(0, parts[0])
        xs_a = _sc_scatter_rows(sa["hp"], sa["posf"], n_blocks * rb)
        src_b = parts[1]
        if isinstance(src_b, tuple):
            yt_b, _ = tie((src_b[1], sa["hp"]))
            src_b = (src_b[0], yt_b) + src_b[2:]
        else:
            src_b, _ = tie((src_b, sa["hp"]))
        sb = front(1, src_b)
        xs_a, _ = tie((xs_a, sb["hp"]))
        xs_b = _sc_scatter_rows(sb["hp"], sb["posf"], n_blocks * rb)
        ys_a = experts_of(sa, xs_a)
        xs_b, _ = tie((xs_b, ys_a))
        yt_a = _sc_gather_rows(ys_a, sa["posf"])
        ys_b = experts_of(sb, xs_b)
        yt_a, _ = tie((yt_a, ys_b))
        yt_b = _sc_gather_rows(ys_b, sb["posf"])
        parts = [deferred(sa, yt_a), deferred(sb, yt_b)]

    out = None
    for s_i, (x1, yt, rw, mod) in enumerate(parts):
        if n_split == 1:
            out = _combine(x1, yt, rw, mod, seq, tm)
        else:
            out = _combine(x1, yt, rw, mod, seq, tm, out_rows=t, row_off=s_i * th, dst=out)
    return out.reshape(bsz, seq, d)
```
